```python
import math
import jax, jax.numpy as jnp
from jax import lax
import numpy as np

D_MODEL = 2048
BATCH = 2
SEQ = 4096
DEPTH = 1

CHUNK = 64
BLK = 128
HEAD_DIM = 64
A_HEADS = 16
A_KV_HEADS = 2
A_REP = A_HEADS // A_KV_HEADS
WINDOW = 128
WIN_CHUNKS = WINDOW // CHUNK
B_HEADS = 16
A_WIDTH = A_HEADS * HEAD_DIM
A_KV_WIDTH = A_KV_HEADS * HEAD_DIM
B_WIDTH = B_HEADS * HEAD_DIM
IN_SIZES = (A_WIDTH, A_KV_WIDTH, A_KV_WIDTH, B_WIDTH, B_WIDTH, B_WIDTH, D_MODEL, D_MODEL)
N_IN = sum(IN_SIZES)
IN_CUTS = tuple(int(c) for c in np.cumsum(IN_SIZES)[:-1])
REL_BUCKETS = 32
REL_MAX_DIST = 128
MEM_LEN = 256
X_HEADS = 4
X_HEAD_DIM = D_MODEL // X_HEADS
N_EXPERTS = 32
TOP_K = 4
D_FF = D_MODEL
SWIGLU_LIMIT = 7.0
SWIGLU_ALPHA = 1.702
EXPERT_BLOCK = 128
LN_EPS = 1e-5
DEEPNORM_ALPHA = (2.0 * DEPTH) ** 0.25
DEEPNORM_BETA = (8.0 * DEPTH) ** -0.25

kernel_name = "hybrid_swa_stickbreaking_moe_streaming_block"


def layer_norm(x, g, b):
    xf = x.astype(jnp.float32)
    mu = xf.mean(-1, keepdims=True)
    var = jnp.square(xf - mu).mean(-1, keepdims=True)
    y = (xf - mu) * lax.rsqrt(var + LN_EPS)
    return (y * g.astype(jnp.float32) + b.astype(jnp.float32)).astype(x.dtype)


def t5_bucket(rel):
    half = REL_BUCKETS // 2
    max_exact = half // 2
    base = jnp.where(rel > 0, half, 0)
    n = jnp.abs(rel)
    nf = jnp.maximum(n, 1).astype(jnp.float32)
    large = max_exact + (jnp.log(nf / max_exact) / math.log(REL_MAX_DIST / max_exact)
                         * (half - max_exact)).astype(jnp.int32)
    large = jnp.minimum(large, half - 1)
    return base + jnp.where(n < max_exact, n, large)


def sliding_window_attn(q, k, v, sinks, rel_bias):
    b_, s_ = q.shape[0], q.shape[1]
    nb = s_ // BLK
    qb = q.reshape(b_, nb, BLK, A_KV_HEADS, A_REP, HEAD_DIM)
    kb = k.reshape(b_, nb, BLK, A_KV_HEADS, HEAD_DIM)
    vb = v.reshape(b_, nb, BLK, A_KV_HEADS, HEAD_DIM)

    def with_prev(t):
        prev = jnp.concatenate([jnp.zeros_like(t[:, :1]), t[:, :-1]], axis=1)
        return jnp.concatenate([prev, t], axis=2)

    kk, vv = with_prev(kb), with_prev(vb)
    logits = jnp.einsum('bnqgrd,bnkgd->bngrqk', qb, kk).astype(jnp.float32) * (HEAD_DIM ** -0.5)
    qi = jnp.arange(BLK)[:, None]
    kj = jnp.arange(2 * BLK)[None, :]
    bias = rel_bias[t5_bucket(kj - BLK - qi)]
    bias = jnp.transpose(bias, (2, 0, 1)).reshape(A_KV_HEADS, A_REP, BLK, 2 * BLK).astype(jnp.float32)
    dchunk = (kj // CHUNK - BLK // CHUNK) - qi // CHUNK
    band = (dchunk <= 0) & (dchunk >= -WIN_CHUNKS)
    key_exists = (jnp.arange(nb)[:, None] > 0) | (jnp.arange(2 * BLK)[None, :] >= BLK)
    mask = band[None] & key_exists[:, None, :]
    logits = jnp.where(mask[None, :, None, None], logits + bias, -jnp.inf)
    sink = sinks.astype(jnp.float32).reshape(1, 1, A_KV_HEADS, A_REP, 1, 1)
    m = jnp.maximum(logits.max(-1, keepdims=True), sink)
    p = jnp.exp(logits - m)
    p = p / (p.sum(-1, keepdims=True) + jnp.exp(sink - m))
    out = jnp.einsum('bngrqk,bnkgd->bnqgrd', p.astype(v.dtype), vv)
    return out.reshape(b_, s_, A_WIDTH)


def stick_breaking_attn(q, k, v):
    b_, s_ = q.shape[0], q.shape[1]
    nb = s_ // BLK
    qb = jnp.moveaxis(q.reshape(b_, nb, BLK, B_HEADS, HEAD_DIM), 1, 0)
    starts = jnp.arange(nb) * BLK
    key_pos = jnp.arange(s_)
    scale = HEAD_DIM ** -0.5

    def block(args):
        qblk, start = args
        z = jnp.einsum('bqhd,bkhd->bhqk', qblk, k).astype(jnp.float32) * scale
        causal = key_pos[None, :] < (start + jnp.arange(BLK))[:, None]
        log_keep = jnp.where(causal, jax.nn.log_sigmoid(-z), 0.0)
        suffix = lax.cumsum(log_keep, axis=3, reverse=True)
        log_w = jax.nn.log_sigmoid(z) + (suffix - log_keep)
        w = jnp.where(causal, jnp.exp(log_w), 0.0).astype(v.dtype)
        return jnp.einsum('bhqk,bkhd->bqhd', w, v)

    out = lax.map(block, (qb, starts))
    return jnp.moveaxis(out, 0, 1).reshape(b_, s_, B_WIDTH)


def memory_cross_attn(h, mem, w_q, w_kv, w_o):
    b_, s_, _ = h.shape
    q = (h @ w_q).reshape(b_, s_, X_HEADS, X_HEAD_DIM)
    k, v = jnp.split(mem @ w_kv, 2, axis=-1)
    k = k.reshape(b_, MEM_LEN, X_HEADS, X_HEAD_DIM)
    v = v.reshape(b_, MEM_LEN, X_HEADS, X_HEAD_DIM)
    logits = jnp.einsum('bqhd,bkhd->bhqk', q, k).astype(jnp.float32) * (X_HEAD_DIM ** -0.5)
    p = jax.nn.softmax(logits, axis=-1).astype(v.dtype)
    o = jnp.einsum('bhqk,bkhd->bqhd', p, v).reshape(b_, s_, D_MODEL)
    return o @ w_o


def moe_ffn(x2d, w_router, b_router, w_up, b_up, w_down, b_down):
    t_ = x2d.shape[0]
    logits = (x2d @ w_router + b_router).astype(jnp.float32)
    top_val, top_idx = lax.top_k(logits, TOP_K)
    gate = jax.nn.softmax(top_val, axis=-1)
    n_pairs = t_ * TOP_K
    pair_expert = top_idx.reshape(n_pairs)
    pair_token = jnp.arange(n_pairs) // TOP_K
    pair_gate = gate.reshape(n_pairs)
    order = jnp.argsort(pair_expert)
    sorted_e = pair_expert[order]
    counts = jnp.zeros((N_EXPERTS,), jnp.int32).at[pair_expert].add(1)
    padded = (counts + EXPERT_BLOCK - 1) // EXPERT_BLOCK * EXPERT_BLOCK
    offset = jnp.cumsum(counts) - counts
    pad_end = jnp.cumsum(padded)
    pad_offset = pad_end - padded
    dest = pad_offset[sorted_e] + (jnp.arange(n_pairs) - offset[sorted_e])
    n_rows = n_pairs + N_EXPERTS * EXPERT_BLOCK
    n_blocks = n_rows // EXPERT_BLOCK
    row_pair = jnp.zeros((n_rows,), jnp.int32).at[dest].set(order.astype(jnp.int32))
    row_valid = jnp.zeros((n_rows,), bool).at[dest].set(True)
    row_token = pair_token[row_pair]
    row_gate = jnp.where(row_valid, pair_gate[row_pair], 0.0)
    block_expert = jnp.minimum(
        jnp.searchsorted(pad_end, jnp.arange(n_blocks) * EXPERT_BLOCK, side='right'), N_EXPERTS - 1)

    def expert_block(args):
        tok, e = args
        hid = x2d[tok] @ w_up[e] + b_up[e]
        glu, lin = hid[:, :D_FF], hid[:, D_FF:]
        glu = jnp.minimum(glu, SWIGLU_LIMIT)
        lin = jnp.clip(lin, -SWIGLU_LIMIT, SWIGLU_LIMIT)
        act = glu * jax.nn.sigmoid(SWIGLU_ALPHA * glu) * (lin + 1.0)
        return act @ w_down[e] + b_down[e]

    y_rows = lax.map(expert_block, (row_token.reshape(n_blocks, EXPERT_BLOCK), block_expert))
    y_rows = y_rows.reshape(n_rows, D_MODEL)
    return jnp.zeros_like(x2d).at[row_token].add(y_rows * row_gate[:, None].astype(y_rows.dtype))


def setup_inputs(seed: int = 0) -> dict:
    key = jax.random.key(seed)
    ks = jax.random.split(key, 26)
    f32 = jnp.float32

    def nrm(k, shape, scale):
        return jax.random.normal(k, shape, f32) * scale

    L, D, F, G = DEPTH, D_MODEL, D_FF, N_EXPERTS
    beta = DEEPNORM_BETA
    return {
        "x": nrm(ks[0], (BATCH, SEQ, D), 1.0),
        "mem": nrm(ks[1], (BATCH, MEM_LEN, D), 1.0),
        "ln_in_g": 1.0 + nrm(ks[2], (D,), 0.02),
        "ln_in_b": nrm(ks[3], (D,), 0.02),
        "rel_bias": nrm(ks[4], (REL_BUCKETS, A_HEADS), 0.5),
        "w_in": nrm(ks[5], (L, D, N_IN), D ** -0.5),
        "b_in": nrm(ks[6], (L, N_IN), 0.02),
        "attn_sinks": nrm(ks[7], (L, A_HEADS), 0.5),
        "w_a_out": nrm(ks[8], (L, A_WIDTH, D), A_WIDTH ** -0.5),
        "w_b_out": nrm(ks[9], (L, B_WIDTH, D), B_WIDTH ** -0.5),
        "w_mix_out": nrm(ks[10], (L, D, D), beta * D ** -0.5),
        "ln1_g": 1.0 + nrm(ks[11], (L, D), 0.02),
        "ln1_b": nrm(ks[12], (L, D), 0.02),
        "w_xq": nrm(ks[13], (L, D, D), D ** -0.5),
        "w_xkv": nrm(ks[14], (L, D, 2 * D), D ** -0.5),
        "w_xo": nrm(ks[15], (L, D, D), beta * D ** -0.5),
        "ln2_g": 1.0 + nrm(ks[16], (L, D), 0.02),
        "ln2_b": nrm(ks[17], (L, D), 0.02),
        "w_router": nrm(ks[18], (L, D, G), D ** -0.5),
        "b_router": nrm(ks[19], (L, G), 0.01),
        "w_up": nrm(ks[20], (L, G, D, 2 * F), D ** -0.5),
        "b_up": nrm(ks[21], (L, G, 2 * F), 0.02),
        "w_down": nrm(ks[22], (L, G, F, D), beta * F ** -0.5),
        "b_down": nrm(ks[23], (L, G, D), 0.02),
        "ln3_g": 1.0 + nrm(ks[24], (L, D), 0.02),
        "ln3_b": nrm(ks[25], (L, D), 0.02),
    }


def reference(x, mem, ln_in_g, ln_in_b, rel_bias, w_in, b_in, attn_sinks, w_a_out, w_b_out,
              w_mix_out, ln1_g, ln1_b, w_xq, w_xkv, w_xo, ln2_g, ln2_b, w_router, b_router,
              w_up, b_up, w_down, b_down, ln3_g, ln3_b):
    b_, s_, _ = x.shape
    h = layer_norm(x, ln_in_g, ln_in_b)
    for l in range(DEPTH):
        proj = h @ w_in[l] + b_in[l]
        qa, ka, va, qb, kb, vb, ga, gb = jnp.split(proj, IN_CUTS, axis=-1)
        ya = sliding_window_attn(qa.reshape(b_, s_, A_HEADS, HEAD_DIM),
                                 ka.reshape(b_, s_, A_KV_HEADS, HEAD_DIM),
                                 va.reshape(b_, s_, A_KV_HEADS, HEAD_DIM),
                                 attn_sinks[l], rel_bias) @ w_a_out[l]
        yb = stick_breaking_attn(qb.reshape(b_, s_, B_HEADS, HEAD_DIM),
                                 kb.reshape(b_, s_, B_HEADS, HEAD_DIM),
                                 vb.reshape(b_, s_, B_HEADS, HEAD_DIM)) @ w_b_out[l]
        mixed = (jax.nn.sigmoid(ga) * ya + jax.nn.sigmoid(gb) * yb) @ w_mix_out[l]
        h = layer_norm(DEEPNORM_ALPHA * h + mixed, ln1_g[l], ln1_b[l])
        xo = memory_cross_attn(h, mem, w_xq[l], w_xkv[l], w_xo[l])
        h = layer_norm(DEEPNORM_ALPHA * h + xo, ln2_g[l], ln2_b[l])
        ff = moe_ffn(h.reshape(b_ * s_, D_MODEL), w_router[l], b_router[l],
                     w_up[l], b_up[l], w_down[l], b_down[l]).reshape(b_, s_, D_MODEL)
        h = layer_norm(DEEPNORM_ALPHA * h + ff, ln3_g[l], ln3_b[l])
    return h
```

```python
import functools

import jax
import jax.numpy as jnp
import numpy as np
from jax import lax
from jax.experimental import pallas as pl
from jax.experimental.pallas import tpu as pltpu

DEPTH = 1
CHUNK = 64
BLK = 128
HEAD_DIM = 64
A_HEADS = 16
A_KV_HEADS = 2
A_REP = A_HEADS // A_KV_HEADS
WIN_CHUNKS = 2
B_HEADS = 16
A_WIDTH = A_HEADS * HEAD_DIM
A_KV_WIDTH = A_KV_HEADS * HEAD_DIM
B_WIDTH = B_HEADS * HEAD_DIM
REL_BUCKETS = 32
REL_MAX_DIST = 128
X_HEADS = 4
N_EXPERTS = 32
TOP_K = 4
SWIGLU_LIMIT = 7.0
SWIGLU_ALPHA = 1.702
LN_EPS = 1e-5
DEEPNORM_ALPHA = (2.0 * DEPTH) ** 0.25
QK_SCALE = HEAD_DIM ** -0.5

V7X_LANES = 128
V7X_VMEM_LIMIT_BYTES = 56 * 1024 * 1024

ROW_TILE = 512
ATT_TILE = 256
SB_TILE = 256
EXPERT_TILE = 256
COMBINE_TILE = 128
SB_LOG_ZERO = -105.0

F32 = jnp.float32
BF16 = jnp.bfloat16
NT_DIMS = (((1,), (1,)), ((), ()))


def _params(semantics):
    return pltpu.CompilerParams(dimension_semantics=semantics,
                                vmem_limit_bytes=V7X_VMEM_LIMIT_BYTES)


def _const_spec(shape):
    nd = len(shape)
    return pl.BlockSpec(shape, lambda *_: (0,) * nd, pipeline_mode=pl.Buffered(1))


def _layer_norm(x, g, b):
    mu = jnp.mean(x, axis=-1, keepdims=True)
    xc = x - mu
    var = jnp.mean(xc * xc, axis=-1, keepdims=True)
    return xc * lax.rsqrt(var + LN_EPS) * g + b


def _ln_in_kernel(x_ref, g_ref, b_ref, h_ref, hb_ref):
    y = _layer_norm(x_ref[...], g_ref[...], b_ref[...])
    h_ref[...] = y
    hb_ref[...] = y.astype(BF16)


def _ln_in(x2d, g, b):
    t, d = x2d.shape
    tm = min(ROW_TILE, t)
    row = pl.BlockSpec((tm, d), lambda i: (i, 0))
    return pl.pallas_call(
        _ln_in_kernel,
        grid=(t // tm,),
        in_specs=[row, _const_spec((1, d)), _const_spec((1, d))],
        out_specs=[row, row],
        out_shape=[jax.ShapeDtypeStruct((t, d), F32), jax.ShapeDtypeStruct((t, d), BF16)],
        compiler_params=_params(("arbitrary",)),
        name="ln_in",
    )(x2d, g.reshape(1, d), b.reshape(1, d))


def _mm_bias_kernel(a_ref, w_ref, b_ref, o_ref):
    acc = jnp.dot(a_ref[...], w_ref[...], preferred_element_type=F32)
    o_ref[...] = (acc + b_ref[...]).astype(o_ref.dtype)


def _pick_tile(n, target):
    best = None
    for c in range(V7X_LANES, min(n, target) + 1, V7X_LANES):
        if n % c == 0:
            best = c
    return n if best is None else best


def _mm_bias(a, w, b, out_dtype, name):
    m, k = a.shape
    n = w.shape[1]
    tm = min(ROW_TILE, m)
    tn = _pick_tile(n, 1024)
    return pl.pallas_call(
        _mm_bias_kernel,
        grid=(n // tn, m // tm),
        in_specs=[pl.BlockSpec((tm, k), lambda j, i: (i, 0)),
                  pl.BlockSpec((k, tn), lambda j, i: (0, j)),
                  pl.BlockSpec((1, tn), lambda j, i: (0, j))],
        out_specs=pl.BlockSpec((tm, tn), lambda j, i: (i, j)),
        out_shape=jax.ShapeDtypeStruct((m, n), out_dtype),
        compiler_params=_params(("arbitrary", "arbitrary")),
        name=name,
    )(a, w, b.reshape(1, n).astype(F32))


def _swa_kernel(sink_ref, q_ref, kp_ref, kc_ref, vp_ref, vc_ref, bias_ref, o_ref):
    n = pl.program_id(1)
    kk = jnp.concatenate([kp_ref[...], kc_ref[...]], axis=0)
    vv = jnp.concatenate([vp_ref[...], vc_ref[...]], axis=0)
    col = lax.broadcasted_iota(jnp.int32, (BLK, 2 * BLK), 1)
    exists = (col >= BLK) | (n > 0)
    for h in range(A_HEADS):
        g = h // A_REP
        qh = q_ref[:, h * HEAD_DIM:(h + 1) * HEAD_DIM] * QK_SCALE
        kg = kk[:, g * HEAD_DIM:(g + 1) * HEAD_DIM]
        vg = vv[:, g * HEAD_DIM:(g + 1) * HEAD_DIM]
        s = lax.dot_general(qh, kg, NT_DIMS, preferred_element_type=F32)
        s = jnp.where(exists, s + bias_ref[h], -jnp.inf)
        sink = sink_ref[h]
        m = jnp.maximum(jnp.max(s, axis=-1, keepdims=True), sink)
        p = jnp.exp(s - m)
        denom = jnp.sum(p, axis=-1, keepdims=True) + jnp.exp(sink - m)
        o = jnp.dot(p.astype(BF16), vg, preferred_element_type=F32) / denom
        o_ref[:, h * HEAD_DIM:(h + 1) * HEAD_DIM] = o.astype(o_ref.dtype)


def _rel_bucket(rel):
    half = REL_BUCKETS // 2
    max_exact = half // 2
    base = jnp.where(rel > 0, half, 0)
    n = jnp.abs(rel)
    nf = jnp.maximum(n, 1).astype(F32)
    large = max_exact + (jnp.log(nf / max_exact) / np.log(REL_MAX_DIST / max_exact)
                         * (half - max_exact)).astype(jnp.int32)
    large = jnp.minimum(large, half - 1)
    return base + jnp.where(n < max_exact, n, large)


def _swa_bias_table(rel_bias):
    qi = jnp.arange(BLK)[:, None]
    kj = jnp.arange(2 * BLK)[None, :]
    bias = jnp.transpose(rel_bias[_rel_bucket(kj - BLK - qi)], (2, 0, 1)).astype(F32)
    dchunk = (kj // CHUNK - BLK // CHUNK) - qi // CHUNK
    band = (dchunk <= 0) & (dchunk >= -WIN_CHUNKS)
    return jnp.where(band[None], bias, -jnp.inf)


def _swa(qkv, sinks, rel_bias, batch, seq):
    nb = seq // BLK
    kcol = A_WIDTH // A_KV_WIDTH
    vcol = kcol + 1
    kv_blk = (BLK, A_KV_WIDTH)
    grid_spec = pltpu.PrefetchScalarGridSpec(
        num_scalar_prefetch=1,
        grid=(batch, nb),
        in_specs=[
            pl.BlockSpec((BLK, A_WIDTH), lambda b, n, s: (b * nb + n, 0)),
            pl.BlockSpec(kv_blk, lambda b, n, s: (b * nb + jnp.maximum(n - 1, 0), kcol)),
            pl.BlockSpec(kv_blk, lambda b, n, s: (b * nb + n, kcol)),
            pl.BlockSpec(kv_blk, lambda b, n, s: (b * nb + jnp.maximum(n - 1, 0), vcol)),
            pl.BlockSpec(kv_blk, lambda b, n, s: (b * nb + n, vcol)),
            pl.BlockSpec((A_HEADS, BLK, 2 * BLK), lambda b, n, s: (0, 0, 0),
                         pipeline_mode=pl.Buffered(1)),
        ],
        out_specs=pl.BlockSpec((BLK, A_WIDTH), lambda b, n, s: (b * nb + n, 0)),
    )
    return pl.pallas_call(
        _swa_kernel,
        grid_spec=grid_spec,
        out_shape=jax.ShapeDtypeStruct((batch * seq, A_WIDTH), BF16),
        compiler_params=_params(("arbitrary", "arbitrary")),
        name="swa_attn",
    )(sinks.astype(F32), qkv, qkv, qkv, qkv, qkv, _swa_bias_table(rel_bias))


def _sb_kernel(q_ref, k_ref, v_ref, tri_ref, o_ref):
    tq = q_ref.shape[0]
    i = pl.program_id(2)
    q2 = q_ref[...] * QK_SCALE
    lane = lax.broadcasted_iota(jnp.int32, (1, 2 * HEAD_DIM), 1)
    first = lane < HEAD_DIM
    zero = jnp.zeros((), BF16)
    q_heads = (jnp.where(first, q2, zero), jnp.where(first, zero, q2))
    row = lax.broadcasted_iota(jnp.int32, (tq, tq), 0)
    col = lax.broadcasted_iota(jnp.int32, (tq, tq), 1)
    tri = tri_ref[...]

    def cond(state):
        j, live = state[0], state[1]
        return (j >= 0) & (live > 0)

    def body(state):
        j, _, ca, cb, acc = state
        start = pl.multiple_of(j * tq, tq)
        k2 = k_ref[pl.ds(start, tq), :]
        v2 = v_ref[pl.ds(start, tq), :]
        causal = (col + (j - i) * tq) < row
        carries = [ca, cb]
        for h in range(2):
            z = lax.dot_general(q_heads[h], k2, NT_DIMS, preferred_element_type=F32)
            sp = jnp.log1p(jnp.exp(-jnp.abs(z)))
            log_beta = jnp.minimum(z, 0.0) - sp
            log_keep = jnp.where(causal, jnp.minimum(-z, 0.0) - sp, 0.0)
            hi = log_keep.astype(BF16)
            lo = (log_keep - hi.astype(F32)).astype(BF16)
            suffix = jnp.dot(jnp.concatenate([hi, lo], axis=1), tri, preferred_element_type=F32)
            log_w = log_beta + (suffix - log_keep) + carries[h]
            w = jnp.where(causal, jnp.exp(log_w), 0.0)
            vh = jnp.where(first if h == 0 else ~first, v2, zero)
            acc = acc + jnp.dot(w.astype(BF16), vh, preferred_element_type=F32)
            carries[h] = carries[h] + suffix[:, 0:1]
        live = (jnp.max(jnp.maximum(carries[0], carries[1])) > SB_LOG_ZERO).astype(jnp.int32)
        return j - 1, live, carries[0], carries[1], acc

    init = (i, jnp.int32(1), jnp.zeros((tq, 1), F32), jnp.zeros((tq, 1), F32),
            jnp.zeros((tq, 2 * HEAD_DIM), F32))
    acc = lax.while_loop(cond, body, init)[4]
    o_ref[...] = acc.astype(o_ref.dtype)


def _stick_breaking(qkv, batch, seq):
    tq = min(SB_TILE, seq)
    nq = seq // tq
    pairs = B_WIDTH // (2 * HEAD_DIM)
    j = np.arange(2 * tq)[:, None] % tq
    s = np.arange(tq)[None, :]
    tri = jnp.asarray((j >= s).astype(np.float32), dtype=BF16)
    return pl.pallas_call(
        _sb_kernel,
        grid=(batch, pairs, nq),
        in_specs=[
            pl.BlockSpec((tq, 2 * HEAD_DIM), lambda b, p, i: (b * nq + i, p)),
            pl.BlockSpec((seq, 2 * HEAD_DIM), lambda b, p, i: (b, pairs + p)),
            pl.BlockSpec((seq, 2 * HEAD_DIM), lambda b, p, i: (b, 2 * pairs + p)),
            _const_spec((2 * tq, tq)),
        ],
        out_specs=pl.BlockSpec((tq, 2 * HEAD_DIM), lambda b, p, i: (b * nq + i, p)),
        out_shape=jax.ShapeDtypeStruct((batch * seq, B_WIDTH), BF16),
        compiler_params=_params(("arbitrary", "arbitrary", "arbitrary")),
        name="stick_breaking",
    )(qkv, qkv, qkv, tri)


def _mix_kernel(ya_ref, yb_ref, gates_a_ref, gates_b_ref, h_ref, wa_ref, wb_ref, wm_ref,
                g_ref, b_ref, o_ref, ob_ref):
    pa = jnp.dot(ya_ref[...], wa_ref[...], preferred_element_type=F32)
    pb = jnp.dot(yb_ref[...], wb_ref[...], preferred_element_type=F32)
    merged = jax.nn.sigmoid(gates_a_ref[...]) * pa + jax.nn.sigmoid(gates_b_ref[...]) * pb
    mixed = jnp.dot(merged.astype(BF16), wm_ref[...], preferred_element_type=F32)
    y = _layer_norm(DEEPNORM_ALPHA * h_ref[...] + mixed, g_ref[...], b_ref[...])
    o_ref[...] = y
    ob_ref[...] = y.astype(BF16)


def _mix(ya, yb, gates, h, wa, wb, wm, g, b):
    t, d = h.shape
    tm = min(ATT_TILE, t)
    row = pl.BlockSpec((tm, d), lambda i: (i, 0))
    return pl.pallas_call(
        _mix_kernel,
        grid=(t // tm,),
        in_specs=[pl.BlockSpec((tm, A_WIDTH), lambda i: (i, 0)),
                  pl.BlockSpec((tm, B_WIDTH), lambda i: (i, 0)),
                  pl.BlockSpec((tm, d), lambda i: (i, 0)),
                  pl.BlockSpec((tm, d), lambda i: (i, 1)),
                  row,
                  _const_spec((A_WIDTH, d)), _const_spec((B_WIDTH, d)), _const_spec((d, d)),
                  _const_spec((1, d)), _const_spec((1, d))],
        out_specs=[row, row],
        out_shape=[jax.ShapeDtypeStruct((t, d), F32), jax.ShapeDtypeStruct((t, d), BF16)],
        compiler_params=_params(("arbitrary",)),
        name="mix_out",
    )(ya, yb, gates, gates, h, wa, wb, wm, g.reshape(1, d), b.reshape(1, d))


def _xattn_kernel(hb_ref, h_ref, k_ref, v_ref, wq_ref, wo_ref, g_ref, b_ref, o_ref):
    d = h_ref.shape[1]
    hd = d // X_HEADS
    scale = hd ** -0.5
    q = jnp.dot(hb_ref[...], wq_ref[...], preferred_element_type=F32).astype(BF16)
    outs = []
    for x in range(X_HEADS):
        sl = slice(x * hd, (x + 1) * hd)
        s = lax.dot_general(q[:, sl], k_ref[:, sl], NT_DIMS, preferred_element_type=F32) * scale
        m = jnp.max(s, axis=-1, keepdims=True)
        p = jnp.exp(s - m)
        denom = jnp.sum(p, axis=-1, keepdims=True)
        o = jnp.dot(p.astype(BF16), v_ref[:, sl], preferred_element_type=F32) / denom
        outs.append(o.astype(BF16))
    xo = jnp.dot(jnp.concatenate(outs, axis=1), wo_ref[...], preferred_element_type=F32)
    o_ref[...] = _layer_norm(DEEPNORM_ALPHA * h_ref[...] + xo, g_ref[...], b_ref[...])


def _xattn(hb, h, kv, wq, wo, g, b, batch, seq, mem_len):
    t, d = h.shape
    tm = min(ATT_TILE, seq)
    ns = seq // tm
    row = pl.BlockSpec((tm, d), lambda bi, i: (bi * ns + i, 0))
    return pl.pallas_call(
        _xattn_kernel,
        grid=(batch, ns),
        in_specs=[row, row,
                  pl.BlockSpec((mem_len, d), lambda bi, i: (bi, 0)),
                  pl.BlockSpec((mem_len, d), lambda bi, i: (bi, 1)),
                  _const_spec((d, d)), _const_spec((d, d)),
                  _const_spec((1, d)), _const_spec((1, d))],
        out_specs=row,
        out_shape=jax.ShapeDtypeStruct((t, d), F32),
        compiler_params=_params(("arbitrary", "arbitrary")),
        name="xattn",
    )(hb, h, kv, kv, wq, wo, g.reshape(1, d), b.reshape(1, d))


def _router_kernel(h_ref, whi_ref, wlo_ref, b_ref, ltri_ref, idx_ref, gate_ref, rank_ref,
                   cnt_ref, run_ref):
    @pl.when(pl.program_id(0) == 0)
    def _():
        run_ref[...] = jnp.zeros_like(run_ref)

    h = h_ref[...]
    h_hi = h.astype(BF16)
    h_lo = (h - h_hi.astype(F32)).astype(BF16)
    w_hi = whi_ref[...]
    logits = (jnp.dot(h_hi, w_hi, preferred_element_type=F32)
              + (jnp.dot(h_lo, w_hi, preferred_element_type=F32)
                 + jnp.dot(h_hi, wlo_ref[...], preferred_element_type=F32))
              + b_ref[...])
    tm = logits.shape[0]
    lane_i = lax.broadcasted_iota(jnp.int32, (tm, V7X_LANES), 1)
    lane = lane_i.astype(F32)
    rem = jnp.where(lane_i < N_EXPERTS, logits, -jnp.inf)
    vals, idxs = [], []
    for _ in range(TOP_K):
        m = jnp.max(rem, axis=-1, keepdims=True)
        ik = jnp.min(jnp.where(rem == m, lane, float(V7X_LANES)), axis=-1, keepdims=True)
        vals.append(m)
        idxs.append(ik)
        rem = jnp.where(lane == ik, -jnp.inf, rem)
    exps = [jnp.exp(v - vals[0]) for v in vals]
    denom = exps[0] + exps[1] + exps[2] + exps[3]
    sel = jnp.zeros((tm, V7X_LANES), F32)
    for ik in idxs:
        sel = sel + (lane == ik).astype(F32)
    before = jnp.dot(ltri_ref[...], sel.astype(BF16), preferred_element_type=F32) + run_ref[...]
    idx_out = jnp.zeros((tm, V7X_LANES), jnp.int32)
    gate_out = jnp.zeros((tm, V7X_LANES), F32)
    rank_out = jnp.zeros((tm, V7X_LANES), jnp.int32)
    for k in range(TOP_K):
        rank_k = jnp.sum(jnp.where(lane == idxs[k], before, 0.0), axis=-1, keepdims=True)
        idx_out = jnp.where(lane_i == k, idxs[k].astype(jnp.int32), idx_out)
        gate_out = jnp.where(lane_i == k, exps[k] / denom, gate_out)
        rank_out = jnp.where(lane_i == k, rank_k.astype(jnp.int32), rank_out)
    idx_ref[...] = idx_out
    gate_ref[...] = gate_out
    rank_ref[...] = rank_out
    run_ref[...] = run_ref[...] + jnp.sum(sel, axis=0, keepdims=True)
    cnt_ref[...] = run_ref[...]


def _router(h, w_router, b_router):
    t, d = h.shape
    tm = min(ATT_TILE, t)
    w_pad = jnp.zeros((d, V7X_LANES), F32).at[:, :N_EXPERTS].set(w_router)
    w_hi = w_pad.astype(BF16)
    w_lo = (w_pad - w_hi.astype(F32)).astype(BF16)
    b_pad = jnp.zeros((1, V7X_LANES), F32).at[0, :N_EXPERTS].set(b_router)
    ltri = jnp.asarray(np.tril(np.ones((tm, tm), np.float32), -1), dtype=BF16)
    small = pl.BlockSpec((tm, V7X_LANES), lambda i: (i, 0))
    return pl.pallas_call(
        _router_kernel,
        grid=(t // tm,),
        in_specs=[pl.BlockSpec((tm, d), lambda i: (i, 0)),
                  _const_spec((d, V7X_LANES)), _const_spec((d, V7X_LANES)),
                  _const_spec((1, V7X_LANES)), _const_spec((tm, tm))],
        out_specs=[small, small, small, pl.BlockSpec((1, V7X_LANES), lambda i: (0, 0))],
        out_shape=[jax.ShapeDtypeStruct((t, V7X_LANES), jnp.int32),
                   jax.ShapeDtypeStruct((t, V7X_LANES), F32),
                   jax.ShapeDtypeStruct((t, V7X_LANES), jnp.int32),
                   jax.ShapeDtypeStruct((1, V7X_LANES), F32)],
        scratch_shapes=[pltpu.VMEM((1, V7X_LANES), F32)],
        compiler_params=_params(("arbitrary",)),
        name="router",
    )(h, w_hi, w_lo, b_pad, ltri)


def _gather_rows_kernel(tok_ref, h_hbm, o_ref, buf_ref, sem):
    tm = o_ref.shape[0]
    base = pl.program_id(0) * tm

    def row_copy(r):
        return pltpu.make_async_copy(h_hbm.at[pl.ds(tok_ref[base + r], 1)],
                                     buf_ref.at[pl.ds(r, 1)], sem)

    def start(r, c):
        row_copy(r).start()
        return c

    def wait(r, c):
        row_copy(r).wait()
        return c

    lax.fori_loop(0, tm, start, 0)
    lax.fori_loop(0, tm, wait, 0)
    o_ref[...] = buf_ref[...].astype(o_ref.dtype)


def _gather_rows(h, row_token, n_rows):
    t, d = h.shape
    tm = EXPERT_TILE
    grid_spec = pltpu.PrefetchScalarGridSpec(
        num_scalar_prefetch=1,
        grid=(n_rows // tm,),
        in_specs=[pl.BlockSpec(memory_space=pl.ANY)],
        out_specs=pl.BlockSpec((tm, d), lambda i, tok: (i, 0)),
        scratch_shapes=[pltpu.VMEM((tm, d), F32), pltpu.SemaphoreType.DMA(())],
    )
    return pl.pallas_call(
        _gather_rows_kernel,
        grid_spec=grid_spec,
        out_shape=jax.ShapeDtypeStruct((n_rows, d), BF16),
        compiler_params=_params(("arbitrary",)),
        name="moe_gather",
    )(row_token, h)


def _expert_changed(te_ref, t):
    prev = te_ref[jnp.maximum(t - 1, 0)]
    return (t == 0) | (te_ref[t] != prev)


def _moe_up_kernel(te_ref, tv_ref, x_ref, wg_ref, wl_ref, bg_ref, bl_ref, o_ref,
                   wg_bf, wl_bf):
    t = pl.program_id(1)

    @pl.when(_expert_changed(te_ref, t))
    def _():
        wg_bf[...] = wg_ref[...].astype(BF16)
        wl_bf[...] = wl_ref[...].astype(BF16)

    @pl.when(tv_ref[t] > 0)
    def _():
        x = x_ref[...]
        glu = jnp.dot(x, wg_bf[...], preferred_element_type=F32) + bg_ref[...]
        lin = jnp.dot(x, wl_bf[...], preferred_element_type=F32) + bl_ref[...]
        glu = jnp.minimum(glu, SWIGLU_LIMIT)
        lin = jnp.clip(lin, -SWIGLU_LIMIT, SWIGLU_LIMIT)
        act = glu * jax.nn.sigmoid(SWIGLU_ALPHA * glu) * (lin + 1.0)
        o_ref[...] = act.astype(o_ref.dtype)

    @pl.when(tv_ref[t] == 0)
    def _():
        o_ref[...] = jnp.zeros_like(o_ref)


def _moe_up(xs, w_up, b_up, tile_expert, tile_valid):
    r, d = xs.shape
    e, _, f2 = w_up.shape
    f = f2 // 2
    tm = EXPERT_TILE
    tn = _pick_tile(f, 1024)
    nj = f // tn
    grid_spec = pltpu.PrefetchScalarGridSpec(
        num_scalar_prefetch=2,
        grid=(nj, r // tm),
        in_specs=[
            pl.BlockSpec((tm, d), lambda j, t, te, tv: (t, 0)),
            pl.BlockSpec((None, d, tn), lambda j, t, te, tv: (te[t], 0, j)),
            pl.BlockSpec((None, d, tn), lambda j, t, te, tv: (te[t], 0, nj + j)),
            pl.BlockSpec((None, 1, tn), lambda j, t, te, tv: (te[t], 0, j)),
            pl.BlockSpec((None, 1, tn), lambda j, t, te, tv: (te[t], 0, nj + j)),
        ],
        out_specs=pl.BlockSpec((tm, tn), lambda j, t, te, tv: (t, j)),
        scratch_shapes=[pltpu.VMEM((d, tn), BF16), pltpu.VMEM((d, tn), BF16)],
    )
    b3 = b_up.reshape(e, 1, f2)
    return pl.pallas_call(
        _moe_up_kernel,
        grid_spec=grid_spec,
        out_shape=jax.ShapeDtypeStruct((r, f), BF16),
        compiler_params=_params(("arbitrary", "arbitrary")),
        name="moe_up",
    )(tile_expert, tile_valid, xs, w_up, w_up, b3, b3)


def _moe_down_kernel(te_ref, tv_ref, a_ref, w_ref, b_ref, o_ref, w_bf):
    t = pl.program_id(1)

    @pl.when(_expert_changed(te_ref, t))
    def _():
        w_bf[...] = w_ref[...].astype(BF16)

    @pl.when(tv_ref[t] > 0)
    def _():
        o_ref[...] = jnp.dot(a_ref[...], w_bf[...], preferred_element_type=F32) + b_ref[...]

    @pl.when(tv_ref[t] == 0)
    def _():
        o_ref[...] = jnp.zeros_like(o_ref)


def _moe_down(act, w_down, b_down, tile_expert, tile_valid):
    r, f = act.shape
    e, _, d = w_down.shape
    tm = EXPERT_TILE
    tn = _pick_tile(d, 1024)
    grid_spec = pltpu.PrefetchScalarGridSpec(
        num_scalar_prefetch=2,
        grid=(d // tn, r // tm),
        in_specs=[
            pl.BlockSpec((tm, f), lambda j, t, te, tv: (t, 0)),
            pl.BlockSpec((None, f, tn), lambda j, t, te, tv: (te[t], 0, j)),
            pl.BlockSpec((None, 1, tn), lambda j, t, te, tv: (te[t], 0, j)),
        ],
        out_specs=pl.BlockSpec((tm, tn), lambda j, t, te, tv: (t, j)),
        scratch_shapes=[pltpu.VMEM((f, tn), BF16)],
    )
    return pl.pallas_call(
        _moe_down_kernel,
        grid_spec=grid_spec,
        out_shape=jax.ShapeDtypeStruct((r, d), F32),
        compiler_params=_params(("arbitrary", "arbitrary")),
        name="moe_down",
    )(tile_expert, tile_valid, act, w_down, b_down.reshape(e, 1, d))


def _combine_kernel(pos_ref, y_hbm, gate_ref, h_ref, g_ref, b_ref, o_ref, buf_ref, sem):
    tm = o_ref.shape[0]
    base = pl.program_id(0) * tm

    def row_copy(r, k):
        return pltpu.make_async_copy(y_hbm.at[pl.ds(pos_ref[(base + r) * TOP_K + k], 1)],
                                     buf_ref.at[k, pl.ds(r, 1)], sem)

    def start(r, c):
        for k in range(TOP_K):
            row_copy(r, k).start()
        return c

    def wait(r, c):
        for k in range(TOP_K):
            row_copy(r, k).wait()
        return c

    lax.fori_loop(0, tm, start, 0)
    lax.fori_loop(0, tm, wait, 0)
    gates = gate_ref[...]
    ff = gates[:, 0:1] * buf_ref[0]
    for k in range(1, TOP_K):
        ff = ff + gates[:, k:k + 1] * buf_ref[k]
    o_ref[...] = _layer_norm(DEEPNORM_ALPHA * h_ref[...] + ff, g_ref[...], b_ref[...])


def _combine(y_rows, pos, gates, h, g, b):
    t, d = h.shape
    tm = min(COMBINE_TILE, t)
    grid_spec = pltpu.PrefetchScalarGridSpec(
        num_scalar_prefetch=1,
        grid=(t // tm,),
        in_specs=[pl.BlockSpec(memory_space=pl.ANY),
                  pl.BlockSpec((tm, V7X_LANES), lambda i, p: (i, 0)),
                  pl.BlockSpec((tm, d), lambda i, p: (i, 0)),
                  pl.BlockSpec((1, d), lambda i, p: (0, 0)),
                  pl.BlockSpec((1, d), lambda i, p: (0, 0))],
        out_specs=pl.BlockSpec((tm, d), lambda i, p: (i, 0)),
        scratch_shapes=[pltpu.VMEM((TOP_K, tm, d), F32), pltpu.SemaphoreType.DMA(())],
    )
    return pl.pallas_call(
        _combine_kernel,
        grid_spec=grid_spec,
        out_shape=jax.ShapeDtypeStruct((t, d), F32),
        compiler_params=_params(("arbitrary",)),
        name="moe_combine",
    )(pos, y_rows, gates, h, g.reshape(1, d), b.reshape(1, d))


def _moe(h, w_router, b_router, w_up, b_up, w_down, b_down, g, b):
    t, d = h.shape
    tm = EXPERT_TILE
    idx_p, gate_p, rank_p, cnt_p = _router(h, w_router, b_router)
    idx = idx_p[:, :TOP_K]
    counts = cnt_p[0, :N_EXPERTS].astype(jnp.int32)
    padded = (counts + tm - 1) // tm * tm
    pad_end = jnp.cumsum(padded)
    pad_off = pad_end - padded
    dest = (pad_off[idx] + rank_p[:, :TOP_K]).reshape(t * TOP_K)
    n_rows = t * TOP_K + N_EXPERTS * tm
    n_tiles = n_rows // tm
    tile_start = jnp.arange(n_tiles, dtype=jnp.int32) * tm
    tile_valid = (tile_start < pad_end[-1]).astype(jnp.int32)
    tile_expert = jnp.minimum(jnp.searchsorted(pad_end, tile_start, side='right'),
                              N_EXPERTS - 1).astype(jnp.int32)
    last_expert = tile_expert[jnp.maximum(pad_end[-1] // tm - 1, 0)]
    tile_expert = jnp.where(tile_valid > 0, tile_expert, last_expert)
    row_token = jnp.zeros((n_rows,), jnp.int32).at[dest].set(
        jnp.arange(t * TOP_K, dtype=jnp.int32) // TOP_K)

    xs = _gather_rows(h, row_token, n_rows)
    act = _moe_up(xs, w_up, b_up, tile_expert, tile_valid)
    y_rows = _moe_down(act, w_down, b_down, tile_expert, tile_valid)
    return _combine(y_rows, dest, gate_p, h, g, b)


def kernel(x, mem, ln_in_g, ln_in_b, rel_bias, w_in, b_in, attn_sinks, w_a_out, w_b_out,
           w_mix_out, ln1_g, ln1_b, w_xq, w_xkv, w_xo, ln2_g, ln2_b, w_router, b_router,
           w_up, b_up, w_down, b_down, ln3_g, ln3_b):
    batch, seq, d = x.shape
    mem_len = mem.shape[1]
    t = batch * seq
    a_cols = A_WIDTH + 2 * A_KV_WIDTH
    b_cols = 3 * B_WIDTH

    h, hb = _ln_in(x.reshape(t, d), ln_in_g, ln_in_b)
    for l in range(DEPTH):
        w_in_b = w_in[l].astype(BF16)
        qkv_a = _mm_bias(hb, w_in_b[:, :a_cols], b_in[l, :a_cols], BF16, "in_proj_a")
        qkv_b = _mm_bias(hb, w_in_b[:, a_cols:a_cols + b_cols],
                         b_in[l, a_cols:a_cols + b_cols], BF16, "in_proj_b")
        gates = _mm_bias(hb, w_in_b[:, a_cols + b_cols:], b_in[l, a_cols + b_cols:], F32,
                         "in_proj_gates")
        ya = _swa(qkv_a, attn_sinks[l], rel_bias, batch, seq)
        yb = _stick_breaking(qkv_b, batch, seq)
        h, hb = _mix(ya, yb, gates, h, w_a_out[l].astype(BF16), w_b_out[l].astype(BF16),
                     w_mix_out[l].astype(BF16), ln1_g[l], ln1_b[l])
        kv = _mm_bias(mem.reshape(batch * mem_len, d).astype(BF16), w_xkv[l].astype(BF16),
                      jnp.zeros((2 * d,), F32), BF16, "mem_kv")
        h = _xattn(hb, h, kv, w_xq[l].astype(BF16), w_xo[l].astype(BF16), ln2_g[l], ln2_b[l],
                   batch, seq, mem_len)
        h = _moe(h, w_router[l], b_router[l], w_up[l], b_up[l], w_down[l], b_down[l],
                 ln3_g[l], ln3_b[l])
        if l + 1 < DEPTH:
            hb = h.astype(BF16)
    return h.reshape(batch, seq, d)
```

```python
import functools

import jax
import jax.numpy as jnp
import numpy as np
from jax import lax
from jax.experimental import pallas as pl
from jax.experimental.pallas import tpu as pltpu

DEPTH = 1
CHUNK = 64
BLK = 128
HEAD_DIM = 64
A_HEADS = 16
A_KV_HEADS = 2
A_REP = A_HEADS // A_KV_HEADS
WIN_CHUNKS = 2
B_HEADS = 16
A_WIDTH = A_HEADS * HEAD_DIM
A_KV_WIDTH = A_KV_HEADS * HEAD_DIM
B_WIDTH = B_HEADS * HEAD_DIM
REL_BUCKETS = 32
REL_MAX_DIST = 128
X_HEADS = 4
N_EXPERTS = 32
TOP_K = 4
SWIGLU_LIMIT = 7.0
SWIGLU_ALPHA = 1.702
LN_EPS = 1e-5
DEEPNORM_ALPHA = (2.0 * DEPTH) ** 0.25
QK_SCALE = HEAD_DIM ** -0.5

V7X_LANES = 128
V7X_VMEM_LIMIT_BYTES = 56 * 1024 * 1024

ROW_TILE = 512
ATT_TILE = 256
SB_TILE = 256
EXPERT_TILE = 256
COMBINE_TILE = 128
SB_LOG_ZERO = -105.0

F32 = jnp.float32
BF16 = jnp.bfloat16
NT_DIMS = (((1,), (1,)), ((), ()))


def _params(semantics):
    return pltpu.CompilerParams(dimension_semantics=semantics,
                                vmem_limit_bytes=V7X_VMEM_LIMIT_BYTES)


def _const_spec(shape):
    nd = len(shape)
    return pl.BlockSpec(shape, lambda *_: (0,) * nd, pipeline_mode=pl.Buffered(1))


def _layer_norm(x, g, b):
    mu = jnp.mean(x, axis=-1, keepdims=True)
    xc = x - mu
    var = jnp.mean(xc * xc, axis=-1, keepdims=True)
    return xc * lax.rsqrt(var + LN_EPS) * g + b


def _ln_in_kernel(x_ref, g_ref, b_ref, h_ref, hb_ref):
    y = _layer_norm(x_ref[...], g_ref[...], b_ref[...])
    h_ref[...] = y
    hb_ref[...] = y.astype(BF16)


def _ln_in(x2d, g, b):
    t, d = x2d.shape
    tm = min(ROW_TILE, t)
    row = pl.BlockSpec((tm, d), lambda i: (i, 0))
    return pl.pallas_call(
        _ln_in_kernel,
        grid=(t // tm,),
        in_specs=[row, _const_spec((1, d)), _const_spec((1, d))],
        out_specs=[row, row],
        out_shape=[jax.ShapeDtypeStruct((t, d), F32), jax.ShapeDtypeStruct((t, d), BF16)],
        compiler_params=_params(("arbitrary",)),
        name="ln_in",
    )(x2d, g.reshape(1, d), b.reshape(1, d))


def _mm_bias_kernel(a_ref, w_ref, b_ref, o_ref):
    acc = jnp.dot(a_ref[...], w_ref[...], preferred_element_type=F32)
    o_ref[...] = (acc + b_ref[...]).astype(o_ref.dtype)


def _pick_tile(n, target):
    best = None
    for c in range(V7X_LANES, min(n, target) + 1, V7X_LANES):
        if n % c == 0:
            best = c
    return n if best is None else best


def _mm_bias(a, w, b, out_dtype, name):
    m, k = a.shape
    n = w.shape[1]
    tm = min(ROW_TILE, m)
    tn = _pick_tile(n, 1024)
    return pl.pallas_call(
        _mm_bias_kernel,
        grid=(n // tn, m // tm),
        in_specs=[pl.BlockSpec((tm, k), lambda j, i: (i, 0)),
                  pl.BlockSpec((k, tn), lambda j, i: (0, j)),
                  pl.BlockSpec((1, tn), lambda j, i: (0, j))],
        out_specs=pl.BlockSpec((tm, tn), lambda j, i: (i, j)),
        out_shape=jax.ShapeDtypeStruct((m, n), out_dtype),
        compiler_params=_params(("arbitrary", "arbitrary")),
        name=name,
    )(a, w, b.reshape(1, n).astype(F32))


def _swa_kernel(sink_ref, q_ref, kp_ref, kc_ref, vp_ref, vc_ref, bias_ref, o_ref):
    n = pl.program_id(1)
    kk = jnp.concatenate([kp_ref[...], kc_ref[...]], axis=0)
    vv = jnp.concatenate([vp_ref[...], vc_ref[...]], axis=0)
    col = lax.broadcasted_iota(jnp.int32, (BLK, 2 * BLK), 1)
    exists = (col >= BLK) | (n > 0)
    for h in range(A_HEADS):
        g = h // A_REP
        qh = q_ref[:, h * HEAD_DIM:(h + 1) * HEAD_DIM] * QK_SCALE
        kg = kk[:, g * HEAD_DIM:(g + 1) * HEAD_DIM]
        vg = vv[:, g * HEAD_DIM:(g + 1) * HEAD_DIM]
        s = lax.dot_general(qh, kg, NT_DIMS, preferred_element_type=F32)
        s = jnp.where(exists, s + bias_ref[h], -jnp.inf)
        sink = sink_ref[h]
        m = jnp.maximum(jnp.max(s, axis=-1, keepdims=True), sink)
        p = jnp.exp(s - m)
        denom = jnp.sum(p, axis=-1, keepdims=True) + jnp.exp(sink - m)
        o = jnp.dot(p.astype(BF16), vg, preferred_element_type=F32) / denom
        o_ref[:, h * HEAD_DIM:(h + 1) * HEAD_DIM] = o.astype(o_ref.dtype)


def _rel_bucket(rel):
    half = REL_BUCKETS // 2
    max_exact = half // 2
    base = jnp.where(rel > 0, half, 0)
    n = jnp.abs(rel)
    nf = jnp.maximum(n, 1).astype(F32)
    large = max_exact + (jnp.log(nf / max_exact) / np.log(REL_MAX_DIST / max_exact)
                         * (half - max_exact)).astype(jnp.int32)
    large = jnp.minimum(large, half - 1)
    return base + jnp.where(n < max_exact, n, large)


def _swa_bias_table(rel_bias):
    qi = jnp.arange(BLK)[:, None]
    kj = jnp.arange(2 * BLK)[None, :]
    bias = jnp.transpose(rel_bias[_rel_bucket(kj - BLK - qi)], (2, 0, 1)).astype(F32)
    dchunk = (kj // CHUNK - BLK // CHUNK) - qi // CHUNK
    band = (dchunk <= 0) & (dchunk >= -WIN_CHUNKS)
    return jnp.where(band[None], bias, -jnp.inf)


def _swa(qkv, sinks, rel_bias, batch, seq):
    nb = seq // BLK
    kcol = A_WIDTH // A_KV_WIDTH
    vcol = kcol + 1
    kv_blk = (BLK, A_KV_WIDTH)
    grid_spec = pltpu.PrefetchScalarGridSpec(
        num_scalar_prefetch=1,
        grid=(batch, nb),
        in_specs=[
            pl.BlockSpec((BLK, A_WIDTH), lambda b, n, s: (b * nb + n, 0)),
            pl.BlockSpec(kv_blk, lambda b, n, s: (b * nb + jnp.maximum(n - 1, 0), kcol)),
            pl.BlockSpec(kv_blk, lambda b, n, s: (b * nb + n, kcol)),
            pl.BlockSpec(kv_blk, lambda b, n, s: (b * nb + jnp.maximum(n - 1, 0), vcol)),
            pl.BlockSpec(kv_blk, lambda b, n, s: (b * nb + n, vcol)),
            pl.BlockSpec((A_HEADS, BLK, 2 * BLK), lambda b, n, s: (0, 0, 0),
                         pipeline_mode=pl.Buffered(1)),
        ],
        out_specs=pl.BlockSpec((BLK, A_WIDTH), lambda b, n, s: (b * nb + n, 0)),
    )
    return pl.pallas_call(
        _swa_kernel,
        grid_spec=grid_spec,
        out_shape=jax.ShapeDtypeStruct((batch * seq, A_WIDTH), BF16),
        compiler_params=_params(("arbitrary", "arbitrary")),
        name="swa_attn",
    )(sinks.astype(F32), qkv, qkv, qkv, qkv, qkv, _swa_bias_table(rel_bias))


def _sb_kernel(q_ref, k_ref, v_ref, tri_ref, o_ref):
    tq = q_ref.shape[0]
    i = pl.program_id(2)
    q2 = q_ref[...] * QK_SCALE
    lane = lax.broadcasted_iota(jnp.int32, (1, 2 * HEAD_DIM), 1)
    first = lane < HEAD_DIM
    zero = jnp.zeros((), BF16)
    q_heads = (jnp.where(first, q2, zero), jnp.where(first, zero, q2))
    row = lax.broadcasted_iota(jnp.int32, (tq, tq), 0)
    col = lax.broadcasted_iota(jnp.int32, (tq, tq), 1)
    tri = tri_ref[...]

    def cond(state):
        j, live = state[0], state[1]
        return (j >= 0) & (live > 0)

    def body(state):
        j, _, ca, cb, acc = state
        start = pl.multiple_of(j * tq, tq)
        k2 = k_ref[pl.ds(start, tq), :]
        v2 = v_ref[pl.ds(start, tq), :]
        causal = (col + (j - i) * tq) < row
        carries = [ca, cb]
        for h in range(2):
            z = lax.dot_general(q_heads[h], k2, NT_DIMS, preferred_element_type=F32)
            sp = jnp.log(1.0 + jnp.exp(-jnp.abs(z)))
            log_keep = jnp.where(causal, jnp.minimum(-z, 0.0) - sp, 0.0)
            hi = log_keep.astype(BF16)
            lo = (log_keep - hi.astype(F32)).astype(BF16)
            suffix = jnp.dot(jnp.concatenate([hi, lo], axis=1), tri, preferred_element_type=F32)
            log_w = z + suffix + carries[h]
            w = jnp.where(causal, jnp.exp(log_w), 0.0)
            vh = jnp.where(first if h == 0 else ~first, v2, zero)
            acc = acc + jnp.dot(w.astype(BF16), vh, preferred_element_type=F32)
            carries[h] = carries[h] + suffix[:, 0:1]
        live = (jnp.max(jnp.maximum(carries[0], carries[1])) > SB_LOG_ZERO).astype(jnp.int32)
        return j - 1, live, carries[0], carries[1], acc

    init = (i, jnp.int32(1), jnp.zeros((tq, 1), F32), jnp.zeros((tq, 1), F32),
            jnp.zeros((tq, 2 * HEAD_DIM), F32))
    acc = lax.while_loop(cond, body, init)[4]
    o_ref[...] = acc.astype(o_ref.dtype)


def _stick_breaking(qkv, batch, seq):
    tq = min(SB_TILE, seq)
    nq = seq // tq
    pairs = B_WIDTH // (2 * HEAD_DIM)
    j = np.arange(2 * tq)[:, None] % tq
    s = np.arange(tq)[None, :]
    tri = jnp.asarray((j >= s).astype(np.float32), dtype=BF16)
    return pl.pallas_call(
        _sb_kernel,
        grid=(batch, pairs, nq),
        in_specs=[
            pl.BlockSpec((tq, 2 * HEAD_DIM), lambda b, p, i: (b * nq + i, p)),
            pl.BlockSpec((seq, 2 * HEAD_DIM), lambda b, p, i: (b, pairs + p)),
            pl.BlockSpec((seq, 2 * HEAD_DIM), lambda b, p, i: (b, 2 * pairs + p)),
            _const_spec((2 * tq, tq)),
        ],
        out_specs=pl.BlockSpec((tq, 2 * HEAD_DIM), lambda b, p, i: (b * nq + i, p)),
        out_shape=jax.ShapeDtypeStruct((batch * seq, B_WIDTH), BF16),
        compiler_params=_params(("arbitrary", "arbitrary", "arbitrary")),
        name="stick_breaking",
    )(qkv, qkv, qkv, tri)


def _mix_kernel(ya_ref, yb_ref, gates_a_ref, gates_b_ref, h_ref, wa_ref, wb_ref, wm_ref,
                g_ref, b_ref, o_ref, ob_ref):
    pa = jnp.dot(ya_ref[...], wa_ref[...], preferred_element_type=F32)
    pb = jnp.dot(yb_ref[...], wb_ref[...], preferred_element_type=F32)
    merged = jax.nn.sigmoid(gates_a_ref[...]) * pa + jax.nn.sigmoid(gates_b_ref[...]) * pb
    mixed = jnp.dot(merged.astype(BF16), wm_ref[...], preferred_element_type=F32)
    y = _layer_norm(DEEPNORM_ALPHA * h_ref[...] + mixed, g_ref[...], b_ref[...])
    o_ref[...] = y
    ob_ref[...] = y.astype(BF16)


def _mix(ya, yb, gates, h, wa, wb, wm, g, b):
    t, d = h.shape
    tm = min(ATT_TILE, t)
    row = pl.BlockSpec((tm, d), lambda i: (i, 0))
    return pl.pallas_call(
        _mix_kernel,
        grid=(t // tm,),
        in_specs=[pl.BlockSpec((tm, A_WIDTH), lambda i: (i, 0)),
                  pl.BlockSpec((tm, B_WIDTH), lambda i: (i, 0)),
                  pl.BlockSpec((tm, d), lambda i: (i, 0)),
                  pl.BlockSpec((tm, d), lambda i: (i, 1)),
                  row,
                  _const_spec((A_WIDTH, d)), _const_spec((B_WIDTH, d)), _const_spec((d, d)),
                  _const_spec((1, d)), _const_spec((1, d))],
        out_specs=[row, row],
        out_shape=[jax.ShapeDtypeStruct((t, d), F32), jax.ShapeDtypeStruct((t, d), BF16)],
        compiler_params=_params(("arbitrary",)),
        name="mix_out",
    )(ya, yb, gates, gates, h, wa, wb, wm, g.reshape(1, d), b.reshape(1, d))


def _xattn_kernel(hb_ref, h_ref, k_ref, v_ref, wq_ref, wo_ref, g_ref, b_ref, o_ref):
    d = h_ref.shape[1]
    hd = d // X_HEADS
    scale = hd ** -0.5
    q = jnp.dot(hb_ref[...], wq_ref[...], preferred_element_type=F32).astype(BF16)
    outs = []
    for x in range(X_HEADS):
        sl = slice(x * hd, (x + 1) * hd)
        s = lax.dot_general(q[:, sl], k_ref[:, sl], NT_DIMS, preferred_element_type=F32) * scale
        m = jnp.max(s, axis=-1, keepdims=True)
        p = jnp.exp(s - m)
        denom = jnp.sum(p, axis=-1, keepdims=True)
        o = jnp.dot(p.astype(BF16), v_ref[:, sl], preferred_element_type=F32) / denom
        outs.append(o.astype(BF16))
    xo = jnp.dot(jnp.concatenate(outs, axis=1), wo_ref[...], preferred_element_type=F32)
    o_ref[...] = _layer_norm(DEEPNORM_ALPHA * h_ref[...] + xo, g_ref[...], b_ref[...])


def _xattn(hb, h, kv, wq, wo, g, b, batch, seq, mem_len):
    t, d = h.shape
    tm = min(ATT_TILE, seq)
    ns = seq // tm
    row = pl.BlockSpec((tm, d), lambda bi, i: (bi * ns + i, 0))
    return pl.pallas_call(
        _xattn_kernel,
        grid=(batch, ns),
        in_specs=[row, row,
                  pl.BlockSpec((mem_len, d), lambda bi, i: (bi, 0)),
                  pl.BlockSpec((mem_len, d), lambda bi, i: (bi, 1)),
                  _const_spec((d, d)), _const_spec((d, d)),
                  _const_spec((1, d)), _const_spec((1, d))],
        out_specs=row,
        out_shape=jax.ShapeDtypeStruct((t, d), F32),
        compiler_params=_params(("arbitrary", "arbitrary")),
        name="xattn",
    )(hb, h, kv, kv, wq, wo, g.reshape(1, d), b.reshape(1, d))


def _router_kernel(h_ref, whi_ref, wlo_ref, b_ref, ltri_ref, idx_ref, gate_ref, rank_ref,
                   cnt_ref, run_ref):
    @pl.when(pl.program_id(0) == 0)
    def _():
        run_ref[...] = jnp.zeros_like(run_ref)

    h = h_ref[...]
    h_hi = h.astype(BF16)
    h_lo = (h - h_hi.astype(F32)).astype(BF16)
    w_hi = whi_ref[...]
    logits = (jnp.dot(h_hi, w_hi, preferred_element_type=F32)
              + (jnp.dot(h_lo, w_hi, preferred_element_type=F32)
                 + jnp.dot(h_hi, wlo_ref[...], preferred_element_type=F32))
              + b_ref[...])
    tm = logits.shape[0]
    lane_i = lax.broadcasted_iota(jnp.int32, (tm, V7X_LANES), 1)
    lane = lane_i.astype(F32)
    rem = jnp.where(lane_i < N_EXPERTS, logits, -jnp.inf)
    vals, idxs = [], []
    for _ in range(TOP_K):
        m = jnp.max(rem, axis=-1, keepdims=True)
        ik = jnp.min(jnp.where(rem == m, lane, float(V7X_LANES)), axis=-1, keepdims=True)
        vals.append(m)
        idxs.append(ik)
        rem = jnp.where(lane == ik, -jnp.inf, rem)
    exps = [jnp.exp(v - vals[0]) for v in vals]
    denom = exps[0] + exps[1] + exps[2] + exps[3]
    sel = jnp.zeros((tm, V7X_LANES), F32)
    for ik in idxs:
        sel = sel + (lane == ik).astype(F32)
    before = jnp.dot(ltri_ref[...], sel.astype(BF16), preferred_element_type=F32) + run_ref[...]
    idx_out = jnp.zeros((tm, V7X_LANES), jnp.int32)
    gate_out = jnp.zeros((tm, V7X_LANES), F32)
    rank_out = jnp.zeros((tm, V7X_LANES), jnp.int32)
    for k in range(TOP_K):
        rank_k = jnp.sum(jnp.where(lane == idxs[k], before, 0.0), axis=-1, keepdims=True)
        idx_out = jnp.where(lane_i == k, idxs[k].astype(jnp.int32), idx_out)
        gate_out = jnp.where(lane_i == k, exps[k] / denom, gate_out)
        rank_out = jnp.where(lane_i == k, rank_k.astype(jnp.int32), rank_out)
    idx_ref[...] = idx_out
    gate_ref[...] = gate_out
    rank_ref[...] = rank_out
    run_ref[...] = run_ref[...] + jnp.sum(sel, axis=0, keepdims=True)
    cnt_ref[...] = run_ref[...]


def _router(h, w_router, b_router):
    t, d = h.shape
    tm = min(ATT_TILE, t)
    w_pad = jnp.zeros((d, V7X_LANES), F32).at[:, :N_EXPERTS].set(w_router)
    w_hi = w_pad.astype(BF16)
    w_lo = (w_pad - w_hi.astype(F32)).astype(BF16)
    b_pad = jnp.zeros((1, V7X_LANES), F32).at[0, :N_EXPERTS].set(b_router)
    ltri = jnp.asarray(np.tril(np.ones((tm, tm), np.float32), -1), dtype=BF16)
    small = pl.BlockSpec((tm, V7X_LANES), lambda i: (i, 0))
    return pl.pallas_call(
        _router_kernel,
        grid=(t // tm,),
        in_specs=[pl.BlockSpec((tm, d), lambda i: (i, 0)),
                  _const_spec((d, V7X_LANES)), _const_spec((d, V7X_LANES)),
                  _const_spec((1, V7X_LANES)), _const_spec((tm, tm))],
        out_specs=[small, small, small, pl.BlockSpec((1, V7X_LANES), lambda i: (0, 0))],
        out_shape=[jax.ShapeDtypeStruct((t, V7X_LANES), jnp.int32),
                   jax.ShapeDtypeStruct((t, V7X_LANES), F32),
                   jax.ShapeDtypeStruct((t, V7X_LANES), jnp.int32),
                   jax.ShapeDtypeStruct((1, V7X_LANES), F32)],
        scratch_shapes=[pltpu.VMEM((1, V7X_LANES), F32)],
        compiler_params=_params(("arbitrary",)),
        name="router",
    )(h, w_hi, w_lo, b_pad, ltri)


DMA_UNROLL = 8
(META_EXPERT, META_VALID, META_FIRST, META_LAST_GROUP, META_NEXT_EXPERT, META_GROUP,
 META_ROWS_USED, META_NUM_GROUPS) = range(8)


def _dispatch_kernel(dest_ref, meta_ref, h_ref, xs_hbm, zero_ref, sem, zsem):
    tm = h_ref.shape[0]
    base = pl.program_id(0) * tm
    et = zero_ref.shape[0]

    @pl.when(pl.program_id(0) == 0)
    def _():
        zero_ref[...] = jnp.zeros_like(zero_ref)

        def fill_copy(tile):
            start_row = pl.multiple_of(tile * et, et)
            return pltpu.make_async_copy(zero_ref, xs_hbm.at[pl.ds(start_row, et)], zsem)

        def fill_start(tile, c):
            @pl.when(meta_ref[META_ROWS_USED, tile] < et)
            def _():
                fill_copy(tile).start()
            return c

        def fill_wait(tile, c):
            @pl.when(meta_ref[META_ROWS_USED, tile] < et)
            def _():
                fill_copy(tile).wait()
            return c

        n_tiles = xs_hbm.shape[0] // et
        lax.fori_loop(0, n_tiles, fill_start, 0)
        lax.fori_loop(0, n_tiles, fill_wait, 0)

    def row_copy(r, k):
        return pltpu.make_async_copy(h_ref.at[pl.ds(r, 1)],
                                     xs_hbm.at[pl.ds(dest_ref[(base + r) * TOP_K + k], 1)], sem)

    def start(r, c):
        for k in range(TOP_K):
            row_copy(r, k).start()
        return c

    def wait(r, c):
        for k in range(TOP_K):
            row_copy(r, k).wait()
        return c

    lax.fori_loop(0, tm, start, 0, unroll=DMA_UNROLL)
    lax.fori_loop(0, tm, wait, 0, unroll=DMA_UNROLL)


def _dispatch(h, dest, meta, n_rows):
    t, d = h.shape
    tm = min(ATT_TILE, t)
    grid_spec = pltpu.PrefetchScalarGridSpec(
        num_scalar_prefetch=2,
        grid=(t // tm,),
        in_specs=[pl.BlockSpec((tm, d), lambda i, dst, m: (i, 0))],
        out_specs=pl.BlockSpec(memory_space=pl.ANY),
        scratch_shapes=[pltpu.VMEM((EXPERT_TILE, d), F32), pltpu.SemaphoreType.DMA(()),
                        pltpu.SemaphoreType.DMA(())],
    )
    return pl.pallas_call(
        _dispatch_kernel,
        grid_spec=grid_spec,
        out_shape=jax.ShapeDtypeStruct((n_rows, d), F32),
        compiler_params=_params(("arbitrary",)),
        name="moe_dispatch",
    )(dest, meta, h)


def _stream_expert_weights(meta_ref, j, t, nj, make_copies, on_arrival):
    @pl.when((j == 0) & (t == 0))
    def _():
        for c in make_copies(meta_ref[META_EXPERT, 0], 0, 0):
            c.start()

    @pl.when(meta_ref[META_FIRST, t] == 1)
    def _():
        slot = (j * meta_ref[META_NUM_GROUPS, t] + meta_ref[META_GROUP, t]) & 1
        for c in make_copies(meta_ref[META_EXPERT, t], j, slot):
            c.wait()
        last_group = meta_ref[META_LAST_GROUP, t] == 1

        @pl.when(jnp.logical_not(last_group & (j == nj - 1)))
        def _():
            next_chunk = jnp.where(last_group, j + 1, j)
            for c in make_copies(meta_ref[META_NEXT_EXPERT, t], next_chunk, 1 - slot):
                c.start()

        on_arrival(slot)


def _moe_up_kernel(meta_ref, x_ref, w_hbm, bg_ref, bl_ref, o_ref, wf32, wbf, sems, *, tn, nj):
    j = pl.program_id(0)
    t = pl.program_id(1)

    def make_copies(expert, chunk, slot):
        glu_col = pl.multiple_of(chunk * tn, tn)
        lin_col = pl.multiple_of((nj + chunk) * tn, tn)
        return (pltpu.make_async_copy(w_hbm.at[expert, :, pl.ds(glu_col, tn)],
                                      wf32.at[slot, 0], sems.at[slot, 0]),
                pltpu.make_async_copy(w_hbm.at[expert, :, pl.ds(lin_col, tn)],
                                      wf32.at[slot, 1], sems.at[slot, 1]))

    def on_arrival(slot):
        wbf[0] = wf32[slot, 0].astype(BF16)
        wbf[1] = wf32[slot, 1].astype(BF16)

    _stream_expert_weights(meta_ref, j, t, nj, make_copies, on_arrival)

    @pl.when(meta_ref[META_VALID, t] > 0)
    def _():
        x = x_ref[...].astype(BF16)
        glu = jnp.dot(x, wbf[0], preferred_element_type=F32) + bg_ref[...]
        lin = jnp.dot(x, wbf[1], preferred_element_type=F32) + bl_ref[...]
        glu = jnp.minimum(glu, SWIGLU_LIMIT)
        lin = jnp.clip(lin, -SWIGLU_LIMIT, SWIGLU_LIMIT)
        act = glu * jax.nn.sigmoid(SWIGLU_ALPHA * glu) * (lin + 1.0)
        o_ref[...] = act.astype(o_ref.dtype)

    @pl.when(meta_ref[META_VALID, t] == 0)
    def _():
        o_ref[...] = jnp.zeros_like(o_ref)


def _moe_up(xs, w_up, b_up, meta):
    r, d = xs.shape
    e, _, f2 = w_up.shape
    f = f2 // 2
    tm = EXPERT_TILE
    tn = _pick_tile(f, 1024)
    nj = f // tn
    grid_spec = pltpu.PrefetchScalarGridSpec(
        num_scalar_prefetch=1,
        grid=(nj, r // tm),
        in_specs=[
            pl.BlockSpec((tm, d), lambda j, t, m: (t, 0)),
            pl.BlockSpec(memory_space=pl.ANY),
            pl.BlockSpec((None, 1, tn), lambda j, t, m: (m[META_EXPERT, t], 0, j)),
            pl.BlockSpec((None, 1, tn), lambda j, t, m: (m[META_EXPERT, t], 0, nj + j)),
        ],
        out_specs=pl.BlockSpec((tm, tn), lambda j, t, m: (t, j)),
        scratch_shapes=[pltpu.VMEM((2, 2, d, tn), F32), pltpu.VMEM((2, d, tn), BF16),
                        pltpu.SemaphoreType.DMA((2, 2))],
    )
    b3 = b_up.reshape(e, 1, f2)
    return pl.pallas_call(
        functools.partial(_moe_up_kernel, tn=tn, nj=nj),
        grid_spec=grid_spec,
        out_shape=jax.ShapeDtypeStruct((r, f), BF16),
        compiler_params=_params(("arbitrary", "arbitrary")),
        name="moe_up",
    )(meta, xs, w_up, b3, b3)


def _moe_down_kernel(meta_ref, a_ref, w_hbm, b_ref, o_ref, wf32, wbf, sems, *, tn, nj):
    j = pl.program_id(0)
    t = pl.program_id(1)

    def make_copies(expert, chunk, slot):
        col = pl.multiple_of(chunk * tn, tn)
        return (pltpu.make_async_copy(w_hbm.at[expert, :, pl.ds(col, tn)],
                                      wf32.at[slot], sems.at[slot]),)

    def on_arrival(slot):
        wbf[...] = wf32[slot].astype(BF16)

    _stream_expert_weights(meta_ref, j, t, nj, make_copies, on_arrival)

    @pl.when(meta_ref[META_VALID, t] > 0)
    def _():
        o_ref[...] = jnp.dot(a_ref[...], wbf[...], preferred_element_type=F32) + b_ref[...]

    @pl.when(meta_ref[META_VALID, t] == 0)
    def _():
        o_ref[...] = jnp.zeros_like(o_ref)


def _moe_down(act, w_down, b_down, meta):
    r, f = act.shape
    e, _, d = w_down.shape
    tm = EXPERT_TILE
    tn = _pick_tile(d, 2048)
    nj = d // tn
    grid_spec = pltpu.PrefetchScalarGridSpec(
        num_scalar_prefetch=1,
        grid=(nj, r // tm),
        in_specs=[
            pl.BlockSpec((tm, f), lambda j, t, m: (t, 0)),
            pl.BlockSpec(memory_space=pl.ANY),
            pl.BlockSpec((None, 1, tn), lambda j, t, m: (m[META_EXPERT, t], 0, j)),
        ],
        out_specs=pl.BlockSpec((tm, tn), lambda j, t, m: (t, j)),
        scratch_shapes=[pltpu.VMEM((2, f, tn), F32), pltpu.VMEM((f, tn), BF16),
                        pltpu.SemaphoreType.DMA((2,))],
    )
    return pl.pallas_call(
        functools.partial(_moe_down_kernel, tn=tn, nj=nj),
        grid_spec=grid_spec,
        out_shape=jax.ShapeDtypeStruct((r, d), F32),
        compiler_params=_params(("arbitrary", "arbitrary")),
        name="moe_down",
    )(meta, act, w_down, b_down.reshape(e, 1, d))


def _combine_kernel(pos_ref, y_hbm, gate_ref, h_ref, g_ref, b_ref, o_ref, buf_ref, sem):
    tm = o_ref.shape[0]
    base = pl.program_id(0) * tm

    def row_copy(r, k):
        return pltpu.make_async_copy(y_hbm.at[pl.ds(pos_ref[(base + r) * TOP_K + k], 1)],
                                     buf_ref.at[k, pl.ds(r, 1)], sem)

    def start(r, c):
        for k in range(TOP_K):
            row_copy(r, k).start()
        return c

    def wait(r, c):
        for k in range(TOP_K):
            row_copy(r, k).wait()
        return c

    lax.fori_loop(0, tm, start, 0, unroll=DMA_UNROLL)
    lax.fori_loop(0, tm, wait, 0, unroll=DMA_UNROLL)
    gates = gate_ref[...]
    ff = gates[:, 0:1] * buf_ref[0]
    for k in range(1, TOP_K):
        ff = ff + gates[:, k:k + 1] * buf_ref[k]
    o_ref[...] = _layer_norm(DEEPNORM_ALPHA * h_ref[...] + ff, g_ref[...], b_ref[...])


def _combine(y_rows, pos, gates, h, g, b):
    t, d = h.shape
    tm = min(COMBINE_TILE, t)
    grid_spec = pltpu.PrefetchScalarGridSpec(
        num_scalar_prefetch=1,
        grid=(t // tm,),
        in_specs=[pl.BlockSpec(memory_space=pl.ANY),
                  pl.BlockSpec((tm, V7X_LANES), lambda i, p: (i, 0)),
                  pl.BlockSpec((tm, d), lambda i, p: (i, 0)),
                  pl.BlockSpec((1, d), lambda i, p: (0, 0)),
                  pl.BlockSpec((1, d), lambda i, p: (0, 0))],
        out_specs=pl.BlockSpec((tm, d), lambda i, p: (i, 0)),
        scratch_shapes=[pltpu.VMEM((TOP_K, tm, d), F32), pltpu.SemaphoreType.DMA(())],
    )
    return pl.pallas_call(
        _combine_kernel,
        grid_spec=grid_spec,
        out_shape=jax.ShapeDtypeStruct((t, d), F32),
        compiler_params=_params(("arbitrary",)),
        name="moe_combine",
    )(pos, y_rows, gates, h, g.reshape(1, d), b.reshape(1, d))


def _expert_layout(idx, rank, counts, n_tiles, tm):
    experts = jnp.arange(N_EXPERTS, dtype=jnp.int32)
    tiles_per = (counts + tm - 1) // tm
    tile_end = jnp.cumsum(tiles_per).astype(jnp.int32)
    tile_off = tile_end - tiles_per
    dest = jnp.sum(jnp.where(idx[..., None] == experts, tile_off * tm, 0), axis=-1) + rank

    nonempty = counts > 0
    first_e = jnp.min(jnp.where(nonempty, experts, N_EXPERTS))
    last_e = jnp.max(jnp.where(nonempty, experts, 0))
    group_of = jnp.cumsum(nonempty.astype(jnp.int32)) - 1
    later = (experts[None, :] > experts[:, None]) & nonempty[None, :]
    next_e = jnp.min(jnp.where(later, experts[None, :], N_EXPERTS), axis=1)
    next_e = jnp.where(next_e == N_EXPERTS, first_e, next_e)

    tid = jnp.arange(n_tiles, dtype=jnp.int32)
    valid = tid < tile_end[-1]
    te = jnp.sum((tid[:, None] >= tile_end[None, :]).astype(jnp.int32), axis=1)
    te = jnp.where(valid, jnp.minimum(te, N_EXPERTS - 1), last_e)
    onehot = te[:, None] == experts[None, :]

    def lookup(table):
        return jnp.sum(jnp.where(onehot, table[None, :], 0), axis=1)

    local = tid - lookup(tile_off)
    rows_used = jnp.where(valid, jnp.clip(lookup(counts) - local * tm, 0, tm), 0)
    meta = jnp.stack([
        te,
        valid.astype(jnp.int32),
        (valid & (local == 0)).astype(jnp.int32),
        (te == last_e).astype(jnp.int32),
        lookup(next_e),
        lookup(group_of),
        rows_used,
        jnp.broadcast_to(jnp.sum(nonempty.astype(jnp.int32)), (n_tiles,)),
    ]).astype(jnp.int32)
    return dest.reshape(-1).astype(jnp.int32), meta


def _moe(h, w_router, b_router, w_up, b_up, w_down, b_down, g, b):
    t, d = h.shape
    tm = EXPERT_TILE
    idx_p, gate_p, rank_p, cnt_p = _router(h, w_router, b_router)
    n_rows = t * TOP_K + N_EXPERTS * tm
    dest, meta = _expert_layout(idx_p[:, :TOP_K], rank_p[:, :TOP_K],
                                cnt_p[0, :N_EXPERTS].astype(jnp.int32), n_rows // tm, tm)
    xs = _dispatch(h, dest, meta, n_rows)
    act = _moe_up(xs, w_up, b_up, meta)
    y_rows = _moe_down(act, w_down, b_down, meta)
    return _combine(y_rows, dest, gate_p, h, g, b)


def kernel(x, mem, ln_in_g, ln_in_b, rel_bias, w_in, b_in, attn_sinks, w_a_out, w_b_out,
           w_mix_out, ln1_g, ln1_b, w_xq, w_xkv, w_xo, ln2_g, ln2_b, w_router, b_router,
           w_up, b_up, w_down, b_down, ln3_g, ln3_b):
    batch, seq, d = x.shape
    mem_len = mem.shape[1]
    t = batch * seq
    a_cols = A_WIDTH + 2 * A_KV_WIDTH
    b_cols = 3 * B_WIDTH

    h, hb = _ln_in(x.reshape(t, d), ln_in_g, ln_in_b)
    for l in range(DEPTH):
        w_in_b = w_in[l].astype(BF16)
        qkv_a = _mm_bias(hb, w_in_b[:, :a_cols], b_in[l, :a_cols], BF16, "in_proj_a")
        qkv_b = _mm_bias(hb, w_in_b[:, a_cols:a_cols + b_cols],
                         b_in[l, a_cols:a_cols + b_cols], BF16, "in_proj_b")
        gates = _mm_bias(hb, w_in_b[:, a_cols + b_cols:], b_in[l, a_cols + b_cols:], F32,
                         "in_proj_gates")
        ya = _swa(qkv_a, attn_sinks[l], rel_bias, batch, seq)
        yb = _stick_breaking(qkv_b, batch, seq)
        h, hb = _mix(ya, yb, gates, h, w_a_out[l].astype(BF16), w_b_out[l].astype(BF16),
                     w_mix_out[l].astype(BF16), ln1_g[l], ln1_b[l])
        kv = _mm_bias(mem.reshape(batch * mem_len, d).astype(BF16), w_xkv[l].astype(BF16),
                      jnp.zeros((2 * d,), F32), BF16, "mem_kv")
        h = _xattn(hb, h, kv, w_xq[l].astype(BF16), w_xo[l].astype(BF16), ln2_g[l], ln2_b[l],
                   batch, seq, mem_len)
        h = _moe(h, w_router[l], b_router[l], w_up[l], b_up[l], w_down[l], b_down[l],
                 ln3_g[l], ln3_b[l])
        if l + 1 < DEPTH:
            hb = h.astype(BF16)
    return h.reshape(batch, seq, d)
```

```python
import functools

import jax
import jax.numpy as jnp
import numpy as np
from jax import lax
from jax.experimental import pallas as pl
from jax.experimental.pallas import tpu as pltpu

DEPTH = 1
CHUNK = 64
BLK = 128
HEAD_DIM = 64
A_HEADS = 16
A_KV_HEADS = 2
A_REP = A_HEADS // A_KV_HEADS
WIN_CHUNKS = 2
B_HEADS = 16
A_WIDTH = A_HEADS * HEAD_DIM
A_KV_WIDTH = A_KV_HEADS * HEAD_DIM
B_WIDTH = B_HEADS * HEAD_DIM
REL_BUCKETS = 32
REL_MAX_DIST = 128
X_HEADS = 4
N_EXPERTS = 32
TOP_K = 4
SWIGLU_LIMIT = 7.0
SWIGLU_ALPHA = 1.702
LN_EPS = 1e-5
DEEPNORM_ALPHA = (2.0 * DEPTH) ** 0.25
QK_SCALE = HEAD_DIM ** -0.5

V7X_LANES = 128
V7X_VMEM_LIMIT_BYTES = 56 * 1024 * 1024

ROW_TILE = 512
ATT_TILE = 256
SB_TILE = 256
SB_ROWS = 32
SB_UNROLL = True
EXPERT_TILE = 256
COMBINE_TILE = 128
SB_LOG_ZERO = -105.0

F32 = jnp.float32
BF16 = jnp.bfloat16
NT_DIMS = (((1,), (1,)), ((), ()))


def _params(semantics):
    return pltpu.CompilerParams(dimension_semantics=semantics,
                                vmem_limit_bytes=V7X_VMEM_LIMIT_BYTES)


def _const_spec(shape):
    nd = len(shape)
    return pl.BlockSpec(shape, lambda *_: (0,) * nd, pipeline_mode=pl.Buffered(1))


def _layer_norm(x, g, b):
    mu = jnp.mean(x, axis=-1, keepdims=True)
    xc = x - mu
    var = jnp.mean(xc * xc, axis=-1, keepdims=True)
    return xc * lax.rsqrt(var + LN_EPS) * g + b


def _ln_in_kernel(x_ref, g_ref, b_ref, h_ref, hb_ref):
    y = _layer_norm(x_ref[...], g_ref[...], b_ref[...])
    h_ref[...] = y
    hb_ref[...] = y.astype(BF16)


def _ln_in(x2d, g, b):
    t, d = x2d.shape
    tm = min(ROW_TILE, t)
    row = pl.BlockSpec((tm, d), lambda i: (i, 0))
    return pl.pallas_call(
        _ln_in_kernel,
        grid=(t // tm,),
        in_specs=[row, _const_spec((1, d)), _const_spec((1, d))],
        out_specs=[row, row],
        out_shape=[jax.ShapeDtypeStruct((t, d), F32), jax.ShapeDtypeStruct((t, d), BF16)],
        compiler_params=_params(("arbitrary",)),
        name="ln_in",
    )(x2d, g.reshape(1, d), b.reshape(1, d))


def _mm_bias_kernel(a_ref, w_ref, b_ref, o_ref):
    acc = jnp.dot(a_ref[...], w_ref[...], preferred_element_type=F32)
    o_ref[...] = (acc + b_ref[...]).astype(o_ref.dtype)


def _pick_tile(n, target):
    best = None
    for c in range(V7X_LANES, min(n, target) + 1, V7X_LANES):
        if n % c == 0:
            best = c
    return n if best is None else best


def _mm_bias(a, w, b, out_dtype, name):
    m, k = a.shape
    n = w.shape[1]
    tm = min(ROW_TILE, m)
    tn = _pick_tile(n, 1024)
    return pl.pallas_call(
        _mm_bias_kernel,
        grid=(n // tn, m // tm),
        in_specs=[pl.BlockSpec((tm, k), lambda j, i: (i, 0)),
                  pl.BlockSpec((k, tn), lambda j, i: (0, j)),
                  pl.BlockSpec((1, tn), lambda j, i: (0, j))],
        out_specs=pl.BlockSpec((tm, tn), lambda j, i: (i, j)),
        out_shape=jax.ShapeDtypeStruct((m, n), out_dtype),
        compiler_params=_params(("arbitrary", "arbitrary")),
        name=name,
    )(a, w, b.reshape(1, n).astype(F32))


def _swa_kernel(sink_ref, q_ref, kp_ref, kc_ref, vp_ref, vc_ref, bias_ref, o_ref):
    n = pl.program_id(1)
    kk = jnp.concatenate([kp_ref[...], kc_ref[...]], axis=0)
    vv = jnp.concatenate([vp_ref[...], vc_ref[...]], axis=0)
    col = lax.broadcasted_iota(jnp.int32, (BLK, 2 * BLK), 1)
    exists = (col >= BLK) | (n > 0)
    for h in range(A_HEADS):
        g = h // A_REP
        qh = q_ref[:, h * HEAD_DIM:(h + 1) * HEAD_DIM] * QK_SCALE
        kg = kk[:, g * HEAD_DIM:(g + 1) * HEAD_DIM]
        vg = vv[:, g * HEAD_DIM:(g + 1) * HEAD_DIM]
        s = lax.dot_general(qh, kg, NT_DIMS, preferred_element_type=F32)
        s = jnp.where(exists, s + bias_ref[h], -jnp.inf)
        sink = sink_ref[h]
        m = jnp.maximum(jnp.max(s, axis=-1, keepdims=True), sink)
        p = jnp.exp(s - m)
        denom = jnp.sum(p, axis=-1, keepdims=True) + jnp.exp(sink - m)
        o = jnp.dot(p.astype(BF16), vg, preferred_element_type=F32) / denom
        o_ref[:, h * HEAD_DIM:(h + 1) * HEAD_DIM] = o.astype(o_ref.dtype)


def _rel_bucket(rel):
    half = REL_BUCKETS // 2
    max_exact = half // 2
    base = jnp.where(rel > 0, half, 0)
    n = jnp.abs(rel)
    nf = jnp.maximum(n, 1).astype(F32)
    large = max_exact + (jnp.log(nf / max_exact) / np.log(REL_MAX_DIST / max_exact)
                         * (half - max_exact)).astype(jnp.int32)
    large = jnp.minimum(large, half - 1)
    return base + jnp.where(n < max_exact, n, large)


def _swa_bias_table(rel_bias):
    rel = jnp.arange(-(2 * BLK - 1), BLK)
    line = jnp.transpose(rel_bias[_rel_bucket(rel)]).astype(F32)
    bias = jnp.stack([line[:, BLK - 1 - q:3 * BLK - 1 - q] for q in range(BLK)], axis=1)
    qi = np.arange(BLK)[:, None]
    kj = np.arange(2 * BLK)[None, :]
    dchunk = (kj // CHUNK - BLK // CHUNK) - qi // CHUNK
    band = (dchunk <= 0) & (dchunk >= -WIN_CHUNKS)
    return jnp.where(jnp.asarray(band)[None], bias, -jnp.inf)


def _swa(qkv, sinks, rel_bias, batch, seq):
    nb = seq // BLK
    kcol = A_WIDTH // A_KV_WIDTH
    vcol = kcol + 1
    kv_blk = (BLK, A_KV_WIDTH)
    grid_spec = pltpu.PrefetchScalarGridSpec(
        num_scalar_prefetch=1,
        grid=(batch, nb),
        in_specs=[
            pl.BlockSpec((BLK, A_WIDTH), lambda b, n, s: (b * nb + n, 0)),
            pl.BlockSpec(kv_blk, lambda b, n, s: (b * nb + jnp.maximum(n - 1, 0), kcol)),
            pl.BlockSpec(kv_blk, lambda b, n, s: (b * nb + n, kcol)),
            pl.BlockSpec(kv_blk, lambda b, n, s: (b * nb + jnp.maximum(n - 1, 0), vcol)),
            pl.BlockSpec(kv_blk, lambda b, n, s: (b * nb + n, vcol)),
            pl.BlockSpec((A_HEADS, BLK, 2 * BLK), lambda b, n, s: (0, 0, 0),
                         pipeline_mode=pl.Buffered(1)),
        ],
        out_specs=pl.BlockSpec((BLK, A_WIDTH), lambda b, n, s: (b * nb + n, 0)),
    )
    return pl.pallas_call(
        _swa_kernel,
        grid_spec=grid_spec,
        out_shape=jax.ShapeDtypeStruct((batch * seq, A_WIDTH), BF16),
        compiler_params=_params(("arbitrary", "arbitrary")),
        name="swa_attn",
    )(sinks.astype(F32), qkv, qkv, qkv, qkv, qkv, _swa_bias_table(rel_bias))


def _sb_kernel(q_ref, k_ref, v_ref, tri_ref, o_ref, qh_scr, z_scr, hl_scr, s_scr, w_scr,
               carry_scr, acc_scr):
    t = q_ref.shape[0]
    i = pl.program_id(2)
    n_chunks = t // SB_ROWS
    lane = lax.broadcasted_iota(jnp.int32, (1, 2 * HEAD_DIM), 1)
    first = lane < HEAD_DIM
    zero = jnp.zeros((), BF16)
    q2 = q_ref[...] * QK_SCALE
    qh_scr[0] = jnp.where(first, q2, zero)
    qh_scr[1] = jnp.where(first, zero, q2)
    carry_scr[...] = jnp.zeros_like(carry_scr)
    acc_scr[...] = jnp.zeros_like(acc_scr)
    row = lax.broadcasted_iota(jnp.int32, (SB_ROWS, t), 0)
    col = lax.broadcasted_iota(jnp.int32, (SB_ROWS, t), 1)

    def key_tile(j, diagonal):
        start = pl.multiple_of(j * t, t)
        k2 = k_ref[pl.ds(start, t), :]
        v2 = v_ref[pl.ds(start, t), :]
        for h in range(2):
            z_scr[h] = lax.dot_general(qh_scr[h], k2, NT_DIMS, preferred_element_type=F32)

        def keep_chunk(c, carry):
            r0 = pl.multiple_of(c * SB_ROWS, SB_ROWS)
            rows = pl.ds(r0, SB_ROWS)
            for h in range(2):
                z = z_scr[h, rows, :]
                keep = jnp.minimum(-z, 0.0) - jnp.log(1.0 + jnp.exp(-jnp.abs(z)))
                if diagonal:
                    keep = jnp.where(col < row + r0, keep, 0.0)
                hi = keep.astype(BF16)
                hl_scr[h, rows, 0:t] = hi
                hl_scr[h, rows, t:2 * t] = (keep - hi.astype(F32)).astype(BF16)
            return carry

        lax.fori_loop(0, n_chunks, keep_chunk, 0, unroll=SB_UNROLL)
        tri = tri_ref[...]
        for h in range(2):
            s_scr[h] = jnp.dot(hl_scr[h], tri, preferred_element_type=F32)

        def weight_chunk(c, top):
            r0 = pl.multiple_of(c * SB_ROWS, SB_ROWS)
            rows = pl.ds(r0, SB_ROWS)
            for h in range(2):
                carry = carry_scr[h, rows, :]
                log_w = (z_scr[h, rows, :] + s_scr[h, rows, 0:t]
                         + jnp.concatenate([carry] * (t // V7X_LANES), axis=1))
                w = jnp.exp(log_w)
                if diagonal:
                    w = jnp.where(col < row + r0, w, 0.0)
                w_scr[rows, h * t:(h + 1) * t] = w.astype(BF16)
                carry = carry + s_scr[h, rows, t:t + V7X_LANES]
                carry_scr[h, rows, :] = carry
                top = jnp.maximum(top, carry)
            return top

        top = lax.fori_loop(0, n_chunks, weight_chunk,
                            jnp.full((SB_ROWS, V7X_LANES), -jnp.inf, F32), unroll=SB_UNROLL)
        v_heads = jnp.concatenate([jnp.where(first, v2, zero), jnp.where(first, zero, v2)], axis=0)
        acc_scr[...] += jnp.dot(w_scr[...], v_heads, preferred_element_type=F32)
        return (jnp.max(top) > SB_LOG_ZERO).astype(jnp.int32)

    live = key_tile(i, True)

    def cond(state):
        return (state[0] >= 0) & (state[1] > 0)

    def body(state):
        return state[0] - 1, key_tile(state[0], False)

    lax.while_loop(cond, body, (i - 1, live))
    o_ref[...] = acc_scr[...].astype(o_ref.dtype)


def _stick_breaking(qkv, batch, seq):
    t = min(SB_TILE, seq)
    nq = seq // t
    pairs = B_WIDTH // (2 * HEAD_DIM)
    j = np.arange(2 * t)[:, None] % t
    s = np.arange(t)[None, :]
    tri = np.concatenate([(j >= s), np.ones((2 * t, V7X_LANES), bool)], axis=1)
    tri = jnp.asarray(tri.astype(np.float32), dtype=BF16)
    return pl.pallas_call(
        _sb_kernel,
        grid=(batch, pairs, nq),
        in_specs=[
            pl.BlockSpec((t, 2 * HEAD_DIM), lambda b, p, i: (b * nq + i, p)),
            pl.BlockSpec((seq, 2 * HEAD_DIM), lambda b, p, i: (b, pairs + p)),
            pl.BlockSpec((seq, 2 * HEAD_DIM), lambda b, p, i: (b, 2 * pairs + p)),
            _const_spec((2 * t, t + V7X_LANES)),
        ],
        out_specs=pl.BlockSpec((t, 2 * HEAD_DIM), lambda b, p, i: (b * nq + i, p)),
        out_shape=jax.ShapeDtypeStruct((batch * seq, B_WIDTH), BF16),
        scratch_shapes=[
            pltpu.VMEM((2, t, 2 * HEAD_DIM), BF16),
            pltpu.VMEM((2, t, t), F32),
            pltpu.VMEM((2, t, 2 * t), BF16),
            pltpu.VMEM((2, t, t + V7X_LANES), F32),
            pltpu.VMEM((t, 2 * t), BF16),
            pltpu.VMEM((2, t, V7X_LANES), F32),
            pltpu.VMEM((t, 2 * HEAD_DIM), F32),
        ],
        compiler_params=_params(("arbitrary", "arbitrary", "arbitrary")),
        name="stick_breaking",
    )(qkv, qkv, qkv, tri)


def _mix_kernel(ya_ref, yb_ref, gates_a_ref, gates_b_ref, h_ref, wa_ref, wb_ref, wm_ref,
                g_ref, b_ref, o_ref, ob_ref):
    pa = jnp.dot(ya_ref[...], wa_ref[...], preferred_element_type=F32)
    pb = jnp.dot(yb_ref[...], wb_ref[...], preferred_element_type=F32)
    merged = jax.nn.sigmoid(gates_a_ref[...]) * pa + jax.nn.sigmoid(gates_b_ref[...]) * pb
    mixed = jnp.dot(merged.astype(BF16), wm_ref[...], preferred_element_type=F32)
    y = _layer_norm(DEEPNORM_ALPHA * h_ref[...] + mixed, g_ref[...], b_ref[...])
    o_ref[...] = y
    ob_ref[...] = y.astype(BF16)


def _mix(ya, yb, gates, h, wa, wb, wm, g, b):
    t, d = h.shape
    tm = min(ATT_TILE, t)
    row = pl.BlockSpec((tm, d), lambda i: (i, 0))
    return pl.pallas_call(
        _mix_kernel,
        grid=(t // tm,),
        in_specs=[pl.BlockSpec((tm, A_WIDTH), lambda i: (i, 0)),
                  pl.BlockSpec((tm, B_WIDTH), lambda i: (i, 0)),
                  pl.BlockSpec((tm, d), lambda i: (i, 0)),
                  pl.BlockSpec((tm, d), lambda i: (i, 1)),
                  row,
                  _const_spec((A_WIDTH, d)), _const_spec((B_WIDTH, d)), _const_spec((d, d)),
                  _const_spec((1, d)), _const_spec((1, d))],
        out_specs=[row, row],
        out_shape=[jax.ShapeDtypeStruct((t, d), F32), jax.ShapeDtypeStruct((t, d), BF16)],
        compiler_params=_params(("arbitrary",)),
        name="mix_out",
    )(ya, yb, gates, gates, h, wa, wb, wm, g.reshape(1, d), b.reshape(1, d))


def _xattn_kernel(hb_ref, h_ref, k_ref, v_ref, wq_ref, wo_ref, g_ref, b_ref, o_ref):
    d = h_ref.shape[1]
    hd = d // X_HEADS
    scale = hd ** -0.5
    q = jnp.dot(hb_ref[...], wq_ref[...], preferred_element_type=F32).astype(BF16)
    outs = []
    for x in range(X_HEADS):
        sl = slice(x * hd, (x + 1) * hd)
        s = lax.dot_general(q[:, sl], k_ref[:, sl], NT_DIMS, preferred_element_type=F32) * scale
        m = jnp.max(s, axis=-1, keepdims=True)
        p = jnp.exp(s - m)
        denom = jnp.sum(p, axis=-1, keepdims=True)
        o = jnp.dot(p.astype(BF16), v_ref[:, sl], preferred_element_type=F32) / denom
        outs.append(o.astype(BF16))
    xo = jnp.dot(jnp.concatenate(outs, axis=1), wo_ref[...], preferred_element_type=F32)
    o_ref[...] = _layer_norm(DEEPNORM_ALPHA * h_ref[...] + xo, g_ref[...], b_ref[...])


def _xattn(hb, h, kv, wq, wo, g, b, batch, seq, mem_len):
    t, d = h.shape
    tm = min(ATT_TILE, seq)
    ns = seq // tm
    row = pl.BlockSpec((tm, d), lambda bi, i: (bi * ns + i, 0))
    return pl.pallas_call(
        _xattn_kernel,
        grid=(batch, ns),
        in_specs=[row, row,
                  pl.BlockSpec((mem_len, d), lambda bi, i: (bi, 0)),
                  pl.BlockSpec((mem_len, d), lambda bi, i: (bi, 1)),
                  _const_spec((d, d)), _const_spec((d, d)),
                  _const_spec((1, d)), _const_spec((1, d))],
        out_specs=row,
        out_shape=jax.ShapeDtypeStruct((t, d), F32),
        compiler_params=_params(("arbitrary", "arbitrary")),
        name="xattn",
    )(hb, h, kv, kv, wq, wo, g.reshape(1, d), b.reshape(1, d))


def _router_kernel(h_ref, whi_ref, wlo_ref, b_ref, ltri_ref, idx_ref, gate_ref, rank_ref,
                   cnt_ref, run_ref):
    @pl.when(pl.program_id(0) == 0)
    def _():
        run_ref[...] = jnp.zeros_like(run_ref)

    h = h_ref[...]
    h_hi = h.astype(BF16)
    h_lo = (h - h_hi.astype(F32)).astype(BF16)
    w_hi = whi_ref[...]
    logits = (jnp.dot(h_hi, w_hi, preferred_element_type=F32)
              + (jnp.dot(h_lo, w_hi, preferred_element_type=F32)
                 + jnp.dot(h_hi, wlo_ref[...], preferred_element_type=F32))
              + b_ref[...])
    tm = logits.shape[0]
    lane_i = lax.broadcasted_iota(jnp.int32, (tm, V7X_LANES), 1)
    lane = lane_i.astype(F32)
    rem = jnp.where(lane_i < N_EXPERTS, logits, -jnp.inf)
    vals, idxs = [], []
    for _ in range(TOP_K):
        m = jnp.max(rem, axis=-1, keepdims=True)
        ik = jnp.min(jnp.where(rem == m, lane, float(V7X_LANES)), axis=-1, keepdims=True)
        vals.append(m)
        idxs.append(ik)
        rem = jnp.where(lane == ik, -jnp.inf, rem)
    exps = [jnp.exp(v - vals[0]) for v in vals]
    denom = exps[0] + exps[1] + exps[2] + exps[3]
    sel = jnp.zeros((tm, V7X_LANES), F32)
    for ik in idxs:
        sel = sel + (lane == ik).astype(F32)
    before = jnp.dot(ltri_ref[...], sel.astype(BF16), preferred_element_type=F32) + run_ref[...]
    idx_out = jnp.zeros((tm, V7X_LANES), jnp.int32)
    gate_out = jnp.zeros((tm, V7X_LANES), F32)
    rank_out = jnp.zeros((tm, V7X_LANES), jnp.int32)
    for k in range(TOP_K):
        rank_k = jnp.sum(jnp.where(lane == idxs[k], before, 0.0), axis=-1, keepdims=True)
        idx_out = jnp.where(lane_i == k, idxs[k].astype(jnp.int32), idx_out)
        gate_out = jnp.where(lane_i == k, exps[k] / denom, gate_out)
        rank_out = jnp.where(lane_i == k, rank_k.astype(jnp.int32), rank_out)
    idx_ref[...] = idx_out
    gate_ref[...] = gate_out
    rank_ref[...] = rank_out
    run_ref[...] = run_ref[...] + jnp.sum(sel, axis=0, keepdims=True)
    cnt_ref[...] = run_ref[...]


def _router(h, w_router, b_router):
    t, d = h.shape
    tm = min(ATT_TILE, t)
    w_pad = jnp.zeros((d, V7X_LANES), F32).at[:, :N_EXPERTS].set(w_router)
    w_hi = w_pad.astype(BF16)
    w_lo = (w_pad - w_hi.astype(F32)).astype(BF16)
    b_pad = jnp.zeros((1, V7X_LANES), F32).at[0, :N_EXPERTS].set(b_router)
    ltri = jnp.asarray(np.tril(np.ones((tm, tm), np.float32), -1), dtype=BF16)
    small = pl.BlockSpec((tm, V7X_LANES), lambda i: (i, 0))
    return pl.pallas_call(
        _router_kernel,
        grid=(t // tm,),
        in_specs=[pl.BlockSpec((tm, d), lambda i: (i, 0)),
                  _const_spec((d, V7X_LANES)), _const_spec((d, V7X_LANES)),
                  _const_spec((1, V7X_LANES)), _const_spec((tm, tm))],
        out_specs=[small, small, small, pl.BlockSpec((1, V7X_LANES), lambda i: (0, 0))],
        out_shape=[jax.ShapeDtypeStruct((t, V7X_LANES), jnp.int32),
                   jax.ShapeDtypeStruct((t, V7X_LANES), F32),
                   jax.ShapeDtypeStruct((t, V7X_LANES), jnp.int32),
                   jax.ShapeDtypeStruct((1, V7X_LANES), F32)],
        scratch_shapes=[pltpu.VMEM((1, V7X_LANES), F32)],
        compiler_params=_params(("arbitrary",)),
        name="router",
    )(h, w_hi, w_lo, b_pad, ltri)


DMA_UNROLL = 8
WEIGHT_DMA_PRIORITY = 1
(META_EXPERT, META_VALID, META_FIRST, META_LAST_GROUP, META_NEXT_EXPERT, META_GROUP,
 META_ROWS_USED, META_NUM_GROUPS) = range(8)


def _dispatch_kernel(dest_ref, meta_ref, h_ref, xs_hbm, zero_ref, sem, zsem):
    tm = h_ref.shape[0]
    base = pl.program_id(0) * tm
    et = zero_ref.shape[0]

    @pl.when(pl.program_id(0) == 0)
    def _():
        zero_ref[...] = jnp.zeros_like(zero_ref)

        def fill_copy(tile):
            start_row = pl.multiple_of(tile * et, et)
            return pltpu.make_async_copy(zero_ref, xs_hbm.at[pl.ds(start_row, et)], zsem)

        def fill_start(tile, c):
            @pl.when(meta_ref[META_ROWS_USED, tile] < et)
            def _():
                fill_copy(tile).start()
            return c

        def fill_wait(tile, c):
            @pl.when(meta_ref[META_ROWS_USED, tile] < et)
            def _():
                fill_copy(tile).wait()
            return c

        n_tiles = xs_hbm.shape[0] // et
        lax.fori_loop(0, n_tiles, fill_start, 0)
        lax.fori_loop(0, n_tiles, fill_wait, 0)

    def row_copy(r, k):
        return pltpu.make_async_copy(h_ref.at[pl.ds(r, 1)],
                                     xs_hbm.at[pl.ds(dest_ref[(base + r) * TOP_K + k], 1)], sem)

    def start(r, c):
        for k in range(TOP_K):
            row_copy(r, k).start()
        return c

    def wait(r, c):
        for k in range(TOP_K):
            row_copy(r, k).wait()
        return c

    lax.fori_loop(0, tm, start, 0, unroll=DMA_UNROLL)
    lax.fori_loop(0, tm, wait, 0, unroll=DMA_UNROLL)


def _dispatch(h, dest, meta, n_rows):
    t, d = h.shape
    tm = min(ATT_TILE, t)
    grid_spec = pltpu.PrefetchScalarGridSpec(
        num_scalar_prefetch=2,
        grid=(t // tm,),
        in_specs=[pl.BlockSpec((tm, d), lambda i, dst, m: (i, 0))],
        out_specs=pl.BlockSpec(memory_space=pl.ANY),
        scratch_shapes=[pltpu.VMEM((EXPERT_TILE, d), F32), pltpu.SemaphoreType.DMA(()),
                        pltpu.SemaphoreType.DMA(())],
    )
    return pl.pallas_call(
        _dispatch_kernel,
        grid_spec=grid_spec,
        out_shape=jax.ShapeDtypeStruct((n_rows, d), F32),
        compiler_params=_params(("arbitrary",)),
        name="moe_dispatch",
    )(dest, meta, h)


def _stream_expert_weights(meta_ref, j, t, nj, make_copies, on_arrival):
    @pl.when((j == 0) & (t == 0))
    def _():
        for c in make_copies(meta_ref[META_EXPERT, 0], 0, 0):
            c.start(priority=WEIGHT_DMA_PRIORITY)

    @pl.when(meta_ref[META_FIRST, t] == 1)
    def _():
        slot = (j * meta_ref[META_NUM_GROUPS, t] + meta_ref[META_GROUP, t]) & 1
        for c in make_copies(meta_ref[META_EXPERT, t], j, slot):
            c.wait()
        last_group = meta_ref[META_LAST_GROUP, t] == 1

        @pl.when(jnp.logical_not(last_group & (j == nj - 1)))
        def _():
            next_chunk = jnp.where(last_group, j + 1, j)
            for c in make_copies(meta_ref[META_NEXT_EXPERT, t], next_chunk, 1 - slot):
                c.start(priority=WEIGHT_DMA_PRIORITY)

        on_arrival(slot)


def _moe_up_kernel(meta_ref, x_ref, w_hbm, bg_ref, bl_ref, o_ref, wf32, wbf, sems, *, tn, nj):
    j = pl.program_id(0)
    t = pl.program_id(1)

    def make_copies(expert, chunk, slot):
        glu_col = pl.multiple_of(chunk * tn, tn)
        lin_col = pl.multiple_of((nj + chunk) * tn, tn)
        return (pltpu.make_async_copy(w_hbm.at[expert, :, pl.ds(glu_col, tn)],
                                      wf32.at[slot, 0], sems.at[slot, 0]),
                pltpu.make_async_copy(w_hbm.at[expert, :, pl.ds(lin_col, tn)],
                                      wf32.at[slot, 1], sems.at[slot, 1]))

    def on_arrival(slot):
        wbf[0] = wf32[slot, 0].astype(BF16)
        wbf[1] = wf32[slot, 1].astype(BF16)

    _stream_expert_weights(meta_ref, j, t, nj, make_copies, on_arrival)

    @pl.when(meta_ref[META_VALID, t] > 0)
    def _():
        x = x_ref[...].astype(BF16)
        glu = jnp.dot(x, wbf[0], preferred_element_type=F32) + bg_ref[...]
        lin = jnp.dot(x, wbf[1], preferred_element_type=F32) + bl_ref[...]
        glu = jnp.minimum(glu, SWIGLU_LIMIT)
        lin = jnp.clip(lin, -SWIGLU_LIMIT, SWIGLU_LIMIT)
        act = glu * jax.nn.sigmoid(SWIGLU_ALPHA * glu) * (lin + 1.0)
        o_ref[...] = act.astype(o_ref.dtype)

    @pl.when(meta_ref[META_VALID, t] == 0)
    def _():
        o_ref[...] = jnp.zeros_like(o_ref)


def _moe_up(xs, w_up, b_up, meta):
    r, d = xs.shape
    e, _, f2 = w_up.shape
    f = f2 // 2
    tm = EXPERT_TILE
    tn = _pick_tile(f, 1024)
    nj = f // tn
    grid_spec = pltpu.PrefetchScalarGridSpec(
        num_scalar_prefetch=1,
        grid=(nj, r // tm),
        in_specs=[
            pl.BlockSpec((tm, d), lambda j, t, m: (t, 0)),
            pl.BlockSpec(memory_space=pl.ANY),
            pl.BlockSpec((None, 1, tn), lambda j, t, m: (m[META_EXPERT, t], 0, j)),
            pl.BlockSpec((None, 1, tn), lambda j, t, m: (m[META_EXPERT, t], 0, nj + j)),
        ],
        out_specs=pl.BlockSpec((tm, tn), lambda j, t, m: (t, j)),
        scratch_shapes=[pltpu.VMEM((2, 2, d, tn), F32), pltpu.VMEM((2, d, tn), BF16),
                        pltpu.SemaphoreType.DMA((2, 2))],
    )
    b3 = b_up.reshape(e, 1, f2)
    return pl.pallas_call(
        functools.partial(_moe_up_kernel, tn=tn, nj=nj),
        grid_spec=grid_spec,
        out_shape=jax.ShapeDtypeStruct((r, f), BF16),
        compiler_params=_params(("arbitrary", "arbitrary")),
        name="moe_up",
    )(meta, xs, w_up, b3, b3)


def _moe_down_kernel(meta_ref, a_ref, w_hbm, b_ref, o_ref, wf32, wbf, sems, *, tn, nj):
    j = pl.program_id(0)
    t = pl.program_id(1)

    def make_copies(expert, chunk, slot):
        col = pl.multiple_of(chunk * tn, tn)
        return (pltpu.make_async_copy(w_hbm.at[expert, :, pl.ds(col, tn)],
                                      wf32.at[slot], sems.at[slot]),)

    def on_arrival(slot):
        wbf[...] = wf32[slot].astype(BF16)

    _stream_expert_weights(meta_ref, j, t, nj, make_copies, on_arrival)

    @pl.when(meta_ref[META_VALID, t] > 0)
    def _():
        o_ref[...] = jnp.dot(a_ref[...], wbf[...], preferred_element_type=F32) + b_ref[...]

    @pl.when(meta_ref[META_VALID, t] == 0)
    def _():
        o_ref[...] = jnp.zeros_like(o_ref)


def _moe_down(act, w_down, b_down, meta):
    r, f = act.shape
    e, _, d = w_down.shape
    tm = EXPERT_TILE
    tn = _pick_tile(d, 2048)
    nj = d // tn
    grid_spec = pltpu.PrefetchScalarGridSpec(
        num_scalar_prefetch=1,
        grid=(nj, r // tm),
        in_specs=[
            pl.BlockSpec((tm, f), lambda j, t, m: (t, 0)),
            pl.BlockSpec(memory_space=pl.ANY),
            pl.BlockSpec((None, 1, tn), lambda j, t, m: (m[META_EXPERT, t], 0, j)),
        ],
        out_specs=pl.BlockSpec((tm, tn), lambda j, t, m: (t, j)),
        scratch_shapes=[pltpu.VMEM((2, f, tn), F32), pltpu.VMEM((f, tn), BF16),
                        pltpu.SemaphoreType.DMA((2,))],
    )
    return pl.pallas_call(
        functools.partial(_moe_down_kernel, tn=tn, nj=nj),
        grid_spec=grid_spec,
        out_shape=jax.ShapeDtypeStruct((r, d), F32),
        compiler_params=_params(("arbitrary", "arbitrary")),
        name="moe_down",
    )(meta, act, w_down, b_down.reshape(e, 1, d))


def _combine_kernel(pos_ref, y_hbm, gate_ref, h_ref, g_ref, b_ref, o_ref, buf_ref, sems):
    tm = o_ref.shape[0]
    i = pl.program_id(0)

    def row_copy(step, r, k, slot):
        src_row = pos_ref[(step * tm + r) * TOP_K + k]
        return pltpu.make_async_copy(y_hbm.at[pl.ds(src_row, 1)],
                                     buf_ref.at[slot, k, pl.ds(r, 1)], sems.at[slot])

    def gather_start(step, slot):
        def body(r, c):
            for k in range(TOP_K):
                row_copy(step, r, k, slot).start()
            return c
        lax.fori_loop(0, tm, body, 0, unroll=DMA_UNROLL)

    def gather_wait(step, slot):
        def body(r, c):
            for k in range(TOP_K):
                row_copy(step, r, k, slot).wait()
            return c
        lax.fori_loop(0, tm, body, 0, unroll=DMA_UNROLL)

    @pl.when(i == 0)
    def _():
        gather_start(0, 0)

    @pl.when(i + 1 < pl.num_programs(0))
    def _():
        gather_start(i + 1, (i + 1) & 1)

    slot = i & 1
    gather_wait(i, slot)
    gates = gate_ref[...]
    ff = gates[:, 0:1] * buf_ref[slot, 0]
    for k in range(1, TOP_K):
        ff = ff + gates[:, k:k + 1] * buf_ref[slot, k]
    o_ref[...] = _layer_norm(DEEPNORM_ALPHA * h_ref[...] + ff, g_ref[...], b_ref[...])


def _combine(y_rows, pos, gates, h, g, b):
    t, d = h.shape
    tm = min(COMBINE_TILE, t)
    grid_spec = pltpu.PrefetchScalarGridSpec(
        num_scalar_prefetch=1,
        grid=(t // tm,),
        in_specs=[pl.BlockSpec(memory_space=pl.ANY),
                  pl.BlockSpec((tm, V7X_LANES), lambda i, p: (i, 0)),
                  pl.BlockSpec((tm, d), lambda i, p: (i, 0)),
                  pl.BlockSpec((1, d), lambda i, p: (0, 0)),
                  pl.BlockSpec((1, d), lambda i, p: (0, 0))],
        out_specs=pl.BlockSpec((tm, d), lambda i, p: (i, 0)),
        scratch_shapes=[pltpu.VMEM((2, TOP_K, tm, d), F32), pltpu.SemaphoreType.DMA((2,))],
    )
    return pl.pallas_call(
        _combine_kernel,
        grid_spec=grid_spec,
        out_shape=jax.ShapeDtypeStruct((t, d), F32),
        compiler_params=_params(("arbitrary",)),
        name="moe_combine",
    )(pos, y_rows, gates, h, g.reshape(1, d), b.reshape(1, d))


def _expert_layout(idx, rank, counts, n_tiles, tm):
    experts = jnp.arange(N_EXPERTS, dtype=jnp.int32)
    tiles_per = (counts + tm - 1) // tm
    tile_end = jnp.cumsum(tiles_per).astype(jnp.int32)
    tile_off = tile_end - tiles_per
    dest = jnp.sum(jnp.where(idx[..., None] == experts, tile_off * tm, 0), axis=-1) + rank

    nonempty = counts > 0
    first_e = jnp.min(jnp.where(nonempty, experts, N_EXPERTS))
    last_e = jnp.max(jnp.where(nonempty, experts, 0))
    group_of = jnp.cumsum(nonempty.astype(jnp.int32)) - 1
    later = (experts[None, :] > experts[:, None]) & nonempty[None, :]
    next_e = jnp.min(jnp.where(later, experts[None, :], N_EXPERTS), axis=1)
    next_e = jnp.where(next_e == N_EXPERTS, first_e, next_e)

    tid = jnp.arange(n_tiles, dtype=jnp.int32)
    valid = tid < tile_end[-1]
    te = jnp.sum((tid[:, None] >= tile_end[None, :]).astype(jnp.int32), axis=1)
    te = jnp.where(valid, jnp.minimum(te, N_EXPERTS - 1), last_e)
    onehot = te[:, None] == experts[None, :]

    def lookup(table):
        return jnp.sum(jnp.where(onehot, table[None, :], 0), axis=1)

    local = tid - lookup(tile_off)
    rows_used = jnp.where(valid, jnp.clip(lookup(counts) - local * tm, 0, tm), 0)
    meta = jnp.stack([
        te,
        valid.astype(jnp.int32),
        (valid & (local == 0)).astype(jnp.int32),
        (te == last_e).astype(jnp.int32),
        lookup(next_e),
        lookup(group_of),
        rows_used,
        jnp.broadcast_to(jnp.sum(nonempty.astype(jnp.int32)), (n_tiles,)),
    ]).astype(jnp.int32)
    return dest.reshape(-1).astype(jnp.int32), meta


def _moe(h, w_router, b_router, w_up, b_up, w_down, b_down, g, b):
    t, d = h.shape
    tm = EXPERT_TILE
    idx_p, gate_p, rank_p, cnt_p = _router(h, w_router, b_router)
    n_rows = t * TOP_K + N_EXPERTS * tm
    dest, meta = _expert_layout(idx_p[:, :TOP_K], rank_p[:, :TOP_K],
                                cnt_p[0, :N_EXPERTS].astype(jnp.int32), n_rows // tm, tm)
    xs = _dispatch(h, dest, meta, n_rows)
    act = _moe_up(xs, w_up, b_up, meta)
    y_rows = _moe_down(act, w_down, b_down, meta)
    return _combine(y_rows, dest, gate_p, h, g, b)


def kernel(x, mem, ln_in_g, ln_in_b, rel_bias, w_in, b_in, attn_sinks, w_a_out, w_b_out,
           w_mix_out, ln1_g, ln1_b, w_xq, w_xkv, w_xo, ln2_g, ln2_b, w_router, b_router,
           w_up, b_up, w_down, b_down, ln3_g, ln3_b):
    batch, seq, d = x.shape
    mem_len = mem.shape[1]
    t = batch * seq
    a_cols = A_WIDTH + 2 * A_KV_WIDTH
    b_cols = 3 * B_WIDTH

    h, hb = _ln_in(x.reshape(t, d), ln_in_g, ln_in_b)
    for l in range(DEPTH):
        w_in_b = w_in[l].astype(BF16)
        qkv_a = _mm_bias(hb, w_in_b[:, :a_cols], b_in[l, :a_cols], BF16, "in_proj_a")
        qkv_b = _mm_bias(hb, w_in_b[:, a_cols:a_cols + b_cols],
                         b_in[l, a_cols:a_cols + b_cols], BF16, "in_proj_b")
        gates = _mm_bias(hb, w_in_b[:, a_cols + b_cols:], b_in[l, a_cols + b_cols:], F32,
                         "in_proj_gates")
        ya = _swa(qkv_a, attn_sinks[l], rel_bias, batch, seq)
        yb = _stick_breaking(qkv_b, batch, seq)
        h, hb = _mix(ya, yb, gates, h, w_a_out[l].astype(BF16), w_b_out[l].astype(BF16),
                     w_mix_out[l].astype(BF16), ln1_g[l], ln1_b[l])
        kv = _mm_bias(mem.reshape(batch * mem_len, d).astype(BF16), w_xkv[l].astype(BF16),
                      jnp.zeros((2 * d,), F32), BF16, "mem_kv")
        h = _xattn(hb, h, kv, w_xq[l].astype(BF16), w_xo[l].astype(BF16), ln2_g[l], ln2_b[l],
                   batch, seq, mem_len)
        h = _moe(h, w_router[l], b_router[l], w_up[l], b_up[l], w_down[l], b_down[l],
                 ln3_g[l], ln3_b[l])
        if l + 1 < DEPTH:
            hb = h.astype(BF16)
    return h.reshape(batch, seq, d)
```

```python
import functools

import jax
import jax.numpy as jnp
import numpy as np
from jax import lax
from jax.experimental import pallas as pl
from jax.experimental.pallas import tpu as pltpu

DEPTH = 1
CHUNK = 64
BLK = 128
HEAD_DIM = 64
A_HEADS = 16
A_KV_HEADS = 2
A_REP = A_HEADS // A_KV_HEADS
WIN_CHUNKS = 2
B_HEADS = 16
A_WIDTH = A_HEADS * HEAD_DIM
A_KV_WIDTH = A_KV_HEADS * HEAD_DIM
B_WIDTH = B_HEADS * HEAD_DIM
REL_BUCKETS = 32
REL_MAX_DIST = 128
X_HEADS = 4
N_EXPERTS = 32
TOP_K = 4
SWIGLU_LIMIT = 7.0
SWIGLU_ALPHA = 1.702
LN_EPS = 1e-5
DEEPNORM_ALPHA = (2.0 * DEPTH) ** 0.25
QK_SCALE = HEAD_DIM ** -0.5

V7X_LANES = 128
V7X_VMEM_LIMIT_BYTES = 56 * 1024 * 1024

ROW_TILE = 512
ATT_TILE = 256
SB_TILE = 256
SB_ROWS = 32
SB_UNROLL = True
EXPERT_TILE = 256
COMBINE_TILE = 128
SB_LOG_ZERO = -105.0

F32 = jnp.float32
BF16 = jnp.bfloat16
NT_DIMS = (((1,), (1,)), ((), ()))


def _params(semantics):
    return pltpu.CompilerParams(dimension_semantics=semantics,
                                vmem_limit_bytes=V7X_VMEM_LIMIT_BYTES)


def _const_spec(shape):
    nd = len(shape)
    return pl.BlockSpec(shape, lambda *_: (0,) * nd, pipeline_mode=pl.Buffered(1))


def _layer_norm(x, g, b):
    mu = jnp.mean(x, axis=-1, keepdims=True)
    xc = x - mu
    var = jnp.mean(xc * xc, axis=-1, keepdims=True)
    return xc * lax.rsqrt(var + LN_EPS) * g + b


def _ln_in_kernel(x_ref, g_ref, b_ref, h_ref, hb_ref):
    y = _layer_norm(x_ref[...], g_ref[...], b_ref[...])
    h_ref[...] = y
    hb_ref[...] = y.astype(BF16)


def _ln_in(x2d, g, b):
    t, d = x2d.shape
    tm = min(ROW_TILE, t)
    row = pl.BlockSpec((tm, d), lambda i: (i, 0))
    return pl.pallas_call(
        _ln_in_kernel,
        grid=(t // tm,),
        in_specs=[row, _const_spec((1, d)), _const_spec((1, d))],
        out_specs=[row, row],
        out_shape=[jax.ShapeDtypeStruct((t, d), F32), jax.ShapeDtypeStruct((t, d), BF16)],
        compiler_params=_params(("arbitrary",)),
        name="ln_in",
    )(x2d, g.reshape(1, d), b.reshape(1, d))


def _mm_bias_kernel(a_ref, w_hbm, b_ref, o_ref, stage, wbf, sem, *, col0, tn, nj):
    j = pl.program_id(0)
    i = pl.program_id(1)

    def slab_copy(jj):
        col = pl.multiple_of(col0 + jj * tn, V7X_LANES)
        return pltpu.make_async_copy(w_hbm.at[:, pl.ds(col, tn)], stage, sem)

    @pl.when(i == 0)
    def _():
        @pl.when(j == 0)
        def _():
            slab_copy(0).start()

        slab_copy(j).wait()
        wbf[...] = stage[...].astype(BF16)

        @pl.when(j + 1 < nj)
        def _():
            slab_copy(j + 1).start()

    acc = jnp.dot(a_ref[...], wbf[...], preferred_element_type=F32)
    o_ref[...] = (acc + b_ref[...]).astype(o_ref.dtype)


def _pick_tile(n, target):
    best = None
    for c in range(V7X_LANES, min(n, target) + 1, V7X_LANES):
        if n % c == 0:
            best = c
    return n if best is None else best


def _mm_bias(a, w, b, col0, n, out_dtype, name):
    m, k = a.shape
    tm = min(ROW_TILE, m)
    tn = _pick_tile(n, 1024)
    nj = n // tn
    return pl.pallas_call(
        functools.partial(_mm_bias_kernel, col0=col0, tn=tn, nj=nj),
        grid=(nj, m // tm),
        in_specs=[pl.BlockSpec((tm, k), lambda j, i: (i, 0)),
                  pl.BlockSpec(memory_space=pl.ANY),
                  pl.BlockSpec((1, tn), lambda j, i: (0, j))],
        out_specs=pl.BlockSpec((tm, tn), lambda j, i: (i, j)),
        out_shape=jax.ShapeDtypeStruct((m, n), out_dtype),
        scratch_shapes=[pltpu.VMEM((k, tn), F32), pltpu.VMEM((k, tn), BF16),
                        pltpu.SemaphoreType.DMA(())],
        compiler_params=_params(("arbitrary", "arbitrary")),
        name=name,
    )(a, w, b.reshape(1, n).astype(F32))


def _swa_kernel(sink_ref, q_ref, kp_ref, kc_ref, vp_ref, vc_ref, bias_ref, o_ref):
    n = pl.program_id(1)
    kk = jnp.concatenate([kp_ref[...], kc_ref[...]], axis=0)
    vv = jnp.concatenate([vp_ref[...], vc_ref[...]], axis=0)
    col = lax.broadcasted_iota(jnp.int32, (BLK, 2 * BLK), 1)
    exists = (col >= BLK) | (n > 0)
    for h in range(A_HEADS):
        g = h // A_REP
        qh = q_ref[:, h * HEAD_DIM:(h + 1) * HEAD_DIM] * QK_SCALE
        kg = kk[:, g * HEAD_DIM:(g + 1) * HEAD_DIM]
        vg = vv[:, g * HEAD_DIM:(g + 1) * HEAD_DIM]
        s = lax.dot_general(qh, kg, NT_DIMS, preferred_element_type=F32)
        s = jnp.where(exists, s + bias_ref[h], -jnp.inf)
        sink = sink_ref[h]
        m = jnp.maximum(jnp.max(s, axis=-1, keepdims=True), sink)
        p = jnp.exp(s - m)
        denom = jnp.sum(p, axis=-1, keepdims=True) + jnp.exp(sink - m)
        o = jnp.dot(p.astype(BF16), vg, preferred_element_type=F32) / denom
        o_ref[:, h * HEAD_DIM:(h + 1) * HEAD_DIM] = o.astype(o_ref.dtype)


def _rel_bucket(rel):
    half = REL_BUCKETS // 2
    max_exact = half // 2
    base = jnp.where(rel > 0, half, 0)
    n = jnp.abs(rel)
    nf = jnp.maximum(n, 1).astype(F32)
    large = max_exact + (jnp.log(nf / max_exact) / np.log(REL_MAX_DIST / max_exact)
                         * (half - max_exact)).astype(jnp.int32)
    large = jnp.minimum(large, half - 1)
    return base + jnp.where(n < max_exact, n, large)


def _swa_bias_table(rel_bias):
    rel = jnp.arange(-(2 * BLK - 1), BLK)
    line = jnp.transpose(rel_bias[_rel_bucket(rel)]).astype(F32)
    bias = jnp.stack([line[:, BLK - 1 - q:3 * BLK - 1 - q] for q in range(BLK)], axis=1)
    qi = np.arange(BLK)[:, None]
    kj = np.arange(2 * BLK)[None, :]
    dchunk = (kj // CHUNK - BLK // CHUNK) - qi // CHUNK
    band = (dchunk <= 0) & (dchunk >= -WIN_CHUNKS)
    return jnp.where(jnp.asarray(band)[None], bias, -jnp.inf)


def _swa(qkv, sinks, rel_bias, batch, seq):
    nb = seq // BLK
    kcol = A_WIDTH // A_KV_WIDTH
    vcol = kcol + 1
    kv_blk = (BLK, A_KV_WIDTH)
    grid_spec = pltpu.PrefetchScalarGridSpec(
        num_scalar_prefetch=1,
        grid=(batch, nb),
        in_specs=[
            pl.BlockSpec((BLK, A_WIDTH), lambda b, n, s: (b * nb + n, 0)),
            pl.BlockSpec(kv_blk, lambda b, n, s: (b * nb + jnp.maximum(n - 1, 0), kcol)),
            pl.BlockSpec(kv_blk, lambda b, n, s: (b * nb + n, kcol)),
            pl.BlockSpec(kv_blk, lambda b, n, s: (b * nb + jnp.maximum(n - 1, 0), vcol)),
            pl.BlockSpec(kv_blk, lambda b, n, s: (b * nb + n, vcol)),
            pl.BlockSpec((A_HEADS, BLK, 2 * BLK), lambda b, n, s: (0, 0, 0),
                         pipeline_mode=pl.Buffered(1)),
        ],
        out_specs=pl.BlockSpec((BLK, A_WIDTH), lambda b, n, s: (b * nb + n, 0)),
    )
    return pl.pallas_call(
        _swa_kernel,
        grid_spec=grid_spec,
        out_shape=jax.ShapeDtypeStruct((batch * seq, A_WIDTH), BF16),
        compiler_params=_params(("arbitrary", "arbitrary")),
        name="swa_attn",
    )(sinks.astype(F32), qkv, qkv, qkv, qkv, qkv, _swa_bias_table(rel_bias))


def _sb_kernel(q_ref, k_ref, v_ref, tri_ref, o_ref, qh_scr, z_scr, hl_scr, s_scr, w_scr,
               carry_scr, acc_scr):
    t = q_ref.shape[0]
    i = pl.program_id(2)
    n_chunks = t // SB_ROWS
    lane = lax.broadcasted_iota(jnp.int32, (1, 2 * HEAD_DIM), 1)
    first = lane < HEAD_DIM
    zero = jnp.zeros((), BF16)
    q2 = q_ref[...] * QK_SCALE
    qh_scr[0] = jnp.where(first, q2, zero)
    qh_scr[1] = jnp.where(first, zero, q2)
    carry_scr[...] = jnp.zeros_like(carry_scr)
    acc_scr[...] = jnp.zeros_like(acc_scr)
    row = lax.broadcasted_iota(jnp.int32, (SB_ROWS, t), 0)
    col = lax.broadcasted_iota(jnp.int32, (SB_ROWS, t), 1)

    def key_tile(j, diagonal):
        start = pl.multiple_of(j * t, t)
        k2 = k_ref[pl.ds(start, t), :]
        v2 = v_ref[pl.ds(start, t), :]
        for h in range(2):
            z_scr[h] = lax.dot_general(qh_scr[h], k2, NT_DIMS, preferred_element_type=F32)

        def keep_chunk(c, carry):
            r0 = pl.multiple_of(c * SB_ROWS, SB_ROWS)
            rows = pl.ds(r0, SB_ROWS)
            for h in range(2):
                z = z_scr[h, rows, :]
                keep = jnp.minimum(-z, 0.0) - jnp.log(1.0 + jnp.exp(-jnp.abs(z)))
                if diagonal:
                    keep = jnp.where(col < row + r0, keep, 0.0)
                hi = keep.astype(BF16)
                hl_scr[h, rows, 0:t] = hi
                hl_scr[h, rows, t:2 * t] = (keep - hi.astype(F32)).astype(BF16)
            return carry

        lax.fori_loop(0, n_chunks, keep_chunk, 0, unroll=SB_UNROLL)
        tri = tri_ref[...]
        for h in range(2):
            s_scr[h] = jnp.dot(hl_scr[h], tri, preferred_element_type=F32)

        def weight_chunk(c, top):
            r0 = pl.multiple_of(c * SB_ROWS, SB_ROWS)
            rows = pl.ds(r0, SB_ROWS)
            for h in range(2):
                carry = carry_scr[h, rows, :]
                log_w = (z_scr[h, rows, :] + s_scr[h, rows, :]
                         + jnp.concatenate([carry] * (t // V7X_LANES), axis=1))
                w = jnp.exp(log_w)
                if diagonal:
                    w = jnp.where(col < row + r0, w, 0.0)
                w_scr[rows, h * t:(h + 1) * t] = w.astype(BF16)
                carry = carry + jnp.broadcast_to(s_scr[h, rows, 0:1], (SB_ROWS, V7X_LANES))
                carry_scr[h, rows, :] = carry
                top = jnp.maximum(top, carry)
            return top

        top = lax.fori_loop(0, n_chunks, weight_chunk,
                            jnp.full((SB_ROWS, V7X_LANES), -jnp.inf, F32), unroll=SB_UNROLL)
        v_heads = jnp.concatenate([jnp.where(first, v2, zero), jnp.where(first, zero, v2)], axis=0)
        acc_scr[...] += jnp.dot(w_scr[...], v_heads, preferred_element_type=F32)
        return (jnp.max(top) > SB_LOG_ZERO).astype(jnp.int32)

    live = key_tile(i, True)

    def cond(state):
        return (state[0] >= 0) & (state[1] > 0)

    def body(state):
        return state[0] - 1, key_tile(state[0], False)

    lax.while_loop(cond, body, (i - 1, live))
    o_ref[...] = acc_scr[...].astype(o_ref.dtype)


def _stick_breaking(qkv, batch, seq):
    t = min(SB_TILE, seq)
    nq = seq // t
    pairs = B_WIDTH // (2 * HEAD_DIM)
    j = np.arange(2 * t)[:, None] % t
    s = np.arange(t)[None, :]
    tri = jnp.asarray((j >= s).astype(np.float32), dtype=BF16)
    return pl.pallas_call(
        _sb_kernel,
        grid=(batch, pairs, nq),
        in_specs=[
            pl.BlockSpec((t, 2 * HEAD_DIM), lambda b, p, i: (b * nq + i, p)),
            pl.BlockSpec((seq, 2 * HEAD_DIM), lambda b, p, i: (b, pairs + p)),
            pl.BlockSpec((seq, 2 * HEAD_DIM), lambda b, p, i: (b, 2 * pairs + p)),
            _const_spec((2 * t, t)),
        ],
        out_specs=pl.BlockSpec((t, 2 * HEAD_DIM), lambda b, p, i: (b * nq + i, p)),
        out_shape=jax.ShapeDtypeStruct((batch * seq, B_WIDTH), BF16),
        scratch_shapes=[
            pltpu.VMEM((2, t, 2 * HEAD_DIM), BF16),
            pltpu.VMEM((2, t, t), F32),
            pltpu.VMEM((2, t, 2 * t), BF16),
            pltpu.VMEM((2, t, t), F32),
            pltpu.VMEM((t, 2 * t), BF16),
            pltpu.VMEM((2, t, V7X_LANES), F32),
            pltpu.VMEM((t, 2 * HEAD_DIM), F32),
        ],
        compiler_params=_params(("arbitrary", "arbitrary", "arbitrary")),
        name="stick_breaking",
    )(qkv, qkv, qkv, tri)


def _mix_kernel(ya_ref, yb_ref, gates_a_ref, gates_b_ref, h_ref, wa_ref, wb_ref, wm_ref,
                g_ref, b_ref, o_ref, ob_ref):
    pa = jnp.dot(ya_ref[...], wa_ref[...], preferred_element_type=F32)
    pb = jnp.dot(yb_ref[...], wb_ref[...], preferred_element_type=F32)
    merged = jax.nn.sigmoid(gates_a_ref[...]) * pa + jax.nn.sigmoid(gates_b_ref[...]) * pb
    mixed = jnp.dot(merged.astype(BF16), wm_ref[...], preferred_element_type=F32)
    y = _layer_norm(DEEPNORM_ALPHA * h_ref[...] + mixed, g_ref[...], b_ref[...])
    o_ref[...] = y
    ob_ref[...] = y.astype(BF16)


def _mix(ya, yb, gates, h, wa, wb, wm, g, b):
    t, d = h.shape
    tm = min(ATT_TILE, t)
    row = pl.BlockSpec((tm, d), lambda i: (i, 0))
    return pl.pallas_call(
        _mix_kernel,
        grid=(t // tm,),
        in_specs=[pl.BlockSpec((tm, A_WIDTH), lambda i: (i, 0)),
                  pl.BlockSpec((tm, B_WIDTH), lambda i: (i, 0)),
                  pl.BlockSpec((tm, d), lambda i: (i, 0)),
                  pl.BlockSpec((tm, d), lambda i: (i, 1)),
                  row,
                  _const_spec((A_WIDTH, d)), _const_spec((B_WIDTH, d)), _const_spec((d, d)),
                  _const_spec((1, d)), _const_spec((1, d))],
        out_specs=[row, row],
        out_shape=[jax.ShapeDtypeStruct((t, d), F32), jax.ShapeDtypeStruct((t, d), BF16)],
        compiler_params=_params(("arbitrary",)),
        name="mix_out",
    )(ya, yb, gates, gates, h, wa, wb, wm, g.reshape(1, d), b.reshape(1, d))


def _xattn_kernel(hb_ref, h_ref, k_ref, v_ref, wq_ref, wo_ref, g_ref, b_ref, o_ref):
    d = h_ref.shape[1]
    hd = d // X_HEADS
    scale = hd ** -0.5
    q = jnp.dot(hb_ref[...], wq_ref[...], preferred_element_type=F32).astype(BF16)
    outs = []
    for x in range(X_HEADS):
        sl = slice(x * hd, (x + 1) * hd)
        s = lax.dot_general(q[:, sl], k_ref[:, sl], NT_DIMS, preferred_element_type=F32) * scale
        m = jnp.max(s, axis=-1, keepdims=True)
        p = jnp.exp(s - m)
        denom = jnp.sum(p, axis=-1, keepdims=True)
        o = jnp.dot(p.astype(BF16), v_ref[:, sl], preferred_element_type=F32) / denom
        outs.append(o.astype(BF16))
    xo = jnp.dot(jnp.concatenate(outs, axis=1), wo_ref[...], preferred_element_type=F32)
    o_ref[...] = _layer_norm(DEEPNORM_ALPHA * h_ref[...] + xo, g_ref[...], b_ref[...])


def _xattn(hb, h, kv, wq, wo, g, b, batch, seq, mem_len):
    t, d = h.shape
    tm = min(ATT_TILE, seq)
    ns = seq // tm
    row = pl.BlockSpec((tm, d), lambda bi, i: (bi * ns + i, 0))
    return pl.pallas_call(
        _xattn_kernel,
        grid=(batch, ns),
        in_specs=[row, row,
                  pl.BlockSpec((mem_len, d), lambda bi, i: (bi, 0)),
                  pl.BlockSpec((mem_len, d), lambda bi, i: (bi, 1)),
                  _const_spec((d, d)), _const_spec((d, d)),
                  _const_spec((1, d)), _const_spec((1, d))],
        out_specs=row,
        out_shape=jax.ShapeDtypeStruct((t, d), F32),
        compiler_params=_params(("arbitrary", "arbitrary")),
        name="xattn",
    )(hb, h, kv, kv, wq, wo, g.reshape(1, d), b.reshape(1, d))


def _router_kernel(h_ref, whi_ref, wlo_ref, b_ref, ltri_ref, idx_ref, gate_ref, rank_ref,
                   cnt_ref, run_ref):
    @pl.when(pl.program_id(0) == 0)
    def _():
        run_ref[...] = jnp.zeros_like(run_ref)

    h = h_ref[...]
    h_hi = h.astype(BF16)
    h_lo = (h - h_hi.astype(F32)).astype(BF16)
    w_hi = whi_ref[...]
    logits = (jnp.dot(h_hi, w_hi, preferred_element_type=F32)
              + (jnp.dot(h_lo, w_hi, preferred_element_type=F32)
                 + jnp.dot(h_hi, wlo_ref[...], preferred_element_type=F32))
              + b_ref[...])
    tm = logits.shape[0]
    lane_i = lax.broadcasted_iota(jnp.int32, (tm, V7X_LANES), 1)
    lane = lane_i.astype(F32)
    rem = jnp.where(lane_i < N_EXPERTS, logits, -jnp.inf)
    vals, idxs = [], []
    for _ in range(TOP_K):
        m = jnp.max(rem, axis=-1, keepdims=True)
        ik = jnp.min(jnp.where(rem == m, lane, float(V7X_LANES)), axis=-1, keepdims=True)
        vals.append(m)
        idxs.append(ik)
        rem = jnp.where(lane == ik, -jnp.inf, rem)
    exps = [jnp.exp(v - vals[0]) for v in vals]
    denom = exps[0] + exps[1] + exps[2] + exps[3]
    sel = jnp.zeros((tm, V7X_LANES), F32)
    for ik in idxs:
        sel = sel + (lane == ik).astype(F32)
    before = jnp.dot(ltri_ref[...], sel.astype(BF16), preferred_element_type=F32) + run_ref[...]
    idx_out = jnp.zeros((tm, V7X_LANES), jnp.int32)
    gate_out = jnp.zeros((tm, V7X_LANES), F32)
    rank_out = jnp.zeros((tm, V7X_LANES), jnp.int32)
    for k in range(TOP_K):
        rank_k = jnp.sum(jnp.where(lane == idxs[k], before, 0.0), axis=-1, keepdims=True)
        idx_out = jnp.where(lane_i == k, idxs[k].astype(jnp.int32), idx_out)
        gate_out = jnp.where(lane_i == k, exps[k] / denom, gate_out)
        rank_out = jnp.where(lane_i == k, rank_k.astype(jnp.int32), rank_out)
    idx_ref[...] = idx_out
    gate_ref[...] = gate_out
    rank_ref[...] = rank_out
    run_ref[...] = run_ref[...] + jnp.sum(sel, axis=0, keepdims=True)
    cnt_ref[...] = run_ref[...]


def _router(h, w_router, b_router):
    t, d = h.shape
    tm = min(ATT_TILE, t)
    w_pad = jnp.zeros((d, V7X_LANES), F32).at[:, :N_EXPERTS].set(w_router)
    w_hi = w_pad.astype(BF16)
    w_lo = (w_pad - w_hi.astype(F32)).astype(BF16)
    b_pad = jnp.zeros((1, V7X_LANES), F32).at[0, :N_EXPERTS].set(b_router)
    ltri = jnp.asarray(np.tril(np.ones((tm, tm), np.float32), -1), dtype=BF16)
    small = pl.BlockSpec((tm, V7X_LANES), lambda i: (i, 0))
    return pl.pallas_call(
        _router_kernel,
        grid=(t // tm,),
        in_specs=[pl.BlockSpec((tm, d), lambda i: (i, 0)),
                  _const_spec((d, V7X_LANES)), _const_spec((d, V7X_LANES)),
                  _const_spec((1, V7X_LANES)), _const_spec((tm, tm))],
        out_specs=[small, small, small, pl.BlockSpec((1, V7X_LANES), lambda i: (0, 0))],
        out_shape=[jax.ShapeDtypeStruct((t, V7X_LANES), jnp.int32),
                   jax.ShapeDtypeStruct((t, V7X_LANES), F32),
                   jax.ShapeDtypeStruct((t, V7X_LANES), jnp.int32),
                   jax.ShapeDtypeStruct((1, V7X_LANES), F32)],
        scratch_shapes=[pltpu.VMEM((1, V7X_LANES), F32)],
        compiler_params=_params(("arbitrary",)),
        name="router",
    )(h, w_hi, w_lo, b_pad, ltri)


DMA_UNROLL = 8
WEIGHT_DMA_PRIORITY = 1
(META_EXPERT, META_VALID, META_FIRST, META_LAST_GROUP, META_NEXT_EXPERT, META_GROUP,
 META_ROWS_USED, META_NUM_GROUPS) = range(8)


def _pack_bf16_pairs(x):
    half = x.shape[1] // 2
    lo = lax.bitcast_convert_type(x[:, :half].astype(BF16).astype(F32), jnp.uint32)
    hi = lax.bitcast_convert_type(x[:, half:].astype(BF16).astype(F32), jnp.uint32)
    return (lo >> 16) | (hi & jnp.uint32(0xFFFF0000))


def _unpack_bf16_pairs(words):
    lo = lax.bitcast_convert_type(words << 16, F32).astype(BF16)
    hi = lax.bitcast_convert_type(words & jnp.uint32(0xFFFF0000), F32).astype(BF16)
    return jnp.concatenate([lo, hi], axis=1)


def _dispatch_kernel(dest_ref, meta_ref, h_ref, xs_hbm, pack_ref, zero_ref, sem, zsem):
    tm = h_ref.shape[0]
    base = pl.program_id(0) * tm
    et = zero_ref.shape[0]
    pack_ref[...] = _pack_bf16_pairs(h_ref[...])

    @pl.when(pl.program_id(0) == 0)
    def _():
        zero_ref[...] = jnp.zeros_like(zero_ref)

        def fill_copy(tile):
            start_row = pl.multiple_of(tile * et, et)
            return pltpu.make_async_copy(zero_ref, xs_hbm.at[pl.ds(start_row, et)], zsem)

        def fill_start(tile, c):
            @pl.when(meta_ref[META_ROWS_USED, tile] < et)
            def _():
                fill_copy(tile).start()
            return c

        def fill_wait(tile, c):
            @pl.when(meta_ref[META_ROWS_USED, tile] < et)
            def _():
                fill_copy(tile).wait()
            return c

        n_tiles = xs_hbm.shape[0] // et
        lax.fori_loop(0, n_tiles, fill_start, 0)
        lax.fori_loop(0, n_tiles, fill_wait, 0)

    def row_copy(r, k):
        return pltpu.make_async_copy(pack_ref.at[pl.ds(r, 1)],
                                     xs_hbm.at[pl.ds(dest_ref[(base + r) * TOP_K + k], 1)], sem)

    def start(r, c):
        for k in range(TOP_K):
            row_copy(r, k).start()
        return c

    def wait(r, c):
        for k in range(TOP_K):
            row_copy(r, k).wait()
        return c

    lax.fori_loop(0, tm, start, 0, unroll=DMA_UNROLL)
    lax.fori_loop(0, tm, wait, 0, unroll=DMA_UNROLL)


def _dispatch(h, dest, meta, n_rows):
    t, d = h.shape
    tm = min(ATT_TILE, t)
    grid_spec = pltpu.PrefetchScalarGridSpec(
        num_scalar_prefetch=2,
        grid=(t // tm,),
        in_specs=[pl.BlockSpec((tm, d), lambda i, dst, m: (i, 0))],
        out_specs=pl.BlockSpec(memory_space=pl.ANY),
        scratch_shapes=[pltpu.VMEM((tm, d // 2), jnp.uint32),
                        pltpu.VMEM((EXPERT_TILE, d // 2), jnp.uint32),
                        pltpu.SemaphoreType.DMA(()), pltpu.SemaphoreType.DMA(())],
    )
    return pl.pallas_call(
        _dispatch_kernel,
        grid_spec=grid_spec,
        out_shape=jax.ShapeDtypeStruct((n_rows, d // 2), jnp.uint32),
        compiler_params=_params(("arbitrary",)),
        name="moe_dispatch",
    )(dest, meta, h)


def _stream_expert_weights(meta_ref, j, t, nj, make_copies, on_arrival):
    @pl.when((j == 0) & (t == 0))
    def _():
        for c in make_copies(meta_ref[META_EXPERT, 0], 0, 0):
            c.start(priority=WEIGHT_DMA_PRIORITY)

    @pl.when(meta_ref[META_FIRST, t] == 1)
    def _():
        slot = (j * meta_ref[META_NUM_GROUPS, t] + meta_ref[META_GROUP, t]) & 1
        for c in make_copies(meta_ref[META_EXPERT, t], j, slot):
            c.wait()
        last_group = meta_ref[META_LAST_GROUP, t] == 1

        @pl.when(jnp.logical_not(last_group & (j == nj - 1)))
        def _():
            next_chunk = jnp.where(last_group, j + 1, j)
            for c in make_copies(meta_ref[META_NEXT_EXPERT, t], next_chunk, 1 - slot):
                c.start(priority=WEIGHT_DMA_PRIORITY)

        on_arrival(slot)


def _moe_up_kernel(meta_ref, x_ref, w_hbm, bg_ref, bl_ref, o_ref, wf32, wbf, sems, *, tn, nj):
    j = pl.program_id(0)
    t = pl.program_id(1)

    def make_copies(expert, chunk, slot):
        glu_col = pl.multiple_of(chunk * tn, tn)
        lin_col = pl.multiple_of((nj + chunk) * tn, tn)
        return (pltpu.make_async_copy(w_hbm.at[expert, :, pl.ds(glu_col, tn)],
                                      wf32.at[slot, 0], sems.at[slot, 0]),
                pltpu.make_async_copy(w_hbm.at[expert, :, pl.ds(lin_col, tn)],
                                      wf32.at[slot, 1], sems.at[slot, 1]))

    def on_arrival(slot):
        wbf[0] = wf32[slot, 0].astype(BF16)
        wbf[1] = wf32[slot, 1].astype(BF16)

    _stream_expert_weights(meta_ref, j, t, nj, make_copies, on_arrival)

    @pl.when(meta_ref[META_VALID, t] > 0)
    def _():
        x = _unpack_bf16_pairs(x_ref[...])
        glu = jnp.dot(x, wbf[0], preferred_element_type=F32) + bg_ref[...]
        lin = jnp.dot(x, wbf[1], preferred_element_type=F32) + bl_ref[...]
        glu = jnp.minimum(glu, SWIGLU_LIMIT)
        lin = jnp.clip(lin, -SWIGLU_LIMIT, SWIGLU_LIMIT)
        act = glu * jax.nn.sigmoid(SWIGLU_ALPHA * glu) * (lin + 1.0)
        o_ref[...] = act.astype(o_ref.dtype)

    @pl.when(meta_ref[META_VALID, t] == 0)
    def _():
        o_ref[...] = jnp.zeros_like(o_ref)


def _moe_up(xs, w_up, b_up, meta):
    r = xs.shape[0]
    e, d, f2 = w_up.shape
    f = f2 // 2
    tm = EXPERT_TILE
    tn = _pick_tile(f, 1024)
    nj = f // tn
    grid_spec = pltpu.PrefetchScalarGridSpec(
        num_scalar_prefetch=1,
        grid=(nj, r // tm),
        in_specs=[
            pl.BlockSpec((tm, d // 2), lambda j, t, m: (t, 0)),
            pl.BlockSpec(memory_space=pl.ANY),
            pl.BlockSpec((None, 1, tn), lambda j, t, m: (m[META_EXPERT, t], 0, j)),
            pl.BlockSpec((None, 1, tn), lambda j, t, m: (m[META_EXPERT, t], 0, nj + j)),
        ],
        out_specs=pl.BlockSpec((tm, tn), lambda j, t, m: (t, j)),
        scratch_shapes=[pltpu.VMEM((2, 2, d, tn), F32), pltpu.VMEM((2, d, tn), BF16),
                        pltpu.SemaphoreType.DMA((2, 2))],
    )
    b3 = b_up.reshape(e, 1, f2)
    return pl.pallas_call(
        functools.partial(_moe_up_kernel, tn=tn, nj=nj),
        grid_spec=grid_spec,
        out_shape=jax.ShapeDtypeStruct((r, f), BF16),
        compiler_params=_params(("arbitrary", "arbitrary")),
        name="moe_up",
    )(meta, xs, w_up, b3, b3)


def _moe_down_kernel(meta_ref, a_ref, w_hbm, b_ref, o_ref, wf32, wbf, sems, *, tn, nj):
    j = pl.program_id(0)
    t = pl.program_id(1)

    def make_copies(expert, chunk, slot):
        col = pl.multiple_of(chunk * tn, tn)
        return (pltpu.make_async_copy(w_hbm.at[expert, :, pl.ds(col, tn)],
                                      wf32.at[slot], sems.at[slot]),)

    def on_arrival(slot):
        wbf[...] = wf32[slot].astype(BF16)

    _stream_expert_weights(meta_ref, j, t, nj, make_copies, on_arrival)

    @pl.when(meta_ref[META_VALID, t] > 0)
    def _():
        o_ref[...] = jnp.dot(a_ref[...], wbf[...], preferred_element_type=F32) + b_ref[...]

    @pl.when(meta_ref[META_VALID, t] == 0)
    def _():
        o_ref[...] = jnp.zeros_like(o_ref)


def _moe_down(act, w_down, b_down, meta):
    r, f = act.shape
    e, _, d = w_down.shape
    tm = EXPERT_TILE
    tn = _pick_tile(d, 2048)
    nj = d // tn
    grid_spec = pltpu.PrefetchScalarGridSpec(
        num_scalar_prefetch=1,
        grid=(nj, r // tm),
        in_specs=[
            pl.BlockSpec((tm, f), lambda j, t, m: (t, 0)),
            pl.BlockSpec(memory_space=pl.ANY),
            pl.BlockSpec((None, 1, tn), lambda j, t, m: (m[META_EXPERT, t], 0, j)),
        ],
        out_specs=pl.BlockSpec((tm, tn), lambda j, t, m: (t, j)),
        scratch_shapes=[pltpu.VMEM((2, f, tn), F32), pltpu.VMEM((f, tn), BF16),
                        pltpu.SemaphoreType.DMA((2,))],
    )
    return pl.pallas_call(
        functools.partial(_moe_down_kernel, tn=tn, nj=nj),
        grid_spec=grid_spec,
        out_shape=jax.ShapeDtypeStruct((r, d), F32),
        compiler_params=_params(("arbitrary", "arbitrary")),
        name="moe_down",
    )(meta, act, w_down, b_down.reshape(e, 1, d))


def _combine_kernel(pos_ref, y_hbm, gate_ref, h_ref, g_ref, b_ref, o_ref, buf_ref, sems):
    tm = o_ref.shape[0]
    i = pl.program_id(0)

    def row_copy(step, r, k, slot):
        src_row = pos_ref[(step * tm + r) * TOP_K + k]
        return pltpu.make_async_copy(y_hbm.at[pl.ds(src_row, 1)],
                                     buf_ref.at[slot, k, pl.ds(r, 1)], sems.at[slot])

    def gather_start(step, slot):
        def body(r, c):
            for k in range(TOP_K):
                row_copy(step, r, k, slot).start()
            return c
        lax.fori_loop(0, tm, body, 0, unroll=DMA_UNROLL)

    def gather_wait(step, slot):
        def body(r, c):
            for k in range(TOP_K):
                row_copy(step, r, k, slot).wait()
            return c
        lax.fori_loop(0, tm, body, 0, unroll=DMA_UNROLL)

    @pl.when(i == 0)
    def _():
        gather_start(0, 0)

    @pl.when(i + 1 < pl.num_programs(0))
    def _():
        gather_start(i + 1, (i + 1) & 1)

    slot = i & 1
    gather_wait(i, slot)
    gates = gate_ref[...]
    ff = gates[:, 0:1] * buf_ref[slot, 0]
    for k in range(1, TOP_K):
        ff = ff + gates[:, k:k + 1] * buf_ref[slot, k]
    o_ref[...] = _layer_norm(DEEPNORM_ALPHA * h_ref[...] + ff, g_ref[...], b_ref[...])


def _combine(y_rows, pos, gates, h, g, b):
    t, d = h.shape
    tm = min(COMBINE_TILE, t)
    grid_spec = pltpu.PrefetchScalarGridSpec(
        num_scalar_prefetch=1,
        grid=(t // tm,),
        in_specs=[pl.BlockSpec(memory_space=pl.ANY),
                  pl.BlockSpec((tm, V7X_LANES), lambda i, p: (i, 0)),
                  pl.BlockSpec((tm, d), lambda i, p: (i, 0)),
                  pl.BlockSpec((1, d), lambda i, p: (0, 0)),
                  pl.BlockSpec((1, d), lambda i, p: (0, 0))],
        out_specs=pl.BlockSpec((tm, d), lambda i, p: (i, 0)),
        scratch_shapes=[pltpu.VMEM((2, TOP_K, tm, d), F32), pltpu.SemaphoreType.DMA((2,))],
    )
    return pl.pallas_call(
        _combine_kernel,
        grid_spec=grid_spec,
        out_shape=jax.ShapeDtypeStruct((t, d), F32),
        compiler_params=_params(("arbitrary",)),
        name="moe_combine",
    )(pos, y_rows, gates, h, g.reshape(1, d), b.reshape(1, d))


def _expert_layout(idx, rank, counts, n_tiles, tm):
    experts = jnp.arange(N_EXPERTS, dtype=jnp.int32)
    tiles_per = (counts + tm - 1) // tm
    tile_end = jnp.cumsum(tiles_per).astype(jnp.int32)
    tile_off = tile_end - tiles_per
    dest = jnp.sum(jnp.where(idx[..., None] == experts, tile_off * tm, 0), axis=-1) + rank

    nonempty = counts > 0
    first_e = jnp.min(jnp.where(nonempty, experts, N_EXPERTS))
    last_e = jnp.max(jnp.where(nonempty, experts, 0))
    group_of = jnp.cumsum(nonempty.astype(jnp.int32)) - 1
    later = (experts[None, :] > experts[:, None]) & nonempty[None, :]
    next_e = jnp.min(jnp.where(later, experts[None, :], N_EXPERTS), axis=1)
    next_e = jnp.where(next_e == N_EXPERTS, first_e, next_e)

    tid = jnp.arange(n_tiles, dtype=jnp.int32)
    valid = tid < tile_end[-1]
    te = jnp.sum((tid[:, None] >= tile_end[None, :]).astype(jnp.int32), axis=1)
    te = jnp.where(valid, jnp.minimum(te, N_EXPERTS - 1), last_e)
    onehot = te[:, None] == experts[None, :]

    def lookup(table):
        return jnp.sum(jnp.where(onehot, table[None, :], 0), axis=1)

    local = tid - lookup(tile_off)
    rows_used = jnp.where(valid, jnp.clip(lookup(counts) - local * tm, 0, tm), 0)
    meta = jnp.stack([
        te,
        valid.astype(jnp.int32),
        (valid & (local == 0)).astype(jnp.int32),
        (te == last_e).astype(jnp.int32),
        lookup(next_e),
        lookup(group_of),
        rows_used,
        jnp.broadcast_to(jnp.sum(nonempty.astype(jnp.int32)), (n_tiles,)),
    ]).astype(jnp.int32)
    return dest.reshape(-1).astype(jnp.int32), meta


def _moe(h, w_router, b_router, w_up, b_up, w_down, b_down, g, b):
    t, d = h.shape
    tm = EXPERT_TILE
    idx_p, gate_p, rank_p, cnt_p = _router(h, w_router, b_router)
    n_rows = t * TOP_K + N_EXPERTS * tm
    dest, meta = _expert_layout(idx_p[:, :TOP_K], rank_p[:, :TOP_K],
                                cnt_p[0, :N_EXPERTS].astype(jnp.int32), n_rows // tm, tm)
    xs = _dispatch(h, dest, meta, n_rows)
    act = _moe_up(xs, w_up, b_up, meta)
    y_rows = _moe_down(act, w_down, b_down, meta)
    return _combine(y_rows, dest, gate_p, h, g, b)


def kernel(x, mem, ln_in_g, ln_in_b, rel_bias, w_in, b_in, attn_sinks, w_a_out, w_b_out,
           w_mix_out, ln1_g, ln1_b, w_xq, w_xkv, w_xo, ln2_g, ln2_b, w_router, b_router,
           w_up, b_up, w_down, b_down, ln3_g, ln3_b):
    batch, seq, d = x.shape
    mem_len = mem.shape[1]
    t = batch * seq
    a_cols = A_WIDTH + 2 * A_KV_WIDTH
    b_cols = 3 * B_WIDTH

    h, hb = _ln_in(x.reshape(t, d), ln_in_g, ln_in_b)
    for l in range(DEPTH):
        g_col = a_cols + b_cols
        qkv_a = _mm_bias(hb, w_in[l], b_in[l, :a_cols], 0, a_cols, BF16, "in_proj_a")
        qkv_b = _mm_bias(hb, w_in[l], b_in[l, a_cols:g_col], a_cols, b_cols, BF16, "in_proj_b")
        gates = _mm_bias(hb, w_in[l], b_in[l, g_col:], g_col, 2 * d, F32, "in_proj_gates")
        ya = _swa(qkv_a, attn_sinks[l], rel_bias, batch, seq)
        yb = _stick_breaking(qkv_b, batch, seq)
        h, hb = _mix(ya, yb, gates, h, w_a_out[l].astype(BF16), w_b_out[l].astype(BF16),
                     w_mix_out[l].astype(BF16), ln1_g[l], ln1_b[l])
        kv = _mm_bias(mem.reshape(batch * mem_len, d).astype(BF16), w_xkv[l],
                      jnp.zeros((2 * d,), F32), 0, 2 * d, BF16, "mem_kv")
        h = _xattn(hb, h, kv, w_xq[l].astype(BF16), w_xo[l].astype(BF16), ln2_g[l], ln2_b[l],
                   batch, seq, mem_len)
        h = _moe(h, w_router[l], b_router[l], w_up[l], b_up[l], w_down[l], b_down[l],
                 ln3_g[l], ln3_b[l])
        if l + 1 < DEPTH:
            hb = h.astype(BF16)
    return h.reshape(batch, seq, d)
```

```python
import functools

import jax
import jax.numpy as jnp
import numpy as np
from jax import lax
from jax.experimental import pallas as pl
from jax.experimental.pallas import tpu as pltpu

DEPTH = 1
CHUNK = 64
BLK = 128
HEAD_DIM = 64
A_HEADS = 16
A_KV_HEADS = 2
A_REP = A_HEADS // A_KV_HEADS
WIN_CHUNKS = 2
B_HEADS = 16
A_WIDTH = A_HEADS * HEAD_DIM
A_KV_WIDTH = A_KV_HEADS * HEAD_DIM
B_WIDTH = B_HEADS * HEAD_DIM
REL_BUCKETS = 32
REL_MAX_DIST = 128
X_HEADS = 4
N_EXPERTS = 32
TOP_K = 4
SWIGLU_LIMIT = 7.0
SWIGLU_ALPHA = 1.702
LN_EPS = 1e-5
DEEPNORM_ALPHA = (2.0 * DEPTH) ** 0.25
QK_SCALE = HEAD_DIM ** -0.5

V7X_LANES = 128
V7X_VMEM_LIMIT_BYTES = 56 * 1024 * 1024

ROW_TILE = 512
MM_ROW_TILE = 1024
ATT_TILE = 256
SB_TILE = 256
SB_PAIRS = 2
SB_ROWS = 32
SB_UNROLL = True
EXPERT_TILE = 256
COMBINE_TILE = 128
SB_LOG_ZERO = -105.0

F32 = jnp.float32
BF16 = jnp.bfloat16
NT_DIMS = (((1,), (1,)), ((), ()))


def _params(semantics):
    return pltpu.CompilerParams(dimension_semantics=semantics,
                                vmem_limit_bytes=V7X_VMEM_LIMIT_BYTES)


def _const_spec(shape):
    nd = len(shape)
    return pl.BlockSpec(shape, lambda *_: (0,) * nd, pipeline_mode=pl.Buffered(1))


def _layer_norm(x, g, b):
    mu = jnp.mean(x, axis=-1, keepdims=True)
    xc = x - mu
    var = jnp.mean(xc * xc, axis=-1, keepdims=True)
    return xc * lax.rsqrt(var + LN_EPS) * g + b


def _ln_in_kernel(x_ref, g_ref, b_ref, h_ref, hb_ref):
    y = _layer_norm(x_ref[...], g_ref[...], b_ref[...])
    h_ref[...] = y
    hb_ref[...] = y.astype(BF16)


def _ln_in(x2d, g, b):
    t, d = x2d.shape
    tm = min(ROW_TILE, t)
    row = pl.BlockSpec((tm, d), lambda i: (i, 0))
    return pl.pallas_call(
        _ln_in_kernel,
        grid=(t // tm,),
        in_specs=[row, _const_spec((1, d)), _const_spec((1, d))],
        out_specs=[row, row],
        out_shape=[jax.ShapeDtypeStruct((t, d), F32), jax.ShapeDtypeStruct((t, d), BF16)],
        compiler_params=_params(("arbitrary",)),
        name="ln_in",
    )(x2d, g.reshape(1, d), b.reshape(1, d))


def _mm_bias_kernel(a_ref, w_hbm, b_ref, o_ref, stage, wbf, sem, *, col0, tn, nj):
    j = pl.program_id(0)
    i = pl.program_id(1)

    def slab_copy(jj):
        col = pl.multiple_of(col0 + jj * tn, V7X_LANES)
        return pltpu.make_async_copy(w_hbm.at[:, pl.ds(col, tn)], stage, sem)

    @pl.when(i == 0)
    def _():
        @pl.when(j == 0)
        def _():
            slab_copy(0).start()

        slab_copy(j).wait()
        wbf[...] = stage[...].astype(BF16)

        @pl.when(j + 1 < nj)
        def _():
            slab_copy(j + 1).start()

    acc = jnp.dot(a_ref[...], wbf[...], preferred_element_type=F32)
    o_ref[...] = (acc + b_ref[...]).astype(o_ref.dtype)


def _pick_tile(n, target):
    best = None
    for c in range(V7X_LANES, min(n, target) + 1, V7X_LANES):
        if n % c == 0:
            best = c
    return n if best is None else best


def _mm_bias(a, w, b, col0, n, out_dtype, name):
    m, k = a.shape
    tm = min(MM_ROW_TILE, m)
    tn = _pick_tile(n, 1024)
    nj = n // tn
    return pl.pallas_call(
        functools.partial(_mm_bias_kernel, col0=col0, tn=tn, nj=nj),
        grid=(nj, m // tm),
        in_specs=[pl.BlockSpec((tm, k), lambda j, i: (i, 0)),
                  pl.BlockSpec(memory_space=pl.ANY),
                  pl.BlockSpec((1, tn), lambda j, i: (0, j))],
        out_specs=pl.BlockSpec((tm, tn), lambda j, i: (i, j)),
        out_shape=jax.ShapeDtypeStruct((m, n), out_dtype),
        scratch_shapes=[pltpu.VMEM((k, tn), F32), pltpu.VMEM((k, tn), BF16),
                        pltpu.SemaphoreType.DMA(())],
        compiler_params=_params(("arbitrary", "arbitrary")),
        name=name,
    )(a, w, b.reshape(1, n).astype(F32))


def _swa_kernel(sink_ref, q_ref, kp_ref, kc_ref, vp_ref, vc_ref, bias_ref, o_ref):
    n = pl.program_id(1)
    kk = jnp.concatenate([kp_ref[...], kc_ref[...]], axis=0)
    vv = jnp.concatenate([vp_ref[...], vc_ref[...]], axis=0)
    col = lax.broadcasted_iota(jnp.int32, (BLK, 2 * BLK), 1)
    exists = (col >= BLK) | (n > 0)
    for h in range(A_HEADS):
        g = h // A_REP
        qh = q_ref[:, h * HEAD_DIM:(h + 1) * HEAD_DIM] * QK_SCALE
        kg = kk[:, g * HEAD_DIM:(g + 1) * HEAD_DIM]
        vg = vv[:, g * HEAD_DIM:(g + 1) * HEAD_DIM]
        s = lax.dot_general(qh, kg, NT_DIMS, preferred_element_type=F32)
        s = jnp.where(exists, s + bias_ref[h], -jnp.inf)
        sink = sink_ref[h]
        m = jnp.maximum(jnp.max(s, axis=-1, keepdims=True), sink)
        p = jnp.exp(s - m)
        denom = jnp.sum(p, axis=-1, keepdims=True) + jnp.exp(sink - m)
        o = jnp.dot(p.astype(BF16), vg, preferred_element_type=F32) / denom
        o_ref[:, h * HEAD_DIM:(h + 1) * HEAD_DIM] = o.astype(o_ref.dtype)


def _rel_bucket(rel):
    half = REL_BUCKETS // 2
    max_exact = half // 2
    base = jnp.where(rel > 0, half, 0)
    n = jnp.abs(rel)
    nf = jnp.maximum(n, 1).astype(F32)
    large = max_exact + (jnp.log(nf / max_exact) / np.log(REL_MAX_DIST / max_exact)
                         * (half - max_exact)).astype(jnp.int32)
    large = jnp.minimum(large, half - 1)
    return base + jnp.where(n < max_exact, n, large)


def _swa_bias_table(rel_bias):
    rel = jnp.arange(-(2 * BLK - 1), BLK)
    line = jnp.transpose(rel_bias[_rel_bucket(rel)]).astype(F32)
    bias = jnp.stack([line[:, BLK - 1 - q:3 * BLK - 1 - q] for q in range(BLK)], axis=1)
    qi = np.arange(BLK)[:, None]
    kj = np.arange(2 * BLK)[None, :]
    dchunk = (kj // CHUNK - BLK // CHUNK) - qi // CHUNK
    band = (dchunk <= 0) & (dchunk >= -WIN_CHUNKS)
    return jnp.where(jnp.asarray(band)[None], bias, -jnp.inf)


def _swa(qkv, sinks, rel_bias, batch, seq):
    nb = seq // BLK
    kcol = A_WIDTH // A_KV_WIDTH
    vcol = kcol + 1
    kv_blk = (BLK, A_KV_WIDTH)
    grid_spec = pltpu.PrefetchScalarGridSpec(
        num_scalar_prefetch=1,
        grid=(batch, nb),
        in_specs=[
            pl.BlockSpec((BLK, A_WIDTH), lambda b, n, s: (b * nb + n, 0)),
            pl.BlockSpec(kv_blk, lambda b, n, s: (b * nb + jnp.maximum(n - 1, 0), kcol)),
            pl.BlockSpec(kv_blk, lambda b, n, s: (b * nb + n, kcol)),
            pl.BlockSpec(kv_blk, lambda b, n, s: (b * nb + jnp.maximum(n - 1, 0), vcol)),
            pl.BlockSpec(kv_blk, lambda b, n, s: (b * nb + n, vcol)),
            pl.BlockSpec((A_HEADS, BLK, 2 * BLK), lambda b, n, s: (0, 0, 0),
                         pipeline_mode=pl.Buffered(1)),
        ],
        out_specs=pl.BlockSpec((BLK, A_WIDTH), lambda b, n, s: (b * nb + n, 0)),
    )
    return pl.pallas_call(
        _swa_kernel,
        grid_spec=grid_spec,
        out_shape=jax.ShapeDtypeStruct((batch * seq, A_WIDTH), BF16),
        compiler_params=_params(("arbitrary", "arbitrary")),
        name="swa_attn",
    )(sinks.astype(F32), qkv, qkv, qkv, qkv, qkv, _swa_bias_table(rel_bias))


def _sb_kernel(q_ref, k_ref, v_ref, tri_ref, o_ref, qh_scr, z_scr, hl_scr, s_scr, w_scr,
               carry_scr, acc_scr):
    t = q_ref.shape[0]
    i = pl.program_id(2)
    n_chunks = t // SB_ROWS
    n_heads = 2 * SB_PAIRS
    pair_w = 2 * HEAD_DIM
    lane = lax.broadcasted_iota(jnp.int32, (1, pair_w), 1)
    first = lane < HEAD_DIM
    zero = jnp.zeros((), BF16)

    def pair_cols(h):
        return slice((h // 2) * pair_w, (h // 2 + 1) * pair_w)

    def own_half(h, x):
        return jnp.where(first, x, zero) if h % 2 == 0 else jnp.where(first, zero, x)

    q2 = q_ref[...] * QK_SCALE
    for h in range(n_heads):
        qh_scr[h] = own_half(h, q2[:, pair_cols(h)])
    carry_scr[...] = jnp.zeros_like(carry_scr)
    acc_scr[...] = jnp.zeros_like(acc_scr)
    row = lax.broadcasted_iota(jnp.int32, (SB_ROWS, t), 0)
    col = lax.broadcasted_iota(jnp.int32, (SB_ROWS, t), 1)

    def key_tile(j, diagonal):
        start = pl.multiple_of(j * t, t)
        k2 = k_ref[pl.ds(start, t), :]
        v2 = v_ref[pl.ds(start, t), :]
        for h in range(n_heads):
            z_scr[h] = lax.dot_general(qh_scr[h], k2[:, pair_cols(h)], NT_DIMS,
                                       preferred_element_type=F32)

        def keep_chunk(c, carry):
            r0 = pl.multiple_of(c * SB_ROWS, SB_ROWS)
            rows = pl.ds(r0, SB_ROWS)
            for h in range(n_heads):
                z = z_scr[h, rows, :]
                keep = jnp.minimum(-z, 0.0) - jnp.log(1.0 + jnp.exp(-jnp.abs(z)))
                if diagonal:
                    keep = jnp.where(col < row + r0, keep, 0.0)
                hi = keep.astype(BF16)
                hl_scr[h, rows, 0:t] = hi
                hl_scr[h, rows, t:2 * t] = (keep - hi.astype(F32)).astype(BF16)
            return carry

        lax.fori_loop(0, n_chunks, keep_chunk, 0, unroll=SB_UNROLL)
        tri = tri_ref[...]
        for h in range(n_heads):
            s_scr[h] = jnp.dot(hl_scr[h], tri, preferred_element_type=F32)

        def weight_chunk(c, top):
            r0 = pl.multiple_of(c * SB_ROWS, SB_ROWS)
            rows = pl.ds(r0, SB_ROWS)
            for h in range(n_heads):
                carry = carry_scr[h, rows, :]
                log_w = (z_scr[h, rows, :] + s_scr[h, rows, :]
                         + jnp.concatenate([carry] * (t // V7X_LANES), axis=1))
                w = jnp.exp(log_w)
                if diagonal:
                    w = jnp.where(col < row + r0, w, 0.0)
                w_scr[h // 2, rows, (h % 2) * t:(h % 2 + 1) * t] = w.astype(BF16)
                carry = carry + jnp.broadcast_to(s_scr[h, rows, 0:1], (SB_ROWS, V7X_LANES))
                carry_scr[h, rows, :] = carry
                top = jnp.maximum(top, carry)
            return top

        top = lax.fori_loop(0, n_chunks, weight_chunk,
                            jnp.full((SB_ROWS, V7X_LANES), -jnp.inf, F32), unroll=SB_UNROLL)
        for p in range(SB_PAIRS):
            cols = pair_cols(2 * p)
            v_pair = jnp.concatenate([own_half(0, v2[:, cols]), own_half(1, v2[:, cols])], axis=0)
            acc_scr[:, cols] += jnp.dot(w_scr[p], v_pair, preferred_element_type=F32)
        return (jnp.max(top) > SB_LOG_ZERO).astype(jnp.int32)

    live = key_tile(i, True)

    def cond(state):
        return (state[0] >= 0) & (state[1] > 0)

    def body(state):
        return state[0] - 1, key_tile(state[0], False)

    lax.while_loop(cond, body, (i - 1, live))
    o_ref[...] = acc_scr[...].astype(o_ref.dtype)


def _stick_breaking(qkv, batch, seq):
    t = min(SB_TILE, seq)
    nq = seq // t
    width = SB_PAIRS * 2 * HEAD_DIM
    groups = B_WIDTH // width
    heads = 2 * SB_PAIRS
    j = np.arange(2 * t)[:, None] % t
    s = np.arange(t)[None, :]
    tri = jnp.asarray((j >= s).astype(np.float32), dtype=BF16)
    return pl.pallas_call(
        _sb_kernel,
        grid=(batch, groups, nq),
        in_specs=[
            pl.BlockSpec((t, width), lambda b, p, i: (b * nq + i, p)),
            pl.BlockSpec((seq, width), lambda b, p, i: (b, groups + p)),
            pl.BlockSpec((seq, width), lambda b, p, i: (b, 2 * groups + p)),
            _const_spec((2 * t, t)),
        ],
        out_specs=pl.BlockSpec((t, width), lambda b, p, i: (b * nq + i, p)),
        out_shape=jax.ShapeDtypeStruct((batch * seq, B_WIDTH), BF16),
        scratch_shapes=[
            pltpu.VMEM((heads, t, 2 * HEAD_DIM), BF16),
            pltpu.VMEM((heads, t, t), F32),
            pltpu.VMEM((heads, t, 2 * t), BF16),
            pltpu.VMEM((heads, t, t), F32),
            pltpu.VMEM((SB_PAIRS, t, 2 * t), BF16),
            pltpu.VMEM((heads, t, V7X_LANES), F32),
            pltpu.VMEM((t, width), F32),
        ],
        compiler_params=_params(("arbitrary", "arbitrary", "arbitrary")),
        name="stick_breaking",
    )(qkv, qkv, qkv, tri)


def _mix_kernel(ya_ref, yb_ref, gates_a_ref, gates_b_ref, h_ref, wa_ref, wb_ref, wm_ref,
                g_ref, b_ref, o_ref, ob_ref):
    pa = jnp.dot(ya_ref[...], wa_ref[...], preferred_element_type=F32)
    pb = jnp.dot(yb_ref[...], wb_ref[...], preferred_element_type=F32)
    merged = jax.nn.sigmoid(gates_a_ref[...]) * pa + jax.nn.sigmoid(gates_b_ref[...]) * pb
    mixed = jnp.dot(merged.astype(BF16), wm_ref[...], preferred_element_type=F32)
    y = _layer_norm(DEEPNORM_ALPHA * h_ref[...] + mixed, g_ref[...], b_ref[...])
    o_ref[...] = y
    ob_ref[...] = y.astype(BF16)


def _mix(ya, yb, gates, h, wa, wb, wm, g, b):
    t, d = h.shape
    tm = min(ATT_TILE, t)
    row = pl.BlockSpec((tm, d), lambda i: (i, 0))
    return pl.pallas_call(
        _mix_kernel,
        grid=(t // tm,),
        in_specs=[pl.BlockSpec((tm, A_WIDTH), lambda i: (i, 0)),
                  pl.BlockSpec((tm, B_WIDTH), lambda i: (i, 0)),
                  pl.BlockSpec((tm, d), lambda i: (i, 0)),
                  pl.BlockSpec((tm, d), lambda i: (i, 1)),
                  row,
                  _const_spec((A_WIDTH, d)), _const_spec((B_WIDTH, d)), _const_spec((d, d)),
                  _const_spec((1, d)), _const_spec((1, d))],
        out_specs=[row, row],
        out_shape=[jax.ShapeDtypeStruct((t, d), F32), jax.ShapeDtypeStruct((t, d), BF16)],
        compiler_params=_params(("arbitrary",)),
        name="mix_out",
    )(ya, yb, gates, gates, h, wa, wb, wm, g.reshape(1, d), b.reshape(1, d))


def _xattn_kernel(hb_ref, h_ref, k_ref, v_ref, wq_ref, wo_ref, g_ref, b_ref, o_ref):
    d = h_ref.shape[1]
    hd = d // X_HEADS
    scale = hd ** -0.5
    q = jnp.dot(hb_ref[...], wq_ref[...], preferred_element_type=F32).astype(BF16)
    outs = []
    for x in range(X_HEADS):
        sl = slice(x * hd, (x + 1) * hd)
        s = lax.dot_general(q[:, sl], k_ref[:, sl], NT_DIMS, preferred_element_type=F32) * scale
        m = jnp.max(s, axis=-1, keepdims=True)
        p = jnp.exp(s - m)
        denom = jnp.sum(p, axis=-1, keepdims=True)
        o = jnp.dot(p.astype(BF16), v_ref[:, sl], preferred_element_type=F32) / denom
        outs.append(o.astype(BF16))
    xo = jnp.dot(jnp.concatenate(outs, axis=1), wo_ref[...], preferred_element_type=F32)
    o_ref[...] = _layer_norm(DEEPNORM_ALPHA * h_ref[...] + xo, g_ref[...], b_ref[...])


def _xattn(hb, h, kv, wq, wo, g, b, batch, seq, mem_len):
    t, d = h.shape
    tm = min(ATT_TILE, seq)
    ns = seq // tm
    row = pl.BlockSpec((tm, d), lambda bi, i: (bi * ns + i, 0))
    return pl.pallas_call(
        _xattn_kernel,
        grid=(batch, ns),
        in_specs=[row, row,
                  pl.BlockSpec((mem_len, d), lambda bi, i: (bi, 0)),
                  pl.BlockSpec((mem_len, d), lambda bi, i: (bi, 1)),
                  _const_spec((d, d)), _const_spec((d, d)),
                  _const_spec((1, d)), _const_spec((1, d))],
        out_specs=row,
        out_shape=jax.ShapeDtypeStruct((t, d), F32),
        compiler_params=_params(("arbitrary", "arbitrary")),
        name="xattn",
    )(hb, h, kv, kv, wq, wo, g.reshape(1, d), b.reshape(1, d))


def _router_kernel(h_ref, whi_ref, wlo_ref, b_ref, ltri_ref, idx_ref, gate_ref, rank_ref,
                   cnt_ref, run_ref):
    @pl.when(pl.program_id(0) == 0)
    def _():
        run_ref[...] = jnp.zeros_like(run_ref)

    h = h_ref[...]
    h_hi = h.astype(BF16)
    h_lo = (h - h_hi.astype(F32)).astype(BF16)
    w_hi = whi_ref[...]
    logits = (jnp.dot(h_hi, w_hi, preferred_element_type=F32)
              + (jnp.dot(h_lo, w_hi, preferred_element_type=F32)
                 + jnp.dot(h_hi, wlo_ref[...], preferred_element_type=F32))
              + b_ref[...])
    tm = logits.shape[0]
    lane_i = lax.broadcasted_iota(jnp.int32, (tm, V7X_LANES), 1)
    lane = lane_i.astype(F32)
    rem = jnp.where(lane_i < N_EXPERTS, logits, -jnp.inf)
    vals, idxs = [], []
    for _ in range(TOP_K):
        m = jnp.max(rem, axis=-1, keepdims=True)
        ik = jnp.min(jnp.where(rem == m, lane, float(V7X_LANES)), axis=-1, keepdims=True)
        vals.append(m)
        idxs.append(ik)
        rem = jnp.where(lane == ik, -jnp.inf, rem)
    exps = [jnp.exp(v - vals[0]) for v in vals]
    denom = exps[0] + exps[1] + exps[2] + exps[3]
    sel = jnp.zeros((tm, V7X_LANES), F32)
    for ik in idxs:
        sel = sel + (lane == ik).astype(F32)
    before = jnp.dot(ltri_ref[...], sel.astype(BF16), preferred_element_type=F32) + run_ref[...]
    idx_out = jnp.zeros((tm, V7X_LANES), jnp.int32)
    gate_out = jnp.zeros((tm, V7X_LANES), F32)
    rank_out = jnp.zeros((tm, V7X_LANES), jnp.int32)
    for k in range(TOP_K):
        rank_k = jnp.sum(jnp.where(lane == idxs[k], before, 0.0), axis=-1, keepdims=True)
        idx_out = jnp.where(lane_i == k, idxs[k].astype(jnp.int32), idx_out)
        gate_out = jnp.where(lane_i == k, exps[k] / denom, gate_out)
        rank_out = jnp.where(lane_i == k, rank_k.astype(jnp.int32), rank_out)
    idx_ref[...] = idx_out
    gate_ref[...] = gate_out
    rank_ref[...] = rank_out
    run_ref[...] = run_ref[...] + jnp.sum(sel, axis=0, keepdims=True)
    cnt_ref[...] = run_ref[...]


def _router(h, w_router, b_router):
    t, d = h.shape
    tm = min(ATT_TILE, t)
    w_pad = jnp.zeros((d, V7X_LANES), F32).at[:, :N_EXPERTS].set(w_router)
    w_hi = w_pad.astype(BF16)
    w_lo = (w_pad - w_hi.astype(F32)).astype(BF16)
    b_pad = jnp.zeros((1, V7X_LANES), F32).at[0, :N_EXPERTS].set(b_router)
    ltri = jnp.asarray(np.tril(np.ones((tm, tm), np.float32), -1), dtype=BF16)
    small = pl.BlockSpec((tm, V7X_LANES), lambda i: (i, 0))
    return pl.pallas_call(
        _router_kernel,
        grid=(t // tm,),
        in_specs=[pl.BlockSpec((tm, d), lambda i: (i, 0)),
                  _const_spec((d, V7X_LANES)), _const_spec((d, V7X_LANES)),
                  _const_spec((1, V7X_LANES)), _const_spec((tm, tm))],
        out_specs=[small, small, small, pl.BlockSpec((1, V7X_LANES), lambda i: (0, 0))],
        out_shape=[jax.ShapeDtypeStruct((t, V7X_LANES), jnp.int32),
                   jax.ShapeDtypeStruct((t, V7X_LANES), F32),
                   jax.ShapeDtypeStruct((t, V7X_LANES), jnp.int32),
                   jax.ShapeDtypeStruct((1, V7X_LANES), F32)],
        scratch_shapes=[pltpu.VMEM((1, V7X_LANES), F32)],
        compiler_params=_params(("arbitrary",)),
        name="router",
    )(h, w_hi, w_lo, b_pad, ltri)


DMA_UNROLL = 8
NUM_DMA_QUEUES = 2
WEIGHT_DMA_PRIORITY = 1
(META_EXPERT, META_VALID, META_FIRST, META_LAST_GROUP, META_NEXT_EXPERT, META_GROUP,
 META_ROWS_USED, META_NUM_GROUPS) = range(8)


def _pack_bf16_pairs(x):
    half = x.shape[1] // 2
    lo = lax.bitcast_convert_type(x[:, :half].astype(BF16).astype(F32), jnp.uint32)
    hi = lax.bitcast_convert_type(x[:, half:].astype(BF16).astype(F32), jnp.uint32)
    return (lo >> 16) | (hi & jnp.uint32(0xFFFF0000))


def _unpack_bf16_pairs(words):
    lo = lax.bitcast_convert_type(words << 16, F32).astype(BF16)
    hi = lax.bitcast_convert_type(words & jnp.uint32(0xFFFF0000), F32).astype(BF16)
    return jnp.concatenate([lo, hi], axis=1)


def _dispatch_kernel(dest_ref, meta_ref, h_ref, xs_hbm, pack_ref, zero_ref, sem, zsem):
    tm = h_ref.shape[0]
    base = pl.program_id(0) * tm
    et = zero_ref.shape[0]
    pack_ref[...] = _pack_bf16_pairs(h_ref[...])

    @pl.when(pl.program_id(0) == 0)
    def _():
        zero_ref[...] = jnp.zeros_like(zero_ref)

        def fill_copy(tile):
            start_row = pl.multiple_of(tile * et, et)
            return pltpu.make_async_copy(zero_ref, xs_hbm.at[pl.ds(start_row, et)], zsem)

        def fill_start(tile, c):
            @pl.when(meta_ref[META_ROWS_USED, tile] < et)
            def _():
                fill_copy(tile).start()
            return c

        def fill_wait(tile, c):
            @pl.when(meta_ref[META_ROWS_USED, tile] < et)
            def _():
                fill_copy(tile).wait()
            return c

        n_tiles = xs_hbm.shape[0] // et
        lax.fori_loop(0, n_tiles, fill_start, 0)
        lax.fori_loop(0, n_tiles, fill_wait, 0)

    def row_copy(r, k):
        return pltpu.make_async_copy(pack_ref.at[pl.ds(r, 1)],
                                     xs_hbm.at[pl.ds(dest_ref[(base + r) * TOP_K + k], 1)], sem)

    def start(r, c):
        for k in range(TOP_K):
            row_copy(r, k).start(priority=k % NUM_DMA_QUEUES)
        return c

    def wait(r, c):
        for k in range(TOP_K):
            row_copy(r, k).wait()
        return c

    lax.fori_loop(0, tm, start, 0, unroll=DMA_UNROLL)
    lax.fori_loop(0, tm, wait, 0, unroll=DMA_UNROLL)


def _dispatch(h, dest, meta, n_rows):
    t, d = h.shape
    tm = min(ATT_TILE, t)
    grid_spec = pltpu.PrefetchScalarGridSpec(
        num_scalar_prefetch=2,
        grid=(t // tm,),
        in_specs=[pl.BlockSpec((tm, d), lambda i, dst, m: (i, 0))],
        out_specs=pl.BlockSpec(memory_space=pl.ANY),
        scratch_shapes=[pltpu.VMEM((tm, d // 2), jnp.uint32),
                        pltpu.VMEM((EXPERT_TILE, d // 2), jnp.uint32),
                        pltpu.SemaphoreType.DMA(()), pltpu.SemaphoreType.DMA(())],
    )
    return pl.pallas_call(
        _dispatch_kernel,
        grid_spec=grid_spec,
        out_shape=jax.ShapeDtypeStruct((n_rows, d // 2), jnp.uint32),
        compiler_params=_params(("arbitrary",)),
        name="moe_dispatch",
    )(dest, meta, h)


def _stream_expert_weights(meta_ref, j, t, nj, make_copies, on_arrival):
    @pl.when((j == 0) & (t == 0))
    def _():
        for c in make_copies(meta_ref[META_EXPERT, 0], 0, 0):
            c.start(priority=WEIGHT_DMA_PRIORITY)

    @pl.when(meta_ref[META_FIRST, t] == 1)
    def _():
        slot = (j * meta_ref[META_NUM_GROUPS, t] + meta_ref[META_GROUP, t]) & 1
        for c in make_copies(meta_ref[META_EXPERT, t], j, slot):
            c.wait()
        last_group = meta_ref[META_LAST_GROUP, t] == 1

        @pl.when(jnp.logical_not(last_group & (j == nj - 1)))
        def _():
            next_chunk = jnp.where(last_group, j + 1, j)
            for c in make_copies(meta_ref[META_NEXT_EXPERT, t], next_chunk, 1 - slot):
                c.start(priority=WEIGHT_DMA_PRIORITY)

        on_arrival(slot)


def _moe_up_kernel(meta_ref, x_ref, w_hbm, bg_ref, bl_ref, o_ref, wf32, wbf, sems, *, tn, nj):
    j = pl.program_id(0)
    t = pl.program_id(1)

    def make_copies(expert, chunk, slot):
        glu_col = pl.multiple_of(chunk * tn, tn)
        lin_col = pl.multiple_of((nj + chunk) * tn, tn)
        return (pltpu.make_async_copy(w_hbm.at[expert, :, pl.ds(glu_col, tn)],
                                      wf32.at[slot, 0], sems.at[slot, 0]),
                pltpu.make_async_copy(w_hbm.at[expert, :, pl.ds(lin_col, tn)],
                                      wf32.at[slot, 1], sems.at[slot, 1]))

    def on_arrival(slot):
        wbf[0] = wf32[slot, 0].astype(BF16)
        wbf[1] = wf32[slot, 1].astype(BF16)

    _stream_expert_weights(meta_ref, j, t, nj, make_copies, on_arrival)

    @pl.when(meta_ref[META_VALID, t] > 0)
    def _():
        x = _unpack_bf16_pairs(x_ref[...])
        glu = jnp.dot(x, wbf[0], preferred_element_type=F32) + bg_ref[...]
        lin = jnp.dot(x, wbf[1], preferred_element_type=F32) + bl_ref[...]
        glu = jnp.minimum(glu, SWIGLU_LIMIT)
        lin = jnp.clip(lin, -SWIGLU_LIMIT, SWIGLU_LIMIT)
        act = glu * jax.nn.sigmoid(SWIGLU_ALPHA * glu) * (lin + 1.0)
        o_ref[...] = act.astype(o_ref.dtype)

    @pl.when(meta_ref[META_VALID, t] == 0)
    def _():
        o_ref[...] = jnp.zeros_like(o_ref)


def _moe_up(xs, w_up, b_up, meta):
    r = xs.shape[0]
    e, d, f2 = w_up.shape
    f = f2 // 2
    tm = EXPERT_TILE
    tn = _pick_tile(f, 1024)
    nj = f // tn
    grid_spec = pltpu.PrefetchScalarGridSpec(
        num_scalar_prefetch=1,
        grid=(nj, r // tm),
        in_specs=[
            pl.BlockSpec((tm, d // 2), lambda j, t, m: (t, 0)),
            pl.BlockSpec(memory_space=pl.ANY),
            pl.BlockSpec((None, 1, tn), lambda j, t, m: (m[META_EXPERT, t], 0, j)),
            pl.BlockSpec((None, 1, tn), lambda j, t, m: (m[META_EXPERT, t], 0, nj + j)),
        ],
        out_specs=pl.BlockSpec((tm, tn), lambda j, t, m: (t, j)),
        scratch_shapes=[pltpu.VMEM((2, 2, d, tn), F32), pltpu.VMEM((2, d, tn), BF16),
                        pltpu.SemaphoreType.DMA((2, 2))],
    )
    b3 = b_up.reshape(e, 1, f2)
    return pl.pallas_call(
        functools.partial(_moe_up_kernel, tn=tn, nj=nj),
        grid_spec=grid_spec,
        out_shape=jax.ShapeDtypeStruct((r, f), BF16),
        compiler_params=_params(("arbitrary", "arbitrary")),
        name="moe_up",
    )(meta, xs, w_up, b3, b3)


def _moe_down_kernel(meta_ref, a_ref, w_hbm, b_ref, o_ref, wf32, wbf, sems, *, tn, nj):
    j = pl.program_id(0)
    t = pl.program_id(1)

    def make_copies(expert, chunk, slot):
        col = pl.multiple_of(chunk * tn, tn)
        return (pltpu.make_async_copy(w_hbm.at[expert, :, pl.ds(col, tn)],
                                      wf32.at[slot], sems.at[slot]),)

    def on_arrival(slot):
        wbf[...] = wf32[slot].astype(BF16)

    _stream_expert_weights(meta_ref, j, t, nj, make_copies, on_arrival)

    @pl.when(meta_ref[META_VALID, t] > 0)
    def _():
        o_ref[...] = jnp.dot(a_ref[...], wbf[...], preferred_element_type=F32) + b_ref[...]

    @pl.when(meta_ref[META_VALID, t] == 0)
    def _():
        o_ref[...] = jnp.zeros_like(o_ref)


def _moe_down(act, w_down, b_down, meta):
    r, f = act.shape
    e, _, d = w_down.shape
    tm = EXPERT_TILE
    tn = _pick_tile(d, 2048)
    nj = d // tn
    grid_spec = pltpu.PrefetchScalarGridSpec(
        num_scalar_prefetch=1,
        grid=(nj, r // tm),
        in_specs=[
            pl.BlockSpec((tm, f), lambda j, t, m: (t, 0)),
            pl.BlockSpec(memory_space=pl.ANY),
            pl.BlockSpec((None, 1, tn), lambda j, t, m: (m[META_EXPERT, t], 0, j)),
        ],
        out_specs=pl.BlockSpec((tm, tn), lambda j, t, m: (t, j)),
        scratch_shapes=[pltpu.VMEM((2, f, tn), F32), pltpu.VMEM((f, tn), BF16),
                        pltpu.SemaphoreType.DMA((2,))],
    )
    return pl.pallas_call(
        functools.partial(_moe_down_kernel, tn=tn, nj=nj),
        grid_spec=grid_spec,
        out_shape=jax.ShapeDtypeStruct((r, d), F32),
        compiler_params=_params(("arbitrary", "arbitrary")),
        name="moe_down",
    )(meta, act, w_down, b_down.reshape(e, 1, d))


def _combine_kernel(pos_ref, y_hbm, gate_ref, h_ref, g_ref, b_ref, o_ref, buf_ref, sems):
    tm = o_ref.shape[0]
    i = pl.program_id(0)

    def row_copy(step, r, k, slot):
        src_row = pos_ref[(step * tm + r) * TOP_K + k]
        return pltpu.make_async_copy(y_hbm.at[pl.ds(src_row, 1)],
                                     buf_ref.at[slot, k, pl.ds(r, 1)], sems.at[slot])

    def gather_start(step, slot):
        def body(r, c):
            for k in range(TOP_K):
                row_copy(step, r, k, slot).start(priority=k % NUM_DMA_QUEUES)
            return c
        lax.fori_loop(0, tm, body, 0, unroll=DMA_UNROLL)

    def gather_wait(step, slot):
        def body(r, c):
            for k in range(TOP_K):
                row_copy(step, r, k, slot).wait()
            return c
        lax.fori_loop(0, tm, body, 0, unroll=DMA_UNROLL)

    @pl.when(i == 0)
    def _():
        gather_start(0, 0)

    @pl.when(i + 1 < pl.num_programs(0))
    def _():
        gather_start(i + 1, (i + 1) & 1)

    slot = i & 1
    gather_wait(i, slot)
    gates = gate_ref[...]
    ff = gates[:, 0:1] * buf_ref[slot, 0]
    for k in range(1, TOP_K):
        ff = ff + gates[:, k:k + 1] * buf_ref[slot, k]
    o_ref[...] = _layer_norm(DEEPNORM_ALPHA * h_ref[...] + ff, g_ref[...], b_ref[...])


def _combine(y_rows, pos, gates, h, g, b):
    t, d = h.shape
    tm = min(COMBINE_TILE, t)
    grid_spec = pltpu.PrefetchScalarGridSpec(
        num_scalar_prefetch=1,
        grid=(t // tm,),
        in_specs=[pl.BlockSpec(memory_space=pl.ANY),
                  pl.BlockSpec((tm, V7X_LANES), lambda i, p: (i, 0)),
                  pl.BlockSpec((tm, d), lambda i, p: (i, 0)),
                  pl.BlockSpec((1, d), lambda i, p: (0, 0)),
                  pl.BlockSpec((1, d), lambda i, p: (0, 0))],
        out_specs=pl.BlockSpec((tm, d), lambda i, p: (i, 0)),
        scratch_shapes=[pltpu.VMEM((2, TOP_K, tm, d), F32), pltpu.SemaphoreType.DMA((2,))],
    )
    return pl.pallas_call(
        _combine_kernel,
        grid_spec=grid_spec,
        out_shape=jax.ShapeDtypeStruct((t, d), F32),
        compiler_params=_params(("arbitrary",)),
        name="moe_combine",
    )(pos, y_rows, gates, h, g.reshape(1, d), b.reshape(1, d))


def _expert_layout(idx, rank, counts, n_tiles, tm):
    experts = jnp.arange(N_EXPERTS, dtype=jnp.int32)
    tiles_per = (counts + tm - 1) // tm
    tile_end = jnp.cumsum(tiles_per).astype(jnp.int32)
    tile_off = tile_end - tiles_per
    dest = jnp.sum(jnp.where(idx[..., None] == experts, tile_off * tm, 0), axis=-1) + rank

    nonempty = counts > 0
    first_e = jnp.min(jnp.where(nonempty, experts, N_EXPERTS))
    last_e = jnp.max(jnp.where(nonempty, experts, 0))
    group_of = jnp.cumsum(nonempty.astype(jnp.int32)) - 1
    later = (experts[None, :] > experts[:, None]) & nonempty[None, :]
    next_e = jnp.min(jnp.where(later, experts[None, :], N_EXPERTS), axis=1)
    next_e = jnp.where(next_e == N_EXPERTS, first_e, next_e)

    tid = jnp.arange(n_tiles, dtype=jnp.int32)
    valid = tid < tile_end[-1]
    te = jnp.sum((tid[:, None] >= tile_end[None, :]).astype(jnp.int32), axis=1)
    te = jnp.where(valid, jnp.minimum(te, N_EXPERTS - 1), last_e)
    onehot = te[:, None] == experts[None, :]

    def lookup(table):
        return jnp.sum(jnp.where(onehot, table[None, :], 0), axis=1)

    local = tid - lookup(tile_off)
    rows_used = jnp.where(valid, jnp.clip(lookup(counts) - local * tm, 0, tm), 0)
    meta = jnp.stack([
        te,
        valid.astype(jnp.int32),
        (valid & (local == 0)).astype(jnp.int32),
        (te == last_e).astype(jnp.int32),
        lookup(next_e),
        lookup(group_of),
        rows_used,
        jnp.broadcast_to(jnp.sum(nonempty.astype(jnp.int32)), (n_tiles,)),
    ]).astype(jnp.int32)
    return dest.reshape(-1).astype(jnp.int32), meta


def _moe(h, w_router, b_router, w_up, b_up, w_down, b_down, g, b):
    t, d = h.shape
    tm = EXPERT_TILE
    idx_p, gate_p, rank_p, cnt_p = _router(h, w_router, b_router)
    n_rows = t * TOP_K + N_EXPERTS * tm
    dest, meta = _expert_layout(idx_p[:, :TOP_K], rank_p[:, :TOP_K],
                                cnt_p[0, :N_EXPERTS].astype(jnp.int32), n_rows // tm, tm)
    xs = _dispatch(h, dest, meta, n_rows)
    act = _moe_up(xs, w_up, b_up, meta)
    y_rows = _moe_down(act, w_down, b_down, meta)
    return _combine(y_rows, dest, gate_p, h, g, b)


def kernel(x, mem, ln_in_g, ln_in_b, rel_bias, w_in, b_in, attn_sinks, w_a_out, w_b_out,
           w_mix_out, ln1_g, ln1_b, w_xq, w_xkv, w_xo, ln2_g, ln2_b, w_router, b_router,
           w_up, b_up, w_down, b_down, ln3_g, ln3_b):
    batch, seq, d = x.shape
    mem_len = mem.shape[1]
    t = batch * seq
    a_cols = A_WIDTH + 2 * A_KV_WIDTH
    b_cols = 3 * B_WIDTH

    h, hb = _ln_in(x.reshape(t, d), ln_in_g, ln_in_b)
    for l in range(DEPTH):
        g_col = a_cols + b_cols
        qkv_a = _mm_bias(hb, w_in[l], b_in[l, :a_cols], 0, a_cols, BF16, "in_proj_a")
        qkv_b = _mm_bias(hb, w_in[l], b_in[l, a_cols:g_col], a_cols, b_cols, BF16, "in_proj_b")
        gates = _mm_bias(hb, w_in[l], b_in[l, g_col:], g_col, 2 * d, F32, "in_proj_gates")
        ya = _swa(qkv_a, attn_sinks[l], rel_bias, batch, seq)
        yb = _stick_breaking(qkv_b, batch, seq)
        h, hb = _mix(ya, yb, gates, h, w_a_out[l].astype(BF16), w_b_out[l].astype(BF16),
                     w_mix_out[l].astype(BF16), ln1_g[l], ln1_b[l])
        kv = _mm_bias(mem.reshape(batch * mem_len, d).astype(BF16), w_xkv[l],
                      jnp.zeros((2 * d,), F32), 0, 2 * d, BF16, "mem_kv")
        h = _xattn(hb, h, kv, w_xq[l].astype(BF16), w_xo[l].astype(BF16), ln2_g[l], ln2_b[l],
                   batch, seq, mem_len)
        h = _moe(h, w_router[l], b_router[l], w_up[l], b_up[l], w_down[l], b_down[l],
                 ln3_g[l], ln3_b[l])
        if l + 1 < DEPTH:
            hb = h.astype(BF16)
    return h.reshape(batch, seq, d)
```

```python
import functools

import jax
import jax.numpy as jnp
import numpy as np
from jax import lax
from jax.experimental import pallas as pl
from jax.experimental.pallas import tpu as pltpu

DEPTH = 1
CHUNK = 64
BLK = 128
HEAD_DIM = 64
A_HEADS = 16
A_KV_HEADS = 2
A_REP = A_HEADS // A_KV_HEADS
WIN_CHUNKS = 2
B_HEADS = 16
A_WIDTH = A_HEADS * HEAD_DIM
A_KV_WIDTH = A_KV_HEADS * HEAD_DIM
B_WIDTH = B_HEADS * HEAD_DIM
REL_BUCKETS = 32
REL_MAX_DIST = 128
X_HEADS = 4
N_EXPERTS = 32
TOP_K = 4
SWIGLU_LIMIT = 7.0
SWIGLU_ALPHA = 1.702
LN_EPS = 1e-5
DEEPNORM_ALPHA = (2.0 * DEPTH) ** 0.25
QK_SCALE = HEAD_DIM ** -0.5

V7X_LANES = 128
V7X_VMEM_LIMIT_BYTES = 56 * 1024 * 1024

ROW_TILE = 512
MM_ROW_TILE = 1024
ATT_TILE = 256
SB_TILE = 256
SB_PAIRS = 2
SB_ROWS = 32
SB_UNROLL = True
EXPERT_TILE = 256
EXPERT_TILE_PARTS = 4
COMBINE_TILE = 128
SB_LOG_ZERO = -105.0

F32 = jnp.float32
BF16 = jnp.bfloat16
NT_DIMS = (((1,), (1,)), ((), ()))


def _params(semantics):
    return pltpu.CompilerParams(dimension_semantics=semantics,
                                vmem_limit_bytes=V7X_VMEM_LIMIT_BYTES)


def _const_spec(shape):
    nd = len(shape)
    return pl.BlockSpec(shape, lambda *_: (0,) * nd, pipeline_mode=pl.Buffered(1))


def _layer_norm(x, g, b):
    mu = jnp.mean(x, axis=-1, keepdims=True)
    xc = x - mu
    var = jnp.mean(xc * xc, axis=-1, keepdims=True)
    return xc * lax.rsqrt(var + LN_EPS) * g + b


def _ln_proj_kernel(x_ref, g_ref, b_ref, w_hbm, bias_ref, h_ref, hb_ref, o_ref, stage, wbf, sem,
                    *, n):
    @pl.when(pl.program_id(0) == 0)
    def _():
        slab = pltpu.make_async_copy(w_hbm.at[:, pl.ds(0, n)], stage, sem)
        slab.start()
        slab.wait()
        wbf[...] = stage[...].astype(BF16)

    y = _layer_norm(x_ref[...], g_ref[...], b_ref[...])
    yb = y.astype(BF16)
    h_ref[...] = y
    hb_ref[...] = yb
    acc = jnp.dot(yb, wbf[...], preferred_element_type=F32)
    o_ref[...] = (acc + bias_ref[...]).astype(o_ref.dtype)


def _ln_proj(x2d, g, b, w, bias, n, out_dtype):
    t, d = x2d.shape
    tm = min(ROW_TILE, t)
    row = pl.BlockSpec((tm, d), lambda i: (i, 0))
    return pl.pallas_call(
        functools.partial(_ln_proj_kernel, n=n),
        grid=(t // tm,),
        in_specs=[row, _const_spec((1, d)), _const_spec((1, d)),
                  pl.BlockSpec(memory_space=pl.ANY), _const_spec((1, n))],
        out_specs=[row, row, pl.BlockSpec((tm, n), lambda i: (i, 0))],
        out_shape=[jax.ShapeDtypeStruct((t, d), F32), jax.ShapeDtypeStruct((t, d), BF16),
                   jax.ShapeDtypeStruct((t, n), out_dtype)],
        scratch_shapes=[pltpu.VMEM((d, n), F32), pltpu.VMEM((d, n), BF16),
                        pltpu.SemaphoreType.DMA(())],
        compiler_params=_params(("arbitrary",)),
        name="ln_in_proj_a",
    )(x2d, g.reshape(1, d), b.reshape(1, d), w, bias.reshape(1, n).astype(F32))


def _mm_bias_kernel(a_ref, w_hbm, b_ref, o_ref, stage, wbf, sem, *, col0, tn, nj):
    j = pl.program_id(0)
    i = pl.program_id(1)

    def slab_copy(jj):
        col = pl.multiple_of(col0 + jj * tn, V7X_LANES)
        return pltpu.make_async_copy(w_hbm.at[:, pl.ds(col, tn)], stage, sem)

    @pl.when(i == 0)
    def _():
        @pl.when(j == 0)
        def _():
            slab_copy(0).start()

        slab_copy(j).wait()
        wbf[...] = stage[...].astype(BF16)

        @pl.when(j + 1 < nj)
        def _():
            slab_copy(j + 1).start()

    acc = jnp.dot(a_ref[...], wbf[...], preferred_element_type=F32)
    o_ref[...] = (acc + b_ref[...]).astype(o_ref.dtype)


def _pick_tile(n, target):
    best = None
    for c in range(V7X_LANES, min(n, target) + 1, V7X_LANES):
        if n % c == 0:
            best = c
    return n if best is None else best


def _mm_bias(a, w, b, col0, n, out_dtype, name):
    m, k = a.shape
    tm = min(MM_ROW_TILE, m)
    tn = _pick_tile(n, 1024)
    nj = n // tn
    return pl.pallas_call(
        functools.partial(_mm_bias_kernel, col0=col0, tn=tn, nj=nj),
        grid=(nj, m // tm),
        in_specs=[pl.BlockSpec((tm, k), lambda j, i: (i, 0)),
                  pl.BlockSpec(memory_space=pl.ANY),
                  pl.BlockSpec((1, tn), lambda j, i: (0, j))],
        out_specs=pl.BlockSpec((tm, tn), lambda j, i: (i, j)),
        out_shape=jax.ShapeDtypeStruct((m, n), out_dtype),
        scratch_shapes=[pltpu.VMEM((k, tn), F32), pltpu.VMEM((k, tn), BF16),
                        pltpu.SemaphoreType.DMA(())],
        compiler_params=_params(("arbitrary", "arbitrary")),
        name=name,
    )(a, w, b.reshape(1, n).astype(F32))


def _swa_kernel(sink_ref, q_ref, kp_ref, kc_ref, vp_ref, vc_ref, bias_ref, o_ref):
    n = pl.program_id(1)
    kk = jnp.concatenate([kp_ref[...], kc_ref[...]], axis=0)
    vv = jnp.concatenate([vp_ref[...], vc_ref[...]], axis=0)
    col = lax.broadcasted_iota(jnp.int32, (BLK, 2 * BLK), 1)
    exists = (col >= BLK) | (n > 0)
    for h in range(A_HEADS):
        g = h // A_REP
        qh = q_ref[:, h * HEAD_DIM:(h + 1) * HEAD_DIM] * QK_SCALE
        kg = kk[:, g * HEAD_DIM:(g + 1) * HEAD_DIM]
        vg = vv[:, g * HEAD_DIM:(g + 1) * HEAD_DIM]
        s = lax.dot_general(qh, kg, NT_DIMS, preferred_element_type=F32)
        s = jnp.where(exists, s + bias_ref[h], -jnp.inf)
        sink = sink_ref[h]
        m = jnp.maximum(jnp.max(s, axis=-1, keepdims=True), sink)
        p = jnp.exp(s - m)
        denom = jnp.sum(p, axis=-1, keepdims=True) + jnp.exp(sink - m)
        o = jnp.dot(p.astype(BF16), vg, preferred_element_type=F32) / denom
        o_ref[:, h * HEAD_DIM:(h + 1) * HEAD_DIM] = o.astype(o_ref.dtype)


def _rel_bucket(rel):
    half = REL_BUCKETS // 2
    max_exact = half // 2
    base = jnp.where(rel > 0, half, 0)
    n = jnp.abs(rel)
    nf = jnp.maximum(n, 1).astype(F32)
    large = max_exact + (jnp.log(nf / max_exact) / np.log(REL_MAX_DIST / max_exact)
                         * (half - max_exact)).astype(jnp.int32)
    large = jnp.minimum(large, half - 1)
    return base + jnp.where(n < max_exact, n, large)


def _swa_bias_table(rel_bias):
    rel = jnp.arange(-(2 * BLK - 1), BLK)
    line = jnp.transpose(rel_bias[_rel_bucket(rel)]).astype(F32)
    bias = jnp.stack([line[:, BLK - 1 - q:3 * BLK - 1 - q] for q in range(BLK)], axis=1)
    qi = np.arange(BLK)[:, None]
    kj = np.arange(2 * BLK)[None, :]
    dchunk = (kj // CHUNK - BLK // CHUNK) - qi // CHUNK
    band = (dchunk <= 0) & (dchunk >= -WIN_CHUNKS)
    return jnp.where(jnp.asarray(band)[None], bias, -jnp.inf)


def _swa(qkv, sinks, rel_bias, batch, seq):
    nb = seq // BLK
    kcol = A_WIDTH // A_KV_WIDTH
    vcol = kcol + 1
    kv_blk = (BLK, A_KV_WIDTH)
    grid_spec = pltpu.PrefetchScalarGridSpec(
        num_scalar_prefetch=1,
        grid=(batch, nb),
        in_specs=[
            pl.BlockSpec((BLK, A_WIDTH), lambda b, n, s: (b * nb + n, 0)),
            pl.BlockSpec(kv_blk, lambda b, n, s: (b * nb + jnp.maximum(n - 1, 0), kcol)),
            pl.BlockSpec(kv_blk, lambda b, n, s: (b * nb + n, kcol)),
            pl.BlockSpec(kv_blk, lambda b, n, s: (b * nb + jnp.maximum(n - 1, 0), vcol)),
            pl.BlockSpec(kv_blk, lambda b, n, s: (b * nb + n, vcol)),
            pl.BlockSpec((A_HEADS, BLK, 2 * BLK), lambda b, n, s: (0, 0, 0),
                         pipeline_mode=pl.Buffered(1)),
        ],
        out_specs=pl.BlockSpec((BLK, A_WIDTH), lambda b, n, s: (b * nb + n, 0)),
    )
    return pl.pallas_call(
        _swa_kernel,
        grid_spec=grid_spec,
        out_shape=jax.ShapeDtypeStruct((batch * seq, A_WIDTH), BF16),
        compiler_params=_params(("arbitrary", "arbitrary")),
        name="swa_attn",
    )(sinks.astype(F32), qkv, qkv, qkv, qkv, qkv, _swa_bias_table(rel_bias))


def _sb_kernel(q_ref, k_ref, v_ref, tri_ref, o_ref, qh_scr, z_scr, hl_scr, s_scr, w_scr,
               carry_scr, acc_scr):
    t = q_ref.shape[0]
    i = pl.program_id(2)
    n_chunks = t // SB_ROWS
    n_heads = 2 * SB_PAIRS
    pair_w = 2 * HEAD_DIM
    lane = lax.broadcasted_iota(jnp.int32, (1, pair_w), 1)
    first = lane < HEAD_DIM
    zero = jnp.zeros((), BF16)

    def pair_cols(h):
        return slice((h // 2) * pair_w, (h // 2 + 1) * pair_w)

    def own_half(h, x):
        return jnp.where(first, x, zero) if h % 2 == 0 else jnp.where(first, zero, x)

    q2 = q_ref[...] * QK_SCALE
    for h in range(n_heads):
        qh_scr[h] = own_half(h, q2[:, pair_cols(h)])
    carry_scr[...] = jnp.zeros_like(carry_scr)
    acc_scr[...] = jnp.zeros_like(acc_scr)
    row = lax.broadcasted_iota(jnp.int32, (SB_ROWS, t), 0)
    col = lax.broadcasted_iota(jnp.int32, (SB_ROWS, t), 1)

    def key_tile(j, diagonal):
        start = pl.multiple_of(j * t, t)
        k2 = k_ref[pl.ds(start, t), :]
        v2 = v_ref[pl.ds(start, t), :]
        for h in range(n_heads):
            z_scr[h] = lax.dot_general(qh_scr[h], k2[:, pair_cols(h)], NT_DIMS,
                                       preferred_element_type=F32)

        def keep_chunk(c, carry):
            r0 = pl.multiple_of(c * SB_ROWS, SB_ROWS)
            rows = pl.ds(r0, SB_ROWS)
            for h in range(n_heads):
                z = z_scr[h, rows, :]
                keep = jnp.minimum(-z, 0.0) - jnp.log(1.0 + jnp.exp(-jnp.abs(z)))
                if diagonal:
                    keep = jnp.where(col < row + r0, keep, 0.0)
                hi = keep.astype(BF16)
                hl_scr[h, rows, 0:t] = hi
                hl_scr[h, rows, t:2 * t] = (keep - hi.astype(F32)).astype(BF16)
            return carry

        lax.fori_loop(0, n_chunks, keep_chunk, 0, unroll=SB_UNROLL)
        tri = tri_ref[...]
        for h in range(n_heads):
            s_scr[h] = jnp.dot(hl_scr[h], tri, preferred_element_type=F32)

        def weight_chunk(c, top):
            r0 = pl.multiple_of(c * SB_ROWS, SB_ROWS)
            rows = pl.ds(r0, SB_ROWS)
            for h in range(n_heads):
                carry = carry_scr[h, rows, :]
                log_w = (z_scr[h, rows, :] + s_scr[h, rows, :]
                         + jnp.concatenate([carry] * (t // V7X_LANES), axis=1))
                w = jnp.exp(log_w)
                if diagonal:
                    w = jnp.where(col < row + r0, w, 0.0)
                w_scr[h // 2, rows, (h % 2) * t:(h % 2 + 1) * t] = w.astype(BF16)
                carry = carry + jnp.broadcast_to(s_scr[h, rows, 0:1], (SB_ROWS, V7X_LANES))
                carry_scr[h, rows, :] = carry
                top = jnp.maximum(top, carry)
            return top

        top = lax.fori_loop(0, n_chunks, weight_chunk,
                            jnp.full((SB_ROWS, V7X_LANES), -jnp.inf, F32), unroll=SB_UNROLL)
        for p in range(SB_PAIRS):
            cols = pair_cols(2 * p)
            v_pair = jnp.concatenate([own_half(0, v2[:, cols]), own_half(1, v2[:, cols])], axis=0)
            acc_scr[:, cols] += jnp.dot(w_scr[p], v_pair, preferred_element_type=F32)
        return (jnp.max(top) > SB_LOG_ZERO).astype(jnp.int32)

    live = key_tile(i, True)

    def cond(state):
        return (state[0] >= 0) & (state[1] > 0)

    def body(state):
        return state[0] - 1, key_tile(state[0], False)

    lax.while_loop(cond, body, (i - 1, live))
    o_ref[...] = acc_scr[...].astype(o_ref.dtype)


def _stick_breaking(qkv, batch, seq):
    t = min(SB_TILE, seq)
    nq = seq // t
    width = SB_PAIRS * 2 * HEAD_DIM
    groups = B_WIDTH // width
    heads = 2 * SB_PAIRS
    j = np.arange(2 * t)[:, None] % t
    s = np.arange(t)[None, :]
    tri = jnp.asarray((j >= s).astype(np.float32), dtype=BF16)
    return pl.pallas_call(
        _sb_kernel,
        grid=(batch, groups, nq),
        in_specs=[
            pl.BlockSpec((t, width), lambda b, p, i: (b * nq + i, p)),
            pl.BlockSpec((seq, width), lambda b, p, i: (b, groups + p)),
            pl.BlockSpec((seq, width), lambda b, p, i: (b, 2 * groups + p)),
            _const_spec((2 * t, t)),
        ],
        out_specs=pl.BlockSpec((t, width), lambda b, p, i: (b * nq + i, p)),
        out_shape=jax.ShapeDtypeStruct((batch * seq, B_WIDTH), BF16),
        scratch_shapes=[
            pltpu.VMEM((heads, t, 2 * HEAD_DIM), BF16),
            pltpu.VMEM((heads, t, t), F32),
            pltpu.VMEM((heads, t, 2 * t), BF16),
            pltpu.VMEM((heads, t, t), F32),
            pltpu.VMEM((SB_PAIRS, t, 2 * t), BF16),
            pltpu.VMEM((heads, t, V7X_LANES), F32),
            pltpu.VMEM((t, width), F32),
        ],
        compiler_params=_params(("arbitrary", "arbitrary", "arbitrary")),
        name="stick_breaking",
    )(qkv, qkv, qkv, tri)


def _mix_kernel(ya_ref, yb_ref, gates_a_ref, gates_b_ref, h_ref, wa_ref, wb_ref, wm_ref,
                g_ref, b_ref, o_ref, ob_ref):
    pa = jnp.dot(ya_ref[...], wa_ref[...], preferred_element_type=F32)
    pb = jnp.dot(yb_ref[...], wb_ref[...], preferred_element_type=F32)
    merged = jax.nn.sigmoid(gates_a_ref[...]) * pa + jax.nn.sigmoid(gates_b_ref[...]) * pb
    mixed = jnp.dot(merged.astype(BF16), wm_ref[...], preferred_element_type=F32)
    y = _layer_norm(DEEPNORM_ALPHA * h_ref[...] + mixed, g_ref[...], b_ref[...])
    o_ref[...] = y
    ob_ref[...] = y.astype(BF16)


def _mix(ya, yb, gates, h, wa, wb, wm, g, b):
    t, d = h.shape
    tm = min(ATT_TILE, t)
    row = pl.BlockSpec((tm, d), lambda i: (i, 0))
    return pl.pallas_call(
        _mix_kernel,
        grid=(t // tm,),
        in_specs=[pl.BlockSpec((tm, A_WIDTH), lambda i: (i, 0)),
                  pl.BlockSpec((tm, B_WIDTH), lambda i: (i, 0)),
                  pl.BlockSpec((tm, d), lambda i: (i, 0)),
                  pl.BlockSpec((tm, d), lambda i: (i, 1)),
                  row,
                  _const_spec((A_WIDTH, d)), _const_spec((B_WIDTH, d)), _const_spec((d, d)),
                  _const_spec((1, d)), _const_spec((1, d))],
        out_specs=[row, row],
        out_shape=[jax.ShapeDtypeStruct((t, d), F32), jax.ShapeDtypeStruct((t, d), BF16)],
        compiler_params=_params(("arbitrary",)),
        name="mix_out",
    )(ya, yb, gates, gates, h, wa, wb, wm, g.reshape(1, d), b.reshape(1, d))


def _xattn_kernel(hb_ref, h_ref, k_ref, v_ref, wq_ref, wo_ref, g_ref, b_ref, whi_ref, wlo_ref,
                  rb_ref, ltri_ref, o_ref, idx_ref, gate_ref, rank_ref, cnt_ref, run_ref):
    d = h_ref.shape[1]
    hd = d // X_HEADS
    scale = hd ** -0.5
    q = jnp.dot(hb_ref[...], wq_ref[...], preferred_element_type=F32).astype(BF16)
    outs = []
    for x in range(X_HEADS):
        sl = slice(x * hd, (x + 1) * hd)
        s = lax.dot_general(q[:, sl], k_ref[:, sl], NT_DIMS, preferred_element_type=F32) * scale
        m = jnp.max(s, axis=-1, keepdims=True)
        p = jnp.exp(s - m)
        denom = jnp.sum(p, axis=-1, keepdims=True)
        o = jnp.dot(p.astype(BF16), v_ref[:, sl], preferred_element_type=F32) / denom
        outs.append(o.astype(BF16))
    xo = jnp.dot(jnp.concatenate(outs, axis=1), wo_ref[...], preferred_element_type=F32)
    h_out = _layer_norm(DEEPNORM_ALPHA * h_ref[...] + xo, g_ref[...], b_ref[...])
    o_ref[...] = h_out
    first_step = (pl.program_id(0) == 0) & (pl.program_id(1) == 0)
    _route_tile(h_out, first_step, whi_ref, wlo_ref, rb_ref, ltri_ref, idx_ref, gate_ref,
                rank_ref, cnt_ref, run_ref)


def _xattn_route(hb, h, kv, wq, wo, g, b, w_router, b_router, batch, seq, mem_len):
    t, d = h.shape
    tm = min(ATT_TILE, seq)
    ns = seq // tm
    w_pad = jnp.zeros((d, V7X_LANES), F32).at[:, :N_EXPERTS].set(w_router)
    w_hi = w_pad.astype(BF16)
    w_lo = (w_pad - w_hi.astype(F32)).astype(BF16)
    b_pad = jnp.zeros((1, V7X_LANES), F32).at[0, :N_EXPERTS].set(b_router)
    ltri = jnp.asarray(np.tril(np.ones((tm, tm), np.float32), -1), dtype=BF16)
    row = pl.BlockSpec((tm, d), lambda bi, i: (bi * ns + i, 0))
    small = pl.BlockSpec((tm, V7X_LANES), lambda bi, i: (bi * ns + i, 0))
    return pl.pallas_call(
        _xattn_kernel,
        grid=(batch, ns),
        in_specs=[row, row,
                  pl.BlockSpec((mem_len, d), lambda bi, i: (bi, 0)),
                  pl.BlockSpec((mem_len, d), lambda bi, i: (bi, 1)),
                  _const_spec((d, d)), _const_spec((d, d)),
                  _const_spec((1, d)), _const_spec((1, d)),
                  _const_spec((d, V7X_LANES)), _const_spec((d, V7X_LANES)),
                  _const_spec((1, V7X_LANES)), _const_spec((tm, tm))],
        out_specs=[row, small, small, small,
                   pl.BlockSpec((1, V7X_LANES), lambda bi, i: (0, 0))],
        out_shape=[jax.ShapeDtypeStruct((t, d), F32),
                   jax.ShapeDtypeStruct((t, V7X_LANES), jnp.int32),
                   jax.ShapeDtypeStruct((t, V7X_LANES), F32),
                   jax.ShapeDtypeStruct((t, V7X_LANES), jnp.int32),
                   jax.ShapeDtypeStruct((1, V7X_LANES), F32)],
        scratch_shapes=[pltpu.VMEM((1, V7X_LANES), F32)],
        compiler_params=_params(("arbitrary", "arbitrary")),
        name="xattn_route",
    )(hb, h, kv, kv, wq, wo, g.reshape(1, d), b.reshape(1, d), w_hi, w_lo, b_pad, ltri)


def _route_tile(h, first_step, whi_ref, wlo_ref, b_ref, ltri_ref, idx_ref, gate_ref, rank_ref,
                cnt_ref, run_ref):
    @pl.when(first_step)
    def _():
        run_ref[...] = jnp.zeros_like(run_ref)

    h_hi = h.astype(BF16)
    h_lo = (h - h_hi.astype(F32)).astype(BF16)
    w_hi = whi_ref[...]
    logits = (jnp.dot(h_hi, w_hi, preferred_element_type=F32)
              + (jnp.dot(h_lo, w_hi, preferred_element_type=F32)
                 + jnp.dot(h_hi, wlo_ref[...], preferred_element_type=F32))
              + b_ref[...])
    tm = logits.shape[0]
    lane_i = lax.broadcasted_iota(jnp.int32, (tm, V7X_LANES), 1)
    lane = lane_i.astype(F32)
    rem = jnp.where(lane_i < N_EXPERTS, logits, -jnp.inf)
    vals, idxs = [], []
    for _ in range(TOP_K):
        m = jnp.max(rem, axis=-1, keepdims=True)
        ik = jnp.min(jnp.where(rem == m, lane, float(V7X_LANES)), axis=-1, keepdims=True)
        vals.append(m)
        idxs.append(ik)
        rem = jnp.where(lane == ik, -jnp.inf, rem)
    exps = [jnp.exp(v - vals[0]) for v in vals]
    denom = exps[0] + exps[1] + exps[2] + exps[3]
    sel = jnp.zeros((tm, V7X_LANES), F32)
    for ik in idxs:
        sel = sel + (lane == ik).astype(F32)
    before = jnp.dot(ltri_ref[...], sel.astype(BF16), preferred_element_type=F32) + run_ref[...]
    idx_out = jnp.zeros((tm, V7X_LANES), jnp.int32)
    gate_out = jnp.zeros((tm, V7X_LANES), F32)
    rank_out = jnp.zeros((tm, V7X_LANES), jnp.int32)
    for k in range(TOP_K):
        rank_k = jnp.sum(jnp.where(lane == idxs[k], before, 0.0), axis=-1, keepdims=True)
        idx_out = jnp.where(lane_i == k, idxs[k].astype(jnp.int32), idx_out)
        gate_out = jnp.where(lane_i == k, exps[k] / denom, gate_out)
        rank_out = jnp.where(lane_i == k, rank_k.astype(jnp.int32), rank_out)
    idx_ref[...] = idx_out
    gate_ref[...] = gate_out
    rank_ref[...] = rank_out
    run_ref[...] = run_ref[...] + jnp.sum(sel, axis=0, keepdims=True)
    cnt_ref[...] = run_ref[...]


DMA_UNROLL = 8
NUM_DMA_QUEUES = 2
WEIGHT_DMA_PRIORITY = 1
(META_EXPERT, META_VALID, META_FIRST, META_LAST_GROUP, META_NEXT_EXPERT, META_GROUP,
 META_ROWS_USED, META_NUM_GROUPS) = range(8)


def _pack_bf16_pairs(x):
    half = x.shape[1] // 2
    lo = lax.bitcast_convert_type(x[:, :half].astype(BF16).astype(F32), jnp.uint32)
    hi = lax.bitcast_convert_type(x[:, half:].astype(BF16).astype(F32), jnp.uint32)
    return (lo >> 16) | (hi & jnp.uint32(0xFFFF0000))


def _unpack_bf16_pairs(words):
    lo = lax.bitcast_convert_type(words << 16, F32).astype(BF16)
    hi = lax.bitcast_convert_type(words & jnp.uint32(0xFFFF0000), F32).astype(BF16)
    return jnp.concatenate([lo, hi], axis=1)


def _dispatch_kernel(dest_ref, meta_ref, h_ref, xs_hbm, pack_ref, zero_ref, sem, zsem):
    tm = h_ref.shape[0]
    base = pl.program_id(0) * tm
    et = zero_ref.shape[0]
    pack_ref[...] = _pack_bf16_pairs(h_ref[...])

    @pl.when(pl.program_id(0) == 0)
    def _():
        zero_ref[...] = jnp.zeros_like(zero_ref)

        def fill_copy(tile):
            start_row = pl.multiple_of(tile * et, et)
            return pltpu.make_async_copy(zero_ref, xs_hbm.at[pl.ds(start_row, et)], zsem)

        def fill_start(tile, c):
            @pl.when(meta_ref[META_ROWS_USED, tile] < et)
            def _():
                fill_copy(tile).start()
            return c

        def fill_wait(tile, c):
            @pl.when(meta_ref[META_ROWS_USED, tile] < et)
            def _():
                fill_copy(tile).wait()
            return c

        n_tiles = xs_hbm.shape[0] // et
        lax.fori_loop(0, n_tiles, fill_start, 0)
        lax.fori_loop(0, n_tiles, fill_wait, 0)

    def row_copy(r, k):
        return pltpu.make_async_copy(pack_ref.at[pl.ds(r, 1)],
                                     xs_hbm.at[pl.ds(dest_ref[(base + r) * TOP_K + k], 1)], sem)

    def start(r, c):
        for k in range(TOP_K):
            row_copy(r, k).start(priority=k % NUM_DMA_QUEUES)
        return c

    def wait(r, c):
        for k in range(TOP_K):
            row_copy(r, k).wait()
        return c

    lax.fori_loop(0, tm, start, 0, unroll=DMA_UNROLL)
    lax.fori_loop(0, tm, wait, 0, unroll=DMA_UNROLL)


def _dispatch(h, dest, meta, n_rows):
    t, d = h.shape
    tm = min(ATT_TILE, t)
    grid_spec = pltpu.PrefetchScalarGridSpec(
        num_scalar_prefetch=2,
        grid=(t // tm,),
        in_specs=[pl.BlockSpec((tm, d), lambda i, dst, m: (i, 0))],
        out_specs=pl.BlockSpec(memory_space=pl.ANY),
        scratch_shapes=[pltpu.VMEM((tm, d // 2), jnp.uint32),
                        pltpu.VMEM((EXPERT_TILE, d // 2), jnp.uint32),
                        pltpu.SemaphoreType.DMA(()), pltpu.SemaphoreType.DMA(())],
    )
    return pl.pallas_call(
        _dispatch_kernel,
        grid_spec=grid_spec,
        out_shape=jax.ShapeDtypeStruct((n_rows, d // 2), jnp.uint32),
        compiler_params=_params(("arbitrary",)),
        name="moe_dispatch",
    )(dest, meta, h)


def _stream_expert_weights(meta_ref, j, t, nj, make_copies, on_arrival):
    @pl.when((j == 0) & (t == 0))
    def _():
        for c in make_copies(meta_ref[META_EXPERT, 0], 0, 0):
            c.start(priority=WEIGHT_DMA_PRIORITY)

    @pl.when(meta_ref[META_FIRST, t] == 1)
    def _():
        slot = (j * meta_ref[META_NUM_GROUPS, t] + meta_ref[META_GROUP, t]) & 1
        for c in make_copies(meta_ref[META_EXPERT, t], j, slot):
            c.wait()
        last_group = meta_ref[META_LAST_GROUP, t] == 1

        @pl.when(jnp.logical_not(last_group & (j == nj - 1)))
        def _():
            next_chunk = jnp.where(last_group, j + 1, j)
            for c in make_copies(meta_ref[META_NEXT_EXPERT, t], next_chunk, 1 - slot):
                c.start(priority=WEIGHT_DMA_PRIORITY)

        on_arrival(slot)


def _for_used_rows(rows_used, o_ref, compute):
    tm = o_ref.shape[0]
    part = tm // EXPERT_TILE_PARTS
    for parts in range(EXPERT_TILE_PARTS + 1):
        used = parts * part

        @pl.when((rows_used > used - part) & (rows_used <= used))
        def _(used=used):
            if used > 0:
                compute(slice(0, used))
            if used < tm:
                o_ref[used:tm, :] = jnp.zeros((tm - used, o_ref.shape[1]), o_ref.dtype)


def _moe_up_kernel(meta_ref, x_ref, w_hbm, bg_ref, bl_ref, o_ref, wf32, wbf, sems, *, tn, nj):
    j = pl.program_id(0)
    t = pl.program_id(1)

    def make_copies(expert, chunk, slot):
        glu_col = pl.multiple_of(chunk * tn, tn)
        lin_col = pl.multiple_of((nj + chunk) * tn, tn)
        return (pltpu.make_async_copy(w_hbm.at[expert, :, pl.ds(glu_col, tn)],
                                      wf32.at[slot, 0], sems.at[slot, 0]),
                pltpu.make_async_copy(w_hbm.at[expert, :, pl.ds(lin_col, tn)],
                                      wf32.at[slot, 1], sems.at[slot, 1]))

    def on_arrival(slot):
        wbf[0] = wf32[slot, 0].astype(BF16)
        wbf[1] = wf32[slot, 1].astype(BF16)

    _stream_expert_weights(meta_ref, j, t, nj, make_copies, on_arrival)

    def expert_rows(rows):
        x = _unpack_bf16_pairs(x_ref[rows, :])
        glu = jnp.dot(x, wbf[0], preferred_element_type=F32) + bg_ref[...]
        lin = jnp.dot(x, wbf[1], preferred_element_type=F32) + bl_ref[...]
        glu = jnp.minimum(glu, SWIGLU_LIMIT)
        lin = jnp.clip(lin, -SWIGLU_LIMIT, SWIGLU_LIMIT)
        act = glu * jax.nn.sigmoid(SWIGLU_ALPHA * glu) * (lin + 1.0)
        o_ref[rows, :] = act.astype(o_ref.dtype)

    _for_used_rows(meta_ref[META_ROWS_USED, t], o_ref, expert_rows)


def _moe_up(xs, w_up, b_up, meta):
    r = xs.shape[0]
    e, d, f2 = w_up.shape
    f = f2 // 2
    tm = EXPERT_TILE
    tn = _pick_tile(f, 1024)
    nj = f // tn
    grid_spec = pltpu.PrefetchScalarGridSpec(
        num_scalar_prefetch=1,
        grid=(nj, r // tm),
        in_specs=[
            pl.BlockSpec((tm, d // 2), lambda j, t, m: (t, 0)),
            pl.BlockSpec(memory_space=pl.ANY),
            pl.BlockSpec((None, 1, tn), lambda j, t, m: (m[META_EXPERT, t], 0, j)),
            pl.BlockSpec((None, 1, tn), lambda j, t, m: (m[META_EXPERT, t], 0, nj + j)),
        ],
        out_specs=pl.BlockSpec((tm, tn), lambda j, t, m: (t, j)),
        scratch_shapes=[pltpu.VMEM((2, 2, d, tn), F32), pltpu.VMEM((2, d, tn), BF16),
                        pltpu.SemaphoreType.DMA((2, 2))],
    )
    b3 = b_up.reshape(e, 1, f2)
    return pl.pallas_call(
        functools.partial(_moe_up_kernel, tn=tn, nj=nj),
        grid_spec=grid_spec,
        out_shape=jax.ShapeDtypeStruct((r, f), BF16),
        compiler_params=_params(("arbitrary", "arbitrary")),
        name="moe_up",
    )(meta, xs, w_up, b3, b3)


def _moe_down_kernel(meta_ref, a_ref, w_hbm, b_ref, o_ref, wf32, wbf, sems, *, tn, nj):
    j = pl.program_id(0)
    t = pl.program_id(1)

    def make_copies(expert, chunk, slot):
        col = pl.multiple_of(chunk * tn, tn)
        return (pltpu.make_async_copy(w_hbm.at[expert, :, pl.ds(col, tn)],
                                      wf32.at[slot], sems.at[slot]),)

    def on_arrival(slot):
        wbf[...] = wf32[slot].astype(BF16)

    _stream_expert_weights(meta_ref, j, t, nj, make_copies, on_arrival)

    def expert_rows(rows):
        o_ref[rows, :] = (jnp.dot(a_ref[rows, :], wbf[...], preferred_element_type=F32)
                          + b_ref[...])

    _for_used_rows(meta_ref[META_ROWS_USED, t], o_ref, expert_rows)


def _moe_down(act, w_down, b_down, meta):
    r, f = act.shape
    e, _, d = w_down.shape
    tm = EXPERT_TILE
    tn = _pick_tile(d, 2048)
    nj = d // tn
    grid_spec = pltpu.PrefetchScalarGridSpec(
        num_scalar_prefetch=1,
        grid=(nj, r // tm),
        in_specs=[
            pl.BlockSpec((tm, f), lambda j, t, m: (t, 0)),
            pl.BlockSpec(memory_space=pl.ANY),
            pl.BlockSpec((None, 1, tn), lambda j, t, m: (m[META_EXPERT, t], 0, j)),
        ],
        out_specs=pl.BlockSpec((tm, tn), lambda j, t, m: (t, j)),
        scratch_shapes=[pltpu.VMEM((2, f, tn), F32), pltpu.VMEM((f, tn), BF16),
                        pltpu.SemaphoreType.DMA((2,))],
    )
    return pl.pallas_call(
        functools.partial(_moe_down_kernel, tn=tn, nj=nj),
        grid_spec=grid_spec,
        out_shape=jax.ShapeDtypeStruct((r, d), F32),
        compiler_params=_params(("arbitrary", "arbitrary")),
        name="moe_down",
    )(meta, act, w_down, b_down.reshape(e, 1, d))


def _combine_kernel(pos_ref, y_hbm, gate_ref, h_ref, g_ref, b_ref, o_ref, buf_ref, sems):
    tm = o_ref.shape[0]
    i = pl.program_id(0)

    def row_copy(step, r, k, slot):
        src_row = pos_ref[(step * tm + r) * TOP_K + k]
        return pltpu.make_async_copy(y_hbm.at[pl.ds(src_row, 1)],
                                     buf_ref.at[slot, k, pl.ds(r, 1)], sems.at[slot])

    def gather_start(step, slot):
        def body(r, c):
            for k in range(TOP_K):
                row_copy(step, r, k, slot).start(priority=k % NUM_DMA_QUEUES)
            return c
        lax.fori_loop(0, tm, body, 0, unroll=DMA_UNROLL)

    def gather_wait(step, slot):
        def body(r, c):
            for k in range(TOP_K):
                row_copy(step, r, k, slot).wait()
            return c
        lax.fori_loop(0, tm, body, 0, unroll=DMA_UNROLL)

    @pl.when(i == 0)
    def _():
        gather_start(0, 0)

    for next_slot in range(2):
        @pl.when((i + 1 < pl.num_programs(0)) & ((i + 1) & 1 == next_slot))
        def _(next_slot=next_slot):
            gather_start(i + 1, next_slot)

    slot = i & 1
    gather_wait(i, slot)
    gates = gate_ref[...]
    ff = gates[:, 0:1] * buf_ref[slot, 0]
    for k in range(1, TOP_K):
        ff = ff + gates[:, k:k + 1] * buf_ref[slot, k]
    o_ref[...] = _layer_norm(DEEPNORM_ALPHA * h_ref[...] + ff, g_ref[...], b_ref[...])


def _combine(y_rows, pos, gates, h, g, b):
    t, d = h.shape
    tm = min(COMBINE_TILE, t)
    grid_spec = pltpu.PrefetchScalarGridSpec(
        num_scalar_prefetch=1,
        grid=(t // tm,),
        in_specs=[pl.BlockSpec(memory_space=pl.ANY),
                  pl.BlockSpec((tm, V7X_LANES), lambda i, p: (i, 0)),
                  pl.BlockSpec((tm, d), lambda i, p: (i, 0)),
                  pl.BlockSpec((1, d), lambda i, p: (0, 0)),
                  pl.BlockSpec((1, d), lambda i, p: (0, 0))],
        out_specs=pl.BlockSpec((tm, d), lambda i, p: (i, 0)),
        scratch_shapes=[pltpu.VMEM((2, TOP_K, tm, d), F32), pltpu.SemaphoreType.DMA((2,))],
    )
    return pl.pallas_call(
        _combine_kernel,
        grid_spec=grid_spec,
        out_shape=jax.ShapeDtypeStruct((t, d), F32),
        compiler_params=_params(("arbitrary",)),
        name="moe_combine",
    )(pos, y_rows, gates, h, g.reshape(1, d), b.reshape(1, d))


def _expert_layout(idx, rank, counts, n_tiles, tm):
    experts = jnp.arange(N_EXPERTS, dtype=jnp.int32)
    tiles_per = (counts + tm - 1) // tm
    tile_end = jnp.cumsum(tiles_per).astype(jnp.int32)
    tile_off = tile_end - tiles_per
    dest = jnp.sum(jnp.where(idx[..., None] == experts, tile_off * tm, 0), axis=-1) + rank

    nonempty = counts > 0
    first_e = jnp.min(jnp.where(nonempty, experts, N_EXPERTS))
    last_e = jnp.max(jnp.where(nonempty, experts, 0))
    group_of = jnp.cumsum(nonempty.astype(jnp.int32)) - 1
    later = (experts[None, :] > experts[:, None]) & nonempty[None, :]
    next_e = jnp.min(jnp.where(later, experts[None, :], N_EXPERTS), axis=1)
    next_e = jnp.where(next_e == N_EXPERTS, first_e, next_e)

    tid = jnp.arange(n_tiles, dtype=jnp.int32)
    valid = tid < tile_end[-1]
    te = jnp.sum((tid[:, None] >= tile_end[None, :]).astype(jnp.int32), axis=1)
    te = jnp.where(valid, jnp.minimum(te, N_EXPERTS - 1), last_e)
    onehot = te[:, None] == experts[None, :]

    def lookup(table):
        return jnp.sum(jnp.where(onehot, table[None, :], 0), axis=1)

    local = tid - lookup(tile_off)
    rows_used = jnp.where(valid, jnp.clip(lookup(counts) - local * tm, 0, tm), 0)
    meta = jnp.stack([
        te,
        valid.astype(jnp.int32),
        (valid & (local == 0)).astype(jnp.int32),
        (te == last_e).astype(jnp.int32),
        lookup(next_e),
        lookup(group_of),
        rows_used,
        jnp.broadcast_to(jnp.sum(nonempty.astype(jnp.int32)), (n_tiles,)),
    ]).astype(jnp.int32)
    return dest.reshape(-1).astype(jnp.int32), meta


def _moe(h, routing, w_up, b_up, w_down, b_down, g, b):
    t, d = h.shape
    tm = EXPERT_TILE
    idx_p, gate_p, rank_p, cnt_p = routing
    n_rows = t * TOP_K + N_EXPERTS * tm
    dest, meta = _expert_layout(idx_p[:, :TOP_K], rank_p[:, :TOP_K],
                                cnt_p[0, :N_EXPERTS].astype(jnp.int32), n_rows // tm, tm)
    xs = _dispatch(h, dest, meta, n_rows)
    act = _moe_up(xs, w_up, b_up, meta)
    y_rows = _moe_down(act, w_down, b_down, meta)
    return _combine(y_rows, dest, gate_p, h, g, b)


def kernel(x, mem, ln_in_g, ln_in_b, rel_bias, w_in, b_in, attn_sinks, w_a_out, w_b_out,
           w_mix_out, ln1_g, ln1_b, w_xq, w_xkv, w_xo, ln2_g, ln2_b, w_router, b_router,
           w_up, b_up, w_down, b_down, ln3_g, ln3_b):
    batch, seq, d = x.shape
    mem_len = mem.shape[1]
    t = batch * seq
    a_cols = A_WIDTH + 2 * A_KV_WIDTH
    b_cols = 3 * B_WIDTH

    g_col = a_cols + b_cols
    h, hb, qkv_a = _ln_proj(x.reshape(t, d), ln_in_g, ln_in_b, w_in[0], b_in[0, :a_cols],
                            a_cols, BF16)
    for l in range(DEPTH):
        if l > 0:
            qkv_a = _mm_bias(hb, w_in[l], b_in[l, :a_cols], 0, a_cols, BF16, "in_proj_a")
        qkv_b = _mm_bias(hb, w_in[l], b_in[l, a_cols:g_col], a_cols, b_cols, BF16, "in_proj_b")
        gates = _mm_bias(hb, w_in[l], b_in[l, g_col:], g_col, 2 * d, F32, "in_proj_gates")
        ya = _swa(qkv_a, attn_sinks[l], rel_bias, batch, seq)
        yb = _stick_breaking(qkv_b, batch, seq)
        h, hb = _mix(ya, yb, gates, h, w_a_out[l].astype(BF16), w_b_out[l].astype(BF16),
                     w_mix_out[l].astype(BF16), ln1_g[l], ln1_b[l])
        kv = _mm_bias(mem.reshape(batch * mem_len, d).astype(BF16), w_xkv[l],
                      jnp.zeros((2 * d,), F32), 0, 2 * d, BF16, "mem_kv")
        h, *routing = _xattn_route(hb, h, kv, w_xq[l].astype(BF16), w_xo[l].astype(BF16),
                                   ln2_g[l], ln2_b[l], w_router[l], b_router[l],
                                   batch, seq, mem_len)
        h = _moe(h, routing, w_up[l], b_up[l], w_down[l], b_down[l], ln3_g[l], ln3_b[l])
        if l + 1 < DEPTH:
            hb = h.astype(BF16)
    return h.reshape(batch, seq, d)
```

```python
import functools

import jax
import jax.numpy as jnp
import numpy as np
from jax import lax
from jax.experimental import pallas as pl
from jax.experimental.pallas import tpu as pltpu

DEPTH = 1
CHUNK = 64
BLK = 128
HEAD_DIM = 64
A_HEADS = 16
A_KV_HEADS = 2
A_REP = A_HEADS // A_KV_HEADS
WIN_CHUNKS = 2
B_HEADS = 16
A_WIDTH = A_HEADS * HEAD_DIM
A_KV_WIDTH = A_KV_HEADS * HEAD_DIM
B_WIDTH = B_HEADS * HEAD_DIM
REL_BUCKETS = 32
REL_MAX_DIST = 128
X_HEADS = 4
N_EXPERTS = 32
TOP_K = 4
SWIGLU_LIMIT = 7.0
SWIGLU_ALPHA = 1.702
LN_EPS = 1e-5
DEEPNORM_ALPHA = (2.0 * DEPTH) ** 0.25
QK_SCALE = HEAD_DIM ** -0.5
V7X_LANES = 128
V7X_VMEM_LIMIT_BYTES = 56 * 1024 * 1024

ROW_TILE = 512
MM_ROW_TILE = 1024
ATT_TILE = 256
SB_TILE = 256
SB_PAIRS = 2
SB_ROWS = 32
SB_UNROLL = True
EXPERT_TILE = 256
EXPERT_TILE_PARTS = 4
COMBINE_TILE = 128
SB_LOG_ZERO = -105.0

F32 = jnp.float32
BF16 = jnp.bfloat16
NT_DIMS = (((1,), (1,)), ((), ()))


def _params(semantics):
    return pltpu.CompilerParams(dimension_semantics=semantics,
                                vmem_limit_bytes=V7X_VMEM_LIMIT_BYTES)


def _const_spec(shape):
    nd = len(shape)
    return pl.BlockSpec(shape, lambda *_: (0,) * nd, pipeline_mode=pl.Buffered(1))


def _layer_norm(x, g, b):
    mu = jnp.mean(x, axis=-1, keepdims=True)
    xc = x - mu
    var = jnp.mean(xc * xc, axis=-1, keepdims=True)
    return xc * lax.rsqrt(var + LN_EPS) * g + b


def _ln_proj_kernel(x_ref, g_ref, b_ref, w_hbm, bias_ref, h_ref, hb_ref, o_ref, stage, wbf, sem,
                    *, n):
    @pl.when(pl.program_id(0) == 0)
    def _():
        slab = pltpu.make_async_copy(w_hbm.at[:, pl.ds(0, n)], stage, sem)
        slab.start()
        slab.wait()
        wbf[...] = stage[...].astype(BF16)

    y = _layer_norm(x_ref[...], g_ref[...], b_ref[...])
    yb = y.astype(BF16)
    h_ref[...] = y
    hb_ref[...] = yb
    acc = jnp.dot(yb, wbf[...], preferred_element_type=F32)
    o_ref[...] = (acc + bias_ref[...]).astype(o_ref.dtype)


def _ln_proj(x2d, g, b, w, bias, n, out_dtype):
    t, d = x2d.shape
    tm = min(ROW_TILE, t)
    row = pl.BlockSpec((tm, d), lambda i: (i, 0))
    return pl.pallas_call(
        functools.partial(_ln_proj_kernel, n=n),
        grid=(t // tm,),
        in_specs=[row, _const_spec((1, d)), _const_spec((1, d)),
                  pl.BlockSpec(memory_space=pl.ANY), _const_spec((1, n))],
        out_specs=[row, row, pl.BlockSpec((tm, n), lambda i: (i, 0))],
        out_shape=[jax.ShapeDtypeStruct((t, d), F32), jax.ShapeDtypeStruct((t, d), BF16),
                   jax.ShapeDtypeStruct((t, n), out_dtype)],
        scratch_shapes=[pltpu.VMEM((d, n), F32), pltpu.VMEM((d, n), BF16),
                        pltpu.SemaphoreType.DMA(())],
        compiler_params=_params(("arbitrary",)),
        name="ln_in_proj_a",
    )(x2d, g.reshape(1, d), b.reshape(1, d), w, bias.reshape(1, n).astype(F32))


def _mm_bias_kernel(a_ref, w_hbm, b_ref, o_ref, stage, wbf, sem, *, col0, tn, nj):
    j = pl.program_id(0)
    i = pl.program_id(1)

    def slab_copy(jj):
        col = pl.multiple_of(col0 + jj * tn, V7X_LANES)
        return pltpu.make_async_copy(w_hbm.at[:, pl.ds(col, tn)], stage, sem)

    @pl.when(i == 0)
    def _():
        @pl.when(j == 0)
        def _():
            slab_copy(0).start()

        slab_copy(j).wait()
        wbf[...] = stage[...].astype(BF16)

        @pl.when(j + 1 < nj)
        def _():
            slab_copy(j + 1).start()

    acc = jnp.dot(a_ref[...], wbf[...], preferred_element_type=F32)
    o_ref[...] = (acc + b_ref[...]).astype(o_ref.dtype)


def _pick_tile(n, target):
    best = None
    for c in range(V7X_LANES, min(n, target) + 1, V7X_LANES):
        if n % c == 0:
            best = c
    return n if best is None else best


def _mm_bias(a, w, b, col0, n, out_dtype, name):
    m, k = a.shape
    tm = min(MM_ROW_TILE, m)
    tn = _pick_tile(n, 1024)
    nj = n // tn
    return pl.pallas_call(
        functools.partial(_mm_bias_kernel, col0=col0, tn=tn, nj=nj),
        grid=(nj, m // tm),
        in_specs=[pl.BlockSpec((tm, k), lambda j, i: (i, 0)),
                  pl.BlockSpec(memory_space=pl.ANY),
                  pl.BlockSpec((1, tn), lambda j, i: (0, j))],
        out_specs=pl.BlockSpec((tm, tn), lambda j, i: (i, j)),
        out_shape=jax.ShapeDtypeStruct((m, n), out_dtype),
        scratch_shapes=[pltpu.VMEM((k, tn), F32), pltpu.VMEM((k, tn), BF16),
                        pltpu.SemaphoreType.DMA(())],
        compiler_params=_params(("arbitrary", "arbitrary")),
        name=name,
    )(a, w, b.reshape(1, n).astype(F32))


def _swa_kernel(sink_ref, q_ref, kp_ref, kc_ref, vp_ref, vc_ref, bias_ref, o_ref):
    n = pl.program_id(1)
    kk = jnp.concatenate([kp_ref[...], kc_ref[...]], axis=0)
    vv = jnp.concatenate([vp_ref[...], vc_ref[...]], axis=0)
    col = lax.broadcasted_iota(jnp.int32, (BLK, 2 * BLK), 1)
    exists = (col >= BLK) | (n > 0)
    for h in range(A_HEADS):
        g = h // A_REP
        qh = q_ref[:, h * HEAD_DIM:(h + 1) * HEAD_DIM] * QK_SCALE
        kg = kk[:, g * HEAD_DIM:(g + 1) * HEAD_DIM]
        vg = vv[:, g * HEAD_DIM:(g + 1) * HEAD_DIM]
        s = lax.dot_general(qh, kg, NT_DIMS, preferred_element_type=F32)
        s = jnp.where(exists, s + bias_ref[h], -jnp.inf)
        sink = sink_ref[h]
        m = jnp.maximum(jnp.max(s, axis=-1, keepdims=True), sink)
        p = jnp.exp(s - m)
        denom = jnp.sum(p, axis=-1, keepdims=True) + jnp.exp(sink - m)
        o = jnp.dot(p.astype(BF16), vg, preferred_element_type=F32) / denom
        o_ref[:, h * HEAD_DIM:(h + 1) * HEAD_DIM] = o.astype(o_ref.dtype)


def _rel_bucket(rel):
    half = REL_BUCKETS // 2
    max_exact = half // 2
    base = jnp.where(rel > 0, half, 0)
    n = jnp.abs(rel)
    nf = jnp.maximum(n, 1).astype(F32)
    large = max_exact + (jnp.log(nf / max_exact) / np.log(REL_MAX_DIST / max_exact)
                         * (half - max_exact)).astype(jnp.int32)
    large = jnp.minimum(large, half - 1)
    return base + jnp.where(n < max_exact, n, large)


def _swa_bias_table(rel_bias):
    rel = jnp.arange(-(2 * BLK - 1), BLK)
    line = jnp.transpose(rel_bias[_rel_bucket(rel)]).astype(F32)
    bias = jnp.stack([line[:, BLK - 1 - q:3 * BLK - 1 - q] for q in range(BLK)], axis=1)
    qi = np.arange(BLK)[:, None]
    kj = np.arange(2 * BLK)[None, :]
    dchunk = (kj // CHUNK - BLK // CHUNK) - qi // CHUNK
    band = (dchunk <= 0) & (dchunk >= -WIN_CHUNKS)
    return jnp.where(jnp.asarray(band)[None], bias, -jnp.inf)


def _swa(qkv, sinks, rel_bias, batch, seq):
    nb = seq // BLK
    kcol = A_WIDTH // A_KV_WIDTH
    vcol = kcol + 1
    kv_blk = (BLK, A_KV_WIDTH)
    grid_spec = pltpu.PrefetchScalarGridSpec(
        num_scalar_prefetch=1,
        grid=(batch, nb),
        in_specs=[
            pl.BlockSpec((BLK, A_WIDTH), lambda b, n, s: (b * nb + n, 0)),
            pl.BlockSpec(kv_blk, lambda b, n, s: (b * nb + jnp.maximum(n - 1, 0), kcol)),
            pl.BlockSpec(kv_blk, lambda b, n, s: (b * nb + n, kcol)),
            pl.BlockSpec(kv_blk, lambda b, n, s: (b * nb + jnp.maximum(n - 1, 0), vcol)),
            pl.BlockSpec(kv_blk, lambda b, n, s: (b * nb + n, vcol)),
            pl.BlockSpec((A_HEADS, BLK, 2 * BLK), lambda b, n, s: (0, 0, 0),
                         pipeline_mode=pl.Buffered(1)),
        ],
        out_specs=pl.BlockSpec((BLK, A_WIDTH), lambda b, n, s: (b * nb + n, 0)),
    )
    return pl.pallas_call(
        _swa_kernel,
        grid_spec=grid_spec,
        out_shape=jax.ShapeDtypeStruct((batch * seq, A_WIDTH), BF16),
        compiler_params=_params(("arbitrary", "arbitrary")),
        name="swa_attn",
    )(sinks.astype(F32), qkv, qkv, qkv, qkv, qkv, _swa_bias_table(rel_bias))


def _sb_kernel(q_ref, k_ref, v_ref, tri_ref, o_ref, qh_scr, z_scr, hl_scr, s_scr, w_scr,
               carry_scr, acc_scr):
    t = q_ref.shape[0]
    i = pl.program_id(2)
    n_chunks = t // SB_ROWS
    n_heads = 2 * SB_PAIRS
    pair_w = 2 * HEAD_DIM
    lane = lax.broadcasted_iota(jnp.int32, (1, pair_w), 1)
    first = lane < HEAD_DIM
    zero = jnp.zeros((), BF16)

    def pair_cols(h):
        return slice((h // 2) * pair_w, (h // 2 + 1) * pair_w)

    def own_half(h, x):
        return jnp.where(first, x, zero) if h % 2 == 0 else jnp.where(first, zero, x)

    q2 = q_ref[...] * QK_SCALE
    for h in range(n_heads):
        qh_scr[h] = own_half(h, q2[:, pair_cols(h)])
    carry_scr[...] = jnp.zeros_like(carry_scr)
    acc_scr[...] = jnp.zeros_like(acc_scr)
    row = lax.broadcasted_iota(jnp.int32, (SB_ROWS, t), 0)
    col = lax.broadcasted_iota(jnp.int32, (SB_ROWS, t), 1)

    def key_tile(j, diagonal):
        start = pl.multiple_of(j * t, t)
        k2 = k_ref[pl.ds(start, t), :]
        v2 = v_ref[pl.ds(start, t), :]
        for h in range(n_heads):
            z_scr[h] = lax.dot_general(qh_scr[h], k2[:, pair_cols(h)], NT_DIMS,
                                       preferred_element_type=F32)

        def keep_chunk(c, carry):
            r0 = pl.multiple_of(c * SB_ROWS, SB_ROWS)
            rows = pl.ds(r0, SB_ROWS)
            for h in range(n_heads):
                z = z_scr[h, rows, :]
                drop = jnp.maximum(z, 0.0) + jnp.log(1.0 + jnp.exp(-jnp.abs(z)))
                if diagonal:
                    drop = jnp.where(col < row + r0, drop, 0.0)
                hi = drop.astype(BF16)
                hl_scr[h, rows, 0:t] = hi
                hl_scr[h, rows, t:2 * t] = (drop - hi.astype(F32)).astype(BF16)
            return carry

        lax.fori_loop(0, n_chunks, keep_chunk, 0, unroll=SB_UNROLL)
        tri = tri_ref[...]
        for h in range(n_heads):
            s_scr[h] = jnp.dot(hl_scr[h], tri, preferred_element_type=F32)

        def weight_chunk(c, top):
            r0 = pl.multiple_of(c * SB_ROWS, SB_ROWS)
            rows = pl.ds(r0, SB_ROWS)
            for h in range(n_heads):
                carry = carry_scr[h, rows, :]
                log_w = (z_scr[h, rows, :] + s_scr[h, rows, :]
                         + jnp.concatenate([carry] * (t // V7X_LANES), axis=1))
                w = jnp.exp(log_w)
                if diagonal:
                    w = jnp.where(col < row + r0, w, 0.0)
                w_scr[h // 2, rows, (h % 2) * t:(h % 2 + 1) * t] = w.astype(BF16)
                carry = carry + jnp.broadcast_to(s_scr[h, rows, 0:1], (SB_ROWS, V7X_LANES))
                carry_scr[h, rows, :] = carry
                top = jnp.maximum(top, carry)
            return top

        top = lax.fori_loop(0, n_chunks, weight_chunk,
                            jnp.full((SB_ROWS, V7X_LANES), -jnp.inf, F32), unroll=SB_UNROLL)
        for p in range(SB_PAIRS):
            cols = pair_cols(2 * p)
            v_pair = jnp.concatenate([own_half(0, v2[:, cols]), own_half(1, v2[:, cols])], axis=0)
            acc_scr[:, cols] += jnp.dot(w_scr[p], v_pair, preferred_element_type=F32)
        return (jnp.max(top) > SB_LOG_ZERO).astype(jnp.int32)

    live = key_tile(i, True)

    def cond(state):
        return (state[0] >= 0) & (state[1] > 0)

    def body(state):
        return state[0] - 1, key_tile(state[0], False)

    lax.while_loop(cond, body, (i - 1, live))
    o_ref[...] = acc_scr[...].astype(o_ref.dtype)


def _stick_breaking(qkv, batch, seq):
    t = min(SB_TILE, seq)
    nq = seq // t
    width = SB_PAIRS * 2 * HEAD_DIM
    groups = B_WIDTH // width
    heads = 2 * SB_PAIRS
    j = np.arange(2 * t)[:, None] % t
    s = np.arange(t)[None, :]
    tri = jnp.asarray(-(j >= s).astype(np.float32), dtype=BF16)
    return pl.pallas_call(
        _sb_kernel,
        grid=(batch, groups, nq),
        in_specs=[
            pl.BlockSpec((t, width), lambda b, p, i: (b * nq + i, p)),
            pl.BlockSpec((seq, width), lambda b, p, i: (b, groups + p)),
            pl.BlockSpec((seq, width), lambda b, p, i: (b, 2 * groups + p)),
            _const_spec((2 * t, t)),
        ],
        out_specs=pl.BlockSpec((t, width), lambda b, p, i: (b * nq + i, p)),
        out_shape=jax.ShapeDtypeStruct((batch * seq, B_WIDTH), BF16),
        scratch_shapes=[
            pltpu.VMEM((heads, t, 2 * HEAD_DIM), BF16),
            pltpu.VMEM((heads, t, t), F32),
            pltpu.VMEM((heads, t, 2 * t), BF16),
            pltpu.VMEM((heads, t, t), F32),
            pltpu.VMEM((SB_PAIRS, t, 2 * t), BF16),
            pltpu.VMEM((heads, t, V7X_LANES), F32),
            pltpu.VMEM((t, width), F32),
        ],
        compiler_params=_params(("arbitrary", "arbitrary", "arbitrary")),
        name="stick_breaking",
    )(qkv, qkv, qkv, tri)


def _load_weight_bf16(w_hbm, w_bf, stage, sems):
    rows = stage.shape[1]
    n = w_hbm.shape[0] // rows

    def chunk_copy(c):
        return pltpu.make_async_copy(w_hbm.at[pl.ds(c * rows, rows)], stage.at[c % 2],
                                     sems.at[c % 2])

    chunk_copy(0).start()
    for c in range(n):
        if c + 1 < n:
            chunk_copy(c + 1).start()
        chunk_copy(c).wait()
        w_bf[pl.ds(c * rows, rows), :] = stage[c % 2].astype(BF16)


WEIGHT_STAGE_ROWS = 512


def _mix_kernel(ya_ref, yb_ref, gates_a_ref, gates_b_ref, h_ref, wa_hbm, wb_hbm, wm_hbm,
                g_ref, b_ref, o_ref, ob_ref, wa_ref, wb_ref, wm_ref, stage, sems):
    @pl.when(pl.program_id(0) == 0)
    def _():
        _load_weight_bf16(wa_hbm, wa_ref, stage, sems)
        _load_weight_bf16(wb_hbm, wb_ref, stage, sems)
        _load_weight_bf16(wm_hbm, wm_ref, stage, sems)

    pa = jnp.dot(ya_ref[...], wa_ref[...], preferred_element_type=F32)
    pb = jnp.dot(yb_ref[...], wb_ref[...], preferred_element_type=F32)
    merged = jax.nn.sigmoid(gates_a_ref[...]) * pa + jax.nn.sigmoid(gates_b_ref[...]) * pb
    mixed = jnp.dot(merged.astype(BF16), wm_ref[...], preferred_element_type=F32)
    y = _layer_norm(DEEPNORM_ALPHA * h_ref[...] + mixed, g_ref[...], b_ref[...])
    o_ref[...] = y
    ob_ref[...] = y.astype(BF16)


def _mix(ya, yb, gates, h, wa, wb, wm, g, b):
    t, d = h.shape
    tm = min(ATT_TILE, t)
    row = pl.BlockSpec((tm, d), lambda i: (i, 0))
    return pl.pallas_call(
        _mix_kernel,
        grid=(t // tm,),
        in_specs=[pl.BlockSpec((tm, A_WIDTH), lambda i: (i, 0)),
                  pl.BlockSpec((tm, B_WIDTH), lambda i: (i, 0)),
                  pl.BlockSpec((tm, d), lambda i: (i, 0)),
                  pl.BlockSpec((tm, d), lambda i: (i, 1)),
                  row,
                  pl.BlockSpec(memory_space=pl.ANY), pl.BlockSpec(memory_space=pl.ANY),
                  pl.BlockSpec(memory_space=pl.ANY),
                  _const_spec((1, d)), _const_spec((1, d))],
        out_specs=[row, row],
        out_shape=[jax.ShapeDtypeStruct((t, d), F32), jax.ShapeDtypeStruct((t, d), BF16)],
        scratch_shapes=[pltpu.VMEM((A_WIDTH, d), BF16), pltpu.VMEM((B_WIDTH, d), BF16),
                        pltpu.VMEM((d, d), BF16),
                        pltpu.VMEM((2, min(WEIGHT_STAGE_ROWS, d), d), F32),
                        pltpu.SemaphoreType.DMA((2,))],
        compiler_params=_params(("arbitrary",)),
        name="mix_out",
    )(ya, yb, gates, gates, h, wa, wb, wm, g.reshape(1, d), b.reshape(1, d))


def _xattn_kernel(hb_ref, h_ref, k_ref, v_ref, wq_hbm, wo_hbm, g_ref, b_ref, wr_ref,
                  rb_ref, ltri_ref, o_ref, idx_ref, gate_ref, rank_ref, cnt_ref, run_ref,
                  wq_ref, wo_ref, stage, sems):
    first_step = (pl.program_id(0) == 0) & (pl.program_id(1) == 0)

    @pl.when(first_step)
    def _():
        _load_weight_bf16(wq_hbm, wq_ref, stage, sems)
        _load_weight_bf16(wo_hbm, wo_ref, stage, sems)

    d = h_ref.shape[1]
    hd = d // X_HEADS
    scale = hd ** -0.5
    q = jnp.dot(hb_ref[...], wq_ref[...], preferred_element_type=F32).astype(BF16)
    outs = []
    for x in range(X_HEADS):
        sl = slice(x * hd, (x + 1) * hd)
        s = lax.dot_general(q[:, sl], k_ref[:, sl], NT_DIMS, preferred_element_type=F32) * scale
        m = jnp.max(s, axis=-1, keepdims=True)
        p = jnp.exp(s - m)
        denom = jnp.sum(p, axis=-1, keepdims=True)
        o = jnp.dot(p.astype(BF16), v_ref[:, sl], preferred_element_type=F32) / denom
        outs.append(o.astype(BF16))
    xo = jnp.dot(jnp.concatenate(outs, axis=1), wo_ref[...], preferred_element_type=F32)
    h_out = _layer_norm(DEEPNORM_ALPHA * h_ref[...] + xo, g_ref[...], b_ref[...])
    o_ref[...] = h_out
    _route_tile(h_out, first_step, wr_ref, rb_ref, ltri_ref, idx_ref, gate_ref,
                rank_ref, cnt_ref, run_ref)


def _xattn_route(hb, h, kv, wq, wo, g, b, w_router, b_router, batch, seq, mem_len):
    t, d = h.shape
    tm = min(ATT_TILE, seq)
    ns = seq // tm
    w_pad = jnp.zeros((d, V7X_LANES), F32).at[:, :N_EXPERTS].set(w_router)
    w_hi = w_pad.astype(BF16)
    w_lo = (w_pad - w_hi.astype(F32)).astype(BF16)
    w_split = jnp.concatenate([w_hi, w_lo], axis=1)
    b_pad = jnp.zeros((1, V7X_LANES), F32).at[0, :N_EXPERTS].set(b_router)
    ltri = jnp.asarray(np.tril(np.ones((tm, tm), np.float32), -1), dtype=BF16)
    row = pl.BlockSpec((tm, d), lambda bi, i: (bi * ns + i, 0))
    small = pl.BlockSpec((tm, V7X_LANES), lambda bi, i: (bi * ns + i, 0))
    return pl.pallas_call(
        _xattn_kernel,
        grid=(batch, ns),
        in_specs=[row, row,
                  pl.BlockSpec((mem_len, d), lambda bi, i: (bi, 0)),
                  pl.BlockSpec((mem_len, d), lambda bi, i: (bi, 1)),
                  pl.BlockSpec(memory_space=pl.ANY), pl.BlockSpec(memory_space=pl.ANY),
                  _const_spec((1, d)), _const_spec((1, d)),
                  _const_spec((d, 2 * V7X_LANES)),
                  _const_spec((1, V7X_LANES)), _const_spec((tm, tm))],
        out_specs=[row, small, small, small,
                   pl.BlockSpec((1, V7X_LANES), lambda bi, i: (0, 0))],
        out_shape=[jax.ShapeDtypeStruct((t, d), F32),
                   jax.ShapeDtypeStruct((t, V7X_LANES), jnp.int32),
                   jax.ShapeDtypeStruct((t, V7X_LANES), F32),
                   jax.ShapeDtypeStruct((t, V7X_LANES), jnp.int32),
                   jax.ShapeDtypeStruct((1, V7X_LANES), F32)],
        scratch_shapes=[pltpu.VMEM((1, V7X_LANES), F32),
                        pltpu.VMEM((d, d), BF16), pltpu.VMEM((d, d), BF16),
                        pltpu.VMEM((2, min(WEIGHT_STAGE_ROWS, d), d), F32),
                        pltpu.SemaphoreType.DMA((2,))],
        compiler_params=_params(("arbitrary", "arbitrary")),
        name="xattn_route",
    )(hb, h, kv, kv, wq, wo, g.reshape(1, d), b.reshape(1, d), w_split, b_pad, ltri)


def _route_tile(h, first_step, w_ref, b_ref, ltri_ref, idx_ref, gate_ref, rank_ref,
                cnt_ref, run_ref):
    @pl.when(first_step)
    def _():
        run_ref[...] = jnp.zeros_like(run_ref)

    h_hi = h.astype(BF16)
    h_lo = (h - h_hi.astype(F32)).astype(BF16)
    hi_terms = jnp.dot(h_hi, w_ref[...], preferred_element_type=F32)
    lo_hi = jnp.dot(h_lo, w_ref[:, 0:V7X_LANES], preferred_element_type=F32)
    logits = (hi_terms[:, 0:V7X_LANES] + (lo_hi + hi_terms[:, V7X_LANES:2 * V7X_LANES])
              + b_ref[...])
    tm = logits.shape[0]
    lane_i = lax.broadcasted_iota(jnp.int32, (tm, V7X_LANES), 1)
    lane = lane_i.astype(F32)
    rem = jnp.where(lane_i < N_EXPERTS, logits, -jnp.inf)
    vals, idxs = [], []
    for _ in range(TOP_K):
        m = jnp.max(rem, axis=-1, keepdims=True)
        ik = jnp.min(jnp.where(rem == m, lane, float(V7X_LANES)), axis=-1, keepdims=True)
        vals.append(m)
        idxs.append(ik)
        rem = jnp.where(lane == ik, -jnp.inf, rem)
    exps = [jnp.exp(v - vals[0]) for v in vals]
    denom = exps[0] + exps[1] + exps[2] + exps[3]
    sel = jnp.zeros((tm, V7X_LANES), F32)
    for ik in idxs:
        sel = sel + (lane == ik).astype(F32)
    before = jnp.dot(ltri_ref[...], sel.astype(BF16), preferred_element_type=F32) + run_ref[...]
    idx_out = jnp.zeros((tm, V7X_LANES), jnp.int32)
    gate_out = jnp.zeros((tm, V7X_LANES), F32)
    rank_out = jnp.zeros((tm, V7X_LANES), jnp.int32)
    for k in range(TOP_K):
        rank_k = jnp.sum(jnp.where(lane == idxs[k], before, 0.0), axis=-1, keepdims=True)
        idx_out = jnp.where(lane_i == k, idxs[k].astype(jnp.int32), idx_out)
        gate_out = jnp.where(lane_i == k, exps[k] / denom, gate_out)
        rank_out = jnp.where(lane_i == k, rank_k.astype(jnp.int32), rank_out)
    idx_ref[...] = idx_out
    gate_ref[...] = gate_out
    rank_ref[...] = rank_out
    run_ref[...] = run_ref[...] + jnp.sum(sel, axis=0, keepdims=True)
    cnt_ref[...] = run_ref[...]


DMA_UNROLL = 8
NUM_DMA_QUEUES = 2
WEIGHT_DMA_PRIORITY = 1
(META_EXPERT, META_VALID, META_FIRST, META_LAST_GROUP, META_NEXT_EXPERT, META_GROUP,
 META_ROWS_USED, META_NUM_GROUPS) = range(8)


def _pack_bf16_pairs(x):
    half = x.shape[1] // 2
    lo = lax.bitcast_convert_type(x[:, :half].astype(BF16).astype(F32), jnp.uint32)
    hi = lax.bitcast_convert_type(x[:, half:].astype(BF16).astype(F32), jnp.uint32)
    return (lo >> 16) | (hi & jnp.uint32(0xFFFF0000))


def _unpack_bf16_pairs(words):
    lo = lax.bitcast_convert_type(words << 16, F32).astype(BF16)
    hi = lax.bitcast_convert_type(words & jnp.uint32(0xFFFF0000), F32).astype(BF16)
    return jnp.concatenate([lo, hi], axis=1)


def _dispatch_kernel(dest_ref, meta_ref, h_ref, xs_hbm, pack_ref, zero_ref, sem, zsem):
    tm = h_ref.shape[0]
    base = pl.program_id(0) * tm
    et = zero_ref.shape[0]
    pack_ref[...] = _pack_bf16_pairs(h_ref[...])

    @pl.when(pl.program_id(0) == 0)
    def _():
        zero_ref[...] = jnp.zeros_like(zero_ref)

        def fill_copy(tile):
            start_row = pl.multiple_of(tile * et, et)
            return pltpu.make_async_copy(zero_ref, xs_hbm.at[pl.ds(start_row, et)], zsem)

        def fill_start(tile, c):
            @pl.when(meta_ref[META_ROWS_USED, tile] < et)
            def _():
                fill_copy(tile).start()
            return c

        def fill_wait(tile, c):
            @pl.when(meta_ref[META_ROWS_USED, tile] < et)
            def _():
                fill_copy(tile).wait()
            return c

        n_tiles = xs_hbm.shape[0] // et
        lax.fori_loop(0, n_tiles, fill_start, 0)
        lax.fori_loop(0, n_tiles, fill_wait, 0)

    def row_copy(r, k):
        return pltpu.make_async_copy(pack_ref.at[pl.ds(r, 1)],
                                     xs_hbm.at[pl.ds(dest_ref[(base + r) * TOP_K + k], 1)], sem)

    def start(r, c):
        for k in range(TOP_K):
            row_copy(r, k).start(priority=k % NUM_DMA_QUEUES)
        return c

    def wait(r, c):
        for k in range(TOP_K):
            row_copy(r, k).wait()
        return c

    lax.fori_loop(0, tm, start, 0, unroll=DMA_UNROLL)
    lax.fori_loop(0, tm, wait, 0, unroll=DMA_UNROLL)


def _dispatch(h, dest, meta, n_rows):
    t, d = h.shape
    tm = min(ATT_TILE, t)
    grid_spec = pltpu.PrefetchScalarGridSpec(
        num_scalar_prefetch=2,
        grid=(t // tm,),
        in_specs=[pl.BlockSpec((tm, d), lambda i, dst, m: (i, 0))],
        out_specs=pl.BlockSpec(memory_space=pl.ANY),
        scratch_shapes=[pltpu.VMEM((tm, d // 2), jnp.uint32),
                        pltpu.VMEM((EXPERT_TILE, d // 2), jnp.uint32),
                        pltpu.SemaphoreType.DMA(()), pltpu.SemaphoreType.DMA(())],
    )
    return pl.pallas_call(
        _dispatch_kernel,
        grid_spec=grid_spec,
        out_shape=jax.ShapeDtypeStruct((n_rows, d // 2), jnp.uint32),
        compiler_params=_params(("arbitrary",)),
        name="moe_dispatch",
    )(dest, meta, h)


def _stream_expert_weights(meta_ref, j, t, nj, make_copies, on_arrival):
    @pl.when((j == 0) & (t == 0))
    def _():
        for c in make_copies(meta_ref[META_EXPERT, 0], 0, 0):
            c.start(priority=WEIGHT_DMA_PRIORITY)

    @pl.when(meta_ref[META_FIRST, t] == 1)
    def _():
        slot = (j * meta_ref[META_NUM_GROUPS, t] + meta_ref[META_GROUP, t]) & 1
        for c in make_copies(meta_ref[META_EXPERT, t], j, slot):
            c.wait()
        last_group = meta_ref[META_LAST_GROUP, t] == 1

        @pl.when(jnp.logical_not(last_group & (j == nj - 1)))
        def _():
            next_chunk = jnp.where(last_group, j + 1, j)
            for c in make_copies(meta_ref[META_NEXT_EXPERT, t], next_chunk, 1 - slot):
                c.start(priority=WEIGHT_DMA_PRIORITY)

        on_arrival(slot)


def _for_used_rows(rows_used, o_ref, compute):
    tm = o_ref.shape[0]
    part = tm // EXPERT_TILE_PARTS
    for parts in range(EXPERT_TILE_PARTS + 1):
        used = parts * part

        @pl.when((rows_used > used - part) & (rows_used <= used))
        def _(used=used):
            if used > 0:
                compute(slice(0, used))
            if used < tm:
                o_ref[used:tm, :] = jnp.zeros((tm - used, o_ref.shape[1]), o_ref.dtype)


def _moe_up_kernel(meta_ref, x_ref, w_hbm, bg_ref, bl_ref, o_ref, wf32, wbf, sems, *, tn, nj):
    j = pl.program_id(0)
    t = pl.program_id(1)

    def make_copies(expert, chunk, slot):
        glu_col = pl.multiple_of(chunk * tn, tn)
        lin_col = pl.multiple_of((nj + chunk) * tn, tn)
        return (pltpu.make_async_copy(w_hbm.at[expert, :, pl.ds(glu_col, tn)],
                                      wf32.at[slot, 0], sems.at[slot, 0]),
                pltpu.make_async_copy(w_hbm.at[expert, :, pl.ds(lin_col, tn)],
                                      wf32.at[slot, 1], sems.at[slot, 1]))

    def on_arrival(slot):
        wbf[0] = wf32[slot, 0].astype(BF16)
        wbf[1] = wf32[slot, 1].astype(BF16)

    _stream_expert_weights(meta_ref, j, t, nj, make_copies, on_arrival)

    def expert_rows(rows):
        x = _unpack_bf16_pairs(x_ref[rows, :])
        glu = jnp.dot(x, wbf[0], preferred_element_type=F32) + bg_ref[...]
        lin = jnp.dot(x, wbf[1], preferred_element_type=F32) + bl_ref[...]
        glu = jnp.minimum(glu, SWIGLU_LIMIT)
        lin = jnp.clip(lin, -SWIGLU_LIMIT, SWIGLU_LIMIT)
        act = glu * jax.nn.sigmoid(SWIGLU_ALPHA * glu) * (lin + 1.0)
        o_ref[rows, :] = act.astype(o_ref.dtype)

    _for_used_rows(meta_ref[META_ROWS_USED, t], o_ref, expert_rows)


def _moe_up(xs, w_up, b_up, meta):
    r = xs.shape[0]
    e, d, f2 = w_up.shape
    f = f2 // 2
    tm = EXPERT_TILE
    tn = _pick_tile(f, 1024)
    nj = f // tn
    grid_spec = pltpu.PrefetchScalarGridSpec(
        num_scalar_prefetch=1,
        grid=(nj, r // tm),
        in_specs=[
            pl.BlockSpec((tm, d // 2), lambda j, t, m: (t, 0)),
            pl.BlockSpec(memory_space=pl.ANY),
            pl.BlockSpec((None, 1, tn), lambda j, t, m: (m[META_EXPERT, t], 0, j)),
            pl.BlockSpec((None, 1, tn), lambda j, t, m: (m[META_EXPERT, t], 0, nj + j)),
        ],
        out_specs=pl.BlockSpec((tm, tn), lambda j, t, m: (t, j)),
        scratch_shapes=[pltpu.VMEM((2, 2, d, tn), F32), pltpu.VMEM((2, d, tn), BF16),
                        pltpu.SemaphoreType.DMA((2, 2))],
    )
    b3 = b_up.reshape(e, 1, f2)
    return pl.pallas_call(
        functools.partial(_moe_up_kernel, tn=tn, nj=nj),
        grid_spec=grid_spec,
        out_shape=jax.ShapeDtypeStruct((r, f), BF16),
        compiler_params=_params(("arbitrary", "arbitrary")),
        name="moe_up",
    )(meta, xs, w_up, b3, b3)


def _moe_down_kernel(meta_ref, a_ref, w_hbm, b_ref, o_ref, wf32, wbf, sems, *, tn, nj):
    j = pl.program_id(0)
    t = pl.program_id(1)

    def make_copies(expert, chunk, slot):
        col = pl.multiple_of(chunk * tn, tn)
        return (pltpu.make_async_copy(w_hbm.at[expert, :, pl.ds(col, tn)],
                                      wf32.at[slot], sems.at[slot]),)

    def on_arrival(slot):
        wbf[...] = wf32[slot].astype(BF16)

    _stream_expert_weights(meta_ref, j, t, nj, make_copies, on_arrival)

    def expert_rows(rows):
        o_ref[rows, :] = (jnp.dot(a_ref[rows, :], wbf[...], preferred_element_type=F32)
                          + b_ref[...])

    _for_used_rows(meta_ref[META_ROWS_USED, t], o_ref, expert_rows)


def _moe_down(act, w_down, b_down, meta):
    r, f = act.shape
    e, _, d = w_down.shape
    tm = EXPERT_TILE
    tn = _pick_tile(d, 2048)
    nj = d // tn
    grid_spec = pltpu.PrefetchScalarGridSpec(
        num_scalar_prefetch=1,
        grid=(nj, r // tm),
        in_specs=[
            pl.BlockSpec((tm, f), lambda j, t, m: (t, 0)),
            pl.BlockSpec(memory_space=pl.ANY),
            pl.BlockSpec((None, 1, tn), lambda j, t, m: (m[META_EXPERT, t], 0, j)),
        ],
        out_specs=pl.BlockSpec((tm, tn), lambda j, t, m: (t, j)),
        scratch_shapes=[pltpu.VMEM((2, f, tn), F32), pltpu.VMEM((f, tn), BF16),
                        pltpu.SemaphoreType.DMA((2,))],
    )
    return pl.pallas_call(
        functools.partial(_moe_down_kernel, tn=tn, nj=nj),
        grid_spec=grid_spec,
        out_shape=jax.ShapeDtypeStruct((r, d), F32),
        compiler_params=_params(("arbitrary", "arbitrary")),
        name="moe_down",
    )(meta, act, w_down, b_down.reshape(e, 1, d))


def _combine_kernel(pos_ref, y_hbm, gate_ref, h_ref, g_ref, b_ref, o_ref, buf_ref, sems):
    tm = o_ref.shape[0]
    i = pl.program_id(0)

    def row_copy(step, r, k, slot):
        src_row = pos_ref[(step * tm + r) * TOP_K + k]
        return pltpu.make_async_copy(y_hbm.at[pl.ds(src_row, 1)],
                                     buf_ref.at[slot, k, pl.ds(r, 1)], sems.at[slot])

    def gather_start(step, slot):
        def body(r, c):
            for k in range(TOP_K):
                row_copy(step, r, k, slot).start(priority=k % NUM_DMA_QUEUES)
            return c
        lax.fori_loop(0, tm, body, 0, unroll=DMA_UNROLL)

    def gather_wait(step, slot):
        def body(r, c):
            for k in range(TOP_K):
                row_copy(step, r, k, slot).wait()
            return c
        lax.fori_loop(0, tm, body, 0, unroll=DMA_UNROLL)

    @pl.when(i == 0)
    def _():
        gather_start(0, 0)

    for next_slot in range(2):
        @pl.when((i + 1 < pl.num_programs(0)) & ((i + 1) & 1 == next_slot))
        def _(next_slot=next_slot):
            gather_start(i + 1, next_slot)

    slot = i & 1
    gather_wait(i, slot)
    gates = gate_ref[...]
    ff = gates[:, 0:1] * buf_ref[slot, 0]
    for k in range(1, TOP_K):
        ff = ff + gates[:, k:k + 1] * buf_ref[slot, k]
    o_ref[...] = _layer_norm(DEEPNORM_ALPHA * h_ref[...] + ff, g_ref[...], b_ref[...])


def _combine(y_rows, pos, gates, h, g, b):
    t, d = h.shape
    tm = min(COMBINE_TILE, t)
    grid_spec = pltpu.PrefetchScalarGridSpec(
        num_scalar_prefetch=1,
        grid=(t // tm,),
        in_specs=[pl.BlockSpec(memory_space=pl.ANY),
                  pl.BlockSpec((tm, V7X_LANES), lambda i, p: (i, 0)),
                  pl.BlockSpec((tm, d), lambda i, p: (i, 0)),
                  pl.BlockSpec((1, d), lambda i, p: (0, 0)),
                  pl.BlockSpec((1, d), lambda i, p: (0, 0))],
        out_specs=pl.BlockSpec((tm, d), lambda i, p: (i, 0)),
        scratch_shapes=[pltpu.VMEM((2, TOP_K, tm, d), F32), pltpu.SemaphoreType.DMA((2,))],
    )
    return pl.pallas_call(
        _combine_kernel,
        grid_spec=grid_spec,
        out_shape=jax.ShapeDtypeStruct((t, d), F32),
        compiler_params=_params(("arbitrary",)),
        name="moe_combine",
    )(pos, y_rows, gates, h, g.reshape(1, d), b.reshape(1, d))


def _expert_layout(idx, rank, counts, n_tiles, tm):
    experts = jnp.arange(N_EXPERTS, dtype=jnp.int32)
    tiles_per = (counts + tm - 1) // tm
    tile_end = jnp.cumsum(tiles_per).astype(jnp.int32)
    tile_off = tile_end - tiles_per
    dest = jnp.sum(jnp.where(idx[..., None] == experts, tile_off * tm, 0), axis=-1) + rank

    nonempty = counts > 0
    first_e = jnp.min(jnp.where(nonempty, experts, N_EXPERTS))
    last_e = jnp.max(jnp.where(nonempty, experts, 0))
    group_of = jnp.cumsum(nonempty.astype(jnp.int32)) - 1
    later = (experts[None, :] > experts[:, None]) & nonempty[None, :]
    next_e = jnp.min(jnp.where(later, experts[None, :], N_EXPERTS), axis=1)
    next_e = jnp.where(next_e == N_EXPERTS, first_e, next_e)

    tid = jnp.arange(n_tiles, dtype=jnp.int32)
    valid = tid < tile_end[-1]
    te = jnp.sum((tid[:, None] >= tile_end[None, :]).astype(jnp.int32), axis=1)
    te = jnp.where(valid, jnp.minimum(te, N_EXPERTS - 1), last_e)
    onehot = te[:, None] == experts[None, :]

    def lookup(table):
        return jnp.sum(jnp.where(onehot, table[None, :], 0), axis=1)

    local = tid - lookup(tile_off)
    rows_used = jnp.where(valid, jnp.clip(lookup(counts) - local * tm, 0, tm), 0)
    meta = jnp.stack([
        te,
        valid.astype(jnp.int32),
        (valid & (local == 0)).astype(jnp.int32),
        (te == last_e).astype(jnp.int32),
        lookup(next_e),
        lookup(group_of),
        rows_used,
        jnp.broadcast_to(jnp.sum(nonempty.astype(jnp.int32)), (n_tiles,)),
    ]).astype(jnp.int32)
    return dest.reshape(-1).astype(jnp.int32), meta


def _moe(h, routing, w_up, b_up, w_down, b_down, g, b):
    t, d = h.shape
    tm = EXPERT_TILE
    idx_p, gate_p, rank_p, cnt_p = routing
    n_rows = t * TOP_K + N_EXPERTS * tm
    dest, meta = _expert_layout(idx_p[:, :TOP_K], rank_p[:, :TOP_K],
                                cnt_p[0, :N_EXPERTS].astype(jnp.int32), n_rows // tm, tm)
    xs = _dispatch(h, dest, meta, n_rows)
    act = _moe_up(xs, w_up, b_up, meta)
    y_rows = _moe_down(act, w_down, b_down, meta)
    return _combine(y_rows, dest, gate_p, h, g, b)


def kernel(x, mem, ln_in_g, ln_in_b, rel_bias, w_in, b_in, attn_sinks, w_a_out, w_b_out,
           w_mix_out, ln1_g, ln1_b, w_xq, w_xkv, w_xo, ln2_g, ln2_b, w_router, b_router,
           w_up, b_up, w_down, b_down, ln3_g, ln3_b):
    batch, seq, d = x.shape
    mem_len = mem.shape[1]
    t = batch * seq
    a_cols = A_WIDTH + 2 * A_KV_WIDTH
    b_cols = 3 * B_WIDTH

    g_col = a_cols + b_cols
    h, hb, qkv_a = _ln_proj(x.reshape(t, d), ln_in_g, ln_in_b, w_in[0], b_in[0, :a_cols],
                            a_cols, BF16)
    for l in range(DEPTH):
        if l > 0:
            qkv_a = _mm_bias(hb, w_in[l], b_in[l, :a_cols], 0, a_cols, BF16, "in_proj_a")
        qkv_b = _mm_bias(hb, w_in[l], b_in[l, a_cols:g_col], a_cols, b_cols, BF16, "in_proj_b")
        gates = _mm_bias(hb, w_in[l], b_in[l, g_col:], g_col, 2 * d, F32, "in_proj_gates")
        ya = _swa(qkv_a, attn_sinks[l], rel_bias, batch, seq)
        yb = _stick_breaking(qkv_b, batch, seq)
        h, hb = _mix(ya, yb, gates, h, w_a_out[l], w_b_out[l], w_mix_out[l], ln1_g[l], ln1_b[l])
        kv = _mm_bias(mem.reshape(batch * mem_len, d).astype(BF16), w_xkv[l],
                      jnp.zeros((2 * d,), F32), 0, 2 * d, BF16, "mem_kv")
        h, *routing = _xattn_route(hb, h, kv, w_xq[l], w_xo[l], ln2_g[l], ln2_b[l],
                                   w_router[l], b_router[l], batch, seq, mem_len)
        h = _moe(h, routing, w_up[l], b_up[l], w_down[l], b_down[l], ln3_g[l], ln3_b[l])
        if l + 1 < DEPTH:
            hb = h.astype(BF16)
    return h.reshape(batch, seq, d)
```

```python
import functools

import jax
import jax.numpy as jnp
import numpy as np
from jax import lax
from jax.experimental import pallas as pl
from jax.experimental.pallas import tpu as pltpu

DEPTH = 1
CHUNK = 64
BLK = 128
HEAD_DIM = 64
A_HEADS = 16
A_KV_HEADS = 2
A_REP = A_HEADS // A_KV_HEADS
WIN_CHUNKS = 2
B_HEADS = 16
A_WIDTH = A_HEADS * HEAD_DIM
A_KV_WIDTH = A_KV_HEADS * HEAD_DIM
B_WIDTH = B_HEADS * HEAD_DIM
REL_BUCKETS = 32
REL_MAX_DIST = 128
X_HEADS = 4
N_EXPERTS = 32
TOP_K = 4
SWIGLU_LIMIT = 7.0
SWIGLU_ALPHA = 1.702
LN_EPS = 1e-5
DEEPNORM_ALPHA = (2.0 * DEPTH) ** 0.25
QK_SCALE = HEAD_DIM ** -0.5
V7X_LANES = 128
V7X_VMEM_LIMIT_BYTES = 56 * 1024 * 1024

ROW_TILE = 512
MM_ROW_TILE = 1024
ATT_TILE = 256
SB_TILE = 256
SB_PAIRS = 4
SB_ROWS = 32
EXPERT_TILE = 256
EXPERT_TILE_PARTS = 4
COMBINE_TILE = 128
SB_LOG_ZERO = -105.0

F32 = jnp.float32
BF16 = jnp.bfloat16
NT_DIMS = (((1,), (1,)), ((), ()))


def _params(semantics):
    return pltpu.CompilerParams(dimension_semantics=semantics,
                                vmem_limit_bytes=V7X_VMEM_LIMIT_BYTES)


def _const_spec(shape):
    nd = len(shape)
    return pl.BlockSpec(shape, lambda *_: (0,) * nd, pipeline_mode=pl.Buffered(1))


def _layer_norm(x, g, b):
    mu = jnp.mean(x, axis=-1, keepdims=True)
    xc = x - mu
    var = jnp.mean(xc * xc, axis=-1, keepdims=True)
    return xc * lax.rsqrt(var + LN_EPS) * g + b


def _ln_proj_kernel(x_ref, g_ref, b_ref, w_hbm, bias_ref, h_ref, hb_ref, o_ref, stage, wbf, sem,
                    *, n):
    @pl.when(pl.program_id(0) == 0)
    def _():
        slab = pltpu.make_async_copy(w_hbm.at[:, pl.ds(0, n)], stage, sem)
        slab.start()
        slab.wait()
        wbf[...] = stage[...].astype(BF16)

    y = _layer_norm(x_ref[...], g_ref[...], b_ref[...])
    yb = y.astype(BF16)
    h_ref[...] = y
    hb_ref[...] = yb
    acc = jnp.dot(yb, wbf[...], preferred_element_type=F32)
    o_ref[...] = (acc + bias_ref[...]).astype(o_ref.dtype)


def _ln_proj(x2d, g, b, w, bias, n, out_dtype):
    t, d = x2d.shape
    tm = min(ROW_TILE, t)
    row = pl.BlockSpec((tm, d), lambda i: (i, 0))
    return pl.pallas_call(
        functools.partial(_ln_proj_kernel, n=n),
        grid=(t // tm,),
        in_specs=[row, _const_spec((1, d)), _const_spec((1, d)),
                  pl.BlockSpec(memory_space=pl.ANY), _const_spec((1, n))],
        out_specs=[row, row, pl.BlockSpec((tm, n), lambda i: (i, 0))],
        out_shape=[jax.ShapeDtypeStruct((t, d), F32), jax.ShapeDtypeStruct((t, d), BF16),
                   jax.ShapeDtypeStruct((t, n), out_dtype)],
        scratch_shapes=[pltpu.VMEM((d, n), F32), pltpu.VMEM((d, n), BF16),
                        pltpu.SemaphoreType.DMA(())],
        compiler_params=_params(("arbitrary",)),
        name="ln_in_proj_a",
    )(x2d, g.reshape(1, d), b.reshape(1, d), w, bias.reshape(1, n).astype(F32))


def _mm_bias_kernel(a_ref, w_hbm, b_ref, o_ref, stage, wbf, sem, *, col0, tn, nj):
    j = pl.program_id(0)
    i = pl.program_id(1)

    def slab_copy(jj):
        col = pl.multiple_of(col0 + jj * tn, V7X_LANES)
        return pltpu.make_async_copy(w_hbm.at[:, pl.ds(col, tn)], stage, sem)

    @pl.when(i == 0)
    def _():
        @pl.when(j == 0)
        def _():
            slab_copy(0).start()

        slab_copy(j).wait()
        wbf[...] = stage[...].astype(BF16)

        @pl.when(j + 1 < nj)
        def _():
            slab_copy(j + 1).start()

    acc = jnp.dot(a_ref[...], wbf[...], preferred_element_type=F32)
    o_ref[...] = (acc + b_ref[...]).astype(o_ref.dtype)


def _pick_tile(n, target):
    best = None
    for c in range(V7X_LANES, min(n, target) + 1, V7X_LANES):
        if n % c == 0:
            best = c
    return n if best is None else best


def _mm_bias(a, w, b, col0, n, out_dtype, name):
    m, k = a.shape
    tm = min(MM_ROW_TILE, m)
    tn = _pick_tile(n, 1024)
    nj = n // tn
    return pl.pallas_call(
        functools.partial(_mm_bias_kernel, col0=col0, tn=tn, nj=nj),
        grid=(nj, m // tm),
        in_specs=[pl.BlockSpec((tm, k), lambda j, i: (i, 0)),
                  pl.BlockSpec(memory_space=pl.ANY),
                  pl.BlockSpec((1, tn), lambda j, i: (0, j))],
        out_specs=pl.BlockSpec((tm, tn), lambda j, i: (i, j)),
        out_shape=jax.ShapeDtypeStruct((m, n), out_dtype),
        scratch_shapes=[pltpu.VMEM((k, tn), F32), pltpu.VMEM((k, tn), BF16),
                        pltpu.SemaphoreType.DMA(())],
        compiler_params=_params(("arbitrary", "arbitrary")),
        name=name,
    )(a, w, b.reshape(1, n).astype(F32))


def _swa_kernel(sink_ref, q_ref, kp_ref, kc_ref, vp_ref, vc_ref, bias_ref, o_ref):
    n = pl.program_id(1)
    kk = jnp.concatenate([kp_ref[...], kc_ref[...]], axis=0)
    vv = jnp.concatenate([vp_ref[...], vc_ref[...]], axis=0)
    col = lax.broadcasted_iota(jnp.int32, (BLK, 2 * BLK), 1)
    exists = (col >= BLK) | (n > 0)
    for h in range(A_HEADS):
        g = h // A_REP
        qh = q_ref[:, h * HEAD_DIM:(h + 1) * HEAD_DIM] * QK_SCALE
        kg = kk[:, g * HEAD_DIM:(g + 1) * HEAD_DIM]
        vg = vv[:, g * HEAD_DIM:(g + 1) * HEAD_DIM]
        s = lax.dot_general(qh, kg, NT_DIMS, preferred_element_type=F32)
        s = jnp.where(exists, s + bias_ref[h], -jnp.inf)
        sink = sink_ref[h]
        m = jnp.maximum(jnp.max(s, axis=-1, keepdims=True), sink)
        p = jnp.exp(s - m)
        denom = jnp.sum(p, axis=-1, keepdims=True) + jnp.exp(sink - m)
        o = jnp.dot(p.astype(BF16), vg, preferred_element_type=F32) / denom
        o_ref[:, h * HEAD_DIM:(h + 1) * HEAD_DIM] = o.astype(o_ref.dtype)


def _rel_bucket(rel):
    half = REL_BUCKETS // 2
    max_exact = half // 2
    base = jnp.where(rel > 0, half, 0)
    n = jnp.abs(rel)
    nf = jnp.maximum(n, 1).astype(F32)
    large = max_exact + (jnp.log(nf / max_exact) / np.log(REL_MAX_DIST / max_exact)
                         * (half - max_exact)).astype(jnp.int32)
    large = jnp.minimum(large, half - 1)
    return base + jnp.where(n < max_exact, n, large)


def _swa_bias_table(rel_bias):
    rel = jnp.arange(-(2 * BLK - 1), BLK)
    line = jnp.transpose(rel_bias[_rel_bucket(rel)]).astype(F32)
    bias = jnp.stack([line[:, BLK - 1 - q:3 * BLK - 1 - q] for q in range(BLK)], axis=1)
    qi = np.arange(BLK)[:, None]
    kj = np.arange(2 * BLK)[None, :]
    dchunk = (kj // CHUNK - BLK // CHUNK) - qi // CHUNK
    band = (dchunk <= 0) & (dchunk >= -WIN_CHUNKS)
    return jnp.where(jnp.asarray(band)[None], bias, -jnp.inf)


def _swa(qkv, sinks, rel_bias, batch, seq):
    nb = seq // BLK
    kcol = A_WIDTH // A_KV_WIDTH
    vcol = kcol + 1
    kv_blk = (BLK, A_KV_WIDTH)
    grid_spec = pltpu.PrefetchScalarGridSpec(
        num_scalar_prefetch=1,
        grid=(batch, nb),
        in_specs=[
            pl.BlockSpec((BLK, A_WIDTH), lambda b, n, s: (b * nb + n, 0)),
            pl.BlockSpec(kv_blk, lambda b, n, s: (b * nb + jnp.maximum(n - 1, 0), kcol)),
            pl.BlockSpec(kv_blk, lambda b, n, s: (b * nb + n, kcol)),
            pl.BlockSpec(kv_blk, lambda b, n, s: (b * nb + jnp.maximum(n - 1, 0), vcol)),
            pl.BlockSpec(kv_blk, lambda b, n, s: (b * nb + n, vcol)),
            pl.BlockSpec((A_HEADS, BLK, 2 * BLK), lambda b, n, s: (0, 0, 0),
                         pipeline_mode=pl.Buffered(1)),
        ],
        out_specs=pl.BlockSpec((BLK, A_WIDTH), lambda b, n, s: (b * nb + n, 0)),
    )
    return pl.pallas_call(
        _swa_kernel,
        grid_spec=grid_spec,
        out_shape=jax.ShapeDtypeStruct((batch * seq, A_WIDTH), BF16),
        compiler_params=_params(("arbitrary", "arbitrary")),
        name="swa_attn",
    )(sinks.astype(F32), qkv, qkv, qkv, qkv, qkv, _swa_bias_table(rel_bias))


def _sb_kernel(q_ref, k_ref, v_ref, tri_ref, o_ref, qh_scr, z_scr, hl_scr, s_scr, w_scr,
               carry_scr, acc_scr):
    t = q_ref.shape[0]
    i = pl.program_id(2)
    n_heads = 2 * SB_PAIRS
    pair_w = 2 * HEAD_DIM
    lane = lax.broadcasted_iota(jnp.int32, (1, pair_w), 1)
    first = lane < HEAD_DIM
    zero = jnp.zeros((), BF16)

    def pair_cols(h):
        return slice((h // 2) * pair_w, (h // 2 + 1) * pair_w)

    def own_half(h, x):
        return jnp.where(first, x, zero) if h % 2 == 0 else jnp.where(first, zero, x)

    q2 = q_ref[...] * QK_SCALE
    for h in range(n_heads):
        qh_scr[h] = own_half(h, q2[:, pair_cols(h)])
    carry_scr[...] = jnp.zeros_like(carry_scr)
    acc_scr[...] = jnp.zeros_like(acc_scr)
    row = lax.broadcasted_iota(jnp.int32, (SB_ROWS, t), 0)
    col = lax.broadcasted_iota(jnp.int32, (SB_ROWS, t), 1)

    def key_tile(j, diagonal):
        start = pl.multiple_of(j * t, t)
        k2 = k_ref[pl.ds(start, t), :]
        v2 = v_ref[pl.ds(start, t), :]
        chunks = [slice(r0, r0 + SB_ROWS) for r0 in range(0, t, SB_ROWS)]
        for h in range(n_heads):
            z_scr[h] = lax.dot_general(qh_scr[h], k2[:, pair_cols(h)], NT_DIMS,
                                       preferred_element_type=F32)
        for rows in chunks:
            for h in range(n_heads):
                z = z_scr[h, rows, :]
                drop = jnp.maximum(z, 0.0) + jnp.log(1.0 + jnp.exp(-jnp.abs(z)))
                if diagonal:
                    drop = jnp.where(col < row + rows.start, drop, 0.0)
                hi = drop.astype(BF16)
                hl_scr[h, rows, 0:t] = hi
                hl_scr[h, rows, t:2 * t] = (drop - hi.astype(F32)).astype(BF16)
        tri = tri_ref[...]
        for h in range(n_heads):
            s_scr[h] = jnp.dot(hl_scr[h], tri, preferred_element_type=F32)
        top = jnp.full((SB_ROWS, V7X_LANES), -jnp.inf, F32)
        for rows in chunks:
            for h in range(n_heads):
                carry = carry_scr[h, rows, :]
                log_w = (z_scr[h, rows, :] + s_scr[h, rows, :]
                         + jnp.concatenate([carry] * (t // V7X_LANES), axis=1))
                w = jnp.exp(log_w)
                if diagonal:
                    w = jnp.where(col < row + rows.start, w, 0.0)
                w_scr[h // 2, rows, (h % 2) * t:(h % 2 + 1) * t] = w.astype(BF16)
                carry = carry + jnp.broadcast_to(s_scr[h, rows, 0:1], (SB_ROWS, V7X_LANES))
                carry_scr[h, rows, :] = carry
                top = jnp.maximum(top, carry)
        for p in range(SB_PAIRS):
            cols = pair_cols(2 * p)
            v_pair = jnp.concatenate([own_half(0, v2[:, cols]), own_half(1, v2[:, cols])], axis=0)
            acc_scr[:, cols] += jnp.dot(w_scr[p], v_pair, preferred_element_type=F32)
        return (jnp.max(top) > SB_LOG_ZERO).astype(jnp.int32)

    live = key_tile(i, True)

    def cond(state):
        return (state[0] >= 0) & (state[1] > 0)

    def body(state):
        return state[0] - 1, key_tile(state[0], False)

    lax.while_loop(cond, body, (i - 1, live))
    o_ref[...] = acc_scr[...].astype(o_ref.dtype)


def _stick_breaking(qkv, batch, seq):
    t = min(SB_TILE, seq)
    nq = seq // t
    width = SB_PAIRS * 2 * HEAD_DIM
    groups = B_WIDTH // width
    heads = 2 * SB_PAIRS
    j = np.arange(2 * t)[:, None] % t
    s = np.arange(t)[None, :]
    tri = jnp.asarray(-(j >= s).astype(np.float32), dtype=BF16)
    return pl.pallas_call(
        _sb_kernel,
        grid=(batch, groups, nq),
        in_specs=[
            pl.BlockSpec((t, width), lambda b, p, i: (b * nq + i, p)),
            pl.BlockSpec((seq, width), lambda b, p, i: (b, groups + p)),
            pl.BlockSpec((seq, width), lambda b, p, i: (b, 2 * groups + p)),
            _const_spec((2 * t, t)),
        ],
        out_specs=pl.BlockSpec((t, width), lambda b, p, i: (b * nq + i, p)),
        out_shape=jax.ShapeDtypeStruct((batch * seq, B_WIDTH), BF16),
        scratch_shapes=[
            pltpu.VMEM((heads, t, 2 * HEAD_DIM), BF16),
            pltpu.VMEM((heads, t, t), F32),
            pltpu.VMEM((heads, t, 2 * t), BF16),
            pltpu.VMEM((heads, t, t), F32),
            pltpu.VMEM((SB_PAIRS, t, 2 * t), BF16),
            pltpu.VMEM((heads, t, V7X_LANES), F32),
            pltpu.VMEM((t, width), F32),
        ],
        compiler_params=_params(("arbitrary", "arbitrary", "arbitrary")),
        name="stick_breaking",
    )(qkv, qkv, qkv, tri)


def _load_weight_bf16(w_hbm, w_bf, stage, sems):
    rows = stage.shape[1]
    n = w_hbm.shape[0] // rows

    def chunk_copy(c):
        return pltpu.make_async_copy(w_hbm.at[pl.ds(c * rows, rows)], stage.at[c % 2],
                                     sems.at[c % 2])

    chunk_copy(0).start()
    for c in range(n):
        if c + 1 < n:
            chunk_copy(c + 1).start()
        chunk_copy(c).wait()
        w_bf[pl.ds(c * rows, rows), :] = stage[c % 2].astype(BF16)


WEIGHT_STAGE_ROWS = 512


def _mix_kernel(ya_ref, yb_ref, gates_a_ref, gates_b_ref, h_ref, wa_hbm, wb_hbm, wm_hbm,
                g_ref, b_ref, o_ref, ob_ref, wa_ref, wb_ref, wm_ref, stage, sems):
    @pl.when(pl.program_id(0) == 0)
    def _():
        _load_weight_bf16(wa_hbm, wa_ref, stage, sems)
        _load_weight_bf16(wb_hbm, wb_ref, stage, sems)
        _load_weight_bf16(wm_hbm, wm_ref, stage, sems)

    pa = jnp.dot(ya_ref[...], wa_ref[...], preferred_element_type=F32)
    pb = jnp.dot(yb_ref[...], wb_ref[...], preferred_element_type=F32)
    merged = jax.nn.sigmoid(gates_a_ref[...]) * pa + jax.nn.sigmoid(gates_b_ref[...]) * pb
    mixed = jnp.dot(merged.astype(BF16), wm_ref[...], preferred_element_type=F32)
    y = _layer_norm(DEEPNORM_ALPHA * h_ref[...] + mixed, g_ref[...], b_ref[...])
    o_ref[...] = y
    ob_ref[...] = y.astype(BF16)


def _mix(ya, yb, gates, h, wa, wb, wm, g, b):
    t, d = h.shape
    tm = min(ATT_TILE, t)
    row = pl.BlockSpec((tm, d), lambda i: (i, 0))
    return pl.pallas_call(
        _mix_kernel,
        grid=(t // tm,),
        in_specs=[pl.BlockSpec((tm, A_WIDTH), lambda i: (i, 0)),
                  pl.BlockSpec((tm, B_WIDTH), lambda i: (i, 0)),
                  pl.BlockSpec((tm, d), lambda i: (i, 0)),
                  pl.BlockSpec((tm, d), lambda i: (i, 1)),
                  row,
                  pl.BlockSpec(memory_space=pl.ANY), pl.BlockSpec(memory_space=pl.ANY),
                  pl.BlockSpec(memory_space=pl.ANY),
                  _const_spec((1, d)), _const_spec((1, d))],
        out_specs=[row, row],
        out_shape=[jax.ShapeDtypeStruct((t, d), F32), jax.ShapeDtypeStruct((t, d), BF16)],
        scratch_shapes=[pltpu.VMEM((A_WIDTH, d), BF16), pltpu.VMEM((B_WIDTH, d), BF16),
                        pltpu.VMEM((d, d), BF16),
                        pltpu.VMEM((2, min(WEIGHT_STAGE_ROWS, d), d), F32),
                        pltpu.SemaphoreType.DMA((2,))],
        compiler_params=_params(("arbitrary",)),
        name="mix_out",
    )(ya, yb, gates, gates, h, wa, wb, wm, g.reshape(1, d), b.reshape(1, d))


def _xattn_kernel(hb_ref, h_ref, k_ref, v_ref, wq_hbm, wo_hbm, g_ref, b_ref, wr_ref,
                  rb_ref, ltri_ref, o_ref, idx_ref, gate_ref, rank_ref, cnt_ref, run_ref,
                  wq_ref, wo_ref, stage, sems):
    first_step = (pl.program_id(0) == 0) & (pl.program_id(1) == 0)

    @pl.when(first_step)
    def _():
        _load_weight_bf16(wq_hbm, wq_ref, stage, sems)
        _load_weight_bf16(wo_hbm, wo_ref, stage, sems)

    d = h_ref.shape[1]
    hd = d // X_HEADS
    scale = hd ** -0.5
    q = jnp.dot(hb_ref[...], wq_ref[...], preferred_element_type=F32).astype(BF16)
    outs = []
    for x in range(X_HEADS):
        sl = slice(x * hd, (x + 1) * hd)
        s = lax.dot_general(q[:, sl], k_ref[:, sl], NT_DIMS, preferred_element_type=F32) * scale
        m = jnp.max(s, axis=-1, keepdims=True)
        p = jnp.exp(s - m)
        denom = jnp.sum(p, axis=-1, keepdims=True)
        o = jnp.dot(p.astype(BF16), v_ref[:, sl], preferred_element_type=F32) / denom
        outs.append(o.astype(BF16))
    xo = jnp.dot(jnp.concatenate(outs, axis=1), wo_ref[...], preferred_element_type=F32)
    h_out = _layer_norm(DEEPNORM_ALPHA * h_ref[...] + xo, g_ref[...], b_ref[...])
    o_ref[...] = h_out
    _route_tile(h_out, first_step, wr_ref, rb_ref, ltri_ref, idx_ref, gate_ref,
                rank_ref, cnt_ref, run_ref)


def _xattn_route(hb, h, kv, wq, wo, g, b, w_router, b_router, batch, seq, mem_len):
    t, d = h.shape
    tm = min(ATT_TILE, seq)
    ns = seq // tm
    w_pad = jnp.zeros((d, V7X_LANES), F32).at[:, :N_EXPERTS].set(w_router)
    w_hi = w_pad.astype(BF16)
    w_lo = (w_pad - w_hi.astype(F32)).astype(BF16)
    w_split = jnp.concatenate([w_hi, w_lo], axis=1)
    b_pad = jnp.zeros((1, V7X_LANES), F32).at[0, :N_EXPERTS].set(b_router)
    ltri = jnp.asarray(np.tril(np.ones((tm, tm), np.float32), -1), dtype=BF16)
    row = pl.BlockSpec((tm, d), lambda bi, i: (bi * ns + i, 0))
    small = pl.BlockSpec((tm, V7X_LANES), lambda bi, i: (bi * ns + i, 0))
    return pl.pallas_call(
        _xattn_kernel,
        grid=(batch, ns),
        in_specs=[row, row,
                  pl.BlockSpec((mem_len, d), lambda bi, i: (bi, 0)),
                  pl.BlockSpec((mem_len, d), lambda bi, i: (bi, 1)),
                  pl.BlockSpec(memory_space=pl.ANY), pl.BlockSpec(memory_space=pl.ANY),
                  _const_spec((1, d)), _const_spec((1, d)),
                  _const_spec((d, 2 * V7X_LANES)),
                  _const_spec((1, V7X_LANES)), _const_spec((tm, tm))],
        out_specs=[row, small, small, small,
                   pl.BlockSpec((1, V7X_LANES), lambda bi, i: (0, 0))],
        out_shape=[jax.ShapeDtypeStruct((t, d), F32),
                   jax.ShapeDtypeStruct((t, V7X_LANES), jnp.int32),
                   jax.ShapeDtypeStruct((t, V7X_LANES), F32),
                   jax.ShapeDtypeStruct((t, V7X_LANES), jnp.int32),
                   jax.ShapeDtypeStruct((1, V7X_LANES), F32)],
        scratch_shapes=[pltpu.VMEM((1, V7X_LANES), F32),
                        pltpu.VMEM((d, d), BF16), pltpu.VMEM((d, d), BF16),
                        pltpu.VMEM((2, min(WEIGHT_STAGE_ROWS, d), d), F32),
                        pltpu.SemaphoreType.DMA((2,))],
        compiler_params=_params(("arbitrary", "arbitrary")),
        name="xattn_route",
    )(hb, h, kv, kv, wq, wo, g.reshape(1, d), b.reshape(1, d), w_split, b_pad, ltri)


def _route_tile(h, first_step, w_ref, b_ref, ltri_ref, idx_ref, gate_ref, rank_ref,
                cnt_ref, run_ref):
    @pl.when(first_step)
    def _():
        run_ref[...] = jnp.zeros_like(run_ref)

    h_hi = h.astype(BF16)
    h_lo = (h - h_hi.astype(F32)).astype(BF16)
    hi_terms = jnp.dot(h_hi, w_ref[...], preferred_element_type=F32)
    lo_hi = jnp.dot(h_lo, w_ref[:, 0:V7X_LANES], preferred_element_type=F32)
    logits = (hi_terms[:, 0:V7X_LANES] + (lo_hi + hi_terms[:, V7X_LANES:2 * V7X_LANES])
              + b_ref[...])
    tm = logits.shape[0]
    lane_i = lax.broadcasted_iota(jnp.int32, (tm, V7X_LANES), 1)
    lane = lane_i.astype(F32)
    rem = jnp.where(lane_i < N_EXPERTS, logits, -jnp.inf)
    vals, idxs = [], []
    for _ in range(TOP_K):
        m = jnp.max(rem, axis=-1, keepdims=True)
        ik = jnp.min(jnp.where(rem == m, lane, float(V7X_LANES)), axis=-1, keepdims=True)
        vals.append(m)
        idxs.append(ik)
        rem = jnp.where(lane == ik, -jnp.inf, rem)
    exps = [jnp.exp(v - vals[0]) for v in vals]
    denom = exps[0] + exps[1] + exps[2] + exps[3]
    sel = jnp.zeros((tm, V7X_LANES), F32)
    for ik in idxs:
        sel = sel + (lane == ik).astype(F32)
    before = jnp.dot(ltri_ref[...], sel.astype(BF16), preferred_element_type=F32) + run_ref[...]
    idx_out = jnp.zeros((tm, V7X_LANES), jnp.int32)
    gate_out = jnp.zeros((tm, V7X_LANES), F32)
    rank_out = jnp.zeros((tm, V7X_LANES), jnp.int32)
    for k in range(TOP_K):
        rank_k = jnp.sum(jnp.where(lane == idxs[k], before, 0.0), axis=-1, keepdims=True)
        idx_out = jnp.where(lane_i == k, idxs[k].astype(jnp.int32), idx_out)
        gate_out = jnp.where(lane_i == k, exps[k] / denom, gate_out)
        rank_out = jnp.where(lane_i == k, rank_k.astype(jnp.int32), rank_out)
    idx_ref[...] = idx_out
    gate_ref[...] = gate_out
    rank_ref[...] = rank_out
    run_ref[...] = run_ref[...] + jnp.sum(sel, axis=0, keepdims=True)
    cnt_ref[...] = run_ref[...]


DMA_UNROLL = 8
NUM_DMA_QUEUES = 2
WEIGHT_DMA_PRIORITY = 1
(META_EXPERT, META_VALID, META_FIRST, META_LAST_GROUP, META_NEXT_EXPERT, META_GROUP,
 META_ROWS_USED, META_NUM_GROUPS) = range(8)


def _pack_bf16_pairs(x):
    half = x.shape[1] // 2
    lo = lax.bitcast_convert_type(x[:, :half].astype(BF16).astype(F32), jnp.uint32)
    hi = lax.bitcast_convert_type(x[:, half:].astype(BF16).astype(F32), jnp.uint32)
    return (lo >> 16) | (hi & jnp.uint32(0xFFFF0000))


def _unpack_bf16_pairs(words):
    lo = lax.bitcast_convert_type(words << 16, F32).astype(BF16)
    hi = lax.bitcast_convert_type(words & jnp.uint32(0xFFFF0000), F32).astype(BF16)
    return jnp.concatenate([lo, hi], axis=1)


def _dispatch_kernel(dest_ref, meta_ref, h_ref, xs_hbm, pack_ref, zero_ref, sem, zsem):
    tm = h_ref.shape[0]
    base = pl.program_id(0) * tm
    et = zero_ref.shape[0]
    pack_ref[...] = _pack_bf16_pairs(h_ref[...])

    @pl.when(pl.program_id(0) == 0)
    def _():
        zero_ref[...] = jnp.zeros_like(zero_ref)

        def fill_copy(tile):
            start_row = pl.multiple_of(tile * et, et)
            return pltpu.make_async_copy(zero_ref, xs_hbm.at[pl.ds(start_row, et)], zsem)

        def fill_start(tile, c):
            @pl.when(meta_ref[META_ROWS_USED, tile] < et)
            def _():
                fill_copy(tile).start()
            return c

        def fill_wait(tile, c):
            @pl.when(meta_ref[META_ROWS_USED, tile] < et)
            def _():
                fill_copy(tile).wait()
            return c

        n_tiles = xs_hbm.shape[0] // et
        lax.fori_loop(0, n_tiles, fill_start, 0)
        lax.fori_loop(0, n_tiles, fill_wait, 0)

    def row_copy(r, k):
        return pltpu.make_async_copy(pack_ref.at[pl.ds(r, 1)],
                                     xs_hbm.at[pl.ds(dest_ref[(base + r) * TOP_K + k], 1)], sem)

    def start(r, c):
        for k in range(TOP_K):
            row_copy(r, k).start(priority=k % NUM_DMA_QUEUES)
        return c

    def wait(r, c):
        for k in range(TOP_K):
            row_copy(r, k).wait()
        return c

    lax.fori_loop(0, tm, start, 0, unroll=DMA_UNROLL)
    lax.fori_loop(0, tm, wait, 0, unroll=DMA_UNROLL)


def _dispatch(h, dest, meta, n_rows):
    t, d = h.shape
    tm = min(ATT_TILE, t)
    grid_spec = pltpu.PrefetchScalarGridSpec(
        num_scalar_prefetch=2,
        grid=(t // tm,),
        in_specs=[pl.BlockSpec((tm, d), lambda i, dst, m: (i, 0))],
        out_specs=pl.BlockSpec(memory_space=pl.ANY),
        scratch_shapes=[pltpu.VMEM((tm, d // 2), jnp.uint32),
                        pltpu.VMEM((EXPERT_TILE, d // 2), jnp.uint32),
                        pltpu.SemaphoreType.DMA(()), pltpu.SemaphoreType.DMA(())],
    )
    return pl.pallas_call(
        _dispatch_kernel,
        grid_spec=grid_spec,
        out_shape=jax.ShapeDtypeStruct((n_rows, d // 2), jnp.uint32),
        compiler_params=_params(("arbitrary",)),
        name="moe_dispatch",
    )(dest, meta, h)


def _stream_expert_weights(meta_ref, j, t, nj, make_copies, on_arrival):
    @pl.when((j == 0) & (t == 0))
    def _():
        for c in make_copies(meta_ref[META_EXPERT, 0], 0, 0):
            c.start(priority=WEIGHT_DMA_PRIORITY)

    @pl.when(meta_ref[META_FIRST, t] == 1)
    def _():
        slot = (j * meta_ref[META_NUM_GROUPS, t] + meta_ref[META_GROUP, t]) & 1
        for c in make_copies(meta_ref[META_EXPERT, t], j, slot):
            c.wait()
        last_group = meta_ref[META_LAST_GROUP, t] == 1

        @pl.when(jnp.logical_not(last_group & (j == nj - 1)))
        def _():
            next_chunk = jnp.where(last_group, j + 1, j)
            for c in make_copies(meta_ref[META_NEXT_EXPERT, t], next_chunk, 1 - slot):
                c.start(priority=WEIGHT_DMA_PRIORITY)

        on_arrival(slot)


def _for_used_rows(rows_used, o_ref, compute):
    tm = o_ref.shape[0]
    part = tm // EXPERT_TILE_PARTS
    for parts in range(EXPERT_TILE_PARTS + 1):
        used = parts * part

        @pl.when((rows_used > used - part) & (rows_used <= used))
        def _(used=used):
            if used > 0:
                compute(slice(0, used))
            if used < tm:
                o_ref[used:tm, :] = jnp.zeros((tm - used, o_ref.shape[1]), o_ref.dtype)


def _moe_up_kernel(meta_ref, x_ref, w_hbm, bg_ref, bl_ref, o_ref, wf32, wbf, sems, *, tn, nj):
    j = pl.program_id(0)
    t = pl.program_id(1)

    def make_copies(expert, chunk, slot):
        glu_col = pl.multiple_of(chunk * tn, tn)
        lin_col = pl.multiple_of((nj + chunk) * tn, tn)
        return (pltpu.make_async_copy(w_hbm.at[expert, :, pl.ds(glu_col, tn)],
                                      wf32.at[slot, 0], sems.at[slot, 0]),
                pltpu.make_async_copy(w_hbm.at[expert, :, pl.ds(lin_col, tn)],
                                      wf32.at[slot, 1], sems.at[slot, 1]))

    def on_arrival(slot):
        wbf[0] = wf32[slot, 0].astype(BF16)
        wbf[1] = wf32[slot, 1].astype(BF16)

    _stream_expert_weights(meta_ref, j, t, nj, make_copies, on_arrival)

    def expert_rows(rows):
        x = _unpack_bf16_pairs(x_ref[rows, :])
        glu = jnp.dot(x, wbf[0], preferred_element_type=F32) + bg_ref[...]
        lin = jnp.dot(x, wbf[1], preferred_element_type=F32) + bl_ref[...]
        glu = jnp.minimum(glu, SWIGLU_LIMIT)
        lin = jnp.clip(lin, -SWIGLU_LIMIT, SWIGLU_LIMIT)
        act = glu * jax.nn.sigmoid(SWIGLU_ALPHA * glu) * (lin + 1.0)
        o_ref[rows, :] = act.astype(o_ref.dtype)

    _for_used_rows(meta_ref[META_ROWS_USED, t], o_ref, expert_rows)


def _moe_up(xs, w_up, b_up, meta):
    r = xs.shape[0]
    e, d, f2 = w_up.shape
    f = f2 // 2
    tm = EXPERT_TILE
    tn = _pick_tile(f, 1024)
    nj = f // tn
    grid_spec = pltpu.PrefetchScalarGridSpec(
        num_scalar_prefetch=1,
        grid=(nj, r // tm),
        in_specs=[
            pl.BlockSpec((tm, d // 2), lambda j, t, m: (t, 0)),
            pl.BlockSpec(memory_space=pl.ANY),
            pl.BlockSpec((None, 1, tn), lambda j, t, m: (m[META_EXPERT, t], 0, j)),
            pl.BlockSpec((None, 1, tn), lambda j, t, m: (m[META_EXPERT, t], 0, nj + j)),
        ],
        out_specs=pl.BlockSpec((tm, tn), lambda j, t, m: (t, j)),
        scratch_shapes=[pltpu.VMEM((2, 2, d, tn), F32), pltpu.VMEM((2, d, tn), BF16),
                        pltpu.SemaphoreType.DMA((2, 2))],
    )
    b3 = b_up.reshape(e, 1, f2)
    return pl.pallas_call(
        functools.partial(_moe_up_kernel, tn=tn, nj=nj),
        grid_spec=grid_spec,
        out_shape=jax.ShapeDtypeStruct((r, f), BF16),
        compiler_params=_params(("arbitrary", "arbitrary")),
        name="moe_up",
    )(meta, xs, w_up, b3, b3)


def _moe_down_kernel(meta_ref, a_ref, w_hbm, b_ref, o_ref, wf32, wbf, sems, *, tn, nj):
    j = pl.program_id(0)
    t = pl.program_id(1)

    def make_copies(expert, chunk, slot):
        col = pl.multiple_of(chunk * tn, tn)
        return (pltpu.make_async_copy(w_hbm.at[expert, :, pl.ds(col, tn)],
                                      wf32.at[slot], sems.at[slot]),)

    def on_arrival(slot):
        wbf[...] = wf32[slot].astype(BF16)

    _stream_expert_weights(meta_ref, j, t, nj, make_copies, on_arrival)

    def expert_rows(rows):
        o_ref[rows, :] = (jnp.dot(a_ref[rows, :], wbf[...], preferred_element_type=F32)
                          + b_ref[...])

    _for_used_rows(meta_ref[META_ROWS_USED, t], o_ref, expert_rows)


def _moe_down(act, w_down, b_down, meta):
    r, f = act.shape
    e, _, d = w_down.shape
    tm = EXPERT_TILE
    tn = _pick_tile(d, 2048)
    nj = d // tn
    grid_spec = pltpu.PrefetchScalarGridSpec(
        num_scalar_prefetch=1,
        grid=(nj, r // tm),
        in_specs=[
            pl.BlockSpec((tm, f), lambda j, t, m: (t, 0)),
            pl.BlockSpec(memory_space=pl.ANY),
            pl.BlockSpec((None, 1, tn), lambda j, t, m: (m[META_EXPERT, t], 0, j)),
        ],
        out_specs=pl.BlockSpec((tm, tn), lambda j, t, m: (t, j)),
        scratch_shapes=[pltpu.VMEM((2, f, tn), F32), pltpu.VMEM((f, tn), BF16),
                        pltpu.SemaphoreType.DMA((2,))],
    )
    return pl.pallas_call(
        functools.partial(_moe_down_kernel, tn=tn, nj=nj),
        grid_spec=grid_spec,
        out_shape=jax.ShapeDtypeStruct((r, d), F32),
        compiler_params=_params(("arbitrary", "arbitrary")),
        name="moe_down",
    )(meta, act, w_down, b_down.reshape(e, 1, d))


def _combine_kernel(pos_ref, y_hbm, gate_ref, h_ref, g_ref, b_ref, o_ref, buf_ref, sems):
    tm = o_ref.shape[0]
    i = pl.program_id(0)

    def row_copy(step, r, k, slot):
        src_row = pos_ref[(step * tm + r) * TOP_K + k]
        return pltpu.make_async_copy(y_hbm.at[pl.ds(src_row, 1)],
                                     buf_ref.at[slot, k, pl.ds(r, 1)], sems.at[slot])

    def gather_start(step, slot):
        def body(r, c):
            for k in range(TOP_K):
                row_copy(step, r, k, slot).start(priority=k % NUM_DMA_QUEUES)
            return c
        lax.fori_loop(0, tm, body, 0, unroll=DMA_UNROLL)

    def gather_wait(step, slot):
        def body(r, c):
            for k in range(TOP_K):
                row_copy(step, r, k, slot).wait()
            return c
        lax.fori_loop(0, tm, body, 0, unroll=DMA_UNROLL)

    @pl.when(i == 0)
    def _():
        gather_start(0, 0)

    for next_slot in range(2):
        @pl.when((i + 1 < pl.num_programs(0)) & ((i + 1) & 1 == next_slot))
        def _(next_slot=next_slot):
            gather_start(i + 1, next_slot)

    slot = i & 1
    gather_wait(i, slot)
    gates = gate_ref[...]
    ff = gates[:, 0:1] * buf_ref[slot, 0]
    for k in range(1, TOP_K):
        ff = ff + gates[:, k:k + 1] * buf_ref[slot, k]
    o_ref[...] = _layer_norm(DEEPNORM_ALPHA * h_ref[...] + ff, g_ref[...], b_ref[...])


def _combine(y_rows, pos, gates, h, g, b):
    t, d = h.shape
    tm = min(COMBINE_TILE, t)
    grid_spec = pltpu.PrefetchScalarGridSpec(
        num_scalar_prefetch=1,
        grid=(t // tm,),
        in_specs=[pl.BlockSpec(memory_space=pl.ANY),
                  pl.BlockSpec((tm, V7X_LANES), lambda i, p: (i, 0)),
                  pl.BlockSpec((tm, d), lambda i, p: (i, 0)),
                  pl.BlockSpec((1, d), lambda i, p: (0, 0)),
                  pl.BlockSpec((1, d), lambda i, p: (0, 0))],
        out_specs=pl.BlockSpec((tm, d), lambda i, p: (i, 0)),
        scratch_shapes=[pltpu.VMEM((2, TOP_K, tm, d), F32), pltpu.SemaphoreType.DMA((2,))],
    )
    return pl.pallas_call(
        _combine_kernel,
        grid_spec=grid_spec,
        out_shape=jax.ShapeDtypeStruct((t, d), F32),
        compiler_params=_params(("arbitrary",)),
        name="moe_combine",
    )(pos, y_rows, gates, h, g.reshape(1, d), b.reshape(1, d))


def _expert_layout(idx, rank, counts, n_tiles, tm):
    experts = jnp.arange(N_EXPERTS, dtype=jnp.int32)
    tiles_per = (counts + tm - 1) // tm
    tile_end = jnp.cumsum(tiles_per).astype(jnp.int32)
    tile_off = tile_end - tiles_per
    dest = jnp.sum(jnp.where(idx[..., None] == experts, tile_off * tm, 0), axis=-1) + rank

    nonempty = counts > 0
    first_e = jnp.min(jnp.where(nonempty, experts, N_EXPERTS))
    last_e = jnp.max(jnp.where(nonempty, experts, 0))
    group_of = jnp.cumsum(nonempty.astype(jnp.int32)) - 1
    later = (experts[None, :] > experts[:, None]) & nonempty[None, :]
    next_e = jnp.min(jnp.where(later, experts[None, :], N_EXPERTS), axis=1)
    next_e = jnp.where(next_e == N_EXPERTS, first_e, next_e)

    tid = jnp.arange(n_tiles, dtype=jnp.int32)
    valid = tid < tile_end[-1]
    te = jnp.sum((tid[:, None] >= tile_end[None, :]).astype(jnp.int32), axis=1)
    te = jnp.where(valid, jnp.minimum(te, N_EXPERTS - 1), last_e)
    onehot = te[:, None] == experts[None, :]

    def lookup(table):
        return jnp.sum(jnp.where(onehot, table[None, :], 0), axis=1)

    local = tid - lookup(tile_off)
    rows_used = jnp.where(valid, jnp.clip(lookup(counts) - local * tm, 0, tm), 0)
    meta = jnp.stack([
        te,
        valid.astype(jnp.int32),
        (valid & (local == 0)).astype(jnp.int32),
        (te == last_e).astype(jnp.int32),
        lookup(next_e),
        lookup(group_of),
        rows_used,
        jnp.broadcast_to(jnp.sum(nonempty.astype(jnp.int32)), (n_tiles,)),
    ]).astype(jnp.int32)
    return dest.reshape(-1).astype(jnp.int32), meta


def _moe(h, routing, w_up, b_up, w_down, b_down, g, b):
    t, d = h.shape
    tm = EXPERT_TILE
    idx_p, gate_p, rank_p, cnt_p = routing
    n_rows = t * TOP_K + N_EXPERTS * tm
    dest, meta = _expert_layout(idx_p[:, :TOP_K], rank_p[:, :TOP_K],
                                cnt_p[0, :N_EXPERTS].astype(jnp.int32), n_rows // tm, tm)
    xs = _dispatch(h, dest, meta, n_rows)
    act = _moe_up(xs, w_up, b_up, meta)
    y_rows = _moe_down(act, w_down, b_down, meta)
    return _combine(y_rows, dest, gate_p, h, g, b)


def kernel(x, mem, ln_in_g, ln_in_b, rel_bias, w_in, b_in, attn_sinks, w_a_out, w_b_out,
           w_mix_out, ln1_g, ln1_b, w_xq, w_xkv, w_xo, ln2_g, ln2_b, w_router, b_router,
           w_up, b_up, w_down, b_down, ln3_g, ln3_b):
    batch, seq, d = x.shape
    mem_len = mem.shape[1]
    t = batch * seq
    a_cols = A_WIDTH + 2 * A_KV_WIDTH
    b_cols = 3 * B_WIDTH

    g_col = a_cols + b_cols
    h, hb, qkv_a = _ln_proj(x.reshape(t, d), ln_in_g, ln_in_b, w_in[0], b_in[0, :a_cols],
                            a_cols, BF16)
    for l in range(DEPTH):
        if l > 0:
            qkv_a = _mm_bias(hb, w_in[l], b_in[l, :a_cols], 0, a_cols, BF16, "in_proj_a")
        qkv_b = _mm_bias(hb, w_in[l], b_in[l, a_cols:g_col], a_cols, b_cols, BF16, "in_proj_b")
        gates = _mm_bias(hb, w_in[l], b_in[l, g_col:], g_col, 2 * d, F32, "in_proj_gates")
        ya = _swa(qkv_a, attn_sinks[l], rel_bias, batch, seq)
        yb = _stick_breaking(qkv_b, batch, seq)
        h, hb = _mix(ya, yb, gates, h, w_a_out[l], w_b_out[l], w_mix_out[l], ln1_g[l], ln1_b[l])
        kv = _mm_bias(mem.reshape(batch * mem_len, d).astype(BF16), w_xkv[l],
                      jnp.zeros((2 * d,), F32), 0, 2 * d, BF16, "mem_kv")
        h, *routing = _xattn_route(hb, h, kv, w_xq[l], w_xo[l], ln2_g[l], ln2_b[l],
                                   w_router[l], b_router[l], batch, seq, mem_len)
        h = _moe(h, routing, w_up[l], b_up[l], w_down[l], b_down[l], ln3_g[l], ln3_b[l])
        if l + 1 < DEPTH:
            hb = h.astype(BF16)
    return h.reshape(batch, seq, d)
```

```python
import functools

import jax
import jax.numpy as jnp
import numpy as np
from jax import lax
from jax.experimental import pallas as pl
from jax.experimental.pallas import tpu as pltpu

DEPTH = 1
CHUNK = 64
BLK = 128
HEAD_DIM = 64
A_HEADS = 16
A_KV_HEADS = 2
A_REP = A_HEADS // A_KV_HEADS
WIN_CHUNKS = 2
B_HEADS = 16
A_WIDTH = A_HEADS * HEAD_DIM
A_KV_WIDTH = A_KV_HEADS * HEAD_DIM
B_WIDTH = B_HEADS * HEAD_DIM
REL_BUCKETS = 32
REL_MAX_DIST = 128
X_HEADS = 4
N_EXPERTS = 32
TOP_K = 4
SWIGLU_LIMIT = 7.0
SWIGLU_ALPHA = 1.702
LN_EPS = 1e-5
DEEPNORM_ALPHA = (2.0 * DEPTH) ** 0.25
QK_SCALE = HEAD_DIM ** -0.5
V7X_LANES = 128
V7X_VMEM_LIMIT_BYTES = 56 * 1024 * 1024

ROW_TILE = 512
MM_ROW_TILE = 1024
ATT_TILE = 256
SB_TILE = 256
SB_PAIRS = 4
SB_ROWS = 32
EXPERT_TILE = 256
EXPERT_TILE_PARTS = 4
DISPATCH_TILE = 512
COMBINE_TILE = 256
SB_LOG_ZERO = -105.0

F32 = jnp.float32
BF16 = jnp.bfloat16
NT_DIMS = (((1,), (1,)), ((), ()))


def _params(semantics):
    return pltpu.CompilerParams(dimension_semantics=semantics,
                                vmem_limit_bytes=V7X_VMEM_LIMIT_BYTES)


def _const_spec(shape):
    nd = len(shape)
    return pl.BlockSpec(shape, lambda *_: (0,) * nd, pipeline_mode=pl.Buffered(1))


def _layer_norm(x, g, b):
    mu = jnp.mean(x, axis=-1, keepdims=True)
    xc = x - mu
    var = jnp.mean(xc * xc, axis=-1, keepdims=True)
    return xc * lax.rsqrt(var + LN_EPS) * g + b


def _ln_proj_kernel(x_ref, g_ref, b_ref, w_hbm, bias_ref, h_ref, hb_ref, o_ref, stage, wbf, sem,
                    *, n):
    @pl.when(pl.program_id(0) == 0)
    def _():
        slab = pltpu.make_async_copy(w_hbm.at[:, pl.ds(0, n)], stage, sem)
        slab.start()
        slab.wait()
        wbf[...] = stage[...].astype(BF16)

    y = _layer_norm(x_ref[...], g_ref[...], b_ref[...])
    yb = y.astype(BF16)
    h_ref[...] = y
    hb_ref[...] = yb
    acc = jnp.dot(yb, wbf[...], preferred_element_type=F32)
    o_ref[...] = (acc + bias_ref[...]).astype(o_ref.dtype)


def _ln_proj(x2d, g, b, w, bias, n, out_dtype):
    t, d = x2d.shape
    tm = min(ROW_TILE, t)
    row = pl.BlockSpec((tm, d), lambda i: (i, 0))
    return pl.pallas_call(
        functools.partial(_ln_proj_kernel, n=n),
        grid=(t // tm,),
        in_specs=[row, _const_spec((1, d)), _const_spec((1, d)),
                  pl.BlockSpec(memory_space=pl.ANY), _const_spec((1, n))],
        out_specs=[row, row, pl.BlockSpec((tm, n), lambda i: (i, 0))],
        out_shape=[jax.ShapeDtypeStruct((t, d), F32), jax.ShapeDtypeStruct((t, d), BF16),
                   jax.ShapeDtypeStruct((t, n), out_dtype)],
        scratch_shapes=[pltpu.VMEM((d, n), F32), pltpu.VMEM((d, n), BF16),
                        pltpu.SemaphoreType.DMA(())],
        compiler_params=_params(("arbitrary",)),
        name="ln_in_proj_a",
    )(x2d, g.reshape(1, d), b.reshape(1, d), w, bias.reshape(1, n).astype(F32))


def _mm_bias_kernel(a_ref, w_hbm, b_ref, o_ref, stage, wbf, sem, *, col0, tn, nj):
    j = pl.program_id(0)
    i = pl.program_id(1)

    def slab_copy(jj):
        col = pl.multiple_of(col0 + jj * tn, V7X_LANES)
        return pltpu.make_async_copy(w_hbm.at[:, pl.ds(col, tn)], stage, sem)

    @pl.when(i == 0)
    def _():
        @pl.when(j == 0)
        def _():
            slab_copy(0).start()

        slab_copy(j).wait()
        wbf[...] = stage[...].astype(BF16)

        @pl.when(j + 1 < nj)
        def _():
            slab_copy(j + 1).start()

    acc = jnp.dot(a_ref[...], wbf[...], preferred_element_type=F32)
    o_ref[...] = (acc + b_ref[...]).astype(o_ref.dtype)


def _pick_tile(n, target):
    best = None
    for c in range(V7X_LANES, min(n, target) + 1, V7X_LANES):
        if n % c == 0:
            best = c
    return n if best is None else best


def _mm_bias(a, w, b, col0, n, out_dtype, name):
    m, k = a.shape
    tm = min(MM_ROW_TILE, m)
    tn = _pick_tile(n, 1024)
    nj = n // tn
    return pl.pallas_call(
        functools.partial(_mm_bias_kernel, col0=col0, tn=tn, nj=nj),
        grid=(nj, m // tm),
        in_specs=[pl.BlockSpec((tm, k), lambda j, i: (i, 0)),
                  pl.BlockSpec(memory_space=pl.ANY),
                  pl.BlockSpec((1, tn), lambda j, i: (0, j))],
        out_specs=pl.BlockSpec((tm, tn), lambda j, i: (i, j)),
        out_shape=jax.ShapeDtypeStruct((m, n), out_dtype),
        scratch_shapes=[pltpu.VMEM((k, tn), F32), pltpu.VMEM((k, tn), BF16),
                        pltpu.SemaphoreType.DMA(())],
        compiler_params=_params(("arbitrary", "arbitrary")),
        name=name,
    )(a, w, b.reshape(1, n).astype(F32))


def _swa_kernel(sink_ref, q_ref, kp_ref, kc_ref, vp_ref, vc_ref, bias_ref, o_ref):
    n = pl.program_id(1)
    kk = jnp.concatenate([kp_ref[...], kc_ref[...]], axis=0)
    vv = jnp.concatenate([vp_ref[...], vc_ref[...]], axis=0)
    col = lax.broadcasted_iota(jnp.int32, (BLK, 2 * BLK), 1)
    exists = (col >= BLK) | (n > 0)
    for h in range(A_HEADS):
        g = h // A_REP
        qh = q_ref[:, h * HEAD_DIM:(h + 1) * HEAD_DIM] * QK_SCALE
        kg = kk[:, g * HEAD_DIM:(g + 1) * HEAD_DIM]
        vg = vv[:, g * HEAD_DIM:(g + 1) * HEAD_DIM]
        s = lax.dot_general(qh, kg, NT_DIMS, preferred_element_type=F32)
        s = jnp.where(exists, s + bias_ref[h], -jnp.inf)
        sink = sink_ref[h]
        m = jnp.maximum(jnp.max(s, axis=-1, keepdims=True), sink)
        p = jnp.exp(s - m)
        denom = jnp.sum(p, axis=-1, keepdims=True) + jnp.exp(sink - m)
        o = jnp.dot(p.astype(BF16), vg, preferred_element_type=F32) / denom
        o_ref[:, h * HEAD_DIM:(h + 1) * HEAD_DIM] = o.astype(o_ref.dtype)


def _rel_bucket(rel):
    half = REL_BUCKETS // 2
    max_exact = half // 2
    base = jnp.where(rel > 0, half, 0)
    n = jnp.abs(rel)
    nf = jnp.maximum(n, 1).astype(F32)
    large = max_exact + (jnp.log(nf / max_exact) / np.log(REL_MAX_DIST / max_exact)
                         * (half - max_exact)).astype(jnp.int32)
    large = jnp.minimum(large, half - 1)
    return base + jnp.where(n < max_exact, n, large)


def _swa_bias_table(rel_bias):
    rel = jnp.arange(-(2 * BLK - 1), BLK)
    line = jnp.transpose(rel_bias[_rel_bucket(rel)]).astype(F32)
    bias = jnp.stack([line[:, BLK - 1 - q:3 * BLK - 1 - q] for q in range(BLK)], axis=1)
    qi = np.arange(BLK)[:, None]
    kj = np.arange(2 * BLK)[None, :]
    dchunk = (kj // CHUNK - BLK // CHUNK) - qi // CHUNK
    band = (dchunk <= 0) & (dchunk >= -WIN_CHUNKS)
    return jnp.where(jnp.asarray(band)[None], bias, -jnp.inf)


def _swa(qkv, sinks, rel_bias, batch, seq):
    nb = seq // BLK
    kcol = A_WIDTH // A_KV_WIDTH
    vcol = kcol + 1
    kv_blk = (BLK, A_KV_WIDTH)
    grid_spec = pltpu.PrefetchScalarGridSpec(
        num_scalar_prefetch=1,
        grid=(batch, nb),
        in_specs=[
            pl.BlockSpec((BLK, A_WIDTH), lambda b, n, s: (b * nb + n, 0)),
            pl.BlockSpec(kv_blk, lambda b, n, s: (b * nb + jnp.maximum(n - 1, 0), kcol)),
            pl.BlockSpec(kv_blk, lambda b, n, s: (b * nb + n, kcol)),
            pl.BlockSpec(kv_blk, lambda b, n, s: (b * nb + jnp.maximum(n - 1, 0), vcol)),
            pl.BlockSpec(kv_blk, lambda b, n, s: (b * nb + n, vcol)),
            pl.BlockSpec((A_HEADS, BLK, 2 * BLK), lambda b, n, s: (0, 0, 0),
                         pipeline_mode=pl.Buffered(1)),
        ],
        out_specs=pl.BlockSpec((BLK, A_WIDTH), lambda b, n, s: (b * nb + n, 0)),
    )
    return pl.pallas_call(
        _swa_kernel,
        grid_spec=grid_spec,
        out_shape=jax.ShapeDtypeStruct((batch * seq, A_WIDTH), BF16),
        compiler_params=_params(("arbitrary", "arbitrary")),
        name="swa_attn",
    )(sinks.astype(F32), qkv, qkv, qkv, qkv, qkv, _swa_bias_table(rel_bias))


def _sb_kernel(q_ref, k_ref, v_ref, tri_ref, o_ref, qh_scr, z_scr, hl_scr, s_scr, w_scr,
               carry_scr, acc_scr):
    t = q_ref.shape[0]
    i = pl.program_id(2)
    n_heads = 2 * SB_PAIRS
    pair_w = 2 * HEAD_DIM
    lane = lax.broadcasted_iota(jnp.int32, (1, pair_w), 1)
    first = lane < HEAD_DIM
    zero = jnp.zeros((), BF16)

    def pair_cols(h):
        return slice((h // 2) * pair_w, (h // 2 + 1) * pair_w)

    def own_half(h, x):
        return jnp.where(first, x, zero) if h % 2 == 0 else jnp.where(first, zero, x)

    q2 = q_ref[...] * QK_SCALE
    for h in range(n_heads):
        qh_scr[h] = own_half(h, q2[:, pair_cols(h)])
    carry_scr[...] = jnp.zeros_like(carry_scr)
    acc_scr[...] = jnp.zeros_like(acc_scr)
    row = lax.broadcasted_iota(jnp.int32, (SB_ROWS, t), 0)
    col = lax.broadcasted_iota(jnp.int32, (SB_ROWS, t), 1)

    def key_tile(j, diagonal):
        start = pl.multiple_of(j * t, t)
        k2 = k_ref[pl.ds(start, t), :]
        v2 = v_ref[pl.ds(start, t), :]
        chunks = [slice(r0, r0 + SB_ROWS) for r0 in range(0, t, SB_ROWS)]
        for h in range(n_heads):
            z_scr[h] = lax.dot_general(qh_scr[h], k2[:, pair_cols(h)], NT_DIMS,
                                       preferred_element_type=F32)
        for rows in chunks:
            for h in range(n_heads):
                z = z_scr[h, rows, :]
                drop = jnp.maximum(z, 0.0) + jnp.log(1.0 + jnp.exp(-jnp.abs(z)))
                if diagonal:
                    drop = jnp.where(col < row + rows.start, drop, 0.0)
                hi = drop.astype(BF16)
                hl_scr[h, rows, 0:t] = hi
                hl_scr[h, rows, t:2 * t] = (drop - hi.astype(F32)).astype(BF16)
        tri = tri_ref[...]
        for h in range(n_heads):
            s_scr[h] = jnp.dot(hl_scr[h], tri, preferred_element_type=F32)
        top = jnp.full((SB_ROWS, V7X_LANES), -jnp.inf, F32)
        for rows in chunks:
            for h in range(n_heads):
                carry = carry_scr[h, rows, :]
                log_w = (z_scr[h, rows, :] + s_scr[h, rows, :]
                         + jnp.concatenate([carry] * (t // V7X_LANES), axis=1))
                w = jnp.exp(log_w)
                if diagonal:
                    w = jnp.where(col < row + rows.start, w, 0.0)
                w_scr[h // 2, rows, (h % 2) * t:(h % 2 + 1) * t] = w.astype(BF16)
                carry = carry + jnp.broadcast_to(s_scr[h, rows, 0:1], (SB_ROWS, V7X_LANES))
                carry_scr[h, rows, :] = carry
                top = jnp.maximum(top, carry)
        for p in range(SB_PAIRS):
            cols = pair_cols(2 * p)
            v_pair = jnp.concatenate([own_half(0, v2[:, cols]), own_half(1, v2[:, cols])], axis=0)
            acc_scr[:, cols] += jnp.dot(w_scr[p], v_pair, preferred_element_type=F32)
        return (jnp.max(top) > SB_LOG_ZERO).astype(jnp.int32)

    live = key_tile(i, True)

    def cond(state):
        return (state[0] >= 0) & (state[1] > 0)

    def body(state):
        return state[0] - 1, key_tile(state[0], False)

    lax.while_loop(cond, body, (i - 1, live))
    o_ref[...] = acc_scr[...].astype(o_ref.dtype)


def _stick_breaking(qkv, batch, seq):
    t = min(SB_TILE, seq)
    nq = seq // t
    width = SB_PAIRS * 2 * HEAD_DIM
    groups = B_WIDTH // width
    heads = 2 * SB_PAIRS
    j = np.arange(2 * t)[:, None] % t
    s = np.arange(t)[None, :]
    tri = jnp.asarray(-(j >= s).astype(np.float32), dtype=BF16)
    return pl.pallas_call(
        _sb_kernel,
        grid=(batch, groups, nq),
        in_specs=[
            pl.BlockSpec((t, width), lambda b, p, i: (b * nq + i, p)),
            pl.BlockSpec((seq, width), lambda b, p, i: (b, groups + p)),
            pl.BlockSpec((seq, width), lambda b, p, i: (b, 2 * groups + p)),
            _const_spec((2 * t, t)),
        ],
        out_specs=pl.BlockSpec((t, width), lambda b, p, i: (b * nq + i, p)),
        out_shape=jax.ShapeDtypeStruct((batch * seq, B_WIDTH), BF16),
        scratch_shapes=[
            pltpu.VMEM((heads, t, 2 * HEAD_DIM), BF16),
            pltpu.VMEM((heads, t, t), F32),
            pltpu.VMEM((heads, t, 2 * t), BF16),
            pltpu.VMEM((heads, t, t), F32),
            pltpu.VMEM((SB_PAIRS, t, 2 * t), BF16),
            pltpu.VMEM((heads, t, V7X_LANES), F32),
            pltpu.VMEM((t, width), F32),
        ],
        compiler_params=_params(("arbitrary", "arbitrary", "arbitrary")),
        name="stick_breaking",
    )(qkv, qkv, qkv, tri)


def _load_weight_bf16(w_hbm, w_bf, stage, sems):
    rows = stage.shape[1]
    n = w_hbm.shape[0] // rows

    def chunk_copy(c):
        return pltpu.make_async_copy(w_hbm.at[pl.ds(c * rows, rows)], stage.at[c % 2],
                                     sems.at[c % 2])

    chunk_copy(0).start()
    for c in range(n):
        if c + 1 < n:
            chunk_copy(c + 1).start()
        chunk_copy(c).wait()
        w_bf[pl.ds(c * rows, rows), :] = stage[c % 2].astype(BF16)


WEIGHT_STAGE_ROWS = 512


def _mix_kernel(ya_ref, yb_ref, gates_a_ref, gates_b_ref, h_ref, wa_hbm, wb_hbm, wm_hbm,
                g_ref, b_ref, o_ref, ob_ref, wa_ref, wb_ref, wm_ref, stage, sems):
    @pl.when(pl.program_id(0) == 0)
    def _():
        _load_weight_bf16(wa_hbm, wa_ref, stage, sems)
        _load_weight_bf16(wb_hbm, wb_ref, stage, sems)
        _load_weight_bf16(wm_hbm, wm_ref, stage, sems)

    pa = jnp.dot(ya_ref[...], wa_ref[...], preferred_element_type=F32)
    pb = jnp.dot(yb_ref[...], wb_ref[...], preferred_element_type=F32)
    merged = jax.nn.sigmoid(gates_a_ref[...]) * pa + jax.nn.sigmoid(gates_b_ref[...]) * pb
    mixed = jnp.dot(merged.astype(BF16), wm_ref[...], preferred_element_type=F32)
    y = _layer_norm(DEEPNORM_ALPHA * h_ref[...] + mixed, g_ref[...], b_ref[...])
    o_ref[...] = y
    ob_ref[...] = y.astype(BF16)


def _mix(ya, yb, gates, h, wa, wb, wm, g, b):
    t, d = h.shape
    tm = min(ATT_TILE, t)
    row = pl.BlockSpec((tm, d), lambda i: (i, 0))
    return pl.pallas_call(
        _mix_kernel,
        grid=(t // tm,),
        in_specs=[pl.BlockSpec((tm, A_WIDTH), lambda i: (i, 0)),
                  pl.BlockSpec((tm, B_WIDTH), lambda i: (i, 0)),
                  pl.BlockSpec((tm, d), lambda i: (i, 0)),
                  pl.BlockSpec((tm, d), lambda i: (i, 1)),
                  row,
                  pl.BlockSpec(memory_space=pl.ANY), pl.BlockSpec(memory_space=pl.ANY),
                  pl.BlockSpec(memory_space=pl.ANY),
                  _const_spec((1, d)), _const_spec((1, d))],
        out_specs=[row, row],
        out_shape=[jax.ShapeDtypeStruct((t, d), F32), jax.ShapeDtypeStruct((t, d), BF16)],
        scratch_shapes=[pltpu.VMEM((A_WIDTH, d), BF16), pltpu.VMEM((B_WIDTH, d), BF16),
                        pltpu.VMEM((d, d), BF16),
                        pltpu.VMEM((2, min(WEIGHT_STAGE_ROWS, d), d), F32),
                        pltpu.SemaphoreType.DMA((2,))],
        compiler_params=_params(("arbitrary",)),
        name="mix_out",
    )(ya, yb, gates, gates, h, wa, wb, wm, g.reshape(1, d), b.reshape(1, d))


def _xattn_kernel(hb_ref, h_ref, k_ref, v_ref, wq_hbm, wo_hbm, g_ref, b_ref, wr_ref,
                  rb_ref, ltri_ref, o_ref, idx_ref, gate_ref, rank_ref, cnt_ref, run_ref,
                  wq_ref, wo_ref, stage, sems):
    first_step = (pl.program_id(0) == 0) & (pl.program_id(1) == 0)

    @pl.when(first_step)
    def _():
        _load_weight_bf16(wq_hbm, wq_ref, stage, sems)
        _load_weight_bf16(wo_hbm, wo_ref, stage, sems)

    d = h_ref.shape[1]
    hd = d // X_HEADS
    scale = hd ** -0.5
    q = jnp.dot(hb_ref[...], wq_ref[...], preferred_element_type=F32).astype(BF16)
    outs = []
    for x in range(X_HEADS):
        sl = slice(x * hd, (x + 1) * hd)
        s = lax.dot_general(q[:, sl], k_ref[:, sl], NT_DIMS, preferred_element_type=F32) * scale
        m = jnp.max(s, axis=-1, keepdims=True)
        p = jnp.exp(s - m)
        denom = jnp.sum(p, axis=-1, keepdims=True)
        o = jnp.dot(p.astype(BF16), v_ref[:, sl], preferred_element_type=F32) / denom
        outs.append(o.astype(BF16))
    xo = jnp.dot(jnp.concatenate(outs, axis=1), wo_ref[...], preferred_element_type=F32)
    h_out = _layer_norm(DEEPNORM_ALPHA * h_ref[...] + xo, g_ref[...], b_ref[...])
    o_ref[...] = h_out
    _route_tile(h_out, first_step, wr_ref, rb_ref, ltri_ref, idx_ref, gate_ref,
                rank_ref, cnt_ref, run_ref)


def _xattn_route(hb, h, kv, wq, wo, g, b, w_router, b_router, batch, seq, mem_len):
    t, d = h.shape
    tm = min(ATT_TILE, seq)
    ns = seq // tm
    w_pad = jnp.zeros((d, V7X_LANES), F32).at[:, :N_EXPERTS].set(w_router)
    w_hi = w_pad.astype(BF16)
    w_lo = (w_pad - w_hi.astype(F32)).astype(BF16)
    w_split = jnp.concatenate([w_hi, w_lo], axis=1)
    b_pad = jnp.zeros((1, V7X_LANES), F32).at[0, :N_EXPERTS].set(b_router)
    ltri = jnp.asarray(np.tril(np.ones((tm, tm), np.float32), -1), dtype=BF16)
    row = pl.BlockSpec((tm, d), lambda bi, i: (bi * ns + i, 0))
    small = pl.BlockSpec((tm, V7X_LANES), lambda bi, i: (bi * ns + i, 0))
    return pl.pallas_call(
        _xattn_kernel,
        grid=(batch, ns),
        in_specs=[row, row,
                  pl.BlockSpec((mem_len, d), lambda bi, i: (bi, 0)),
                  pl.BlockSpec((mem_len, d), lambda bi, i: (bi, 1)),
                  pl.BlockSpec(memory_space=pl.ANY), pl.BlockSpec(memory_space=pl.ANY),
                  _const_spec((1, d)), _const_spec((1, d)),
                  _const_spec((d, 2 * V7X_LANES)),
                  _const_spec((1, V7X_LANES)), _const_spec((tm, tm))],
        out_specs=[row, small, small, small,
                   pl.BlockSpec((1, V7X_LANES), lambda bi, i: (0, 0))],
        out_shape=[jax.ShapeDtypeStruct((t, d), F32),
                   jax.ShapeDtypeStruct((t, V7X_LANES), jnp.int32),
                   jax.ShapeDtypeStruct((t, V7X_LANES), F32),
                   jax.ShapeDtypeStruct((t, V7X_LANES), jnp.int32),
                   jax.ShapeDtypeStruct((1, V7X_LANES), F32)],
        scratch_shapes=[pltpu.VMEM((1, V7X_LANES), F32),
                        pltpu.VMEM((d, d), BF16), pltpu.VMEM((d, d), BF16),
                        pltpu.VMEM((2, min(WEIGHT_STAGE_ROWS, d), d), F32),
                        pltpu.SemaphoreType.DMA((2,))],
        compiler_params=_params(("arbitrary", "arbitrary")),
        name="xattn_route",
    )(hb, h, kv, kv, wq, wo, g.reshape(1, d), b.reshape(1, d), w_split, b_pad, ltri)


def _route_tile(h, first_step, w_ref, b_ref, ltri_ref, idx_ref, gate_ref, rank_ref,
                cnt_ref, run_ref):
    @pl.when(first_step)
    def _():
        run_ref[...] = jnp.zeros_like(run_ref)

    h_hi = h.astype(BF16)
    h_lo = (h - h_hi.astype(F32)).astype(BF16)
    hi_terms = jnp.dot(h_hi, w_ref[...], preferred_element_type=F32)
    lo_hi = jnp.dot(h_lo, w_ref[:, 0:V7X_LANES], preferred_element_type=F32)
    logits = (hi_terms[:, 0:V7X_LANES] + (lo_hi + hi_terms[:, V7X_LANES:2 * V7X_LANES])
              + b_ref[...])
    tm = logits.shape[0]
    lane_i = lax.broadcasted_iota(jnp.int32, (tm, V7X_LANES), 1)
    lane = lane_i.astype(F32)
    rem = jnp.where(lane_i < N_EXPERTS, logits, -jnp.inf)
    vals, idxs = [], []
    for _ in range(TOP_K):
        m = jnp.max(rem, axis=-1, keepdims=True)
        ik = jnp.min(jnp.where(rem == m, lane, float(V7X_LANES)), axis=-1, keepdims=True)
        vals.append(m)
        idxs.append(ik)
        rem = jnp.where(lane == ik, -jnp.inf, rem)
    exps = [jnp.exp(v - vals[0]) for v in vals]
    denom = exps[0] + exps[1] + exps[2] + exps[3]
    sel = jnp.zeros((tm, V7X_LANES), F32)
    for ik in idxs:
        sel = sel + (lane == ik).astype(F32)
    before = jnp.dot(ltri_ref[...], sel.astype(BF16), preferred_element_type=F32) + run_ref[...]
    idx_out = jnp.zeros((tm, V7X_LANES), jnp.int32)
    gate_out = jnp.zeros((tm, V7X_LANES), F32)
    rank_out = jnp.zeros((tm, V7X_LANES), jnp.int32)
    for k in range(TOP_K):
        rank_k = jnp.sum(jnp.where(lane == idxs[k], before, 0.0), axis=-1, keepdims=True)
        idx_out = jnp.where(lane_i == k, idxs[k].astype(jnp.int32), idx_out)
        gate_out = jnp.where(lane_i == k, exps[k] / denom, gate_out)
        rank_out = jnp.where(lane_i == k, rank_k.astype(jnp.int32), rank_out)
    idx_ref[...] = idx_out
    gate_ref[...] = gate_out
    rank_ref[...] = rank_out
    run_ref[...] = run_ref[...] + jnp.sum(sel, axis=0, keepdims=True)
    cnt_ref[...] = run_ref[...]


DMA_UNROLL = 8
NUM_DMA_QUEUES = 2
WEIGHT_DMA_PRIORITY = 1
(META_EXPERT, META_FIRST, META_LAST_GROUP, META_NEXT_EXPERT, META_GROUP,
 META_ROWS_USED, META_NUM_GROUPS) = range(7)


def _pack_bf16_pairs(x):
    half = x.shape[1] // 2
    lo = lax.bitcast_convert_type(x[:, :half].astype(BF16).astype(F32), jnp.uint32)
    hi = lax.bitcast_convert_type(x[:, half:].astype(BF16).astype(F32), jnp.uint32)
    return (lo >> 16) | (hi & jnp.uint32(0xFFFF0000))


def _unpack_bf16_pairs(words):
    lo = lax.bitcast_convert_type(words << 16, F32).astype(BF16)
    hi = lax.bitcast_convert_type(words & jnp.uint32(0xFFFF0000), F32).astype(BF16)
    return jnp.concatenate([lo, hi], axis=1)


def _dispatch_kernel(dest_ref, meta_ref, h_ref, xs_hbm, pack_ref, zero_ref, sem, zsem):
    tm = h_ref.shape[0]
    base = pl.program_id(0) * tm
    et = zero_ref.shape[0]
    pack_ref[...] = _pack_bf16_pairs(h_ref[...])

    @pl.when(pl.program_id(0) == 0)
    def _():
        zero_ref[...] = jnp.zeros_like(zero_ref)

        def fill_copy(tile):
            start_row = pl.multiple_of(tile * et, et)
            return pltpu.make_async_copy(zero_ref, xs_hbm.at[pl.ds(start_row, et)], zsem)

        def fill_start(tile, c):
            @pl.when(meta_ref[META_ROWS_USED, tile] < et)
            def _():
                fill_copy(tile).start()
            return c

        def fill_wait(tile, c):
            @pl.when(meta_ref[META_ROWS_USED, tile] < et)
            def _():
                fill_copy(tile).wait()
            return c

        n_tiles = xs_hbm.shape[0] // et
        lax.fori_loop(0, n_tiles, fill_start, 0)
        lax.fori_loop(0, n_tiles, fill_wait, 0)

    def row_copy(r, k):
        return pltpu.make_async_copy(pack_ref.at[pl.ds(r, 1)],
                                     xs_hbm.at[pl.ds(dest_ref[(base + r) * TOP_K + k], 1)], sem)

    def start(r, c):
        for k in range(TOP_K):
            row_copy(r, k).start(priority=k % NUM_DMA_QUEUES)
        return c

    def wait(r, c):
        for k in range(TOP_K):
            row_copy(r, k).wait()
        return c

    lax.fori_loop(0, tm, start, 0, unroll=DMA_UNROLL)
    lax.fori_loop(0, tm, wait, 0, unroll=DMA_UNROLL)


def _dispatch(h, dest, meta, n_rows):
    t, d = h.shape
    tm = min(DISPATCH_TILE, t)
    grid_spec = pltpu.PrefetchScalarGridSpec(
        num_scalar_prefetch=2,
        grid=(t // tm,),
        in_specs=[pl.BlockSpec((tm, d), lambda i, dst, m: (i, 0))],
        out_specs=pl.BlockSpec(memory_space=pl.ANY),
        scratch_shapes=[pltpu.VMEM((tm, d // 2), jnp.uint32),
                        pltpu.VMEM((EXPERT_TILE, d // 2), jnp.uint32),
                        pltpu.SemaphoreType.DMA(()), pltpu.SemaphoreType.DMA(())],
    )
    return pl.pallas_call(
        _dispatch_kernel,
        grid_spec=grid_spec,
        out_shape=jax.ShapeDtypeStruct((n_rows, d // 2), jnp.uint32),
        compiler_params=_params(("arbitrary",)),
        name="moe_dispatch",
    )(dest, meta, h)


def _stream_expert_weights(meta_ref, j, t, nj, make_copies, on_arrival):
    @pl.when((j == 0) & (t == 0))
    def _():
        for c in make_copies(meta_ref[META_EXPERT, 0], 0, 0):
            c.start(priority=WEIGHT_DMA_PRIORITY)

    @pl.when(meta_ref[META_FIRST, t] == 1)
    def _():
        slot = (j * meta_ref[META_NUM_GROUPS, t] + meta_ref[META_GROUP, t]) & 1
        for c in make_copies(meta_ref[META_EXPERT, t], j, slot):
            c.wait()
        last_group = meta_ref[META_LAST_GROUP, t] == 1

        @pl.when(jnp.logical_not(last_group & (j == nj - 1)))
        def _():
            next_chunk = jnp.where(last_group, j + 1, j)
            for c in make_copies(meta_ref[META_NEXT_EXPERT, t], next_chunk, 1 - slot):
                c.start(priority=WEIGHT_DMA_PRIORITY)

        on_arrival(slot)


def _for_used_rows(rows_used, o_ref, compute):
    tm = o_ref.shape[0]
    part = tm // EXPERT_TILE_PARTS
    for parts in range(EXPERT_TILE_PARTS + 1):
        used = parts * part

        @pl.when((rows_used > used - part) & (rows_used <= used))
        def _(used=used):
            if used > 0:
                compute(slice(0, used))
            if used < tm:
                o_ref[used:tm, :] = jnp.zeros((tm - used, o_ref.shape[1]), o_ref.dtype)


def _moe_up_kernel(meta_ref, x_ref, w_hbm, bg_ref, bl_ref, o_ref, wf32, wbf, sems, *, tn, nj):
    j = pl.program_id(0)
    t = pl.program_id(1)

    def make_copies(expert, chunk, slot):
        glu_col = pl.multiple_of(chunk * tn, tn)
        lin_col = pl.multiple_of((nj + chunk) * tn, tn)
        return (pltpu.make_async_copy(w_hbm.at[expert, :, pl.ds(glu_col, tn)],
                                      wf32.at[slot, 0], sems.at[slot, 0]),
                pltpu.make_async_copy(w_hbm.at[expert, :, pl.ds(lin_col, tn)],
                                      wf32.at[slot, 1], sems.at[slot, 1]))

    def on_arrival(slot):
        wbf[0] = wf32[slot, 0].astype(BF16)
        wbf[1] = wf32[slot, 1].astype(BF16)

    _stream_expert_weights(meta_ref, j, t, nj, make_copies, on_arrival)

    def expert_rows(rows):
        x = _unpack_bf16_pairs(x_ref[rows, :])
        glu = jnp.dot(x, wbf[0], preferred_element_type=F32) + bg_ref[...]
        lin = jnp.dot(x, wbf[1], preferred_element_type=F32) + bl_ref[...]
        glu = jnp.minimum(glu, SWIGLU_LIMIT)
        lin = jnp.clip(lin, -SWIGLU_LIMIT, SWIGLU_LIMIT)
        act = glu * jax.nn.sigmoid(SWIGLU_ALPHA * glu) * (lin + 1.0)
        o_ref[rows, :] = act.astype(o_ref.dtype)

    _for_used_rows(meta_ref[META_ROWS_USED, t], o_ref, expert_rows)


def _moe_up(xs, w_up, b_up, meta):
    r = xs.shape[0]
    e, d, f2 = w_up.shape
    f = f2 // 2
    tm = EXPERT_TILE
    tn = _pick_tile(f, 1024)
    nj = f // tn
    grid_spec = pltpu.PrefetchScalarGridSpec(
        num_scalar_prefetch=1,
        grid=(nj, r // tm),
        in_specs=[
            pl.BlockSpec((tm, d // 2), lambda j, t, m: (t, 0)),
            pl.BlockSpec(memory_space=pl.ANY),
            pl.BlockSpec((None, 1, tn), lambda j, t, m: (m[META_EXPERT, t], 0, j)),
            pl.BlockSpec((None, 1, tn), lambda j, t, m: (m[META_EXPERT, t], 0, nj + j)),
        ],
        out_specs=pl.BlockSpec((tm, tn), lambda j, t, m: (t, j)),
        scratch_shapes=[pltpu.VMEM((2, 2, d, tn), F32), pltpu.VMEM((2, d, tn), BF16),
                        pltpu.SemaphoreType.DMA((2, 2))],
    )
    b3 = b_up.reshape(e, 1, f2)
    return pl.pallas_call(
        functools.partial(_moe_up_kernel, tn=tn, nj=nj),
        grid_spec=grid_spec,
        out_shape=jax.ShapeDtypeStruct((r, f), BF16),
        compiler_params=_params(("arbitrary", "arbitrary")),
        name="moe_up",
    )(meta, xs, w_up, b3, b3)


def _moe_down_kernel(meta_ref, a_ref, w_hbm, b_ref, o_ref, wf32, wbf, sems, *, tn, nj):
    j = pl.program_id(0)
    t = pl.program_id(1)

    def make_copies(expert, chunk, slot):
        col = pl.multiple_of(chunk * tn, tn)
        return (pltpu.make_async_copy(w_hbm.at[expert, :, pl.ds(col, tn)],
                                      wf32.at[slot], sems.at[slot]),)

    def on_arrival(slot):
        wbf[...] = wf32[slot].astype(BF16)

    _stream_expert_weights(meta_ref, j, t, nj, make_copies, on_arrival)

    def expert_rows(rows):
        o_ref[rows, :] = (jnp.dot(a_ref[rows, :], wbf[...], preferred_element_type=F32)
                          + b_ref[...])

    _for_used_rows(meta_ref[META_ROWS_USED, t], o_ref, expert_rows)


def _moe_down(act, w_down, b_down, meta):
    r, f = act.shape
    e, _, d = w_down.shape
    tm = EXPERT_TILE
    tn = _pick_tile(d, 2048)
    nj = d // tn
    grid_spec = pltpu.PrefetchScalarGridSpec(
        num_scalar_prefetch=1,
        grid=(nj, r // tm),
        in_specs=[
            pl.BlockSpec((tm, f), lambda j, t, m: (t, 0)),
            pl.BlockSpec(memory_space=pl.ANY),
            pl.BlockSpec((None, 1, tn), lambda j, t, m: (m[META_EXPERT, t], 0, j)),
        ],
        out_specs=pl.BlockSpec((tm, tn), lambda j, t, m: (t, j)),
        scratch_shapes=[pltpu.VMEM((2, f, tn), F32), pltpu.VMEM((f, tn), BF16),
                        pltpu.SemaphoreType.DMA((2,))],
    )
    return pl.pallas_call(
        functools.partial(_moe_down_kernel, tn=tn, nj=nj),
        grid_spec=grid_spec,
        out_shape=jax.ShapeDtypeStruct((r, d), F32),
        compiler_params=_params(("arbitrary", "arbitrary")),
        name="moe_down",
    )(meta, act, w_down, b_down.reshape(e, 1, d))


def _combine_kernel(pos_ref, y_hbm, gate_ref, h_ref, g_ref, b_ref, o_ref, buf_ref, sems):
    tm = o_ref.shape[0]
    i = pl.program_id(0)

    def row_copy(step, r, k, slot):
        src_row = pos_ref[(step * tm + r) * TOP_K + k]
        return pltpu.make_async_copy(y_hbm.at[pl.ds(src_row, 1)],
                                     buf_ref.at[slot, k, pl.ds(r, 1)], sems.at[slot])

    def gather_start(step, slot):
        def body(r, c):
            for k in range(TOP_K):
                row_copy(step, r, k, slot).start(priority=k % NUM_DMA_QUEUES)
            return c
        lax.fori_loop(0, tm, body, 0, unroll=DMA_UNROLL)

    def gather_wait(step, slot):
        def body(r, c):
            for k in range(TOP_K):
                row_copy(step, r, k, slot).wait()
            return c
        lax.fori_loop(0, tm, body, 0, unroll=DMA_UNROLL)

    @pl.when(i == 0)
    def _():
        gather_start(0, 0)

    for next_slot in range(2):
        @pl.when((i + 1 < pl.num_programs(0)) & ((i + 1) & 1 == next_slot))
        def _(next_slot=next_slot):
            gather_start(i + 1, next_slot)

    slot = i & 1
    gather_wait(i, slot)
    gates = gate_ref[...]
    ff = gates[:, 0:1] * buf_ref[slot, 0]
    for k in range(1, TOP_K):
        ff = ff + gates[:, k:k + 1] * buf_ref[slot, k]
    o_ref[...] = _layer_norm(DEEPNORM_ALPHA * h_ref[...] + ff, g_ref[...], b_ref[...])


def _combine(y_rows, pos, gates, h, g, b):
    t, d = h.shape
    tm = min(COMBINE_TILE, t)
    grid_spec = pltpu.PrefetchScalarGridSpec(
        num_scalar_prefetch=1,
        grid=(t // tm,),
        in_specs=[pl.BlockSpec(memory_space=pl.ANY),
                  pl.BlockSpec((tm, V7X_LANES), lambda i, p: (i, 0)),
                  pl.BlockSpec((tm, d), lambda i, p: (i, 0)),
                  pl.BlockSpec((1, d), lambda i, p: (0, 0)),
                  pl.BlockSpec((1, d), lambda i, p: (0, 0))],
        out_specs=pl.BlockSpec((tm, d), lambda i, p: (i, 0)),
        scratch_shapes=[pltpu.VMEM((2, TOP_K, tm, d), F32), pltpu.SemaphoreType.DMA((2,))],
    )
    return pl.pallas_call(
        _combine_kernel,
        grid_spec=grid_spec,
        out_shape=jax.ShapeDtypeStruct((t, d), F32),
        compiler_params=_params(("arbitrary",)),
        name="moe_combine",
    )(pos, y_rows, gates, h, g.reshape(1, d), b.reshape(1, d))


def _expert_layout(idx, rank, counts, n_tiles, tm):
    experts = jnp.arange(N_EXPERTS, dtype=jnp.int32)
    tiles_per = (counts + tm - 1) // tm
    tile_end = jnp.cumsum(tiles_per).astype(jnp.int32)
    tile_off = tile_end - tiles_per
    dest = jnp.sum(jnp.where(idx[..., None] == experts, tile_off * tm, 0), axis=-1) + rank

    nonempty = counts > 0
    first_e = jnp.min(jnp.where(nonempty, experts, N_EXPERTS))
    last_e = jnp.max(jnp.where(nonempty, experts, 0))
    group_of = jnp.cumsum(nonempty.astype(jnp.int32)) - 1
    later = (experts[None, :] > experts[:, None]) & nonempty[None, :]
    next_e = jnp.min(jnp.where(later, experts[None, :], N_EXPERTS), axis=1)
    next_e = jnp.where(next_e == N_EXPERTS, first_e, next_e)

    tid = jnp.arange(n_tiles, dtype=jnp.int32)
    valid = tid < tile_end[-1]
    te = jnp.sum((tid[:, None] >= tile_end[None, :]).astype(jnp.int32), axis=1)
    te = jnp.where(valid, jnp.minimum(te, N_EXPERTS - 1), last_e)
    onehot = te[:, None] == experts[None, :]

    def lookup(table):
        return jnp.sum(jnp.where(onehot, table[None, :], 0), axis=1)

    local = tid - lookup(tile_off)
    rows_used = jnp.where(valid, jnp.clip(lookup(counts) - local * tm, 0, tm), 0)
    meta = jnp.stack([
        te,
        (valid & (local == 0)).astype(jnp.int32),
        (te == last_e).astype(jnp.int32),
        lookup(next_e),
        lookup(group_of),
        rows_used,
        jnp.broadcast_to(jnp.sum(nonempty.astype(jnp.int32)), (n_tiles,)),
    ]).astype(jnp.int32)
    return dest.reshape(-1).astype(jnp.int32), meta


def _moe(h, routing, w_up, b_up, w_down, b_down, g, b):
    t, d = h.shape
    tm = EXPERT_TILE
    idx_p, gate_p, rank_p, cnt_p = routing
    n_rows = t * TOP_K + N_EXPERTS * tm
    dest, meta = _expert_layout(idx_p[:, :TOP_K], rank_p[:, :TOP_K],
                                cnt_p[0, :N_EXPERTS].astype(jnp.int32), n_rows // tm, tm)
    xs = _dispatch(h, dest, meta, n_rows)
    act = _moe_up(xs, w_up, b_up, meta)
    y_rows = _moe_down(act, w_down, b_down, meta)
    return _combine(y_rows, dest, gate_p, h, g, b)


def kernel(x, mem, ln_in_g, ln_in_b, rel_bias, w_in, b_in, attn_sinks, w_a_out, w_b_out,
           w_mix_out, ln1_g, ln1_b, w_xq, w_xkv, w_xo, ln2_g, ln2_b, w_router, b_router,
           w_up, b_up, w_down, b_down, ln3_g, ln3_b):
    batch, seq, d = x.shape
    mem_len = mem.shape[1]
    t = batch * seq
    a_cols = A_WIDTH + 2 * A_KV_WIDTH
    b_cols = 3 * B_WIDTH

    g_col = a_cols + b_cols
    h, hb, qkv_a = _ln_proj(x.reshape(t, d), ln_in_g, ln_in_b, w_in[0], b_in[0, :a_cols],
                            a_cols, BF16)
    for l in range(DEPTH):
        if l > 0:
            qkv_a = _mm_bias(hb, w_in[l], b_in[l, :a_cols], 0, a_cols, BF16, "in_proj_a")
        qkv_b = _mm_bias(hb, w_in[l], b_in[l, a_cols:g_col], a_cols, b_cols, BF16, "in_proj_b")
        gates = _mm_bias(hb, w_in[l], b_in[l, g_col:], g_col, 2 * d, F32, "in_proj_gates")
        ya = _swa(qkv_a, attn_sinks[l], rel_bias, batch, seq)
        yb = _stick_breaking(qkv_b, batch, seq)
        h, hb = _mix(ya, yb, gates, h, w_a_out[l], w_b_out[l], w_mix_out[l], ln1_g[l], ln1_b[l])
        kv = _mm_bias(mem.reshape(batch * mem_len, d).astype(BF16), w_xkv[l],
                      jnp.zeros((2 * d,), F32), 0, 2 * d, BF16, "mem_kv")
        h, *routing = _xattn_route(hb, h, kv, w_xq[l], w_xo[l], ln2_g[l], ln2_b[l],
                                   w_router[l], b_router[l], batch, seq, mem_len)
        h = _moe(h, routing, w_up[l], b_up[l], w_down[l], b_down[l], ln3_g[l], ln3_b[l])
        if l + 1 < DEPTH:
            hb = h.astype(BF16)
    return h.reshape(batch, seq, d)
```

```python
import functools

import jax
import jax.numpy as jnp
import numpy as np
from jax import lax
from jax.experimental import pallas as pl
from jax.experimental.pallas import tpu as pltpu

DEPTH = 1
CHUNK = 64
BLK = 128
HEAD_DIM = 64
A_HEADS = 16
A_KV_HEADS = 2
A_REP = A_HEADS // A_KV_HEADS
WIN_CHUNKS = 2
B_HEADS = 16
A_WIDTH = A_HEADS * HEAD_DIM
A_KV_WIDTH = A_KV_HEADS * HEAD_DIM
B_WIDTH = B_HEADS * HEAD_DIM
REL_BUCKETS = 32
REL_MAX_DIST = 128
X_HEADS = 4
N_EXPERTS = 32
TOP_K = 4
SWIGLU_LIMIT = 7.0
SWIGLU_ALPHA = 1.702
LN_EPS = 1e-5
DEEPNORM_ALPHA = (2.0 * DEPTH) ** 0.25
QK_SCALE = HEAD_DIM ** -0.5
V7X_LANES = 128
V7X_VMEM_LIMIT_BYTES = 56 * 1024 * 1024
ROW_TILE = 512
MM_ROW_TILE = 1024
ATT_TILE = 256
SB_TILE = 256
SB_PAIRS = 4
SB_ROWS = 32
EXPERT_TILE = 256
EXPERT_TILE_PARTS = 4
DISPATCH_TILE = 512
COMBINE_TILE = 256
SB_LOG_ZERO = -105.0

F32 = jnp.float32
BF16 = jnp.bfloat16
NT_DIMS = (((1,), (1,)), ((), ()))


def _params(semantics):
    return pltpu.CompilerParams(dimension_semantics=semantics,
                                vmem_limit_bytes=V7X_VMEM_LIMIT_BYTES)


def _const_spec(shape):
    nd = len(shape)
    return pl.BlockSpec(shape, lambda *_: (0,) * nd, pipeline_mode=pl.Buffered(1))


def _layer_norm(x, g, b):
    mu = jnp.mean(x, axis=-1, keepdims=True)
    xc = x - mu
    var = jnp.mean(xc * xc, axis=-1, keepdims=True)
    return xc * lax.rsqrt(var + LN_EPS) * g + b


def _ln_proj_kernel(x_ref, g_ref, b_ref, w_hbm, bias_ref, h_ref, hb_ref, o_ref, stage, wbf, sem,
                    *, n):
    @pl.when(pl.program_id(0) == 0)
    def _():
        slab = pltpu.make_async_copy(w_hbm.at[:, pl.ds(0, n)], stage, sem)
        slab.start()
        slab.wait()
        wbf[...] = stage[...].astype(BF16)

    y = _layer_norm(x_ref[...], g_ref[...], b_ref[...])
    yb = y.astype(BF16)
    h_ref[...] = y
    hb_ref[...] = yb
    acc = jnp.dot(yb, wbf[...], preferred_element_type=F32)
    o_ref[...] = (acc + bias_ref[...]).astype(o_ref.dtype)


def _ln_proj(x2d, g, b, w, bias, n, out_dtype):
    t, d = x2d.shape
    tm = min(ROW_TILE, t)
    row = pl.BlockSpec((tm, d), lambda i: (i, 0))
    return pl.pallas_call(
        functools.partial(_ln_proj_kernel, n=n),
        grid=(t // tm,),
        in_specs=[row, _const_spec((1, d)), _const_spec((1, d)),
                  pl.BlockSpec(memory_space=pl.ANY), _const_spec((1, n))],
        out_specs=[row, row, pl.BlockSpec((tm, n), lambda i: (i, 0))],
        out_shape=[jax.ShapeDtypeStruct((t, d), F32), jax.ShapeDtypeStruct((t, d), BF16),
                   jax.ShapeDtypeStruct((t, n), out_dtype)],
        scratch_shapes=[pltpu.VMEM((d, n), F32), pltpu.VMEM((d, n), BF16),
                        pltpu.SemaphoreType.DMA(())],
        compiler_params=_params(("arbitrary",)),
        name="ln_in_proj_a",
    )(x2d, g.reshape(1, d), b.reshape(1, d), w, bias.reshape(1, n).astype(F32))


def _mm_bias_kernel(a_ref, w_hbm, b_ref, o_ref, stage, wbf, sem, *, col0, tn, nj):
    j = pl.program_id(0)
    i = pl.program_id(1)

    def slab_copy(jj):
        col = pl.multiple_of(col0 + jj * tn, V7X_LANES)
        return pltpu.make_async_copy(w_hbm.at[:, pl.ds(col, tn)], stage, sem)

    @pl.when(i == 0)
    def _():
        @pl.when(j == 0)
        def _():
            slab_copy(0).start()

        slab_copy(j).wait()
        wbf[...] = stage[...].astype(BF16)

        @pl.when(j + 1 < nj)
        def _():
            slab_copy(j + 1).start()

    acc = jnp.dot(a_ref[...], wbf[...], preferred_element_type=F32)
    o_ref[...] = (acc + b_ref[...]).astype(o_ref.dtype)


def _pick_tile(n, target):
    best = None
    for c in range(V7X_LANES, min(n, target) + 1, V7X_LANES):
        if n % c == 0:
            best = c
    return n if best is None else best


def _mm_bias(a, w, b, col0, n, out_dtype, name):
    m, k = a.shape
    tm = min(MM_ROW_TILE, m)
    tn = _pick_tile(n, 1024)
    nj = n // tn
    return pl.pallas_call(
        functools.partial(_mm_bias_kernel, col0=col0, tn=tn, nj=nj),
        grid=(nj, m // tm),
        in_specs=[pl.BlockSpec((tm, k), lambda j, i: (i, 0)),
                  pl.BlockSpec(memory_space=pl.ANY),
                  pl.BlockSpec((1, tn), lambda j, i: (0, j))],
        out_specs=pl.BlockSpec((tm, tn), lambda j, i: (i, j)),
        out_shape=jax.ShapeDtypeStruct((m, n), out_dtype),
        scratch_shapes=[pltpu.VMEM((k, tn), F32), pltpu.VMEM((k, tn), BF16),
                        pltpu.SemaphoreType.DMA(())],
        compiler_params=_params(("arbitrary", "arbitrary")),
        name=name,
    )(a, w, b.reshape(1, n).astype(F32))


def _swa_kernel(sink_ref, q_ref, kp_ref, kc_ref, vp_ref, vc_ref, bias_ref, o_ref):
    n = pl.program_id(1)
    kk = jnp.concatenate([kp_ref[...], kc_ref[...]], axis=0)
    vv = jnp.concatenate([vp_ref[...], vc_ref[...]], axis=0)
    col = lax.broadcasted_iota(jnp.int32, (BLK, 2 * BLK), 1)
    exists = (col >= BLK) | (n > 0)
    for h in range(A_HEADS):
        g = h // A_REP
        qh = q_ref[:, h * HEAD_DIM:(h + 1) * HEAD_DIM] * QK_SCALE
        kg = kk[:, g * HEAD_DIM:(g + 1) * HEAD_DIM]
        vg = vv[:, g * HEAD_DIM:(g + 1) * HEAD_DIM]
        s = lax.dot_general(qh, kg, NT_DIMS, preferred_element_type=F32)
        s = jnp.where(exists, s + bias_ref[h], -jnp.inf)
        sink = sink_ref[h]
        m = jnp.maximum(jnp.max(s, axis=-1, keepdims=True), sink)
        p = jnp.exp(s - m)
        denom = jnp.sum(p, axis=-1, keepdims=True) + jnp.exp(sink - m)
        o = jnp.dot(p.astype(BF16), vg, preferred_element_type=F32) / denom
        o_ref[:, h * HEAD_DIM:(h + 1) * HEAD_DIM] = o.astype(o_ref.dtype)


def _rel_bucket(rel):
    half = REL_BUCKETS // 2
    max_exact = half // 2
    base = jnp.where(rel > 0, half, 0)
    n = jnp.abs(rel)
    nf = jnp.maximum(n, 1).astype(F32)
    large = max_exact + (jnp.log(nf / max_exact) / np.log(REL_MAX_DIST / max_exact)
                         * (half - max_exact)).astype(jnp.int32)
    large = jnp.minimum(large, half - 1)
    return base + jnp.where(n < max_exact, n, large)


def _swa_bias_table(rel_bias):
    rel = jnp.arange(-(2 * BLK - 1), BLK)
    line = jnp.transpose(rel_bias[_rel_bucket(rel)]).astype(F32)
    bias = jnp.stack([line[:, BLK - 1 - q:3 * BLK - 1 - q] for q in range(BLK)], axis=1)
    qi = np.arange(BLK)[:, None]
    kj = np.arange(2 * BLK)[None, :]
    dchunk = (kj // CHUNK - BLK // CHUNK) - qi // CHUNK
    band = (dchunk <= 0) & (dchunk >= -WIN_CHUNKS)
    return jnp.where(jnp.asarray(band)[None], bias, -jnp.inf)


def _swa(qkv, sinks, rel_bias, batch, seq):
    nb = seq // BLK
    kcol = A_WIDTH // A_KV_WIDTH
    vcol = kcol + 1
    kv_blk = (BLK, A_KV_WIDTH)
    grid_spec = pltpu.PrefetchScalarGridSpec(
        num_scalar_prefetch=1,
        grid=(batch, nb),
        in_specs=[
            pl.BlockSpec((BLK, A_WIDTH), lambda b, n, s: (b * nb + n, 0)),
            pl.BlockSpec(kv_blk, lambda b, n, s: (b * nb + jnp.maximum(n - 1, 0), kcol)),
            pl.BlockSpec(kv_blk, lambda b, n, s: (b * nb + n, kcol)),
            pl.BlockSpec(kv_blk, lambda b, n, s: (b * nb + jnp.maximum(n - 1, 0), vcol)),
            pl.BlockSpec(kv_blk, lambda b, n, s: (b * nb + n, vcol)),
            pl.BlockSpec((A_HEADS, BLK, 2 * BLK), lambda b, n, s: (0, 0, 0),
                         pipeline_mode=pl.Buffered(1)),
        ],
        out_specs=pl.BlockSpec((BLK, A_WIDTH), lambda b, n, s: (b * nb + n, 0)),
    )
    return pl.pallas_call(
        _swa_kernel,
        grid_spec=grid_spec,
        out_shape=jax.ShapeDtypeStruct((batch * seq, A_WIDTH), BF16),
        compiler_params=_params(("arbitrary", "arbitrary")),
        name="swa_attn",
    )(sinks.astype(F32), qkv, qkv, qkv, qkv, qkv, _swa_bias_table(rel_bias))


def _sb_kernel(q_ref, k_ref, v_ref, tri_ref, o_ref, qh_scr, z_scr, hl_scr, s_scr, w_scr,
               carry_scr, acc_scr):
    t = q_ref.shape[0]
    i = pl.program_id(2)
    n_heads = 2 * SB_PAIRS
    pair_w = 2 * HEAD_DIM
    lane = lax.broadcasted_iota(jnp.int32, (1, pair_w), 1)
    first = lane < HEAD_DIM
    zero = jnp.zeros((), BF16)

    def pair_cols(h):
        return slice((h // 2) * pair_w, (h // 2 + 1) * pair_w)

    def own_half(h, x):
        return jnp.where(first, x, zero) if h % 2 == 0 else jnp.where(first, zero, x)

    q2 = q_ref[...] * QK_SCALE
    for h in range(n_heads):
        qh_scr[h] = own_half(h, q2[:, pair_cols(h)])
    carry_scr[...] = jnp.zeros_like(carry_scr)
    acc_scr[...] = jnp.zeros_like(acc_scr)
    row = lax.broadcasted_iota(jnp.int32, (SB_ROWS, t), 0)
    col = lax.broadcasted_iota(jnp.int32, (SB_ROWS, t), 1)

    def key_tile(j, diagonal):
        start = pl.multiple_of(j * t, t)
        k2 = k_ref[pl.ds(start, t), :]
        v2 = v_ref[pl.ds(start, t), :]
        chunks = [slice(r0, r0 + SB_ROWS) for r0 in range(0, t, SB_ROWS)]
        for h in range(n_heads):
            z_scr[h] = lax.dot_general(qh_scr[h], k2[:, pair_cols(h)], NT_DIMS,
                                       preferred_element_type=F32)
        for rows in chunks:
            for h in range(n_heads):
                z = z_scr[h, rows, :]
                drop = jnp.maximum(z, 0.0) + jnp.log(1.0 + jnp.exp(-jnp.abs(z)))
                if diagonal:
                    drop = jnp.where(col < row + rows.start, drop, 0.0)
                hi = drop.astype(BF16)
                hl_scr[h, rows, 0:t] = hi
                hl_scr[h, rows, t:2 * t] = (drop - hi.astype(F32)).astype(BF16)
        tri = tri_ref[...]
        for h in range(n_heads):
            s_scr[h] = jnp.dot(hl_scr[h], tri, preferred_element_type=F32)
        top = jnp.full((SB_ROWS, V7X_LANES), -jnp.inf, F32)
        for rows in chunks:
            for h in range(n_heads):
                carry = carry_scr[h, rows, :]
                log_w = (z_scr[h, rows, :] + s_scr[h, rows, :]
                         + jnp.concatenate([carry] * (t // V7X_LANES), axis=1))
                w = jnp.exp(log_w)
                if diagonal:
                    w = jnp.where(col < row + rows.start, w, 0.0)
                w_scr[h // 2, rows, (h % 2) * t:(h % 2 + 1) * t] = w.astype(BF16)
                carry = carry + jnp.broadcast_to(s_scr[h, rows, 0:1], (SB_ROWS, V7X_LANES))
                carry_scr[h, rows, :] = carry
                top = jnp.maximum(top, carry)
        for p in range(SB_PAIRS):
            cols = pair_cols(2 * p)
            v_pair = jnp.concatenate([own_half(0, v2[:, cols]), own_half(1, v2[:, cols])], axis=0)
            acc_scr[:, cols] += jnp.dot(w_scr[p], v_pair, preferred_element_type=F32)
        return (jnp.max(top) > SB_LOG_ZERO).astype(jnp.int32)

    live = key_tile(i, True)

    def cond(state):
        return (state[0] >= 0) & (state[1] > 0)

    def body(state):
        return state[0] - 1, key_tile(state[0], False)

    lax.while_loop(cond, body, (i - 1, live))
    o_ref[...] = acc_scr[...].astype(o_ref.dtype)


def _stick_breaking(qkv, batch, seq):
    t = min(SB_TILE, seq)
    nq = seq // t
    width = SB_PAIRS * 2 * HEAD_DIM
    groups = B_WIDTH // width
    heads = 2 * SB_PAIRS
    j = np.arange(2 * t)[:, None] % t
    s = np.arange(t)[None, :]
    tri = jnp.asarray(-(j >= s).astype(np.float32), dtype=BF16)
    return pl.pallas_call(
        _sb_kernel,
        grid=(batch, groups, nq),
        in_specs=[
            pl.BlockSpec((t, width), lambda b, p, i: (b * nq + i, p)),
            pl.BlockSpec((seq, width), lambda b, p, i: (b, groups + p)),
            pl.BlockSpec((seq, width), lambda b, p, i: (b, 2 * groups + p)),
            _const_spec((2 * t, t)),
        ],
        out_specs=pl.BlockSpec((t, width), lambda b, p, i: (b * nq + i, p)),
        out_shape=jax.ShapeDtypeStruct((batch * seq, B_WIDTH), BF16),
        scratch_shapes=[
            pltpu.VMEM((heads, t, 2 * HEAD_DIM), BF16),
            pltpu.VMEM((heads, t, t), F32),
            pltpu.VMEM((heads, t, 2 * t), BF16),
            pltpu.VMEM((heads, t, t), F32),
            pltpu.VMEM((SB_PAIRS, t, 2 * t), BF16),
            pltpu.VMEM((heads, t, V7X_LANES), F32),
            pltpu.VMEM((t, width), F32),
        ],
        compiler_params=_params(("arbitrary", "arbitrary", "arbitrary")),
        name="stick_breaking",
    )(qkv, qkv, qkv, tri)


def _load_weight_bf16(w_hbm, w_bf, stage, sems):
    rows = stage.shape[1]
    n = w_hbm.shape[0] // rows

    def chunk_copy(c):
        return pltpu.make_async_copy(w_hbm.at[pl.ds(c * rows, rows)], stage.at[c % 2],
                                     sems.at[c % 2])

    chunk_copy(0).start()
    for c in range(n):
        if c + 1 < n:
            chunk_copy(c + 1).start()
        chunk_copy(c).wait()
        w_bf[pl.ds(c * rows, rows), :] = stage[c % 2].astype(BF16)


WEIGHT_STAGE_ROWS = 512


def _mix_kernel(ya_ref, yb_ref, gates_a_ref, gates_b_ref, h_ref, wa_hbm, wb_hbm, wm_hbm,
                g_ref, b_ref, o_ref, ob_ref, wa_ref, wb_ref, wm_ref, stage, sems):
    @pl.when(pl.program_id(0) == 0)
    def _():
        _load_weight_bf16(wa_hbm, wa_ref, stage, sems)
        _load_weight_bf16(wb_hbm, wb_ref, stage, sems)
        _load_weight_bf16(wm_hbm, wm_ref, stage, sems)

    pa = jnp.dot(ya_ref[...], wa_ref[...], preferred_element_type=F32)
    pb = jnp.dot(yb_ref[...], wb_ref[...], preferred_element_type=F32)
    merged = jax.nn.sigmoid(gates_a_ref[...]) * pa + jax.nn.sigmoid(gates_b_ref[...]) * pb
    mixed = jnp.dot(merged.astype(BF16), wm_ref[...], preferred_element_type=F32)
    y = _layer_norm(DEEPNORM_ALPHA * h_ref[...] + mixed, g_ref[...], b_ref[...])
    o_ref[...] = y
    ob_ref[...] = y.astype(BF16)


def _mix(ya, yb, gates, h, wa, wb, wm, g, b):
    t, d = h.shape
    tm = min(ATT_TILE, t)
    row = pl.BlockSpec((tm, d), lambda i: (i, 0))
    return pl.pallas_call(
        _mix_kernel,
        grid=(t // tm,),
        in_specs=[pl.BlockSpec((tm, A_WIDTH), lambda i: (i, 0)),
                  pl.BlockSpec((tm, B_WIDTH), lambda i: (i, 0)),
                  pl.BlockSpec((tm, d), lambda i: (i, 0)),
                  pl.BlockSpec((tm, d), lambda i: (i, 1)),
                  row,
                  pl.BlockSpec(memory_space=pl.ANY), pl.BlockSpec(memory_space=pl.ANY),
                  pl.BlockSpec(memory_space=pl.ANY),
                  _const_spec((1, d)), _const_spec((1, d))],
        out_specs=[row, row],
        out_shape=[jax.ShapeDtypeStruct((t, d), F32), jax.ShapeDtypeStruct((t, d), BF16)],
        scratch_shapes=[pltpu.VMEM((A_WIDTH, d), BF16), pltpu.VMEM((B_WIDTH, d), BF16),
                        pltpu.VMEM((d, d), BF16),
                        pltpu.VMEM((2, min(WEIGHT_STAGE_ROWS, d), d), F32),
                        pltpu.SemaphoreType.DMA((2,))],
        compiler_params=_params(("arbitrary",)),
        name="mix_out",
    )(ya, yb, gates, gates, h, wa, wb, wm, g.reshape(1, d), b.reshape(1, d))


def _xattn_kernel(hb_ref, h_ref, k_ref, v_ref, wq_hbm, wo_hbm, g_ref, b_ref, wr_ref,
                  rb_ref, ltri_ref, o_ref, idx_ref, gate_ref, rank_ref, cnt_ref, run_ref,
                  wq_ref, wo_ref, stage, sems):
    first_step = (pl.program_id(0) == 0) & (pl.program_id(1) == 0)

    @pl.when(first_step)
    def _():
        _load_weight_bf16(wq_hbm, wq_ref, stage, sems)
        _load_weight_bf16(wo_hbm, wo_ref, stage, sems)

    d = h_ref.shape[1]
    hd = d // X_HEADS
    scale = hd ** -0.5
    q = jnp.dot(hb_ref[...], wq_ref[...], preferred_element_type=F32).astype(BF16)
    outs = []
    for x in range(X_HEADS):
        sl = slice(x * hd, (x + 1) * hd)
        s = lax.dot_general(q[:, sl], k_ref[:, sl], NT_DIMS, preferred_element_type=F32) * scale
        m = jnp.max(s, axis=-1, keepdims=True)
        p = jnp.exp(s - m)
        denom = jnp.sum(p, axis=-1, keepdims=True)
        o = jnp.dot(p.astype(BF16), v_ref[:, sl], preferred_element_type=F32) / denom
        outs.append(o.astype(BF16))
    xo = jnp.dot(jnp.concatenate(outs, axis=1), wo_ref[...], preferred_element_type=F32)
    h_out = _layer_norm(DEEPNORM_ALPHA * h_ref[...] + xo, g_ref[...], b_ref[...])
    o_ref[...] = h_out
    _route_tile(h_out, first_step, wr_ref, rb_ref, ltri_ref, idx_ref, gate_ref,
                rank_ref, cnt_ref, run_ref)


def _xattn_route(hb, h, kv, wq, wo, g, b, w_router, b_router, batch, seq, mem_len):
    t, d = h.shape
    tm = min(ATT_TILE, seq)
    ns = seq // tm
    w_pad = jnp.zeros((d, V7X_LANES), F32).at[:, :N_EXPERTS].set(w_router)
    w_hi = w_pad.astype(BF16)
    w_lo = (w_pad - w_hi.astype(F32)).astype(BF16)
    w_split = jnp.concatenate([w_hi, w_lo], axis=1)
    b_pad = jnp.zeros((1, V7X_LANES), F32).at[0, :N_EXPERTS].set(b_router)
    ltri = jnp.asarray(np.tril(np.ones((tm, tm), np.float32), -1), dtype=BF16)
    row = pl.BlockSpec((tm, d), lambda bi, i: (bi * ns + i, 0))
    small = pl.BlockSpec((tm, V7X_LANES), lambda bi, i: (bi * ns + i, 0))
    return pl.pallas_call(
        _xattn_kernel,
        grid=(batch, ns),
        in_specs=[row, row,
                  pl.BlockSpec((mem_len, d), lambda bi, i: (bi, 0)),
                  pl.BlockSpec((mem_len, d), lambda bi, i: (bi, 1)),
                  pl.BlockSpec(memory_space=pl.ANY), pl.BlockSpec(memory_space=pl.ANY),
                  _const_spec((1, d)), _const_spec((1, d)),
                  _const_spec((d, 2 * V7X_LANES)),
                  _const_spec((1, V7X_LANES)), _const_spec((tm, tm))],
        out_specs=[row, small, small, small,
                   pl.BlockSpec((1, V7X_LANES), lambda bi, i: (0, 0))],
        out_shape=[jax.ShapeDtypeStruct((t, d), F32),
                   jax.ShapeDtypeStruct((t, V7X_LANES), jnp.int32),
                   jax.ShapeDtypeStruct((t, V7X_LANES), F32),
                   jax.ShapeDtypeStruct((t, V7X_LANES), jnp.int32),
                   jax.ShapeDtypeStruct((1, V7X_LANES), F32)],
        scratch_shapes=[pltpu.VMEM((1, V7X_LANES), F32),
                        pltpu.VMEM((d, d), BF16), pltpu.VMEM((d, d), BF16),
                        pltpu.VMEM((2, min(WEIGHT_STAGE_ROWS, d), d), F32),
                        pltpu.SemaphoreType.DMA((2,))],
        compiler_params=_params(("arbitrary", "arbitrary")),
        name="xattn_route",
    )(hb, h, kv, kv, wq, wo, g.reshape(1, d), b.reshape(1, d), w_split, b_pad, ltri)


def _route_tile(h, first_step, w_ref, b_ref, ltri_ref, idx_ref, gate_ref, rank_ref,
                cnt_ref, run_ref):
    @pl.when(first_step)
    def _():
        run_ref[...] = jnp.zeros_like(run_ref)

    h_hi = h.astype(BF16)
    h_lo = (h - h_hi.astype(F32)).astype(BF16)
    hi_terms = jnp.dot(h_hi, w_ref[...], preferred_element_type=F32)
    lo_hi = jnp.dot(h_lo, w_ref[:, 0:V7X_LANES], preferred_element_type=F32)
    logits = (hi_terms[:, 0:V7X_LANES] + (lo_hi + hi_terms[:, V7X_LANES:2 * V7X_LANES])
              + b_ref[...])
    tm = logits.shape[0]
    lane_i = lax.broadcasted_iota(jnp.int32, (tm, V7X_LANES), 1)
    lane = lane_i.astype(F32)
    rem = jnp.where(lane_i < N_EXPERTS, logits, -jnp.inf)
    vals, idxs = [], []
    for _ in range(TOP_K):
        m = jnp.max(rem, axis=-1, keepdims=True)
        ik = jnp.min(jnp.where(rem == m, lane, float(V7X_LANES)), axis=-1, keepdims=True)
        vals.append(m)
        idxs.append(ik)
        rem = jnp.where(lane == ik, -jnp.inf, rem)
    exps = [jnp.exp(v - vals[0]) for v in vals]
    denom = exps[0] + exps[1] + exps[2] + exps[3]
    sel = jnp.zeros((tm, V7X_LANES), F32)
    for ik in idxs:
        sel = sel + (lane == ik).astype(F32)
    before = jnp.dot(ltri_ref[...], sel.astype(BF16), preferred_element_type=F32) + run_ref[...]
    idx_out = jnp.zeros((tm, V7X_LANES), jnp.int32)
    gate_out = jnp.zeros((tm, V7X_LANES), F32)
    rank_out = jnp.zeros((tm, V7X_LANES), jnp.int32)
    for k in range(TOP_K):
        rank_k = jnp.sum(jnp.where(lane == idxs[k], before, 0.0), axis=-1, keepdims=True)
        idx_out = jnp.where(lane_i == k, idxs[k].astype(jnp.int32), idx_out)
        gate_out = jnp.where(lane_i == k, exps[k] / denom, gate_out)
        rank_out = jnp.where(lane_i == k, rank_k.astype(jnp.int32), rank_out)
    idx_ref[...] = idx_out
    gate_ref[...] = gate_out
    rank_ref[...] = rank_out
    run_ref[...] = run_ref[...] + jnp.sum(sel, axis=0, keepdims=True)
    cnt_ref[...] = run_ref[...]


DMA_UNROLL = 8
NUM_DMA_QUEUES = 2
WEIGHT_DMA_PRIORITY = 1
(META_EXPERT, META_FIRST, META_LAST_GROUP, META_NEXT_EXPERT, META_GROUP,
 META_ROWS_USED, META_NUM_GROUPS) = range(7)


def _pack_bf16_pairs(x):
    half = x.shape[1] // 2
    lo = lax.bitcast_convert_type(x[:, :half].astype(BF16).astype(F32), jnp.uint32)
    hi = lax.bitcast_convert_type(x[:, half:].astype(BF16).astype(F32), jnp.uint32)
    return (lo >> 16) | (hi & jnp.uint32(0xFFFF0000))


def _unpack_bf16_pairs(words):
    lo = lax.bitcast_convert_type(words << 16, F32).astype(BF16)
    hi = lax.bitcast_convert_type(words & jnp.uint32(0xFFFF0000), F32).astype(BF16)
    return jnp.concatenate([lo, hi], axis=1)


def _dispatch_kernel(dest_ref, meta_ref, h_ref, xs_hbm, pack_ref, zero_ref, sem, zsem):
    tm = h_ref.shape[0]
    base = pl.program_id(0) * tm
    et = zero_ref.shape[0]
    pack_ref[...] = _pack_bf16_pairs(h_ref[...])

    @pl.when(pl.program_id(0) == 0)
    def _():
        zero_ref[...] = jnp.zeros_like(zero_ref)

        def fill_copy(tile):
            start_row = pl.multiple_of(tile * et, et)
            return pltpu.make_async_copy(zero_ref, xs_hbm.at[pl.ds(start_row, et)], zsem)

        def fill_start(tile, c):
            @pl.when(meta_ref[META_ROWS_USED, tile] < et)
            def _():
                fill_copy(tile).start()
            return c

        def fill_wait(tile, c):
            @pl.when(meta_ref[META_ROWS_USED, tile] < et)
            def _():
                fill_copy(tile).wait()
            return c

        n_tiles = xs_hbm.shape[0] // et
        lax.fori_loop(0, n_tiles, fill_start, 0)
        lax.fori_loop(0, n_tiles, fill_wait, 0)

    def row_copy(r, k):
        return pltpu.make_async_copy(pack_ref.at[pl.ds(r, 1)],
                                     xs_hbm.at[pl.ds(dest_ref[(base + r) * TOP_K + k], 1)], sem)

    def start(r, c):
        for k in range(TOP_K):
            row_copy(r, k).start(priority=k % NUM_DMA_QUEUES)
        return c

    def wait(r, c):
        for k in range(TOP_K):
            row_copy(r, k).wait()
        return c

    lax.fori_loop(0, tm, start, 0, unroll=DMA_UNROLL)
    lax.fori_loop(0, tm, wait, 0, unroll=DMA_UNROLL)


def _dispatch(h, dest, meta, n_rows):
    t, d = h.shape
    tm = min(DISPATCH_TILE, t)
    grid_spec = pltpu.PrefetchScalarGridSpec(
        num_scalar_prefetch=2,
        grid=(t // tm,),
        in_specs=[pl.BlockSpec((tm, d), lambda i, dst, m: (i, 0))],
        out_specs=pl.BlockSpec(memory_space=pl.ANY),
        scratch_shapes=[pltpu.VMEM((tm, d // 2), jnp.uint32),
                        pltpu.VMEM((EXPERT_TILE, d // 2), jnp.uint32),
                        pltpu.SemaphoreType.DMA(()), pltpu.SemaphoreType.DMA(())],
    )
    return pl.pallas_call(
        _dispatch_kernel,
        grid_spec=grid_spec,
        out_shape=jax.ShapeDtypeStruct((n_rows, d // 2), jnp.uint32),
        compiler_params=_params(("arbitrary",)),
        name="moe_dispatch",
    )(dest, meta, h)


def _stream_expert_weights(meta_ref, j, t, nj, slots, make_copies, on_arrival):
    @pl.when((j == 0) & (t == 0))
    def _():
        for c in make_copies(meta_ref[META_EXPERT, 0], 0, 0):
            c.start(priority=WEIGHT_DMA_PRIORITY)

    @pl.when(meta_ref[META_FIRST, t] == 1)
    def _():
        if slots == 1:
            slot = next_slot = 0
        else:
            slot = (j * meta_ref[META_NUM_GROUPS, t] + meta_ref[META_GROUP, t]) & 1
            next_slot = 1 - slot
        for c in make_copies(meta_ref[META_EXPERT, t], j, slot):
            c.wait()
        if slots == 1:
            on_arrival(slot)
        last_group = meta_ref[META_LAST_GROUP, t] == 1

        @pl.when(jnp.logical_not(last_group & (j == nj - 1)))
        def _():
            next_chunk = jnp.where(last_group, j + 1, j)
            for c in make_copies(meta_ref[META_NEXT_EXPERT, t], next_chunk, next_slot):
                c.start(priority=WEIGHT_DMA_PRIORITY)

        if slots == 2:
            on_arrival(slot)


def _for_used_rows(rows_used, o_ref, compute):
    tm = o_ref.shape[0]
    part = tm // EXPERT_TILE_PARTS
    for parts in range(EXPERT_TILE_PARTS + 1):
        used = parts * part

        @pl.when((rows_used > used - part) & (rows_used <= used))
        def _(used=used):
            if used > 0:
                compute(slice(0, used))
            if used < tm:
                o_ref[used:tm, :] = jnp.zeros((tm - used, o_ref.shape[1]), o_ref.dtype)


def _moe_up_kernel(meta_ref, x_ref, w_hbm, b_ref, o_ref, wf32, wbf, sem, *, col_chunk):
    t = pl.program_id(0)
    f = o_ref.shape[1]

    def make_copies(expert, chunk, slot):
        del chunk, slot
        return (pltpu.make_async_copy(w_hbm.at[expert], wf32, sem),)

    def on_arrival(slot):
        del slot
        wbf[...] = wf32[...].astype(BF16)

    _stream_expert_weights(meta_ref, 0, t, 1, 1, make_copies, on_arrival)

    def expert_rows(rows):
        x = _unpack_bf16_pairs(x_ref[rows, :])
        for c0 in range(0, f, col_chunk):
            glu_cols = slice(c0, c0 + col_chunk)
            lin_cols = slice(f + c0, f + c0 + col_chunk)
            glu = jnp.dot(x, wbf[:, glu_cols], preferred_element_type=F32) + b_ref[:, glu_cols]
            lin = jnp.dot(x, wbf[:, lin_cols], preferred_element_type=F32) + b_ref[:, lin_cols]
            glu = jnp.minimum(glu, SWIGLU_LIMIT)
            lin = jnp.clip(lin, -SWIGLU_LIMIT, SWIGLU_LIMIT)
            act = glu * jax.nn.sigmoid(SWIGLU_ALPHA * glu) * (lin + 1.0)
            o_ref[rows, glu_cols] = act.astype(o_ref.dtype)

    _for_used_rows(meta_ref[META_ROWS_USED, t], o_ref, expert_rows)


def _moe_up(xs, w_up, b_up, meta):
    r = xs.shape[0]
    e, d, f2 = w_up.shape
    f = f2 // 2
    tm = EXPERT_TILE
    grid_spec = pltpu.PrefetchScalarGridSpec(
        num_scalar_prefetch=1,
        grid=(r // tm,),
        in_specs=[
            pl.BlockSpec((tm, d // 2), lambda t, m: (t, 0)),
            pl.BlockSpec(memory_space=pl.ANY),
            pl.BlockSpec((None, 1, f2), lambda t, m: (m[META_EXPERT, t], 0, 0)),
        ],
        out_specs=pl.BlockSpec((tm, f), lambda t, m: (t, 0)),
        scratch_shapes=[pltpu.VMEM((d, f2), F32), pltpu.VMEM((d, f2), BF16),
                        pltpu.SemaphoreType.DMA(())],
    )
    return pl.pallas_call(
        functools.partial(_moe_up_kernel, col_chunk=_pick_tile(f, 1024)),
        grid_spec=grid_spec,
        out_shape=jax.ShapeDtypeStruct((r, f), BF16),
        compiler_params=_params(("arbitrary",)),
        name="moe_up",
    )(meta, xs, w_up, b_up.reshape(e, 1, f2))


def _moe_down_kernel(meta_ref, a_ref, w_hbm, b_ref, o_ref, wf32, wbf, sems, *, tn, nj):
    j = pl.program_id(0)
    t = pl.program_id(1)

    def make_copies(expert, chunk, slot):
        col = pl.multiple_of(chunk * tn, tn)
        return (pltpu.make_async_copy(w_hbm.at[expert, :, pl.ds(col, tn)],
                                      wf32.at[slot], sems.at[slot]),)

    def on_arrival(slot):
        wbf[...] = wf32[slot].astype(BF16)

    _stream_expert_weights(meta_ref, j, t, nj, 2, make_copies, on_arrival)

    def expert_rows(rows):
        o_ref[rows, :] = (jnp.dot(a_ref[rows, :], wbf[...], preferred_element_type=F32)
                          + b_ref[...])

    _for_used_rows(meta_ref[META_ROWS_USED, t], o_ref, expert_rows)


def _moe_down(act, w_down, b_down, meta):
    r, f = act.shape
    e, _, d = w_down.shape
    tm = EXPERT_TILE
    tn = _pick_tile(d, 2048)
    nj = d // tn
    grid_spec = pltpu.PrefetchScalarGridSpec(
        num_scalar_prefetch=1,
        grid=(nj, r // tm),
        in_specs=[
            pl.BlockSpec((tm, f), lambda j, t, m: (t, 0)),
            pl.BlockSpec(memory_space=pl.ANY),
            pl.BlockSpec((None, 1, tn), lambda j, t, m: (m[META_EXPERT, t], 0, j)),
        ],
        out_specs=pl.BlockSpec((tm, tn), lambda j, t, m: (t, j)),
        scratch_shapes=[pltpu.VMEM((2, f, tn), F32), pltpu.VMEM((f, tn), BF16),
                        pltpu.SemaphoreType.DMA((2,))],
    )
    return pl.pallas_call(
        functools.partial(_moe_down_kernel, tn=tn, nj=nj),
        grid_spec=grid_spec,
        out_shape=jax.ShapeDtypeStruct((r, d), F32),
        compiler_params=_params(("arbitrary", "arbitrary")),
        name="moe_down",
    )(meta, act, w_down, b_down.reshape(e, 1, d))


def _combine_kernel(pos_ref, y_hbm, gate_ref, h_ref, g_ref, b_ref, o_ref, buf_ref, sems):
    tm = o_ref.shape[0]
    i = pl.program_id(0)

    def row_copy(step, r, k, slot):
        src_row = pos_ref[(step * tm + r) * TOP_K + k]
        return pltpu.make_async_copy(y_hbm.at[pl.ds(src_row, 1)],
                                     buf_ref.at[slot, k, pl.ds(r, 1)], sems.at[slot])

    def gather_start(step, slot):
        def body(r, c):
            for k in range(TOP_K):
                row_copy(step, r, k, slot).start(priority=k % NUM_DMA_QUEUES)
            return c
        lax.fori_loop(0, tm, body, 0, unroll=DMA_UNROLL)

    def gather_wait(step, slot):
        def body(r, c):
            for k in range(TOP_K):
                row_copy(step, r, k, slot).wait()
            return c
        lax.fori_loop(0, tm, body, 0, unroll=DMA_UNROLL)

    @pl.when(i == 0)
    def _():
        gather_start(0, 0)

    for next_slot in range(2):
        @pl.when((i + 1 < pl.num_programs(0)) & ((i + 1) & 1 == next_slot))
        def _(next_slot=next_slot):
            gather_start(i + 1, next_slot)

    slot = i & 1
    gather_wait(i, slot)
    gates = gate_ref[...]
    ff = gates[:, 0:1] * buf_ref[slot, 0]
    for k in range(1, TOP_K):
        ff = ff + gates[:, k:k + 1] * buf_ref[slot, k]
    o_ref[...] = _layer_norm(DEEPNORM_ALPHA * h_ref[...] + ff, g_ref[...], b_ref[...])


def _combine(y_rows, pos, gates, h, g, b):
    t, d = h.shape
    tm = min(COMBINE_TILE, t)
    grid_spec = pltpu.PrefetchScalarGridSpec(
        num_scalar_prefetch=1,
        grid=(t // tm,),
        in_specs=[pl.BlockSpec(memory_space=pl.ANY),
                  pl.BlockSpec((tm, V7X_LANES), lambda i, p: (i, 0)),
                  pl.BlockSpec((tm, d), lambda i, p: (i, 0)),
                  pl.BlockSpec((1, d), lambda i, p: (0, 0)),
                  pl.BlockSpec((1, d), lambda i, p: (0, 0))],
        out_specs=pl.BlockSpec((tm, d), lambda i, p: (i, 0)),
        scratch_shapes=[pltpu.VMEM((2, TOP_K, tm, d), F32), pltpu.SemaphoreType.DMA((2,))],
    )
    return pl.pallas_call(
        _combine_kernel,
        grid_spec=grid_spec,
        out_shape=jax.ShapeDtypeStruct((t, d), F32),
        compiler_params=_params(("arbitrary",)),
        name="moe_combine",
    )(pos, y_rows, gates, h, g.reshape(1, d), b.reshape(1, d))


def _expert_layout(idx, rank, counts, n_tiles, tm):
    experts = jnp.arange(N_EXPERTS, dtype=jnp.int32)
    tiles_per = (counts + tm - 1) // tm
    tile_end = jnp.cumsum(tiles_per).astype(jnp.int32)
    tile_off = tile_end - tiles_per
    dest = jnp.sum(jnp.where(idx[..., None] == experts, tile_off * tm, 0), axis=-1) + rank

    nonempty = counts > 0
    first_e = jnp.min(jnp.where(nonempty, experts, N_EXPERTS))
    last_e = jnp.max(jnp.where(nonempty, experts, 0))
    group_of = jnp.cumsum(nonempty.astype(jnp.int32)) - 1
    later = (experts[None, :] > experts[:, None]) & nonempty[None, :]
    next_e = jnp.min(jnp.where(later, experts[None, :], N_EXPERTS), axis=1)
    next_e = jnp.where(next_e == N_EXPERTS, first_e, next_e)

    tid = jnp.arange(n_tiles, dtype=jnp.int32)
    valid = tid < tile_end[-1]
    te = jnp.sum((tid[:, None] >= tile_end[None, :]).astype(jnp.int32), axis=1)
    te = jnp.where(valid, jnp.minimum(te, N_EXPERTS - 1), last_e)
    onehot = te[:, None] == experts[None, :]

    def lookup(table):
        return jnp.sum(jnp.where(onehot, table[None, :], 0), axis=1)

    local = tid - lookup(tile_off)
    rows_used = jnp.where(valid, jnp.clip(lookup(counts) - local * tm, 0, tm), 0)
    meta = jnp.stack([
        te,
        (valid & (local == 0)).astype(jnp.int32),
        (te == last_e).astype(jnp.int32),
        lookup(next_e),
        lookup(group_of),
        rows_used,
        jnp.broadcast_to(jnp.sum(nonempty.astype(jnp.int32)), (n_tiles,)),
    ]).astype(jnp.int32)
    return dest.reshape(-1).astype(jnp.int32), meta


def _moe(h, routing, w_up, b_up, w_down, b_down, g, b):
    t, d = h.shape
    tm = EXPERT_TILE
    idx_p, gate_p, rank_p, cnt_p = routing
    n_rows = t * TOP_K + N_EXPERTS * tm
    dest, meta = _expert_layout(idx_p[:, :TOP_K], rank_p[:, :TOP_K],
                                cnt_p[0, :N_EXPERTS].astype(jnp.int32), n_rows // tm, tm)
    xs = _dispatch(h, dest, meta, n_rows)
    act = _moe_up(xs, w_up, b_up, meta)
    y_rows = _moe_down(act, w_down, b_down, meta)
    return _combine(y_rows, dest, gate_p, h, g, b)


def kernel(x, mem, ln_in_g, ln_in_b, rel_bias, w_in, b_in, attn_sinks, w_a_out, w_b_out,
           w_mix_out, ln1_g, ln1_b, w_xq, w_xkv, w_xo, ln2_g, ln2_b, w_router, b_router,
           w_up, b_up, w_down, b_down, ln3_g, ln3_b):
    batch, seq, d = x.shape
    mem_len = mem.shape[1]
    t = batch * seq
    a_cols = A_WIDTH + 2 * A_KV_WIDTH
    b_cols = 3 * B_WIDTH

    g_col = a_cols + b_cols
    h, hb, qkv_a = _ln_proj(x.reshape(t, d), ln_in_g, ln_in_b, w_in[0], b_in[0, :a_cols],
                            a_cols, BF16)
    for l in range(DEPTH):
        if l > 0:
            qkv_a = _mm_bias(hb, w_in[l], b_in[l, :a_cols], 0, a_cols, BF16, "in_proj_a")
        qkv_b = _mm_bias(hb, w_in[l], b_in[l, a_cols:g_col], a_cols, b_cols, BF16, "in_proj_b")
        gates = _mm_bias(hb, w_in[l], b_in[l, g_col:], g_col, 2 * d, F32, "in_proj_gates")
        ya = _swa(qkv_a, attn_sinks[l], rel_bias, batch, seq)
        yb = _stick_breaking(qkv_b, batch, seq)
        h, hb = _mix(ya, yb, gates, h, w_a_out[l], w_b_out[l], w_mix_out[l], ln1_g[l], ln1_b[l])
        kv = _mm_bias(mem.reshape(batch * mem_len, d).astype(BF16), w_xkv[l],
                      jnp.zeros((2 * d,), F32), 0, 2 * d, BF16, "mem_kv")
        h, *routing = _xattn_route(hb, h, kv, w_xq[l], w_xo[l], ln2_g[l], ln2_b[l],
                                   w_router[l], b_router[l], batch, seq, mem_len)
        h = _moe(h, routing, w_up[l], b_up[l], w_down[l], b_down[l], ln3_g[l], ln3_b[l])
        if l + 1 < DEPTH:
            hb = h.astype(BF16)
    return h.reshape(batch, seq, d)
```

```python
import functools

import jax
import jax.numpy as jnp
import numpy as np
from jax import lax
from jax.experimental import pallas as pl
from jax.experimental.pallas import tpu as pltpu

DEPTH = 1
CHUNK = 64
BLK = 128
HEAD_DIM = 64
A_HEADS = 16
A_KV_HEADS = 2
A_REP = A_HEADS // A_KV_HEADS
WIN_CHUNKS = 2
B_HEADS = 16
A_WIDTH = A_HEADS * HEAD_DIM
A_KV_WIDTH = A_KV_HEADS * HEAD_DIM
B_WIDTH = B_HEADS * HEAD_DIM
REL_BUCKETS = 32
REL_MAX_DIST = 128
X_HEADS = 4
N_EXPERTS = 32
TOP_K = 4
SWIGLU_LIMIT = 7.0
SWIGLU_ALPHA = 1.702
LN_EPS = 1e-5
DEEPNORM_ALPHA = (2.0 * DEPTH) ** 0.25
QK_SCALE = HEAD_DIM ** -0.5
V7X_LANES = 128
V7X_VMEM_LIMIT_BYTES = 56 * 1024 * 1024
ROW_TILE = 512
MM_ROW_TILE = 2048
MM_COL_TILE = 1024
EXPERT_DOWN_COL_TILE = 2048
ATT_TILE = 256
SB_TILE = 256
SB_PAIRS = 4
SB_ROWS = 32
EXPERT_TILE = 256
EXPERT_TILE_PARTS = 4
DISPATCH_TILE = 512
COMBINE_TILE = 256
SB_LOG_ZERO = -105.0

F32 = jnp.float32
BF16 = jnp.bfloat16
NT_DIMS = (((1,), (1,)), ((), ()))


def _params(semantics):
    return pltpu.CompilerParams(dimension_semantics=semantics,
                                vmem_limit_bytes=V7X_VMEM_LIMIT_BYTES)


def _const_spec(shape):
    nd = len(shape)
    return pl.BlockSpec(shape, lambda *_: (0,) * nd, pipeline_mode=pl.Buffered(1))


def _layer_norm(x, g, b):
    mu = jnp.mean(x, axis=-1, keepdims=True)
    xc = x - mu
    var = jnp.mean(xc * xc, axis=-1, keepdims=True)
    return xc * lax.rsqrt(var + LN_EPS) * g + b


def _ln_proj_kernel(x_ref, g_ref, b_ref, w_hbm, bias_ref, h_ref, hb_ref, o_ref, stage, wbf, sem,
                    *, n):
    @pl.when(pl.program_id(0) == 0)
    def _():
        slab = pltpu.make_async_copy(w_hbm.at[:, pl.ds(0, n)], stage, sem)
        slab.start()
        slab.wait()
        wbf[...] = stage[...].astype(BF16)

    y = _layer_norm(x_ref[...], g_ref[...], b_ref[...])
    yb = y.astype(BF16)
    h_ref[...] = y
    hb_ref[...] = yb
    acc = jnp.dot(yb, wbf[...], preferred_element_type=F32)
    o_ref[...] = (acc + bias_ref[...]).astype(o_ref.dtype)


def _ln_proj(x2d, g, b, w, bias, n, out_dtype):
    t, d = x2d.shape
    tm = min(ROW_TILE, t)
    row = pl.BlockSpec((tm, d), lambda i: (i, 0))
    return pl.pallas_call(
        functools.partial(_ln_proj_kernel, n=n),
        grid=(t // tm,),
        in_specs=[row, _const_spec((1, d)), _const_spec((1, d)),
                  pl.BlockSpec(memory_space=pl.ANY), _const_spec((1, n))],
        out_specs=[row, row, pl.BlockSpec((tm, n), lambda i: (i, 0))],
        out_shape=[jax.ShapeDtypeStruct((t, d), F32), jax.ShapeDtypeStruct((t, d), BF16),
                   jax.ShapeDtypeStruct((t, n), out_dtype)],
        scratch_shapes=[pltpu.VMEM((d, n), F32), pltpu.VMEM((d, n), BF16),
                        pltpu.SemaphoreType.DMA(())],
        compiler_params=_params(("arbitrary",)),
        name="ln_in_proj_a",
    )(x2d, g.reshape(1, d), b.reshape(1, d), w, bias.reshape(1, n).astype(F32))


def _mm_bias_kernel(a_ref, w_hbm, b_ref, o_ref, stage, wbf, sem, *, col0, tn, nj):
    j = pl.program_id(0)
    i = pl.program_id(1)

    def slab_copy(jj):
        col = pl.multiple_of(col0 + jj * tn, V7X_LANES)
        return pltpu.make_async_copy(w_hbm.at[:, pl.ds(col, tn)], stage, sem)

    @pl.when(i == 0)
    def _():
        @pl.when(j == 0)
        def _():
            slab_copy(0).start()

        slab_copy(j).wait()
        wbf[...] = stage[...].astype(BF16)

        @pl.when(j + 1 < nj)
        def _():
            slab_copy(j + 1).start()

    acc = jnp.dot(a_ref[...], wbf[...], preferred_element_type=F32)
    o_ref[...] = (acc + b_ref[...]).astype(o_ref.dtype)


def _pick_tile(n, target):
    best = None
    for c in range(V7X_LANES, min(n, target) + 1, V7X_LANES):
        if n % c == 0:
            best = c
    return n if best is None else best


def _mm_bias(a, w, b, col0, n, out_dtype, name):
    m, k = a.shape
    tm = min(MM_ROW_TILE, m)
    tn = _pick_tile(n, MM_COL_TILE)
    nj = n // tn
    return pl.pallas_call(
        functools.partial(_mm_bias_kernel, col0=col0, tn=tn, nj=nj),
        grid=(nj, m // tm),
        in_specs=[pl.BlockSpec((tm, k), lambda j, i: (i, 0)),
                  pl.BlockSpec(memory_space=pl.ANY),
                  pl.BlockSpec((1, tn), lambda j, i: (0, j))],
        out_specs=pl.BlockSpec((tm, tn), lambda j, i: (i, j)),
        out_shape=jax.ShapeDtypeStruct((m, n), out_dtype),
        scratch_shapes=[pltpu.VMEM((k, tn), F32), pltpu.VMEM((k, tn), BF16),
                        pltpu.SemaphoreType.DMA(())],
        compiler_params=_params(("arbitrary", "arbitrary")),
        name=name,
    )(a, w, b.reshape(1, n).astype(F32))


def _swa_kernel(sink_ref, q_ref, kp_ref, kc_ref, vp_ref, vc_ref, bias_ref, o_ref):
    n = pl.program_id(1)
    kk = jnp.concatenate([kp_ref[...], kc_ref[...]], axis=0)
    vv = jnp.concatenate([vp_ref[...], vc_ref[...]], axis=0)
    col = lax.broadcasted_iota(jnp.int32, (BLK, 2 * BLK), 1)
    exists = (col >= BLK) | (n > 0)
    for h in range(A_HEADS):
        g = h // A_REP
        qh = q_ref[:, h * HEAD_DIM:(h + 1) * HEAD_DIM] * QK_SCALE
        kg = kk[:, g * HEAD_DIM:(g + 1) * HEAD_DIM]
        vg = vv[:, g * HEAD_DIM:(g + 1) * HEAD_DIM]
        s = lax.dot_general(qh, kg, NT_DIMS, preferred_element_type=F32)
        s = jnp.where(exists, s + bias_ref[h], -jnp.inf)
        sink = sink_ref[h]
        m = jnp.maximum(jnp.max(s, axis=-1, keepdims=True), sink)
        p = jnp.exp(s - m)
        denom = jnp.sum(p, axis=-1, keepdims=True) + jnp.exp(sink - m)
        o = jnp.dot(p.astype(BF16), vg, preferred_element_type=F32) / denom
        o_ref[:, h * HEAD_DIM:(h + 1) * HEAD_DIM] = o.astype(o_ref.dtype)


def _rel_bucket(rel):
    half = REL_BUCKETS // 2
    max_exact = half // 2
    base = jnp.where(rel > 0, half, 0)
    n = jnp.abs(rel)
    nf = jnp.maximum(n, 1).astype(F32)
    large = max_exact + (jnp.log(nf / max_exact) / np.log(REL_MAX_DIST / max_exact)
                         * (half - max_exact)).astype(jnp.int32)
    large = jnp.minimum(large, half - 1)
    return base + jnp.where(n < max_exact, n, large)


def _swa_bias_table(rel_bias):
    rel = jnp.arange(-(2 * BLK - 1), BLK)
    line = jnp.transpose(rel_bias[_rel_bucket(rel)]).astype(F32)
    bias = jnp.stack([line[:, BLK - 1 - q:3 * BLK - 1 - q] for q in range(BLK)], axis=1)
    qi = np.arange(BLK)[:, None]
    kj = np.arange(2 * BLK)[None, :]
    dchunk = (kj // CHUNK - BLK // CHUNK) - qi // CHUNK
    band = (dchunk <= 0) & (dchunk >= -WIN_CHUNKS)
    return jnp.where(jnp.asarray(band)[None], bias, -jnp.inf)


def _swa(qkv, sinks, rel_bias, batch, seq):
    nb = seq // BLK
    kcol = A_WIDTH // A_KV_WIDTH
    vcol = kcol + 1
    kv_blk = (BLK, A_KV_WIDTH)
    grid_spec = pltpu.PrefetchScalarGridSpec(
        num_scalar_prefetch=1,
        grid=(batch, nb),
        in_specs=[
            pl.BlockSpec((BLK, A_WIDTH), lambda b, n, s: (b * nb + n, 0)),
            pl.BlockSpec(kv_blk, lambda b, n, s: (b * nb + jnp.maximum(n - 1, 0), kcol)),
            pl.BlockSpec(kv_blk, lambda b, n, s: (b * nb + n, kcol)),
            pl.BlockSpec(kv_blk, lambda b, n, s: (b * nb + jnp.maximum(n - 1, 0), vcol)),
            pl.BlockSpec(kv_blk, lambda b, n, s: (b * nb + n, vcol)),
            pl.BlockSpec((A_HEADS, BLK, 2 * BLK), lambda b, n, s: (0, 0, 0),
                         pipeline_mode=pl.Buffered(1)),
        ],
        out_specs=pl.BlockSpec((BLK, A_WIDTH), lambda b, n, s: (b * nb + n, 0)),
    )
    return pl.pallas_call(
        _swa_kernel,
        grid_spec=grid_spec,
        out_shape=jax.ShapeDtypeStruct((batch * seq, A_WIDTH), BF16),
        compiler_params=_params(("arbitrary", "arbitrary")),
        name="swa_attn",
    )(sinks.astype(F32), qkv, qkv, qkv, qkv, qkv, _swa_bias_table(rel_bias))


def _sb_kernel(q_ref, k_ref, v_ref, tri_ref, o_ref, qh_scr, z_scr, hl_scr, s_scr, w_scr,
               carry_scr, acc_scr):
    t = q_ref.shape[0]
    i = pl.program_id(2)
    n_heads = 2 * SB_PAIRS
    pair_w = 2 * HEAD_DIM
    lane = lax.broadcasted_iota(jnp.int32, (1, pair_w), 1)
    first = lane < HEAD_DIM
    zero = jnp.zeros((), BF16)

    def pair_cols(h):
        return slice((h // 2) * pair_w, (h // 2 + 1) * pair_w)

    def own_half(h, x):
        return jnp.where(first, x, zero) if h % 2 == 0 else jnp.where(first, zero, x)

    q2 = q_ref[...] * QK_SCALE
    for h in range(n_heads):
        qh_scr[h] = own_half(h, q2[:, pair_cols(h)])
    carry_scr[...] = jnp.zeros_like(carry_scr)
    acc_scr[...] = jnp.zeros_like(acc_scr)
    row = lax.broadcasted_iota(jnp.int32, (SB_ROWS, t), 0)
    col = lax.broadcasted_iota(jnp.int32, (SB_ROWS, t), 1)

    def key_tile(j, diagonal):
        start = pl.multiple_of(j * t, t)
        k2 = k_ref[pl.ds(start, t), :]
        v2 = v_ref[pl.ds(start, t), :]
        chunks = [slice(r0, r0 + SB_ROWS) for r0 in range(0, t, SB_ROWS)]
        for h in range(n_heads):
            z_scr[h] = lax.dot_general(qh_scr[h], k2[:, pair_cols(h)], NT_DIMS,
                                       preferred_element_type=F32)
        for rows in chunks:
            for h in range(n_heads):
                z = z_scr[h, rows, :]
                drop = jnp.maximum(z, 0.0) + jnp.log(1.0 + jnp.exp(-jnp.abs(z)))
                if diagonal:
                    drop = jnp.where(col < row + rows.start, drop, 0.0)
                hi = drop.astype(BF16)
                hl_scr[h, rows, 0:t] = hi
                hl_scr[h, rows, t:2 * t] = (drop - hi.astype(F32)).astype(BF16)
        tri = tri_ref[...]
        for h in range(n_heads):
            s_scr[h] = jnp.dot(hl_scr[h], tri, preferred_element_type=F32)
        top = jnp.full((SB_ROWS, V7X_LANES), -jnp.inf, F32)
        for rows in chunks:
            for h in range(n_heads):
                carry = carry_scr[h, rows, :]
                log_w = (z_scr[h, rows, :] + s_scr[h, rows, :]
                         + jnp.concatenate([carry] * (t // V7X_LANES), axis=1))
                w = jnp.exp(log_w)
                if diagonal:
                    w = jnp.where(col < row + rows.start, w, 0.0)
                w_scr[h // 2, rows, (h % 2) * t:(h % 2 + 1) * t] = w.astype(BF16)
                carry = carry + jnp.broadcast_to(s_scr[h, rows, 0:1], (SB_ROWS, V7X_LANES))
                carry_scr[h, rows, :] = carry
                top = jnp.maximum(top, carry)
        for p in range(SB_PAIRS):
            cols = pair_cols(2 * p)
            v_pair = jnp.concatenate([own_half(0, v2[:, cols]), own_half(1, v2[:, cols])], axis=0)
            acc_scr[:, cols] += jnp.dot(w_scr[p], v_pair, preferred_element_type=F32)
        return (jnp.max(top) > SB_LOG_ZERO).astype(jnp.int32)

    live = key_tile(i, True)

    def cond(state):
        return (state[0] >= 0) & (state[1] > 0)

    def body(state):
        return state[0] - 1, key_tile(state[0], False)

    lax.while_loop(cond, body, (i - 1, live))
    o_ref[...] = acc_scr[...].astype(o_ref.dtype)


def _stick_breaking(qkv, batch, seq):
    t = min(SB_TILE, seq)
    nq = seq // t
    width = SB_PAIRS * 2 * HEAD_DIM
    groups = B_WIDTH // width
    heads = 2 * SB_PAIRS
    j = np.arange(2 * t)[:, None] % t
    s = np.arange(t)[None, :]
    tri = jnp.asarray(-(j >= s).astype(np.float32), dtype=BF16)
    return pl.pallas_call(
        _sb_kernel,
        grid=(batch, groups, nq),
        in_specs=[
            pl.BlockSpec((t, width), lambda b, p, i: (b * nq + i, p)),
            pl.BlockSpec((seq, width), lambda b, p, i: (b, groups + p)),
            pl.BlockSpec((seq, width), lambda b, p, i: (b, 2 * groups + p)),
            _const_spec((2 * t, t)),
        ],
        out_specs=pl.BlockSpec((t, width), lambda b, p, i: (b * nq + i, p)),
        out_shape=jax.ShapeDtypeStruct((batch * seq, B_WIDTH), BF16),
        scratch_shapes=[
            pltpu.VMEM((heads, t, 2 * HEAD_DIM), BF16),
            pltpu.VMEM((heads, t, t), F32),
            pltpu.VMEM((heads, t, 2 * t), BF16),
            pltpu.VMEM((heads, t, t), F32),
            pltpu.VMEM((SB_PAIRS, t, 2 * t), BF16),
            pltpu.VMEM((heads, t, V7X_LANES), F32),
            pltpu.VMEM((t, width), F32),
        ],
        compiler_params=_params(("arbitrary", "arbitrary", "arbitrary")),
        name="stick_breaking",
    )(qkv, qkv, qkv, tri)


def _load_weight_bf16(w_hbm, w_bf, stage, sems):
    rows = stage.shape[1]
    n = w_hbm.shape[0] // rows

    def chunk_copy(c):
        return pltpu.make_async_copy(w_hbm.at[pl.ds(c * rows, rows)], stage.at[c % 2],
                                     sems.at[c % 2])

    chunk_copy(0).start()
    for c in range(n):
        if c + 1 < n:
            chunk_copy(c + 1).start()
        chunk_copy(c).wait()
        w_bf[pl.ds(c * rows, rows), :] = stage[c % 2].astype(BF16)


WEIGHT_STAGE_ROWS = 512


def _mix_kernel(ya_ref, yb_ref, gates_a_ref, gates_b_ref, h_ref, wa_hbm, wb_hbm, wm_hbm,
                g_ref, b_ref, o_ref, ob_ref, wa_ref, wb_ref, wm_ref, stage, sems):
    @pl.when(pl.program_id(0) == 0)
    def _():
        _load_weight_bf16(wa_hbm, wa_ref, stage, sems)
        _load_weight_bf16(wb_hbm, wb_ref, stage, sems)
        _load_weight_bf16(wm_hbm, wm_ref, stage, sems)

    pa = jnp.dot(ya_ref[...], wa_ref[...], preferred_element_type=F32)
    pb = jnp.dot(yb_ref[...], wb_ref[...], preferred_element_type=F32)
    merged = jax.nn.sigmoid(gates_a_ref[...]) * pa + jax.nn.sigmoid(gates_b_ref[...]) * pb
    mixed = jnp.dot(merged.astype(BF16), wm_ref[...], preferred_element_type=F32)
    y = _layer_norm(DEEPNORM_ALPHA * h_ref[...] + mixed, g_ref[...], b_ref[...])
    o_ref[...] = y
    ob_ref[...] = y.astype(BF16)


def _mix(ya, yb, gates, h, wa, wb, wm, g, b):
    t, d = h.shape
    tm = min(ATT_TILE, t)
    row = pl.BlockSpec((tm, d), lambda i: (i, 0))
    return pl.pallas_call(
        _mix_kernel,
        grid=(t // tm,),
        in_specs=[pl.BlockSpec((tm, A_WIDTH), lambda i: (i, 0)),
                  pl.BlockSpec((tm, B_WIDTH), lambda i: (i, 0)),
                  pl.BlockSpec((tm, d), lambda i: (i, 0)),
                  pl.BlockSpec((tm, d), lambda i: (i, 1)),
                  row,
                  pl.BlockSpec(memory_space=pl.ANY), pl.BlockSpec(memory_space=pl.ANY),
                  pl.BlockSpec(memory_space=pl.ANY),
                  _const_spec((1, d)), _const_spec((1, d))],
        out_specs=[row, row],
        out_shape=[jax.ShapeDtypeStruct((t, d), F32), jax.ShapeDtypeStruct((t, d), BF16)],
        scratch_shapes=[pltpu.VMEM((A_WIDTH, d), BF16), pltpu.VMEM((B_WIDTH, d), BF16),
                        pltpu.VMEM((d, d), BF16),
                        pltpu.VMEM((2, min(WEIGHT_STAGE_ROWS, d), d), F32),
                        pltpu.SemaphoreType.DMA((2,))],
        compiler_params=_params(("arbitrary",)),
        name="mix_out",
    )(ya, yb, gates, gates, h, wa, wb, wm, g.reshape(1, d), b.reshape(1, d))


def _xattn_kernel(hb_ref, h_ref, k_ref, v_ref, wq_hbm, wo_hbm, g_ref, b_ref, wr_ref,
                  rb_ref, ltri_ref, o_ref, idx_ref, gate_ref, rank_ref, cnt_ref, run_ref,
                  wq_ref, wo_ref, stage, sems):
    first_step = (pl.program_id(0) == 0) & (pl.program_id(1) == 0)

    @pl.when(first_step)
    def _():
        _load_weight_bf16(wq_hbm, wq_ref, stage, sems)
        _load_weight_bf16(wo_hbm, wo_ref, stage, sems)

    d = h_ref.shape[1]
    hd = d // X_HEADS
    scale = hd ** -0.5
    q = jnp.dot(hb_ref[...], wq_ref[...], preferred_element_type=F32).astype(BF16)
    outs = []
    for x in range(X_HEADS):
        sl = slice(x * hd, (x + 1) * hd)
        s = lax.dot_general(q[:, sl], k_ref[:, sl], NT_DIMS, preferred_element_type=F32) * scale
        m = jnp.max(s, axis=-1, keepdims=True)
        p = jnp.exp(s - m)
        denom = jnp.sum(p, axis=-1, keepdims=True)
        o = jnp.dot(p.astype(BF16), v_ref[:, sl], preferred_element_type=F32) / denom
        outs.append(o.astype(BF16))
    xo = jnp.dot(jnp.concatenate(outs, axis=1), wo_ref[...], preferred_element_type=F32)
    h_out = _layer_norm(DEEPNORM_ALPHA * h_ref[...] + xo, g_ref[...], b_ref[...])
    o_ref[...] = h_out
    _route_tile(h_out, first_step, wr_ref, rb_ref, ltri_ref, idx_ref, gate_ref,
                rank_ref, cnt_ref, run_ref)


def _xattn_route(hb, h, kv, wq, wo, g, b, w_router, b_router, batch, seq, mem_len):
    t, d = h.shape
    tm = min(ATT_TILE, seq)
    ns = seq // tm
    w_pad = jnp.zeros((d, V7X_LANES), F32).at[:, :N_EXPERTS].set(w_router)
    w_hi = w_pad.astype(BF16)
    w_lo = (w_pad - w_hi.astype(F32)).astype(BF16)
    w_split = jnp.concatenate([w_hi, w_lo], axis=1)
    b_pad = jnp.zeros((1, V7X_LANES), F32).at[0, :N_EXPERTS].set(b_router)
    ltri = jnp.asarray(np.tril(np.ones((tm, tm), np.float32), -1), dtype=BF16)
    row = pl.BlockSpec((tm, d), lambda bi, i: (bi * ns + i, 0))
    small = pl.BlockSpec((tm, V7X_LANES), lambda bi, i: (bi * ns + i, 0))
    return pl.pallas_call(
        _xattn_kernel,
        grid=(batch, ns),
        in_specs=[row, row,
                  pl.BlockSpec((mem_len, d), lambda bi, i: (bi, 0)),
                  pl.BlockSpec((mem_len, d), lambda bi, i: (bi, 1)),
                  pl.BlockSpec(memory_space=pl.ANY), pl.BlockSpec(memory_space=pl.ANY),
                  _const_spec((1, d)), _const_spec((1, d)),
                  _const_spec((d, 2 * V7X_LANES)),
                  _const_spec((1, V7X_LANES)), _const_spec((tm, tm))],
        out_specs=[row, small, small, small,
                   pl.BlockSpec((1, V7X_LANES), lambda bi, i: (0, 0))],
        out_shape=[jax.ShapeDtypeStruct((t, d), F32),
                   jax.ShapeDtypeStruct((t, V7X_LANES), jnp.int32),
                   jax.ShapeDtypeStruct((t, V7X_LANES), F32),
                   jax.ShapeDtypeStruct((t, V7X_LANES), jnp.int32),
                   jax.ShapeDtypeStruct((1, V7X_LANES), F32)],
        scratch_shapes=[pltpu.VMEM((1, V7X_LANES), F32),
                        pltpu.VMEM((d, d), BF16), pltpu.VMEM((d, d), BF16),
                        pltpu.VMEM((2, min(WEIGHT_STAGE_ROWS, d), d), F32),
                        pltpu.SemaphoreType.DMA((2,))],
        compiler_params=_params(("arbitrary", "arbitrary")),
        name="xattn_route",
    )(hb, h, kv, kv, wq, wo, g.reshape(1, d), b.reshape(1, d), w_split, b_pad, ltri)


def _route_tile(h, first_step, w_ref, b_ref, ltri_ref, idx_ref, gate_ref, rank_ref,
                cnt_ref, run_ref):
    @pl.when(first_step)
    def _():
        run_ref[...] = jnp.zeros_like(run_ref)

    h_hi = h.astype(BF16)
    h_lo = (h - h_hi.astype(F32)).astype(BF16)
    hi_terms = jnp.dot(h_hi, w_ref[...], preferred_element_type=F32)
    lo_hi = jnp.dot(h_lo, w_ref[:, 0:V7X_LANES], preferred_element_type=F32)
    logits = (hi_terms[:, 0:V7X_LANES] + (lo_hi + hi_terms[:, V7X_LANES:2 * V7X_LANES])
              + b_ref[...])
    tm = logits.shape[0]
    lane_i = lax.broadcasted_iota(jnp.int32, (tm, V7X_LANES), 1)
    lane = lane_i.astype(F32)
    rem = jnp.where(lane_i < N_EXPERTS, logits, -jnp.inf)
    vals, idxs = [], []
    for _ in range(TOP_K):
        m = jnp.max(rem, axis=-1, keepdims=True)
        ik = jnp.min(jnp.where(rem == m, lane, float(V7X_LANES)), axis=-1, keepdims=True)
        vals.append(m)
        idxs.append(ik)
        rem = jnp.where(lane == ik, -jnp.inf, rem)
    exps = [jnp.exp(v - vals[0]) for v in vals]
    denom = exps[0] + exps[1] + exps[2] + exps[3]
    sel = jnp.zeros((tm, V7X_LANES), F32)
    for ik in idxs:
        sel = sel + (lane == ik).astype(F32)
    before = jnp.dot(ltri_ref[...], sel.astype(BF16), preferred_element_type=F32) + run_ref[...]
    idx_out = jnp.zeros((tm, V7X_LANES), jnp.int32)
    gate_out = jnp.zeros((tm, V7X_LANES), F32)
    rank_out = jnp.zeros((tm, V7X_LANES), jnp.int32)
    for k in range(TOP_K):
        rank_k = jnp.sum(jnp.where(lane == idxs[k], before, 0.0), axis=-1, keepdims=True)
        idx_out = jnp.where(lane_i == k, idxs[k].astype(jnp.int32), idx_out)
        gate_out = jnp.where(lane_i == k, exps[k] / denom, gate_out)
        rank_out = jnp.where(lane_i == k, rank_k.astype(jnp.int32), rank_out)
    idx_ref[...] = idx_out
    gate_ref[...] = gate_out
    rank_ref[...] = rank_out
    run_ref[...] = run_ref[...] + jnp.sum(sel, axis=0, keepdims=True)
    cnt_ref[...] = run_ref[...]


DMA_UNROLL = 8
NUM_DMA_QUEUES = 2
WEIGHT_DMA_PRIORITY = 1
(META_EXPERT, META_FIRST, META_LAST_GROUP, META_NEXT_EXPERT, META_GROUP,
 META_ROWS_USED, META_NUM_GROUPS) = range(7)


def _pack_bf16_pairs(x):
    half = x.shape[1] // 2
    lo = lax.bitcast_convert_type(x[:, :half].astype(BF16).astype(F32), jnp.uint32)
    hi = lax.bitcast_convert_type(x[:, half:].astype(BF16).astype(F32), jnp.uint32)
    return (lo >> 16) | (hi & jnp.uint32(0xFFFF0000))


def _unpack_bf16_pairs(words):
    lo = lax.bitcast_convert_type(words << 16, F32).astype(BF16)
    hi = lax.bitcast_convert_type(words & jnp.uint32(0xFFFF0000), F32).astype(BF16)
    return jnp.concatenate([lo, hi], axis=1)


def _dispatch_kernel(dest_ref, meta_ref, h_ref, xs_hbm, pack_ref, zero_ref, sem, zsem):
    tm = h_ref.shape[0]
    base = pl.program_id(0) * tm
    et = zero_ref.shape[0]
    pack_ref[...] = _pack_bf16_pairs(h_ref[...])

    @pl.when(pl.program_id(0) == 0)
    def _():
        zero_ref[...] = jnp.zeros_like(zero_ref)

        def fill_copy(tile):
            start_row = pl.multiple_of(tile * et, et)
            return pltpu.make_async_copy(zero_ref, xs_hbm.at[pl.ds(start_row, et)], zsem)

        def fill_start(tile, c):
            @pl.when(meta_ref[META_ROWS_USED, tile] < et)
            def _():
                fill_copy(tile).start()
            return c

        def fill_wait(tile, c):
            @pl.when(meta_ref[META_ROWS_USED, tile] < et)
            def _():
                fill_copy(tile).wait()
            return c

        n_tiles = xs_hbm.shape[0] // et
        lax.fori_loop(0, n_tiles, fill_start, 0)
        lax.fori_loop(0, n_tiles, fill_wait, 0)

    def row_copy(r, k):
        return pltpu.make_async_copy(pack_ref.at[pl.ds(r, 1)],
                                     xs_hbm.at[pl.ds(dest_ref[(base + r) * TOP_K + k], 1)], sem)

    def start(r, c):
        for k in range(TOP_K):
            row_copy(r, k).start(priority=k % NUM_DMA_QUEUES)
        return c

    def wait(r, c):
        for k in range(TOP_K):
            row_copy(r, k).wait()
        return c

    lax.fori_loop(0, tm, start, 0, unroll=DMA_UNROLL)
    lax.fori_loop(0, tm, wait, 0, unroll=DMA_UNROLL)


def _dispatch(h, dest, meta, n_rows):
    t, d = h.shape
    tm = min(DISPATCH_TILE, t)
    grid_spec = pltpu.PrefetchScalarGridSpec(
        num_scalar_prefetch=2,
        grid=(t // tm,),
        in_specs=[pl.BlockSpec((tm, d), lambda i, dst, m: (i, 0))],
        out_specs=pl.BlockSpec(memory_space=pl.ANY),
        scratch_shapes=[pltpu.VMEM((tm, d // 2), jnp.uint32),
                        pltpu.VMEM((EXPERT_TILE, d // 2), jnp.uint32),
                        pltpu.SemaphoreType.DMA(()), pltpu.SemaphoreType.DMA(())],
    )
    return pl.pallas_call(
        _dispatch_kernel,
        grid_spec=grid_spec,
        out_shape=jax.ShapeDtypeStruct((n_rows, d // 2), jnp.uint32),
        compiler_params=_params(("arbitrary",)),
        name="moe_dispatch",
    )(dest, meta, h)


def _stream_expert_weights(meta_ref, j, t, nj, slots, make_copies, on_arrival):
    @pl.when((j == 0) & (t == 0))
    def _():
        for c in make_copies(meta_ref[META_EXPERT, 0], 0, 0):
            c.start(priority=WEIGHT_DMA_PRIORITY)

    @pl.when(meta_ref[META_FIRST, t] == 1)
    def _():
        if slots == 1:
            slot = next_slot = 0
        else:
            slot = (j * meta_ref[META_NUM_GROUPS, t] + meta_ref[META_GROUP, t]) & 1
            next_slot = 1 - slot
        for c in make_copies(meta_ref[META_EXPERT, t], j, slot):
            c.wait()
        if slots == 1:
            on_arrival(slot)
        last_group = meta_ref[META_LAST_GROUP, t] == 1

        @pl.when(jnp.logical_not(last_group & (j == nj - 1)))
        def _():
            next_chunk = jnp.where(last_group, j + 1, j)
            for c in make_copies(meta_ref[META_NEXT_EXPERT, t], next_chunk, next_slot):
                c.start(priority=WEIGHT_DMA_PRIORITY)

        if slots == 2:
            on_arrival(slot)


def _for_used_rows(rows_used, o_ref, compute):
    tm = o_ref.shape[0]
    part = tm // EXPERT_TILE_PARTS
    for parts in range(EXPERT_TILE_PARTS + 1):
        used = parts * part

        @pl.when((rows_used > used - part) & (rows_used <= used))
        def _(used=used):
            if used > 0:
                compute(slice(0, used))
            if used < tm:
                o_ref[used:tm, :] = jnp.zeros((tm - used, o_ref.shape[1]), o_ref.dtype)


def _moe_up_kernel(meta_ref, x_ref, w_hbm, b_ref, o_ref, wf32, wbf, sem, *, col_chunk):
    t = pl.program_id(0)
    f = o_ref.shape[1]

    def make_copies(expert, chunk, slot):
        del chunk, slot
        return (pltpu.make_async_copy(w_hbm.at[expert], wf32, sem),)

    def on_arrival(slot):
        del slot
        wbf[...] = wf32[...].astype(BF16)

    _stream_expert_weights(meta_ref, 0, t, 1, 1, make_copies, on_arrival)

    def expert_rows(rows):
        x = _unpack_bf16_pairs(x_ref[rows, :])
        for c0 in range(0, f, col_chunk):
            glu_cols = slice(c0, c0 + col_chunk)
            lin_cols = slice(f + c0, f + c0 + col_chunk)
            glu = jnp.dot(x, wbf[:, glu_cols], preferred_element_type=F32) + b_ref[:, glu_cols]
            lin = jnp.dot(x, wbf[:, lin_cols], preferred_element_type=F32) + b_ref[:, lin_cols]
            glu = jnp.minimum(glu, SWIGLU_LIMIT)
            lin = jnp.clip(lin, -SWIGLU_LIMIT, SWIGLU_LIMIT)
            act = glu * jax.nn.sigmoid(SWIGLU_ALPHA * glu) * (lin + 1.0)
            o_ref[rows, glu_cols] = act.astype(o_ref.dtype)

    _for_used_rows(meta_ref[META_ROWS_USED, t], o_ref, expert_rows)


def _moe_up(xs, w_up, b_up, meta):
    r = xs.shape[0]
    e, d, f2 = w_up.shape
    f = f2 // 2
    tm = EXPERT_TILE
    grid_spec = pltpu.PrefetchScalarGridSpec(
        num_scalar_prefetch=1,
        grid=(r // tm,),
        in_specs=[
            pl.BlockSpec((tm, d // 2), lambda t, m: (t, 0)),
            pl.BlockSpec(memory_space=pl.ANY),
            pl.BlockSpec((None, 1, f2), lambda t, m: (m[META_EXPERT, t], 0, 0)),
        ],
        out_specs=pl.BlockSpec((tm, f), lambda t, m: (t, 0)),
        scratch_shapes=[pltpu.VMEM((d, f2), F32), pltpu.VMEM((d, f2), BF16),
                        pltpu.SemaphoreType.DMA(())],
    )
    return pl.pallas_call(
        functools.partial(_moe_up_kernel, col_chunk=_pick_tile(f, MM_COL_TILE)),
        grid_spec=grid_spec,
        out_shape=jax.ShapeDtypeStruct((r, f), BF16),
        compiler_params=_params(("arbitrary",)),
        name="moe_up",
    )(meta, xs, w_up, b_up.reshape(e, 1, f2))


def _moe_down_kernel(meta_ref, a_ref, w_hbm, b_ref, o_ref, wf32, wbf, sems, *, tn, nj):
    j = pl.program_id(0)
    t = pl.program_id(1)

    def make_copies(expert, chunk, slot):
        col = pl.multiple_of(chunk * tn, tn)
        return (pltpu.make_async_copy(w_hbm.at[expert, :, pl.ds(col, tn)],
                                      wf32.at[slot], sems.at[slot]),)

    def on_arrival(slot):
        wbf[...] = wf32[slot].astype(BF16)

    _stream_expert_weights(meta_ref, j, t, nj, 2, make_copies, on_arrival)

    def expert_rows(rows):
        o_ref[rows, :] = (jnp.dot(a_ref[rows, :], wbf[...], preferred_element_type=F32)
                          + b_ref[...])

    _for_used_rows(meta_ref[META_ROWS_USED, t], o_ref, expert_rows)


def _moe_down(act, w_down, b_down, meta):
    r, f = act.shape
    e, _, d = w_down.shape
    tm = EXPERT_TILE
    tn = _pick_tile(d, EXPERT_DOWN_COL_TILE)
    nj = d // tn
    grid_spec = pltpu.PrefetchScalarGridSpec(
        num_scalar_prefetch=1,
        grid=(nj, r // tm),
        in_specs=[
            pl.BlockSpec((tm, f), lambda j, t, m: (t, 0)),
            pl.BlockSpec(memory_space=pl.ANY),
            pl.BlockSpec((None, 1, tn), lambda j, t, m: (m[META_EXPERT, t], 0, j)),
        ],
        out_specs=pl.BlockSpec((tm, tn), lambda j, t, m: (t, j)),
        scratch_shapes=[pltpu.VMEM((2, f, tn), F32), pltpu.VMEM((f, tn), BF16),
                        pltpu.SemaphoreType.DMA((2,))],
    )
    return pl.pallas_call(
        functools.partial(_moe_down_kernel, tn=tn, nj=nj),
        grid_spec=grid_spec,
        out_shape=jax.ShapeDtypeStruct((r, d), F32),
        compiler_params=_params(("arbitrary", "arbitrary")),
        name="moe_down",
    )(meta, act, w_down, b_down.reshape(e, 1, d))


def _combine_kernel(pos_ref, y_hbm, gate_ref, h_ref, g_ref, b_ref, o_ref, buf_ref, sems):
    tm = o_ref.shape[0]
    i = pl.program_id(0)

    def row_copy(step, r, k, slot):
        src_row = pos_ref[(step * tm + r) * TOP_K + k]
        return pltpu.make_async_copy(y_hbm.at[pl.ds(src_row, 1)],
                                     buf_ref.at[slot, k, pl.ds(r, 1)], sems.at[slot])

    def gather_start(step, slot):
        def body(r, c):
            for k in range(TOP_K):
                row_copy(step, r, k, slot).start(priority=k % NUM_DMA_QUEUES)
            return c
        lax.fori_loop(0, tm, body, 0, unroll=DMA_UNROLL)

    def gather_wait(step, slot):
        def body(r, c):
            for k in range(TOP_K):
                row_copy(step, r, k, slot).wait()
            return c
        lax.fori_loop(0, tm, body, 0, unroll=DMA_UNROLL)

    @pl.when(i == 0)
    def _():
        gather_start(0, 0)

    for next_slot in range(2):
        @pl.when((i + 1 < pl.num_programs(0)) & ((i + 1) & 1 == next_slot))
        def _(next_slot=next_slot):
            gather_start(i + 1, next_slot)

    slot = i & 1
    gather_wait(i, slot)
    gates = gate_ref[...]
    ff = gates[:, 0:1] * buf_ref[slot, 0]
    for k in range(1, TOP_K):
        ff = ff + gates[:, k:k + 1] * buf_ref[slot, k]
    o_ref[...] = _layer_norm(DEEPNORM_ALPHA * h_ref[...] + ff, g_ref[...], b_ref[...])


def _combine(y_rows, pos, gates, h, g, b):
    t, d = h.shape
    tm = min(COMBINE_TILE, t)
    grid_spec = pltpu.PrefetchScalarGridSpec(
        num_scalar_prefetch=1,
        grid=(t // tm,),
        in_specs=[pl.BlockSpec(memory_space=pl.ANY),
                  pl.BlockSpec((tm, V7X_LANES), lambda i, p: (i, 0)),
                  pl.BlockSpec((tm, d), lambda i, p: (i, 0)),
                  pl.BlockSpec((1, d), lambda i, p: (0, 0)),
                  pl.BlockSpec((1, d), lambda i, p: (0, 0))],
        out_specs=pl.BlockSpec((tm, d), lambda i, p: (i, 0)),
        scratch_shapes=[pltpu.VMEM((2, TOP_K, tm, d), F32), pltpu.SemaphoreType.DMA((2,))],
    )
    return pl.pallas_call(
        _combine_kernel,
        grid_spec=grid_spec,
        out_shape=jax.ShapeDtypeStruct((t, d), F32),
        compiler_params=_params(("arbitrary",)),
        name="moe_combine",
    )(pos, y_rows, gates, h, g.reshape(1, d), b.reshape(1, d))


def _expert_layout(idx, rank, counts, n_tiles, tm):
    experts = jnp.arange(N_EXPERTS, dtype=jnp.int32)
    tiles_per = (counts + tm - 1) // tm
    tile_end = jnp.cumsum(tiles_per).astype(jnp.int32)
    tile_off = tile_end - tiles_per
    dest = jnp.sum(jnp.where(idx[..., None] == experts, tile_off * tm, 0), axis=-1) + rank

    nonempty = counts > 0
    first_e = jnp.min(jnp.where(nonempty, experts, N_EXPERTS))
    last_e = jnp.max(jnp.where(nonempty, experts, 0))
    group_of = jnp.cumsum(nonempty.astype(jnp.int32)) - 1
    later = (experts[None, :] > experts[:, None]) & nonempty[None, :]
    next_e = jnp.min(jnp.where(later, experts[None, :], N_EXPERTS), axis=1)
    next_e = jnp.where(next_e == N_EXPERTS, first_e, next_e)

    tid = jnp.arange(n_tiles, dtype=jnp.int32)
    valid = tid < tile_end[-1]
    te = jnp.sum((tid[:, None] >= tile_end[None, :]).astype(jnp.int32), axis=1)
    te = jnp.where(valid, jnp.minimum(te, N_EXPERTS - 1), last_e)
    onehot = te[:, None] == experts[None, :]

    def lookup(table):
        return jnp.sum(jnp.where(onehot, table[None, :], 0), axis=1)

    local = tid - lookup(tile_off)
    rows_used = jnp.where(valid, jnp.clip(lookup(counts) - local * tm, 0, tm), 0)
    meta = jnp.stack([
        te,
        (valid & (local == 0)).astype(jnp.int32),
        (te == last_e).astype(jnp.int32),
        lookup(next_e),
        lookup(group_of),
        rows_used,
        jnp.broadcast_to(jnp.sum(nonempty.astype(jnp.int32)), (n_tiles,)),
    ]).astype(jnp.int32)
    return dest.reshape(-1).astype(jnp.int32), meta


def _moe(h, routing, w_up, b_up, w_down, b_down, g, b):
    t, d = h.shape
    tm = EXPERT_TILE
    idx_p, gate_p, rank_p, cnt_p = routing
    n_rows = t * TOP_K + N_EXPERTS * tm
    dest, meta = _expert_layout(idx_p[:, :TOP_K], rank_p[:, :TOP_K],
                                cnt_p[0, :N_EXPERTS].astype(jnp.int32), n_rows // tm, tm)
    xs = _dispatch(h, dest, meta, n_rows)
    act = _moe_up(xs, w_up, b_up, meta)
    y_rows = _moe_down(act, w_down, b_down, meta)
    return _combine(y_rows, dest, gate_p, h, g, b)


def kernel(x, mem, ln_in_g, ln_in_b, rel_bias, w_in, b_in, attn_sinks, w_a_out, w_b_out,
           w_mix_out, ln1_g, ln1_b, w_xq, w_xkv, w_xo, ln2_g, ln2_b, w_router, b_router,
           w_up, b_up, w_down, b_down, ln3_g, ln3_b):
    batch, seq, d = x.shape
    mem_len = mem.shape[1]
    t = batch * seq
    a_cols = A_WIDTH + 2 * A_KV_WIDTH
    b_cols = 3 * B_WIDTH

    g_col = a_cols + b_cols
    h, hb, qkv_a = _ln_proj(x.reshape(t, d), ln_in_g, ln_in_b, w_in[0], b_in[0, :a_cols],
                            a_cols, BF16)
    for l in range(DEPTH):
        if l > 0:
            qkv_a = _mm_bias(hb, w_in[l], b_in[l, :a_cols], 0, a_cols, BF16, "in_proj_a")
        qkv_b = _mm_bias(hb, w_in[l], b_in[l, a_cols:g_col], a_cols, b_cols, BF16, "in_proj_b")
        gates = _mm_bias(hb, w_in[l], b_in[l, g_col:], g_col, 2 * d, F32, "in_proj_gates")
        ya = _swa(qkv_a, attn_sinks[l], rel_bias, batch, seq)
        yb = _stick_breaking(qkv_b, batch, seq)
        h, hb = _mix(ya, yb, gates, h, w_a_out[l], w_b_out[l], w_mix_out[l], ln1_g[l], ln1_b[l])
        kv = _mm_bias(mem.reshape(batch * mem_len, d).astype(BF16), w_xkv[l],
                      jnp.zeros((2 * d,), F32), 0, 2 * d, BF16, "mem_kv")
        h, *routing = _xattn_route(hb, h, kv, w_xq[l], w_xo[l], ln2_g[l], ln2_b[l],
                                   w_router[l], b_router[l], batch, seq, mem_len)
        h = _moe(h, routing, w_up[l], b_up[l], w_down[l], b_down[l], ln3_g[l], ln3_b[l])
        if l + 1 < DEPTH:
            hb = h.astype(BF16)
    return h.reshape(batch, seq, d)
```

```python
import functools

import jax
import jax.numpy as jnp
import numpy as np
from jax import lax
from jax.experimental import pallas as pl
from jax.experimental.pallas import tpu as pltpu

DEPTH = 1
CHUNK = 64
BLK = 128
HEAD_DIM = 64
A_HEADS = 16
A_KV_HEADS = 2
A_REP = A_HEADS // A_KV_HEADS
WIN_CHUNKS = 2
B_HEADS = 16
A_WIDTH = A_HEADS * HEAD_DIM
A_KV_WIDTH = A_KV_HEADS * HEAD_DIM
B_WIDTH = B_HEADS * HEAD_DIM
REL_BUCKETS = 32
REL_MAX_DIST = 128
X_HEADS = 4
N_EXPERTS = 32
TOP_K = 4
SWIGLU_LIMIT = 7.0
SWIGLU_ALPHA = 1.702
LN_EPS = 1e-5
DEEPNORM_ALPHA = (2.0 * DEPTH) ** 0.25
QK_SCALE = HEAD_DIM ** -0.5
V7X_LANES = 128
V7X_VMEM_LIMIT_BYTES = 56 * 1024 * 1024
WEIGHT_DMA_PRIORITY = 1
ROW_TILE = 512
MM_ROW_TILE = 1024
MM_COL_TILE = 1024
EXPERT_DOWN_COL_TILE = 2048
MEM_KV_COL_TILE = 2048
ATT_TILE = 256
SB_TILE = 256
SB_PAIRS = 4
SB_ROWS = 32
EXPERT_TILE = 256
EXPERT_TILE_PARTS = 4
DISPATCH_TILE = 512
COMBINE_TILE = 256
SB_LOG_ZERO = -105.0

F32 = jnp.float32
BF16 = jnp.bfloat16
NT_DIMS = (((1,), (1,)), ((), ()))


def _params(semantics):
    return pltpu.CompilerParams(dimension_semantics=semantics,
                                vmem_limit_bytes=V7X_VMEM_LIMIT_BYTES)


def _const_spec(shape):
    nd = len(shape)
    return pl.BlockSpec(shape, lambda *_: (0,) * nd, pipeline_mode=pl.Buffered(1))


def _layer_norm(x, g, b):
    mu = jnp.mean(x, axis=-1, keepdims=True)
    xc = x - mu
    var = jnp.mean(xc * xc, axis=-1, keepdims=True)
    return xc * lax.rsqrt(var + LN_EPS) * g + b


def _ln_proj_kernel(x_ref, g_ref, b_ref, w_hbm, bias_ref, h_ref, hb_ref, o_ref, stage, wbf, sem,
                    *, n):
    @pl.when(pl.program_id(0) == 0)
    def _():
        slab = pltpu.make_async_copy(w_hbm.at[:, pl.ds(0, n)], stage, sem)
        slab.start()
        slab.wait()
        wbf[...] = stage[...].astype(BF16)

    y = _layer_norm(x_ref[...], g_ref[...], b_ref[...])
    yb = y.astype(BF16)
    h_ref[...] = y
    hb_ref[...] = yb
    acc = jnp.dot(yb, wbf[...], preferred_element_type=F32)
    o_ref[...] = (acc + bias_ref[...]).astype(o_ref.dtype)


def _ln_proj(x2d, g, b, w, bias, n, out_dtype):
    t, d = x2d.shape
    tm = min(ROW_TILE, t)
    row = pl.BlockSpec((tm, d), lambda i: (i, 0))
    return pl.pallas_call(
        functools.partial(_ln_proj_kernel, n=n),
        grid=(t // tm,),
        in_specs=[row, _const_spec((1, d)), _const_spec((1, d)),
                  pl.BlockSpec(memory_space=pl.ANY), _const_spec((1, n))],
        out_specs=[row, row, pl.BlockSpec((tm, n), lambda i: (i, 0))],
        out_shape=[jax.ShapeDtypeStruct((t, d), F32), jax.ShapeDtypeStruct((t, d), BF16),
                   jax.ShapeDtypeStruct((t, n), out_dtype)],
        scratch_shapes=[pltpu.VMEM((d, n), F32), pltpu.VMEM((d, n), BF16),
                        pltpu.SemaphoreType.DMA(())],
        compiler_params=_params(("arbitrary",)),
        name="ln_in_proj_a",
    )(x2d, g.reshape(1, d), b.reshape(1, d), w, bias.reshape(1, n).astype(F32))


def _mm_bias_kernel(a_ref, w_hbm, b_ref, o_ref, stage, wbf, sem, *, col0, tn, nj):
    j = pl.program_id(0)
    i = pl.program_id(1)

    def slab_copy(jj):
        col = pl.multiple_of(col0 + jj * tn, V7X_LANES)
        return pltpu.make_async_copy(w_hbm.at[:, pl.ds(col, tn)], stage, sem)

    @pl.when(i == 0)
    def _():
        @pl.when(j == 0)
        def _():
            slab_copy(0).start(priority=WEIGHT_DMA_PRIORITY)

        slab_copy(j).wait()
        wbf[...] = stage[...].astype(BF16)

        @pl.when(j + 1 < nj)
        def _():
            slab_copy(j + 1).start(priority=WEIGHT_DMA_PRIORITY)

    acc = jnp.dot(a_ref[...], wbf[...], preferred_element_type=F32)
    o_ref[...] = (acc + b_ref[...]).astype(o_ref.dtype)


def _pick_tile(n, target):
    best = None
    for c in range(V7X_LANES, min(n, target) + 1, V7X_LANES):
        if n % c == 0:
            best = c
    return n if best is None else best


def _mm_bias(a, w, b, col0, n, out_dtype, name, col_tile=MM_COL_TILE):
    m, k = a.shape
    tm = min(MM_ROW_TILE, m)
    tn = _pick_tile(n, col_tile)
    nj = n // tn
    return pl.pallas_call(
        functools.partial(_mm_bias_kernel, col0=col0, tn=tn, nj=nj),
        grid=(nj, m // tm),
        in_specs=[pl.BlockSpec((tm, k), lambda j, i: (i, 0)),
                  pl.BlockSpec(memory_space=pl.ANY),
                  pl.BlockSpec((1, tn), lambda j, i: (0, j))],
        out_specs=pl.BlockSpec((tm, tn), lambda j, i: (i, j)),
        out_shape=jax.ShapeDtypeStruct((m, n), out_dtype),
        scratch_shapes=[pltpu.VMEM((k, tn), F32), pltpu.VMEM((k, tn), BF16),
                        pltpu.SemaphoreType.DMA(())],
        compiler_params=_params(("arbitrary", "arbitrary")),
        name=name,
    )(a, w, b.reshape(1, n).astype(F32))


def _swa_kernel(sink_ref, q_ref, kp_ref, kc_ref, vp_ref, vc_ref, bias_ref, o_ref):
    n = pl.program_id(1)
    kk = jnp.concatenate([kp_ref[...], kc_ref[...]], axis=0)
    vv = jnp.concatenate([vp_ref[...], vc_ref[...]], axis=0)
    col = lax.broadcasted_iota(jnp.int32, (BLK, 2 * BLK), 1)
    exists = (col >= BLK) | (n > 0)
    for h in range(A_HEADS):
        g = h // A_REP
        qh = q_ref[:, h * HEAD_DIM:(h + 1) * HEAD_DIM] * QK_SCALE
        kg = kk[:, g * HEAD_DIM:(g + 1) * HEAD_DIM]
        vg = vv[:, g * HEAD_DIM:(g + 1) * HEAD_DIM]
        s = lax.dot_general(qh, kg, NT_DIMS, preferred_element_type=F32)
        s = jnp.where(exists, s + bias_ref[h], -jnp.inf)
        sink = sink_ref[h]
        m = jnp.maximum(jnp.max(s, axis=-1, keepdims=True), sink)
        p = jnp.exp(s - m)
        denom = jnp.sum(p, axis=-1, keepdims=True) + jnp.exp(sink - m)
        o = jnp.dot(p.astype(BF16), vg, preferred_element_type=F32) / denom
        o_ref[:, h * HEAD_DIM:(h + 1) * HEAD_DIM] = o.astype(o_ref.dtype)


def _rel_bucket(rel):
    half = REL_BUCKETS // 2
    max_exact = half // 2
    base = jnp.where(rel > 0, half, 0)
    n = jnp.abs(rel)
    nf = jnp.maximum(n, 1).astype(F32)
    large = max_exact + (jnp.log(nf / max_exact) / np.log(REL_MAX_DIST / max_exact)
                         * (half - max_exact)).astype(jnp.int32)
    large = jnp.minimum(large, half - 1)
    return base + jnp.where(n < max_exact, n, large)


def _swa_bias_table(rel_bias):
    rel = jnp.arange(-(2 * BLK - 1), BLK)
    line = jnp.transpose(rel_bias[_rel_bucket(rel)]).astype(F32)
    bias = jnp.stack([line[:, BLK - 1 - q:3 * BLK - 1 - q] for q in range(BLK)], axis=1)
    qi = np.arange(BLK)[:, None]
    kj = np.arange(2 * BLK)[None, :]
    dchunk = (kj // CHUNK - BLK // CHUNK) - qi // CHUNK
    band = (dchunk <= 0) & (dchunk >= -WIN_CHUNKS)
    return jnp.where(jnp.asarray(band)[None], bias, -jnp.inf)


def _swa(qkv, sinks, rel_bias, batch, seq):
    nb = seq // BLK
    kcol = A_WIDTH // A_KV_WIDTH
    vcol = kcol + 1
    kv_blk = (BLK, A_KV_WIDTH)
    grid_spec = pltpu.PrefetchScalarGridSpec(
        num_scalar_prefetch=1,
        grid=(batch, nb),
        in_specs=[
            pl.BlockSpec((BLK, A_WIDTH), lambda b, n, s: (b * nb + n, 0)),
            pl.BlockSpec(kv_blk, lambda b, n, s: (b * nb + jnp.maximum(n - 1, 0), kcol)),
            pl.BlockSpec(kv_blk, lambda b, n, s: (b * nb + n, kcol)),
            pl.BlockSpec(kv_blk, lambda b, n, s: (b * nb + jnp.maximum(n - 1, 0), vcol)),
            pl.BlockSpec(kv_blk, lambda b, n, s: (b * nb + n, vcol)),
            pl.BlockSpec((A_HEADS, BLK, 2 * BLK), lambda b, n, s: (0, 0, 0),
                         pipeline_mode=pl.Buffered(1)),
        ],
        out_specs=pl.BlockSpec((BLK, A_WIDTH), lambda b, n, s: (b * nb + n, 0)),
    )
    return pl.pallas_call(
        _swa_kernel,
        grid_spec=grid_spec,
        out_shape=jax.ShapeDtypeStruct((batch * seq, A_WIDTH), BF16),
        compiler_params=_params(("arbitrary", "arbitrary")),
        name="swa_attn",
    )(sinks.astype(F32), qkv, qkv, qkv, qkv, qkv, _swa_bias_table(rel_bias))


def _sb_kernel(q_ref, k_ref, v_ref, tri_ref, o_ref, qh_scr, z_scr, hl_scr, s_scr, w_scr,
               carry_scr, acc_scr):
    t = q_ref.shape[0]
    i = pl.program_id(2)
    n_heads = 2 * SB_PAIRS
    pair_w = 2 * HEAD_DIM
    lane = lax.broadcasted_iota(jnp.int32, (1, pair_w), 1)
    first = lane < HEAD_DIM
    zero = jnp.zeros((), BF16)

    def pair_cols(h):
        return slice((h // 2) * pair_w, (h // 2 + 1) * pair_w)

    def own_half(h, x):
        return jnp.where(first, x, zero) if h % 2 == 0 else jnp.where(first, zero, x)

    q2 = q_ref[...] * QK_SCALE
    for h in range(n_heads):
        qh_scr[h] = own_half(h, q2[:, pair_cols(h)])
    carry_scr[...] = jnp.zeros_like(carry_scr)
    acc_scr[...] = jnp.zeros_like(acc_scr)
    row = lax.broadcasted_iota(jnp.int32, (SB_ROWS, t), 0)
    col = lax.broadcasted_iota(jnp.int32, (SB_ROWS, t), 1)

    def key_tile(j, diagonal):
        start = pl.multiple_of(j * t, t)
        k2 = k_ref[pl.ds(start, t), :]
        v2 = v_ref[pl.ds(start, t), :]
        chunks = [slice(r0, r0 + SB_ROWS) for r0 in range(0, t, SB_ROWS)]
        for h in range(n_heads):
            z_scr[h] = lax.dot_general(qh_scr[h], k2[:, pair_cols(h)], NT_DIMS,
                                       preferred_element_type=F32)
        for rows in chunks:
            for h in range(n_heads):
                z = z_scr[h, rows, :]
                drop = jnp.maximum(z, 0.0) + jnp.log(1.0 + jnp.exp(-jnp.abs(z)))
                if diagonal:
                    drop = jnp.where(col < row + rows.start, drop, 0.0)
                hi = drop.astype(BF16)
                hl_scr[h, rows, 0:t] = hi
                hl_scr[h, rows, t:2 * t] = (drop - hi.astype(F32)).astype(BF16)
        tri = tri_ref[...]
        for h in range(n_heads):
            s_scr[h] = jnp.dot(hl_scr[h], tri, preferred_element_type=F32)
        top = jnp.full((SB_ROWS, V7X_LANES), -jnp.inf, F32)
        for rows in chunks:
            for h in range(n_heads):
                carry = carry_scr[h, rows, :]
                log_w = (z_scr[h, rows, :] + s_scr[h, rows, :]
                         + jnp.concatenate([carry] * (t // V7X_LANES), axis=1))
                w = jnp.exp(log_w)
                if diagonal:
                    w = jnp.where(col < row + rows.start, w, 0.0)
                w_scr[h // 2, rows, (h % 2) * t:(h % 2 + 1) * t] = w.astype(BF16)
                carry = carry + jnp.broadcast_to(s_scr[h, rows, 0:1], (SB_ROWS, V7X_LANES))
                carry_scr[h, rows, :] = carry
                top = jnp.maximum(top, carry)
        for p in range(SB_PAIRS):
            cols = pair_cols(2 * p)
            v_pair = jnp.concatenate([own_half(0, v2[:, cols]), own_half(1, v2[:, cols])], axis=0)
            acc_scr[:, cols] += jnp.dot(w_scr[p], v_pair, preferred_element_type=F32)
        return (jnp.max(top) > SB_LOG_ZERO).astype(jnp.int32)

    live = key_tile(i, True)

    def cond(state):
        return (state[0] >= 0) & (state[1] > 0)

    def body(state):
        return state[0] - 1, key_tile(state[0], False)

    lax.while_loop(cond, body, (i - 1, live))
    o_ref[...] = acc_scr[...].astype(o_ref.dtype)


def _stick_breaking(qkv, batch, seq):
    t = min(SB_TILE, seq)
    nq = seq // t
    width = SB_PAIRS * 2 * HEAD_DIM
    groups = B_WIDTH // width
    heads = 2 * SB_PAIRS
    j = np.arange(2 * t)[:, None] % t
    s = np.arange(t)[None, :]
    tri = jnp.asarray(-(j >= s).astype(np.float32), dtype=BF16)
    return pl.pallas_call(
        _sb_kernel,
        grid=(batch, groups, nq),
        in_specs=[
            pl.BlockSpec((t, width), lambda b, p, i: (b * nq + i, p)),
            pl.BlockSpec((seq, width), lambda b, p, i: (b, groups + p)),
            pl.BlockSpec((seq, width), lambda b, p, i: (b, 2 * groups + p)),
            _const_spec((2 * t, t)),
        ],
        out_specs=pl.BlockSpec((t, width), lambda b, p, i: (b * nq + i, p)),
        out_shape=jax.ShapeDtypeStruct((batch * seq, B_WIDTH), BF16),
        scratch_shapes=[
            pltpu.VMEM((heads, t, 2 * HEAD_DIM), BF16),
            pltpu.VMEM((heads, t, t), F32),
            pltpu.VMEM((heads, t, 2 * t), BF16),
            pltpu.VMEM((heads, t, t), F32),
            pltpu.VMEM((SB_PAIRS, t, 2 * t), BF16),
            pltpu.VMEM((heads, t, V7X_LANES), F32),
            pltpu.VMEM((t, width), F32),
        ],
        compiler_params=_params(("arbitrary", "arbitrary", "arbitrary")),
        name="stick_breaking",
    )(qkv, qkv, qkv, tri)


def _load_weight_bf16(w_hbm, w_bf, stage, sems):
    rows = stage.shape[1]
    n = w_hbm.shape[0] // rows

    def chunk_copy(c):
        return pltpu.make_async_copy(w_hbm.at[pl.ds(c * rows, rows)], stage.at[c % 2],
                                     sems.at[c % 2])

    chunk_copy(0).start()
    for c in range(n):
        if c + 1 < n:
            chunk_copy(c + 1).start()
        chunk_copy(c).wait()
        w_bf[pl.ds(c * rows, rows), :] = stage[c % 2].astype(BF16)


WEIGHT_STAGE_ROWS = 512


def _mix_kernel(ya_ref, yb_ref, gates_a_ref, gates_b_ref, h_ref, wa_hbm, wb_hbm, wm_hbm,
                g_ref, b_ref, o_ref, ob_ref, wa_ref, wb_ref, wm_ref, stage, sems):
    @pl.when(pl.program_id(0) == 0)
    def _():
        _load_weight_bf16(wa_hbm, wa_ref, stage, sems)
        _load_weight_bf16(wb_hbm, wb_ref, stage, sems)
        _load_weight_bf16(wm_hbm, wm_ref, stage, sems)

    pa = jnp.dot(ya_ref[...], wa_ref[...], preferred_element_type=F32)
    pb = jnp.dot(yb_ref[...], wb_ref[...], preferred_element_type=F32)
    merged = jax.nn.sigmoid(gates_a_ref[...]) * pa + jax.nn.sigmoid(gates_b_ref[...]) * pb
    mixed = jnp.dot(merged.astype(BF16), wm_ref[...], preferred_element_type=F32)
    y = _layer_norm(DEEPNORM_ALPHA * h_ref[...] + mixed, g_ref[...], b_ref[...])
    o_ref[...] = y
    ob_ref[...] = y.astype(BF16)


def _mix(ya, yb, gates, h, wa, wb, wm, g, b):
    t, d = h.shape
    tm = min(ATT_TILE, t)
    row = pl.BlockSpec((tm, d), lambda i: (i, 0))
    return pl.pallas_call(
        _mix_kernel,
        grid=(t // tm,),
        in_specs=[pl.BlockSpec((tm, A_WIDTH), lambda i: (i, 0)),
                  pl.BlockSpec((tm, B_WIDTH), lambda i: (i, 0)),
                  pl.BlockSpec((tm, d), lambda i: (i, 0)),
                  pl.BlockSpec((tm, d), lambda i: (i, 1)),
                  row,
                  pl.BlockSpec(memory_space=pl.ANY), pl.BlockSpec(memory_space=pl.ANY),
                  pl.BlockSpec(memory_space=pl.ANY),
                  _const_spec((1, d)), _const_spec((1, d))],
        out_specs=[row, row],
        out_shape=[jax.ShapeDtypeStruct((t, d), F32), jax.ShapeDtypeStruct((t, d), BF16)],
        scratch_shapes=[pltpu.VMEM((A_WIDTH, d), BF16), pltpu.VMEM((B_WIDTH, d), BF16),
                        pltpu.VMEM((d, d), BF16),
                        pltpu.VMEM((2, min(WEIGHT_STAGE_ROWS, d), d), F32),
                        pltpu.SemaphoreType.DMA((2,))],
        compiler_params=_params(("arbitrary",)),
        name="mix_out",
    )(ya, yb, gates, gates, h, wa, wb, wm, g.reshape(1, d), b.reshape(1, d))


def _xattn_kernel(hb_ref, h_ref, k_ref, v_ref, wq_hbm, wo_hbm, g_ref, b_ref, wr_ref,
                  rb_ref, ltri_ref, o_ref, idx_ref, gate_ref, rank_ref, cnt_ref, run_ref,
                  wq_ref, wo_ref, stage, sems):
    first_step = (pl.program_id(0) == 0) & (pl.program_id(1) == 0)

    @pl.when(first_step)
    def _():
        _load_weight_bf16(wq_hbm, wq_ref, stage, sems)
        _load_weight_bf16(wo_hbm, wo_ref, stage, sems)

    d = h_ref.shape[1]
    hd = d // X_HEADS
    scale = hd ** -0.5
    q = jnp.dot(hb_ref[...], wq_ref[...], preferred_element_type=F32).astype(BF16)
    outs = []
    for x in range(X_HEADS):
        sl = slice(x * hd, (x + 1) * hd)
        s = lax.dot_general(q[:, sl], k_ref[:, sl], NT_DIMS, preferred_element_type=F32) * scale
        m = jnp.max(s, axis=-1, keepdims=True)
        p = jnp.exp(s - m)
        denom = jnp.sum(p, axis=-1, keepdims=True)
        o = jnp.dot(p.astype(BF16), v_ref[:, sl], preferred_element_type=F32) / denom
        outs.append(o.astype(BF16))
    xo = jnp.dot(jnp.concatenate(outs, axis=1), wo_ref[...], preferred_element_type=F32)
    h_out = _layer_norm(DEEPNORM_ALPHA * h_ref[...] + xo, g_ref[...], b_ref[...])
    o_ref[...] = h_out
    _route_tile(h_out, first_step, wr_ref, rb_ref, ltri_ref, idx_ref, gate_ref,
                rank_ref, cnt_ref, run_ref)


def _xattn_route(hb, h, kv, wq, wo, g, b, w_router, b_router, batch, seq, mem_len):
    t, d = h.shape
    tm = min(ATT_TILE, seq)
    ns = seq // tm
    w_pad = jnp.zeros((d, V7X_LANES), F32).at[:, :N_EXPERTS].set(w_router)
    w_hi = w_pad.astype(BF16)
    w_lo = (w_pad - w_hi.astype(F32)).astype(BF16)
    w_split = jnp.concatenate([w_hi, w_lo], axis=1)
    b_pad = jnp.zeros((1, V7X_LANES), F32).at[0, :N_EXPERTS].set(b_router)
    ltri = jnp.asarray(np.tril(np.ones((tm, tm), np.float32), -1), dtype=BF16)
    row = pl.BlockSpec((tm, d), lambda bi, i: (bi * ns + i, 0))
    small = pl.BlockSpec((tm, V7X_LANES), lambda bi, i: (bi * ns + i, 0))
    return pl.pallas_call(
        _xattn_kernel,
        grid=(batch, ns),
        in_specs=[row, row,
                  pl.BlockSpec((mem_len, d), lambda bi, i: (bi, 0)),
                  pl.BlockSpec((mem_len, d), lambda bi, i: (bi, 1)),
                  pl.BlockSpec(memory_space=pl.ANY), pl.BlockSpec(memory_space=pl.ANY),
                  _const_spec((1, d)), _const_spec((1, d)),
                  _const_spec((d, 2 * V7X_LANES)),
                  _const_spec((1, V7X_LANES)), _const_spec((tm, tm))],
        out_specs=[row, small, small, small,
                   pl.BlockSpec((1, V7X_LANES), lambda bi, i: (0, 0))],
        out_shape=[jax.ShapeDtypeStruct((t, d), F32),
                   jax.ShapeDtypeStruct((t, V7X_LANES), jnp.int32),
                   jax.ShapeDtypeStruct((t, V7X_LANES), F32),
                   jax.ShapeDtypeStruct((t, V7X_LANES), jnp.int32),
                   jax.ShapeDtypeStruct((1, V7X_LANES), F32)],
        scratch_shapes=[pltpu.VMEM((1, V7X_LANES), F32),
                        pltpu.VMEM((d, d), BF16), pltpu.VMEM((d, d), BF16),
                        pltpu.VMEM((2, min(WEIGHT_STAGE_ROWS, d), d), F32),
                        pltpu.SemaphoreType.DMA((2,))],
        compiler_params=_params(("arbitrary", "arbitrary")),
        name="xattn_route",
    )(hb, h, kv, kv, wq, wo, g.reshape(1, d), b.reshape(1, d), w_split, b_pad, ltri)


def _route_tile(h, first_step, w_ref, b_ref, ltri_ref, idx_ref, gate_ref, rank_ref,
                cnt_ref, run_ref):
    @pl.when(first_step)
    def _():
        run_ref[...] = jnp.zeros_like(run_ref)

    h_hi = h.astype(BF16)
    h_lo = (h - h_hi.astype(F32)).astype(BF16)
    hi_terms = jnp.dot(h_hi, w_ref[...], preferred_element_type=F32)
    lo_hi = jnp.dot(h_lo, w_ref[:, 0:V7X_LANES], preferred_element_type=F32)
    logits = (hi_terms[:, 0:V7X_LANES] + (lo_hi + hi_terms[:, V7X_LANES:2 * V7X_LANES])
              + b_ref[...])
    tm = logits.shape[0]
    lane_i = lax.broadcasted_iota(jnp.int32, (tm, V7X_LANES), 1)
    lane = lane_i.astype(F32)
    rem = jnp.where(lane_i < N_EXPERTS, logits, -jnp.inf)
    vals, idxs = [], []
    for _ in range(TOP_K):
        m = jnp.max(rem, axis=-1, keepdims=True)
        ik = jnp.min(jnp.where(rem == m, lane, float(V7X_LANES)), axis=-1, keepdims=True)
        vals.append(m)
        idxs.append(ik)
        rem = jnp.where(lane == ik, -jnp.inf, rem)
    exps = [jnp.exp(v - vals[0]) for v in vals]
    denom = exps[0] + exps[1] + exps[2] + exps[3]
    sel = jnp.zeros((tm, V7X_LANES), F32)
    for ik in idxs:
        sel = sel + (lane == ik).astype(F32)
    before = jnp.dot(ltri_ref[...], sel.astype(BF16), preferred_element_type=F32) + run_ref[...]
    idx_out = jnp.zeros((tm, V7X_LANES), jnp.int32)
    gate_out = jnp.zeros((tm, V7X_LANES), F32)
    rank_out = jnp.zeros((tm, V7X_LANES), jnp.int32)
    for k in range(TOP_K):
        rank_k = jnp.sum(jnp.where(lane == idxs[k], before, 0.0), axis=-1, keepdims=True)
        idx_out = jnp.where(lane_i == k, idxs[k].astype(jnp.int32), idx_out)
        gate_out = jnp.where(lane_i == k, exps[k] / denom, gate_out)
        rank_out = jnp.where(lane_i == k, rank_k.astype(jnp.int32), rank_out)
    idx_ref[...] = idx_out
    gate_ref[...] = gate_out
    rank_ref[...] = rank_out
    run_ref[...] = run_ref[...] + jnp.sum(sel, axis=0, keepdims=True)
    cnt_ref[...] = run_ref[...]


DMA_UNROLL = 8
NUM_DMA_QUEUES = 2
(META_EXPERT, META_FIRST, META_LAST_GROUP, META_NEXT_EXPERT, META_GROUP,
 META_ROWS_USED, META_NUM_GROUPS) = range(7)


def _pack_bf16_pairs(x):
    half = x.shape[1] // 2
    lo = lax.bitcast_convert_type(x[:, :half].astype(BF16).astype(F32), jnp.uint32)
    hi = lax.bitcast_convert_type(x[:, half:].astype(BF16).astype(F32), jnp.uint32)
    return (lo >> 16) | (hi & jnp.uint32(0xFFFF0000))


def _unpack_bf16_pairs(words):
    lo = lax.bitcast_convert_type(words << 16, F32).astype(BF16)
    hi = lax.bitcast_convert_type(words & jnp.uint32(0xFFFF0000), F32).astype(BF16)
    return jnp.concatenate([lo, hi], axis=1)


def _dispatch_kernel(dest_ref, meta_ref, h_ref, xs_hbm, pack_ref, zero_ref, sem, zsem):
    tm = h_ref.shape[0]
    base = pl.program_id(0) * tm
    et = zero_ref.shape[0]
    pack_ref[...] = _pack_bf16_pairs(h_ref[...])

    @pl.when(pl.program_id(0) == 0)
    def _():
        zero_ref[...] = jnp.zeros_like(zero_ref)

        def fill_copy(tile):
            start_row = pl.multiple_of(tile * et, et)
            return pltpu.make_async_copy(zero_ref, xs_hbm.at[pl.ds(start_row, et)], zsem)

        def fill_start(tile, c):
            @pl.when(meta_ref[META_ROWS_USED, tile] < et)
            def _():
                fill_copy(tile).start()
            return c

        def fill_wait(tile, c):
            @pl.when(meta_ref[META_ROWS_USED, tile] < et)
            def _():
                fill_copy(tile).wait()
            return c

        n_tiles = xs_hbm.shape[0] // et
        lax.fori_loop(0, n_tiles, fill_start, 0)
        lax.fori_loop(0, n_tiles, fill_wait, 0)

    def row_copy(r, k):
        return pltpu.make_async_copy(pack_ref.at[pl.ds(r, 1)],
                                     xs_hbm.at[pl.ds(dest_ref[(base + r) * TOP_K + k], 1)], sem)

    def start(r, c):
        for k in range(TOP_K):
            row_copy(r, k).start(priority=k % NUM_DMA_QUEUES)
        return c

    def wait(r, c):
        for k in range(TOP_K):
            row_copy(r, k).wait()
        return c

    lax.fori_loop(0, tm, start, 0, unroll=DMA_UNROLL)
    lax.fori_loop(0, tm, wait, 0, unroll=DMA_UNROLL)


def _dispatch(h, dest, meta, n_rows):
    t, d = h.shape
    tm = min(DISPATCH_TILE, t)
    grid_spec = pltpu.PrefetchScalarGridSpec(
        num_scalar_prefetch=2,
        grid=(t // tm,),
        in_specs=[pl.BlockSpec((tm, d), lambda i, dst, m: (i, 0))],
        out_specs=pl.BlockSpec(memory_space=pl.ANY),
        scratch_shapes=[pltpu.VMEM((tm, d // 2), jnp.uint32),
                        pltpu.VMEM((EXPERT_TILE, d // 2), jnp.uint32),
                        pltpu.SemaphoreType.DMA(()), pltpu.SemaphoreType.DMA(())],
    )
    return pl.pallas_call(
        _dispatch_kernel,
        grid_spec=grid_spec,
        out_shape=jax.ShapeDtypeStruct((n_rows, d // 2), jnp.uint32),
        compiler_params=_params(("arbitrary",)),
        name="moe_dispatch",
    )(dest, meta, h)


def _stream_expert_weights(meta_ref, j, t, nj, slots, make_copies, on_arrival):
    @pl.when((j == 0) & (t == 0))
    def _():
        for c in make_copies(meta_ref[META_EXPERT, 0], 0, 0):
            c.start(priority=WEIGHT_DMA_PRIORITY)

    @pl.when(meta_ref[META_FIRST, t] == 1)
    def _():
        if slots == 1:
            slot = next_slot = 0
        else:
            slot = (j * meta_ref[META_NUM_GROUPS, t] + meta_ref[META_GROUP, t]) & 1
            next_slot = 1 - slot
        for c in make_copies(meta_ref[META_EXPERT, t], j, slot):
            c.wait()
        if slots == 1:
            on_arrival(slot)
        last_group = meta_ref[META_LAST_GROUP, t] == 1

        @pl.when(jnp.logical_not(last_group & (j == nj - 1)))
        def _():
            next_chunk = jnp.where(last_group, j + 1, j)
            for c in make_copies(meta_ref[META_NEXT_EXPERT, t], next_chunk, next_slot):
                c.start(priority=WEIGHT_DMA_PRIORITY)

        if slots == 2:
            on_arrival(slot)


def _for_used_rows(rows_used, o_ref, compute):
    tm = o_ref.shape[0]
    part = tm // EXPERT_TILE_PARTS
    for parts in range(EXPERT_TILE_PARTS + 1):
        used = parts * part

        @pl.when((rows_used > used - part) & (rows_used <= used))
        def _(used=used):
            if used > 0:
                compute(slice(0, used))
            if used < tm:
                o_ref[used:tm, :] = jnp.zeros((tm - used, o_ref.shape[1]), o_ref.dtype)


def _moe_up_kernel(meta_ref, x_ref, w_hbm, b_ref, o_ref, wf32, wbf, sem, *, col_chunk):
    t = pl.program_id(0)
    f = o_ref.shape[1]

    def make_copies(expert, chunk, slot):
        del chunk, slot
        return (pltpu.make_async_copy(w_hbm.at[expert], wf32, sem),)

    def on_arrival(slot):
        del slot
        wbf[...] = wf32[...].astype(BF16)

    _stream_expert_weights(meta_ref, 0, t, 1, 1, make_copies, on_arrival)

    def expert_rows(rows):
        x = _unpack_bf16_pairs(x_ref[rows, :])
        for c0 in range(0, f, col_chunk):
            glu_cols = slice(c0, c0 + col_chunk)
            lin_cols = slice(f + c0, f + c0 + col_chunk)
            glu = jnp.dot(x, wbf[:, glu_cols], preferred_element_type=F32) + b_ref[:, glu_cols]
            lin = jnp.dot(x, wbf[:, lin_cols], preferred_element_type=F32) + b_ref[:, lin_cols]
            glu = jnp.minimum(glu, SWIGLU_LIMIT)
            lin = jnp.clip(lin, -SWIGLU_LIMIT, SWIGLU_LIMIT)
            act = glu * jax.nn.sigmoid(SWIGLU_ALPHA * glu) * (lin + 1.0)
            o_ref[rows, glu_cols] = act.astype(o_ref.dtype)

    _for_used_rows(meta_ref[META_ROWS_USED, t], o_ref, expert_rows)


def _moe_up(xs, w_up, b_up, meta):
    r = xs.shape[0]
    e, d, f2 = w_up.shape
    f = f2 // 2
    tm = EXPERT_TILE
    grid_spec = pltpu.PrefetchScalarGridSpec(
        num_scalar_prefetch=1,
        grid=(r // tm,),
        in_specs=[
            pl.BlockSpec((tm, d // 2), lambda t, m: (t, 0)),
            pl.BlockSpec(memory_space=pl.ANY),
            pl.BlockSpec((None, 1, f2), lambda t, m: (m[META_EXPERT, t], 0, 0)),
        ],
        out_specs=pl.BlockSpec((tm, f), lambda t, m: (t, 0)),
        scratch_shapes=[pltpu.VMEM((d, f2), F32), pltpu.VMEM((d, f2), BF16),
                        pltpu.SemaphoreType.DMA(())],
    )
    return pl.pallas_call(
        functools.partial(_moe_up_kernel, col_chunk=_pick_tile(f, MM_COL_TILE)),
        grid_spec=grid_spec,
        out_shape=jax.ShapeDtypeStruct((r, f), BF16),
        compiler_params=_params(("arbitrary",)),
        name="moe_up",
    )(meta, xs, w_up, b_up.reshape(e, 1, f2))


def _moe_down_kernel(meta_ref, a_ref, w_hbm, b_ref, o_ref, wf32, wbf, sems, *, tn, nj):
    j = pl.program_id(0)
    t = pl.program_id(1)

    def make_copies(expert, chunk, slot):
        col = pl.multiple_of(chunk * tn, tn)
        return (pltpu.make_async_copy(w_hbm.at[expert, :, pl.ds(col, tn)],
                                      wf32.at[slot], sems.at[slot]),)

    def on_arrival(slot):
        wbf[...] = wf32[slot].astype(BF16)

    _stream_expert_weights(meta_ref, j, t, nj, 2, make_copies, on_arrival)

    def expert_rows(rows):
        o_ref[rows, :] = (jnp.dot(a_ref[rows, :], wbf[...], preferred_element_type=F32)
                          + b_ref[...])

    _for_used_rows(meta_ref[META_ROWS_USED, t], o_ref, expert_rows)


def _moe_down(act, w_down, b_down, meta):
    r, f = act.shape
    e, _, d = w_down.shape
    tm = EXPERT_TILE
    tn = _pick_tile(d, EXPERT_DOWN_COL_TILE)
    nj = d // tn
    grid_spec = pltpu.PrefetchScalarGridSpec(
        num_scalar_prefetch=1,
        grid=(nj, r // tm),
        in_specs=[
            pl.BlockSpec((tm, f), lambda j, t, m: (t, 0)),
            pl.BlockSpec(memory_space=pl.ANY),
            pl.BlockSpec((None, 1, tn), lambda j, t, m: (m[META_EXPERT, t], 0, j)),
        ],
        out_specs=pl.BlockSpec((tm, tn), lambda j, t, m: (t, j)),
        scratch_shapes=[pltpu.VMEM((2, f, tn), F32), pltpu.VMEM((f, tn), BF16),
                        pltpu.SemaphoreType.DMA((2,))],
    )
    return pl.pallas_call(
        functools.partial(_moe_down_kernel, tn=tn, nj=nj),
        grid_spec=grid_spec,
        out_shape=jax.ShapeDtypeStruct((r, d), F32),
        compiler_params=_params(("arbitrary", "arbitrary")),
        name="moe_down",
    )(meta, act, w_down, b_down.reshape(e, 1, d))


def _combine_kernel(pos_ref, y_hbm, gate_ref, h_ref, g_ref, b_ref, o_ref, buf_ref, sems):
    tm = o_ref.shape[0]
    i = pl.program_id(0)

    def row_copy(step, r, k, slot):
        src_row = pos_ref[(step * tm + r) * TOP_K + k]
        return pltpu.make_async_copy(y_hbm.at[pl.ds(src_row, 1)],
                                     buf_ref.at[slot, k, pl.ds(r, 1)], sems.at[slot])

    def gather_start(step, slot):
        def body(r, c):
            for k in range(TOP_K):
                row_copy(step, r, k, slot).start(priority=k % NUM_DMA_QUEUES)
            return c
        lax.fori_loop(0, tm, body, 0, unroll=DMA_UNROLL)

    def gather_wait(step, slot):
        def body(r, c):
            for k in range(TOP_K):
                row_copy(step, r, k, slot).wait()
            return c
        lax.fori_loop(0, tm, body, 0, unroll=DMA_UNROLL)

    @pl.when(i == 0)
    def _():
        gather_start(0, 0)

    for next_slot in range(2):
        @pl.when((i + 1 < pl.num_programs(0)) & ((i + 1) & 1 == next_slot))
        def _(next_slot=next_slot):
            gather_start(i + 1, next_slot)

    slot = i & 1
    gather_wait(i, slot)
    gates = gate_ref[...]
    ff = gates[:, 0:1] * buf_ref[slot, 0]
    for k in range(1, TOP_K):
        ff = ff + gates[:, k:k + 1] * buf_ref[slot, k]
    o_ref[...] = _layer_norm(DEEPNORM_ALPHA * h_ref[...] + ff, g_ref[...], b_ref[...])


def _combine(y_rows, pos, gates, h, g, b):
    t, d = h.shape
    tm = min(COMBINE_TILE, t)
    grid_spec = pltpu.PrefetchScalarGridSpec(
        num_scalar_prefetch=1,
        grid=(t // tm,),
        in_specs=[pl.BlockSpec(memory_space=pl.ANY),
                  pl.BlockSpec((tm, V7X_LANES), lambda i, p: (i, 0)),
                  pl.BlockSpec((tm, d), lambda i, p: (i, 0)),
                  pl.BlockSpec((1, d), lambda i, p: (0, 0)),
                  pl.BlockSpec((1, d), lambda i, p: (0, 0))],
        out_specs=pl.BlockSpec((tm, d), lambda i, p: (i, 0)),
        scratch_shapes=[pltpu.VMEM((2, TOP_K, tm, d), F32), pltpu.SemaphoreType.DMA((2,))],
    )
    return pl.pallas_call(
        _combine_kernel,
        grid_spec=grid_spec,
        out_shape=jax.ShapeDtypeStruct((t, d), F32),
        compiler_params=_params(("arbitrary",)),
        name="moe_combine",
    )(pos, y_rows, gates, h, g.reshape(1, d), b.reshape(1, d))


def _expert_layout(idx, rank, counts, n_tiles, tm):
    experts = jnp.arange(N_EXPERTS, dtype=jnp.int32)
    tiles_per = (counts + tm - 1) // tm
    tile_end = jnp.cumsum(tiles_per).astype(jnp.int32)
    tile_off = tile_end - tiles_per
    dest = jnp.sum(jnp.where(idx[..., None] == experts, tile_off * tm, 0), axis=-1) + rank

    nonempty = counts > 0
    first_e = jnp.min(jnp.where(nonempty, experts, N_EXPERTS))
    last_e = jnp.max(jnp.where(nonempty, experts, 0))
    group_of = jnp.cumsum(nonempty.astype(jnp.int32)) - 1
    later = (experts[None, :] > experts[:, None]) & nonempty[None, :]
    next_e = jnp.min(jnp.where(later, experts[None, :], N_EXPERTS), axis=1)
    next_e = jnp.where(next_e == N_EXPERTS, first_e, next_e)

    tid = jnp.arange(n_tiles, dtype=jnp.int32)
    valid = tid < tile_end[-1]
    te = jnp.sum((tid[:, None] >= tile_end[None, :]).astype(jnp.int32), axis=1)
    te = jnp.where(valid, jnp.minimum(te, N_EXPERTS - 1), last_e)
    onehot = te[:, None] == experts[None, :]

    def lookup(table):
        return jnp.sum(jnp.where(onehot, table[None, :], 0), axis=1)

    local = tid - lookup(tile_off)
    rows_used = jnp.where(valid, jnp.clip(lookup(counts) - local * tm, 0, tm), 0)
    meta = jnp.stack([
        te,
        (valid & (local == 0)).astype(jnp.int32),
        (te == last_e).astype(jnp.int32),
        lookup(next_e),
        lookup(group_of),
        rows_used,
        jnp.broadcast_to(jnp.sum(nonempty.astype(jnp.int32)), (n_tiles,)),
    ]).astype(jnp.int32)
    return dest.reshape(-1).astype(jnp.int32), meta


def _moe(h, routing, w_up, b_up, w_down, b_down, g, b):
    t, d = h.shape
    tm = EXPERT_TILE
    idx_p, gate_p, rank_p, cnt_p = routing
    n_rows = t * TOP_K + N_EXPERTS * tm
    dest, meta = _expert_layout(idx_p[:, :TOP_K], rank_p[:, :TOP_K],
                                cnt_p[0, :N_EXPERTS].astype(jnp.int32), n_rows // tm, tm)
    xs = _dispatch(h, dest, meta, n_rows)
    act = _moe_up(xs, w_up, b_up, meta)
    y_rows = _moe_down(act, w_down, b_down, meta)
    return _combine(y_rows, dest, gate_p, h, g, b)


def kernel(x, mem, ln_in_g, ln_in_b, rel_bias, w_in, b_in, attn_sinks, w_a_out, w_b_out,
           w_mix_out, ln1_g, ln1_b, w_xq, w_xkv, w_xo, ln2_g, ln2_b, w_router, b_router,
           w_up, b_up, w_down, b_down, ln3_g, ln3_b):
    batch, seq, d = x.shape
    mem_len = mem.shape[1]
    t = batch * seq
    a_cols = A_WIDTH + 2 * A_KV_WIDTH
    b_cols = 3 * B_WIDTH

    g_col = a_cols + b_cols
    h, hb, qkv_a = _ln_proj(x.reshape(t, d), ln_in_g, ln_in_b, w_in[0], b_in[0, :a_cols],
                            a_cols, BF16)
    for l in range(DEPTH):
        if l > 0:
            qkv_a = _mm_bias(hb, w_in[l], b_in[l, :a_cols], 0, a_cols, BF16, "in_proj_a")
        qkv_b = _mm_bias(hb, w_in[l], b_in[l, a_cols:g_col], a_cols, b_cols, BF16, "in_proj_b")
        gates = _mm_bias(hb, w_in[l], b_in[l, g_col:], g_col, 2 * d, F32, "in_proj_gates")
        ya = _swa(qkv_a, attn_sinks[l], rel_bias, batch, seq)
        yb = _stick_breaking(qkv_b, batch, seq)
        h, hb = _mix(ya, yb, gates, h, w_a_out[l], w_b_out[l], w_mix_out[l], ln1_g[l], ln1_b[l])
        kv = _mm_bias(mem.reshape(batch * mem_len, d).astype(BF16), w_xkv[l],
                      jnp.zeros((2 * d,), F32), 0, 2 * d, BF16, "mem_kv",
                      col_tile=MEM_KV_COL_TILE)
        h, *routing = _xattn_route(hb, h, kv, w_xq[l], w_xo[l], ln2_g[l], ln2_b[l],
                                   w_router[l], b_router[l], batch, seq, mem_len)
        h = _moe(h, routing, w_up[l], b_up[l], w_down[l], b_down[l], ln3_g[l], ln3_b[l])
        if l + 1 < DEPTH:
            hb = h.astype(BF16)
    return h.reshape(batch, seq, d)
```

```python
import functools

import jax
import jax.numpy as jnp
import numpy as np
from jax import lax
from jax.experimental import pallas as pl
from jax.experimental.pallas import tpu as pltpu

DEPTH = 1
CHUNK = 64
BLK = 128
HEAD_DIM = 64
A_HEADS = 16
A_KV_HEADS = 2
A_REP = A_HEADS // A_KV_HEADS
WIN_CHUNKS = 2
B_HEADS = 16
A_WIDTH = A_HEADS * HEAD_DIM
A_KV_WIDTH = A_KV_HEADS * HEAD_DIM
B_WIDTH = B_HEADS * HEAD_DIM
REL_BUCKETS = 32
REL_MAX_DIST = 128
X_HEADS = 4
N_EXPERTS = 32
TOP_K = 4
SWIGLU_LIMIT = 7.0
SWIGLU_ALPHA = 1.702
LN_EPS = 1e-5
DEEPNORM_ALPHA = (2.0 * DEPTH) ** 0.25
QK_SCALE = HEAD_DIM ** -0.5
V7X_LANES = 128
V7X_VMEM_LIMIT_BYTES = 56 * 1024 * 1024
WEIGHT_DMA_PRIORITY = 1
ROW_TILE = 512
MM_ROW_TILE = 1024
MM_COL_TILE = 1024
EXPERT_DOWN_COL_TILE = 2048
MEM_KV_COL_TILE = 2048
ATT_TILE = 256
SB_TILE = 256
SB_PAIRS = 4
SB_ROWS = 32
EXPERT_TILE = 256
EXPERT_TILE_PARTS = 4
DISPATCH_TILE = 1024
COMBINE_TILE = 512
SB_LOG_ZERO = -105.0

F32 = jnp.float32
BF16 = jnp.bfloat16
NT_DIMS = (((1,), (1,)), ((), ()))


def _params(semantics):
    return pltpu.CompilerParams(dimension_semantics=semantics,
                                vmem_limit_bytes=V7X_VMEM_LIMIT_BYTES)


def _const_spec(shape):
    nd = len(shape)
    return pl.BlockSpec(shape, lambda *_: (0,) * nd, pipeline_mode=pl.Buffered(1))


def _layer_norm(x, g, b):
    mu = jnp.mean(x, axis=-1, keepdims=True)
    xc = x - mu
    var = jnp.mean(xc * xc, axis=-1, keepdims=True)
    return xc * lax.rsqrt(var + LN_EPS) * g + b


def _ln_proj_kernel(x_ref, g_ref, b_ref, w_hbm, bias_ref, h_ref, hb_ref, o_ref, stage, wbf, sem,
                    *, n):
    @pl.when(pl.program_id(0) == 0)
    def _():
        slab = pltpu.make_async_copy(w_hbm.at[:, pl.ds(0, n)], stage, sem)
        slab.start()
        slab.wait()
        wbf[...] = stage[...].astype(BF16)

    y = _layer_norm(x_ref[...], g_ref[...], b_ref[...])
    yb = y.astype(BF16)
    h_ref[...] = y
    hb_ref[...] = yb
    acc = jnp.dot(yb, wbf[...], preferred_element_type=F32)
    o_ref[...] = (acc + bias_ref[...]).astype(o_ref.dtype)


def _ln_proj(x2d, g, b, w, bias, n, out_dtype):
    t, d = x2d.shape
    tm = min(ROW_TILE, t)
    row = pl.BlockSpec((tm, d), lambda i: (i, 0))
    return pl.pallas_call(
        functools.partial(_ln_proj_kernel, n=n),
        grid=(t // tm,),
        in_specs=[row, _const_spec((1, d)), _const_spec((1, d)),
                  pl.BlockSpec(memory_space=pl.ANY), _const_spec((1, n))],
        out_specs=[row, row, pl.BlockSpec((tm, n), lambda i: (i, 0))],
        out_shape=[jax.ShapeDtypeStruct((t, d), F32), jax.ShapeDtypeStruct((t, d), BF16),
                   jax.ShapeDtypeStruct((t, n), out_dtype)],
        scratch_shapes=[pltpu.VMEM((d, n), F32), pltpu.VMEM((d, n), BF16),
                        pltpu.SemaphoreType.DMA(())],
        compiler_params=_params(("arbitrary",)),
        name="ln_in_proj_a",
    )(x2d, g.reshape(1, d), b.reshape(1, d), w, bias.reshape(1, n).astype(F32))


def _mm_bias_kernel(a_ref, w_hbm, b_ref, o_ref, stage, wbf, sem, *, col0, tn, nj):
    j = pl.program_id(0)
    i = pl.program_id(1)

    def slab_copy(jj):
        col = pl.multiple_of(col0 + jj * tn, V7X_LANES)
        return pltpu.make_async_copy(w_hbm.at[:, pl.ds(col, tn)], stage, sem)

    @pl.when(i == 0)
    def _():
        @pl.when(j == 0)
        def _():
            slab_copy(0).start(priority=WEIGHT_DMA_PRIORITY)

        slab_copy(j).wait()
        wbf[...] = stage[...].astype(BF16)

        @pl.when(j + 1 < nj)
        def _():
            slab_copy(j + 1).start(priority=WEIGHT_DMA_PRIORITY)

    acc = jnp.dot(a_ref[...], wbf[...], preferred_element_type=F32)
    o_ref[...] = (acc + b_ref[...]).astype(o_ref.dtype)


def _pick_tile(n, target):
    best = None
    for c in range(V7X_LANES, min(n, target) + 1, V7X_LANES):
        if n % c == 0:
            best = c
    return n if best is None else best


def _mm_bias(a, w, b, col0, n, out_dtype, name, col_tile=MM_COL_TILE):
    m, k = a.shape
    tm = min(MM_ROW_TILE, m)
    tn = _pick_tile(n, col_tile)
    nj = n // tn
    return pl.pallas_call(
        functools.partial(_mm_bias_kernel, col0=col0, tn=tn, nj=nj),
        grid=(nj, m // tm),
        in_specs=[pl.BlockSpec((tm, k), lambda j, i: (i, 0)),
                  pl.BlockSpec(memory_space=pl.ANY),
                  pl.BlockSpec((1, tn), lambda j, i: (0, j))],
        out_specs=pl.BlockSpec((tm, tn), lambda j, i: (i, j)),
        out_shape=jax.ShapeDtypeStruct((m, n), out_dtype),
        scratch_shapes=[pltpu.VMEM((k, tn), F32), pltpu.VMEM((k, tn), BF16),
                        pltpu.SemaphoreType.DMA(())],
        compiler_params=_params(("arbitrary", "arbitrary")),
        name=name,
    )(a, w, b.reshape(1, n).astype(F32))


def _swa_kernel(sink_ref, q_ref, kp_ref, kc_ref, vp_ref, vc_ref, bias_ref, o_ref):
    n = pl.program_id(1)
    kk = jnp.concatenate([kp_ref[...], kc_ref[...]], axis=0)
    vv = jnp.concatenate([vp_ref[...], vc_ref[...]], axis=0)
    col = lax.broadcasted_iota(jnp.int32, (BLK, 2 * BLK), 1)
    exists = (col >= BLK) | (n > 0)
    for h in range(A_HEADS):
        g = h // A_REP
        qh = q_ref[:, h * HEAD_DIM:(h + 1) * HEAD_DIM] * QK_SCALE
        kg = kk[:, g * HEAD_DIM:(g + 1) * HEAD_DIM]
        vg = vv[:, g * HEAD_DIM:(g + 1) * HEAD_DIM]
        s = lax.dot_general(qh, kg, NT_DIMS, preferred_element_type=F32)
        s = jnp.where(exists, s + bias_ref[h], -jnp.inf)
        sink = sink_ref[h]
        m = jnp.maximum(jnp.max(s, axis=-1, keepdims=True), sink)
        p = jnp.exp(s - m)
        denom = jnp.sum(p, axis=-1, keepdims=True) + jnp.exp(sink - m)
        o = jnp.dot(p.astype(BF16), vg, preferred_element_type=F32) / denom
        o_ref[:, h * HEAD_DIM:(h + 1) * HEAD_DIM] = o.astype(o_ref.dtype)


def _rel_bucket(rel):
    half = REL_BUCKETS // 2
    max_exact = half // 2
    base = jnp.where(rel > 0, half, 0)
    n = jnp.abs(rel)
    nf = jnp.maximum(n, 1).astype(F32)
    large = max_exact + (jnp.log(nf / max_exact) / np.log(REL_MAX_DIST / max_exact)
                         * (half - max_exact)).astype(jnp.int32)
    large = jnp.minimum(large, half - 1)
    return base + jnp.where(n < max_exact, n, large)


def _swa_bias_table(rel_bias):
    rel = jnp.arange(-(2 * BLK - 1), BLK)
    line = jnp.transpose(rel_bias[_rel_bucket(rel)]).astype(F32)
    bias = jnp.stack([line[:, BLK - 1 - q:3 * BLK - 1 - q] for q in range(BLK)], axis=1)
    qi = np.arange(BLK)[:, None]
    kj = np.arange(2 * BLK)[None, :]
    dchunk = (kj // CHUNK - BLK // CHUNK) - qi // CHUNK
    band = (dchunk <= 0) & (dchunk >= -WIN_CHUNKS)
    return jnp.where(jnp.asarray(band)[None], bias, -jnp.inf)


def _swa(qkv, sinks, rel_bias, batch, seq):
    nb = seq // BLK
    kcol = A_WIDTH // A_KV_WIDTH
    vcol = kcol + 1
    kv_blk = (BLK, A_KV_WIDTH)
    grid_spec = pltpu.PrefetchScalarGridSpec(
        num_scalar_prefetch=1,
        grid=(batch, nb),
        in_specs=[
            pl.BlockSpec((BLK, A_WIDTH), lambda b, n, s: (b * nb + n, 0)),
            pl.BlockSpec(kv_blk, lambda b, n, s: (b * nb + jnp.maximum(n - 1, 0), kcol)),
            pl.BlockSpec(kv_blk, lambda b, n, s: (b * nb + n, kcol)),
            pl.BlockSpec(kv_blk, lambda b, n, s: (b * nb + jnp.maximum(n - 1, 0), vcol)),
            pl.BlockSpec(kv_blk, lambda b, n, s: (b * nb + n, vcol)),
            pl.BlockSpec((A_HEADS, BLK, 2 * BLK), lambda b, n, s: (0, 0, 0),
                         pipeline_mode=pl.Buffered(1)),
        ],
        out_specs=pl.BlockSpec((BLK, A_WIDTH), lambda b, n, s: (b * nb + n, 0)),
    )
    return pl.pallas_call(
        _swa_kernel,
        grid_spec=grid_spec,
        out_shape=jax.ShapeDtypeStruct((batch * seq, A_WIDTH), BF16),
        compiler_params=_params(("arbitrary", "arbitrary")),
        name="swa_attn",
    )(sinks.astype(F32), qkv, qkv, qkv, qkv, qkv, _swa_bias_table(rel_bias))


def _sb_kernel(q_ref, k_ref, v_ref, tri_ref, o_ref, qh_scr, z_scr, hl_scr, s_scr, w_scr,
               carry_scr, acc_scr):
    t = q_ref.shape[0]
    i = pl.program_id(2)
    n_heads = 2 * SB_PAIRS
    pair_w = 2 * HEAD_DIM
    lane = lax.broadcasted_iota(jnp.int32, (1, pair_w), 1)
    first = lane < HEAD_DIM
    zero = jnp.zeros((), BF16)

    def pair_cols(h):
        return slice((h // 2) * pair_w, (h // 2 + 1) * pair_w)

    def own_half(h, x):
        return jnp.where(first, x, zero) if h % 2 == 0 else jnp.where(first, zero, x)

    q2 = q_ref[...] * QK_SCALE
    for h in range(n_heads):
        qh_scr[h] = own_half(h, q2[:, pair_cols(h)])
    carry_scr[...] = jnp.zeros_like(carry_scr)
    acc_scr[...] = jnp.zeros_like(acc_scr)
    row = lax.broadcasted_iota(jnp.int32, (SB_ROWS, t), 0)
    col = lax.broadcasted_iota(jnp.int32, (SB_ROWS, t), 1)

    def key_tile(j, diagonal):
        start = pl.multiple_of(j * t, t)
        k2 = k_ref[pl.ds(start, t), :]
        v2 = v_ref[pl.ds(start, t), :]
        chunks = [slice(r0, r0 + SB_ROWS) for r0 in range(0, t, SB_ROWS)]
        for h in range(n_heads):
            z_scr[h] = lax.dot_general(qh_scr[h], k2[:, pair_cols(h)], NT_DIMS,
                                       preferred_element_type=F32)
        for rows in chunks:
            for h in range(n_heads):
                z = z_scr[h, rows, :]
                drop = jnp.maximum(z, 0.0) + jnp.log(1.0 + jnp.exp(-jnp.abs(z)))
                if diagonal:
                    drop = jnp.where(col < row + rows.start, drop, 0.0)
                hi = drop.astype(BF16)
                hl_scr[h, rows, 0:t] = hi
                hl_scr[h, rows, t:2 * t] = (drop - hi.astype(F32)).astype(BF16)
        tri = tri_ref[...]
        for h in range(n_heads):
            s_scr[h] = jnp.dot(hl_scr[h], tri, preferred_element_type=F32)
        top = jnp.full((SB_ROWS, V7X_LANES), -jnp.inf, F32)
        for rows in chunks:
            for h in range(n_heads):
                carry = carry_scr[h, rows, :]
                log_w = (z_scr[h, rows, :] + s_scr[h, rows, :]
                         + jnp.concatenate([carry] * (t // V7X_LANES), axis=1))
                w = jnp.exp(log_w)
                if diagonal:
                    w = jnp.where(col < row + rows.start, w, 0.0)
                w_scr[h // 2, rows, (h % 2) * t:(h % 2 + 1) * t] = w.astype(BF16)
                carry = carry + jnp.broadcast_to(s_scr[h, rows, 0:1], (SB_ROWS, V7X_LANES))
                carry_scr[h, rows, :] = carry
                top = jnp.maximum(top, carry)
        for p in range(SB_PAIRS):
            cols = pair_cols(2 * p)
            v_pair = jnp.concatenate([own_half(0, v2[:, cols]), own_half(1, v2[:, cols])], axis=0)
            acc_scr[:, cols] += jnp.dot(w_scr[p], v_pair, preferred_element_type=F32)
        return (jnp.max(top) > SB_LOG_ZERO).astype(jnp.int32)

    live = key_tile(i, True)

    def cond(state):
        return (state[0] >= 0) & (state[1] > 0)

    def body(state):
        return state[0] - 1, key_tile(state[0], False)

    lax.while_loop(cond, body, (i - 1, live))
    o_ref[...] = acc_scr[...].astype(o_ref.dtype)


def _stick_breaking(qkv, batch, seq):
    t = min(SB_TILE, seq)
    nq = seq // t
    width = SB_PAIRS * 2 * HEAD_DIM
    groups = B_WIDTH // width
    heads = 2 * SB_PAIRS
    j = np.arange(2 * t)[:, None] % t
    s = np.arange(t)[None, :]
    tri = jnp.asarray(-(j >= s).astype(np.float32), dtype=BF16)
    return pl.pallas_call(
        _sb_kernel,
        grid=(batch, groups, nq),
        in_specs=[
            pl.BlockSpec((t, width), lambda b, p, i: (b * nq + i, p)),
            pl.BlockSpec((seq, width), lambda b, p, i: (b, groups + p)),
            pl.BlockSpec((seq, width), lambda b, p, i: (b, 2 * groups + p)),
            _const_spec((2 * t, t)),
        ],
        out_specs=pl.BlockSpec((t, width), lambda b, p, i: (b * nq + i, p)),
        out_shape=jax.ShapeDtypeStruct((batch * seq, B_WIDTH), BF16),
        scratch_shapes=[
            pltpu.VMEM((heads, t, 2 * HEAD_DIM), BF16),
            pltpu.VMEM((heads, t, t), F32),
            pltpu.VMEM((heads, t, 2 * t), BF16),
            pltpu.VMEM((heads, t, t), F32),
            pltpu.VMEM((SB_PAIRS, t, 2 * t), BF16),
            pltpu.VMEM((heads, t, V7X_LANES), F32),
            pltpu.VMEM((t, width), F32),
        ],
        compiler_params=_params(("arbitrary", "arbitrary", "arbitrary")),
        name="stick_breaking",
    )(qkv, qkv, qkv, tri)


def _load_weight_bf16(w_hbm, w_bf, stage, sems):
    rows = stage.shape[1]
    n = w_hbm.shape[0] // rows

    def chunk_copy(c):
        return pltpu.make_async_copy(w_hbm.at[pl.ds(c * rows, rows)], stage.at[c % 2],
                                     sems.at[c % 2])

    chunk_copy(0).start()
    for c in range(n):
        if c + 1 < n:
            chunk_copy(c + 1).start()
        chunk_copy(c).wait()
        w_bf[pl.ds(c * rows, rows), :] = stage[c % 2].astype(BF16)


WEIGHT_STAGE_ROWS = 512


def _mix_kernel(ya_ref, yb_ref, gates_a_ref, gates_b_ref, h_ref, wa_hbm, wb_hbm, wm_hbm,
                g_ref, b_ref, o_ref, ob_ref, wa_ref, wb_ref, wm_ref, stage, sems):
    @pl.when(pl.program_id(0) == 0)
    def _():
        _load_weight_bf16(wa_hbm, wa_ref, stage, sems)
        _load_weight_bf16(wb_hbm, wb_ref, stage, sems)
        _load_weight_bf16(wm_hbm, wm_ref, stage, sems)

    pa = jnp.dot(ya_ref[...], wa_ref[...], preferred_element_type=F32)
    pb = jnp.dot(yb_ref[...], wb_ref[...], preferred_element_type=F32)
    merged = jax.nn.sigmoid(gates_a_ref[...]) * pa + jax.nn.sigmoid(gates_b_ref[...]) * pb
    mixed = jnp.dot(merged.astype(BF16), wm_ref[...], preferred_element_type=F32)
    y = _layer_norm(DEEPNORM_ALPHA * h_ref[...] + mixed, g_ref[...], b_ref[...])
    o_ref[...] = y
    ob_ref[...] = y.astype(BF16)


def _mix(ya, yb, gates, h, wa, wb, wm, g, b):
    t, d = h.shape
    tm = min(ATT_TILE, t)
    row = pl.BlockSpec((tm, d), lambda i: (i, 0))
    return pl.pallas_call(
        _mix_kernel,
        grid=(t // tm,),
        in_specs=[pl.BlockSpec((tm, A_WIDTH), lambda i: (i, 0)),
                  pl.BlockSpec((tm, B_WIDTH), lambda i: (i, 0)),
                  pl.BlockSpec((tm, d), lambda i: (i, 0)),
                  pl.BlockSpec((tm, d), lambda i: (i, 1)),
                  row,
                  pl.BlockSpec(memory_space=pl.ANY), pl.BlockSpec(memory_space=pl.ANY),
                  pl.BlockSpec(memory_space=pl.ANY),
                  _const_spec((1, d)), _const_spec((1, d))],
        out_specs=[row, row],
        out_shape=[jax.ShapeDtypeStruct((t, d), F32), jax.ShapeDtypeStruct((t, d), BF16)],
        scratch_shapes=[pltpu.VMEM((A_WIDTH, d), BF16), pltpu.VMEM((B_WIDTH, d), BF16),
                        pltpu.VMEM((d, d), BF16),
                        pltpu.VMEM((2, min(WEIGHT_STAGE_ROWS, d), d), F32),
                        pltpu.SemaphoreType.DMA((2,))],
        compiler_params=_params(("arbitrary",)),
        name="mix_out",
    )(ya, yb, gates, gates, h, wa, wb, wm, g.reshape(1, d), b.reshape(1, d))


def _xattn_kernel(hb_ref, h_ref, k_ref, v_ref, wq_hbm, wo_hbm, g_ref, b_ref, wr_ref,
                  rb_ref, ltri_ref, o_ref, idx_ref, gate_ref, rank_ref, cnt_ref, run_ref,
                  wq_ref, wo_ref, stage, sems):
    first_step = (pl.program_id(0) == 0) & (pl.program_id(1) == 0)

    @pl.when(first_step)
    def _():
        _load_weight_bf16(wq_hbm, wq_ref, stage, sems)
        _load_weight_bf16(wo_hbm, wo_ref, stage, sems)

    d = h_ref.shape[1]
    hd = d // X_HEADS
    scale = hd ** -0.5
    q = jnp.dot(hb_ref[...], wq_ref[...], preferred_element_type=F32).astype(BF16)
    outs = []
    for x in range(X_HEADS):
        sl = slice(x * hd, (x + 1) * hd)
        s = lax.dot_general(q[:, sl], k_ref[:, sl], NT_DIMS, preferred_element_type=F32) * scale
        m = jnp.max(s, axis=-1, keepdims=True)
        p = jnp.exp(s - m)
        denom = jnp.sum(p, axis=-1, keepdims=True)
        o = jnp.dot(p.astype(BF16), v_ref[:, sl], preferred_element_type=F32) / denom
        outs.append(o.astype(BF16))
    xo = jnp.dot(jnp.concatenate(outs, axis=1), wo_ref[...], preferred_element_type=F32)
    h_out = _layer_norm(DEEPNORM_ALPHA * h_ref[...] + xo, g_ref[...], b_ref[...])
    o_ref[...] = h_out
    _route_tile(h_out, first_step, wr_ref, rb_ref, ltri_ref, idx_ref, gate_ref,
                rank_ref, cnt_ref, run_ref)


def _xattn_route(hb, h, kv, wq, wo, g, b, w_router, b_router, batch, seq, mem_len):
    t, d = h.shape
    tm = min(ATT_TILE, seq)
    ns = seq // tm
    w_pad = jnp.zeros((d, V7X_LANES), F32).at[:, :N_EXPERTS].set(w_router)
    w_hi = w_pad.astype(BF16)
    w_lo = (w_pad - w_hi.astype(F32)).astype(BF16)
    w_split = jnp.concatenate([w_hi, w_lo], axis=1)
    b_pad = jnp.zeros((1, V7X_LANES), F32).at[0, :N_EXPERTS].set(b_router)
    ltri = jnp.asarray(np.tril(np.ones((tm, tm), np.float32), -1), dtype=BF16)
    row = pl.BlockSpec((tm, d), lambda bi, i: (bi * ns + i, 0))
    small = pl.BlockSpec((tm, V7X_LANES), lambda bi, i: (bi * ns + i, 0))
    return pl.pallas_call(
        _xattn_kernel,
        grid=(batch, ns),
        in_specs=[row, row,
                  pl.BlockSpec((mem_len, d), lambda bi, i: (bi, 0)),
                  pl.BlockSpec((mem_len, d), lambda bi, i: (bi, 1)),
                  pl.BlockSpec(memory_space=pl.ANY), pl.BlockSpec(memory_space=pl.ANY),
                  _const_spec((1, d)), _const_spec((1, d)),
                  _const_spec((d, 2 * V7X_LANES)),
                  _const_spec((1, V7X_LANES)), _const_spec((tm, tm))],
        out_specs=[row, small, small, small,
                   pl.BlockSpec((1, V7X_LANES), lambda bi, i: (0, 0))],
        out_shape=[jax.ShapeDtypeStruct((t, d), F32),
                   jax.ShapeDtypeStruct((t, V7X_LANES), jnp.int32),
                   jax.ShapeDtypeStruct((t, V7X_LANES), F32),
                   jax.ShapeDtypeStruct((t, V7X_LANES), jnp.int32),
                   jax.ShapeDtypeStruct((1, V7X_LANES), F32)],
        scratch_shapes=[pltpu.VMEM((1, V7X_LANES), F32),
                        pltpu.VMEM((d, d), BF16), pltpu.VMEM((d, d), BF16),
                        pltpu.VMEM((2, min(WEIGHT_STAGE_ROWS, d), d), F32),
                        pltpu.SemaphoreType.DMA((2,))],
        compiler_params=_params(("arbitrary", "arbitrary")),
        name="xattn_route",
    )(hb, h, kv, kv, wq, wo, g.reshape(1, d), b.reshape(1, d), w_split, b_pad, ltri)


def _route_tile(h, first_step, w_ref, b_ref, ltri_ref, idx_ref, gate_ref, rank_ref,
                cnt_ref, run_ref):
    @pl.when(first_step)
    def _():
        run_ref[...] = jnp.zeros_like(run_ref)

    h_hi = h.astype(BF16)
    h_lo = (h - h_hi.astype(F32)).astype(BF16)
    hi_terms = jnp.dot(h_hi, w_ref[...], preferred_element_type=F32)
    lo_hi = jnp.dot(h_lo, w_ref[:, 0:V7X_LANES], preferred_element_type=F32)
    logits = (hi_terms[:, 0:V7X_LANES] + (lo_hi + hi_terms[:, V7X_LANES:2 * V7X_LANES])
              + b_ref[...])
    tm = logits.shape[0]
    lane_i = lax.broadcasted_iota(jnp.int32, (tm, V7X_LANES), 1)
    lane = lane_i.astype(F32)
    rem = jnp.where(lane_i < N_EXPERTS, logits, -jnp.inf)
    vals, idxs = [], []
    for _ in range(TOP_K):
        m = jnp.max(rem, axis=-1, keepdims=True)
        ik = jnp.min(jnp.where(rem == m, lane, float(V7X_LANES)), axis=-1, keepdims=True)
        vals.append(m)
        idxs.append(ik)
        rem = jnp.where(lane == ik, -jnp.inf, rem)
    exps = [jnp.exp(v - vals[0]) for v in vals]
    denom = exps[0] + exps[1] + exps[2] + exps[3]
    sel = jnp.zeros((tm, V7X_LANES), F32)
    for ik in idxs:
        sel = sel + (lane == ik).astype(F32)
    before = jnp.dot(ltri_ref[...], sel.astype(BF16), preferred_element_type=F32) + run_ref[...]
    idx_out = jnp.zeros((tm, V7X_LANES), jnp.int32)
    gate_out = jnp.zeros((tm, V7X_LANES), F32)
    rank_out = jnp.zeros((tm, V7X_LANES), jnp.int32)
    for k in range(TOP_K):
        rank_k = jnp.sum(jnp.where(lane == idxs[k], before, 0.0), axis=-1, keepdims=True)
        idx_out = jnp.where(lane_i == k, idxs[k].astype(jnp.int32), idx_out)
        gate_out = jnp.where(lane_i == k, exps[k] / denom, gate_out)
        rank_out = jnp.where(lane_i == k, rank_k.astype(jnp.int32), rank_out)
    idx_ref[...] = idx_out
    gate_ref[...] = gate_out
    rank_ref[...] = rank_out
    run_ref[...] = run_ref[...] + jnp.sum(sel, axis=0, keepdims=True)
    cnt_ref[...] = run_ref[...]


DMA_UNROLL = 8
NUM_DMA_QUEUES = 2
(META_EXPERT, META_FIRST, META_LAST_GROUP, META_NEXT_EXPERT, META_GROUP,
 META_ROWS_USED, META_NUM_GROUPS) = range(7)


def _pack_bf16_pairs(x):
    half = x.shape[1] // 2
    lo = lax.bitcast_convert_type(x[:, :half].astype(BF16).astype(F32), jnp.uint32)
    hi = lax.bitcast_convert_type(x[:, half:].astype(BF16).astype(F32), jnp.uint32)
    return (lo >> 16) | (hi & jnp.uint32(0xFFFF0000))


def _unpack_bf16_pairs(words):
    lo = lax.bitcast_convert_type(words << 16, F32).astype(BF16)
    hi = lax.bitcast_convert_type(words & jnp.uint32(0xFFFF0000), F32).astype(BF16)
    return jnp.concatenate([lo, hi], axis=1)


def _dispatch_kernel(dest_ref, meta_ref, h_ref, xs_hbm, pack_ref, zero_ref, sem, zsem):
    tm = h_ref.shape[0]
    base = pl.program_id(0) * tm
    et = zero_ref.shape[0]
    pack_ref[...] = _pack_bf16_pairs(h_ref[...])

    @pl.when(pl.program_id(0) == 0)
    def _():
        zero_ref[...] = jnp.zeros_like(zero_ref)

        def fill_copy(tile):
            start_row = pl.multiple_of(tile * et, et)
            return pltpu.make_async_copy(zero_ref, xs_hbm.at[pl.ds(start_row, et)], zsem)

        def fill_start(tile, c):
            @pl.when(meta_ref[META_ROWS_USED, tile] < et)
            def _():
                fill_copy(tile).start()
            return c

        def fill_wait(tile, c):
            @pl.when(meta_ref[META_ROWS_USED, tile] < et)
            def _():
                fill_copy(tile).wait()
            return c

        n_tiles = xs_hbm.shape[0] // et
        lax.fori_loop(0, n_tiles, fill_start, 0)
        lax.fori_loop(0, n_tiles, fill_wait, 0)

    def row_copy(r, k):
        return pltpu.make_async_copy(pack_ref.at[pl.ds(r, 1)],
                                     xs_hbm.at[pl.ds(dest_ref[(base + r) * TOP_K + k], 1)], sem)

    def start(r, c):
        for k in range(TOP_K):
            row_copy(r, k).start(priority=k % NUM_DMA_QUEUES)
        return c

    def wait(r, c):
        for k in range(TOP_K):
            row_copy(r, k).wait()
        return c

    lax.fori_loop(0, tm, start, 0, unroll=DMA_UNROLL)
    lax.fori_loop(0, tm, wait, 0, unroll=DMA_UNROLL)


def _dispatch(h, dest, meta, n_rows):
    t, d = h.shape
    tm = min(DISPATCH_TILE, t)
    grid_spec = pltpu.PrefetchScalarGridSpec(
        num_scalar_prefetch=2,
        grid=(t // tm,),
        in_specs=[pl.BlockSpec((tm, d), lambda i, dst, m: (i, 0))],
        out_specs=pl.BlockSpec(memory_space=pl.ANY),
        scratch_shapes=[pltpu.VMEM((tm, d // 2), jnp.uint32),
                        pltpu.VMEM((EXPERT_TILE, d // 2), jnp.uint32),
                        pltpu.SemaphoreType.DMA(()), pltpu.SemaphoreType.DMA(())],
    )
    return pl.pallas_call(
        _dispatch_kernel,
        grid_spec=grid_spec,
        out_shape=jax.ShapeDtypeStruct((n_rows, d // 2), jnp.uint32),
        compiler_params=_params(("arbitrary",)),
        name="moe_dispatch",
    )(dest, meta, h)


def _stream_expert_weights(meta_ref, j, t, nj, slots, make_copies, on_arrival):
    @pl.when((j == 0) & (t == 0))
    def _():
        for c in make_copies(meta_ref[META_EXPERT, 0], 0, 0):
            c.start(priority=WEIGHT_DMA_PRIORITY)

    @pl.when(meta_ref[META_FIRST, t] == 1)
    def _():
        if slots == 1:
            slot = next_slot = 0
        else:
            slot = (j * meta_ref[META_NUM_GROUPS, t] + meta_ref[META_GROUP, t]) & 1
            next_slot = 1 - slot
        for c in make_copies(meta_ref[META_EXPERT, t], j, slot):
            c.wait()
        if slots == 1:
            on_arrival(slot)
        last_group = meta_ref[META_LAST_GROUP, t] == 1

        @pl.when(jnp.logical_not(last_group & (j == nj - 1)))
        def _():
            next_chunk = jnp.where(last_group, j + 1, j)
            for c in make_copies(meta_ref[META_NEXT_EXPERT, t], next_chunk, next_slot):
                c.start(priority=WEIGHT_DMA_PRIORITY)

        if slots == 2:
            on_arrival(slot)


def _for_used_rows(rows_used, o_ref, compute):
    tm = o_ref.shape[0]
    part = tm // EXPERT_TILE_PARTS
    for parts in range(EXPERT_TILE_PARTS + 1):
        used = parts * part

        @pl.when((rows_used > used - part) & (rows_used <= used))
        def _(used=used):
            if used > 0:
                compute(slice(0, used))
            if used < tm:
                o_ref[used:tm, :] = jnp.zeros((tm - used, o_ref.shape[1]), o_ref.dtype)


def _moe_up_kernel(meta_ref, x_ref, w_hbm, b_ref, o_ref, wf32, wbf, sem, *, col_chunk):
    t = pl.program_id(0)
    f = o_ref.shape[1]

    def make_copies(expert, chunk, slot):
        del chunk, slot
        return (pltpu.make_async_copy(w_hbm.at[expert], wf32, sem),)

    def on_arrival(slot):
        del slot
        wbf[...] = wf32[...].astype(BF16)

    _stream_expert_weights(meta_ref, 0, t, 1, 1, make_copies, on_arrival)

    def expert_rows(rows):
        x = _unpack_bf16_pairs(x_ref[rows, :])
        for c0 in range(0, f, col_chunk):
            glu_cols = slice(c0, c0 + col_chunk)
            lin_cols = slice(f + c0, f + c0 + col_chunk)
            glu = jnp.dot(x, wbf[:, glu_cols], preferred_element_type=F32) + b_ref[:, glu_cols]
            lin = jnp.dot(x, wbf[:, lin_cols], preferred_element_type=F32) + b_ref[:, lin_cols]
            glu = jnp.minimum(glu, SWIGLU_LIMIT)
            lin = jnp.clip(lin, -SWIGLU_LIMIT, SWIGLU_LIMIT)
            act = glu * jax.nn.sigmoid(SWIGLU_ALPHA * glu) * (lin + 1.0)
            o_ref[rows, glu_cols] = act.astype(o_ref.dtype)

    _for_used_rows(meta_ref[META_ROWS_USED, t], o_ref, expert_rows)


def _moe_up(xs, w_up, b_up, meta):
    r = xs.shape[0]
    e, d, f2 = w_up.shape
    f = f2 // 2
    tm = EXPERT_TILE
    grid_spec = pltpu.PrefetchScalarGridSpec(
        num_scalar_prefetch=1,
        grid=(r // tm,),
        in_specs=[
            pl.BlockSpec((tm, d // 2), lambda t, m: (t, 0)),
            pl.BlockSpec(memory_space=pl.ANY),
            pl.BlockSpec((None, 1, f2), lambda t, m: (m[META_EXPERT, t], 0, 0)),
        ],
        out_specs=pl.BlockSpec((tm, f), lambda t, m: (t, 0)),
        scratch_shapes=[pltpu.VMEM((d, f2), F32), pltpu.VMEM((d, f2), BF16),
                        pltpu.SemaphoreType.DMA(())],
    )
    return pl.pallas_call(
        functools.partial(_moe_up_kernel, col_chunk=_pick_tile(f, MM_COL_TILE)),
        grid_spec=grid_spec,
        out_shape=jax.ShapeDtypeStruct((r, f), BF16),
        compiler_params=_params(("arbitrary",)),
        name="moe_up",
    )(meta, xs, w_up, b_up.reshape(e, 1, f2))


def _moe_down_kernel(meta_ref, a_ref, w_hbm, b_ref, o_ref, wf32, wbf, sems, *, tn, nj):
    j = pl.program_id(0)
    t = pl.program_id(1)

    def make_copies(expert, chunk, slot):
        col = pl.multiple_of(chunk * tn, tn)
        return (pltpu.make_async_copy(w_hbm.at[expert, :, pl.ds(col, tn)],
                                      wf32.at[slot], sems.at[slot]),)

    def on_arrival(slot):
        wbf[...] = wf32[slot].astype(BF16)

    _stream_expert_weights(meta_ref, j, t, nj, 2, make_copies, on_arrival)

    def expert_rows(rows):
        o_ref[rows, :] = (jnp.dot(a_ref[rows, :], wbf[...], preferred_element_type=F32)
                          + b_ref[...])

    _for_used_rows(meta_ref[META_ROWS_USED, t], o_ref, expert_rows)


def _moe_down(act, w_down, b_down, meta):
    r, f = act.shape
    e, _, d = w_down.shape
    tm = EXPERT_TILE
    tn = _pick_tile(d, EXPERT_DOWN_COL_TILE)
    nj = d // tn
    grid_spec = pltpu.PrefetchScalarGridSpec(
        num_scalar_prefetch=1,
        grid=(nj, r // tm),
        in_specs=[
            pl.BlockSpec((tm, f), lambda j, t, m: (t, 0)),
            pl.BlockSpec(memory_space=pl.ANY),
            pl.BlockSpec((None, 1, tn), lambda j, t, m: (m[META_EXPERT, t], 0, j)),
        ],
        out_specs=pl.BlockSpec((tm, tn), lambda j, t, m: (t, j)),
        scratch_shapes=[pltpu.VMEM((2, f, tn), F32), pltpu.VMEM((f, tn), BF16),
                        pltpu.SemaphoreType.DMA((2,))],
    )
    return pl.pallas_call(
        functools.partial(_moe_down_kernel, tn=tn, nj=nj),
        grid_spec=grid_spec,
        out_shape=jax.ShapeDtypeStruct((r, d), F32),
        compiler_params=_params(("arbitrary", "arbitrary")),
        name="moe_down",
    )(meta, act, w_down, b_down.reshape(e, 1, d))


def _combine_kernel(pos_ref, y_hbm, gate_ref, h_ref, g_ref, b_ref, o_ref, buf_ref, sems):
    tm = o_ref.shape[0]
    i = pl.program_id(0)

    def row_copy(step, r, k, slot):
        src_row = pos_ref[(step * tm + r) * TOP_K + k]
        return pltpu.make_async_copy(y_hbm.at[pl.ds(src_row, 1)],
                                     buf_ref.at[slot, k, pl.ds(r, 1)], sems.at[slot])

    def gather_start(step, slot):
        def body(r, c):
            for k in range(TOP_K):
                row_copy(step, r, k, slot).start(priority=k % NUM_DMA_QUEUES)
            return c
        lax.fori_loop(0, tm, body, 0, unroll=DMA_UNROLL)

    def gather_wait(step, slot):
        def body(r, c):
            for k in range(TOP_K):
                row_copy(step, r, k, slot).wait()
            return c
        lax.fori_loop(0, tm, body, 0, unroll=DMA_UNROLL)

    @pl.when(i == 0)
    def _():
        gather_start(0, 0)

    for next_slot in range(2):
        @pl.when((i + 1 < pl.num_programs(0)) & ((i + 1) & 1 == next_slot))
        def _(next_slot=next_slot):
            gather_start(i + 1, next_slot)

    slot = i & 1
    gather_wait(i, slot)
    gates = gate_ref[...]
    ff = gates[:, 0:1] * buf_ref[slot, 0]
    for k in range(1, TOP_K):
        ff = ff + gates[:, k:k + 1] * buf_ref[slot, k]
    o_ref[...] = _layer_norm(DEEPNORM_ALPHA * h_ref[...] + ff, g_ref[...], b_ref[...])


def _combine(y_rows, pos, gates, h, g, b):
    t, d = h.shape
    tm = min(COMBINE_TILE, t)
    grid_spec = pltpu.PrefetchScalarGridSpec(
        num_scalar_prefetch=1,
        grid=(t // tm,),
        in_specs=[pl.BlockSpec(memory_space=pl.ANY),
                  pl.BlockSpec((tm, V7X_LANES), lambda i, p: (i, 0)),
                  pl.BlockSpec((tm, d), lambda i, p: (i, 0)),
                  pl.BlockSpec((1, d), lambda i, p: (0, 0)),
                  pl.BlockSpec((1, d), lambda i, p: (0, 0))],
        out_specs=pl.BlockSpec((tm, d), lambda i, p: (i, 0)),
        scratch_shapes=[pltpu.VMEM((2, TOP_K, tm, d), F32), pltpu.SemaphoreType.DMA((2,))],
    )
    return pl.pallas_call(
        _combine_kernel,
        grid_spec=grid_spec,
        out_shape=jax.ShapeDtypeStruct((t, d), F32),
        compiler_params=_params(("arbitrary",)),
        name="moe_combine",
    )(pos, y_rows, gates, h, g.reshape(1, d), b.reshape(1, d))


def _expert_layout(idx, rank, counts, n_tiles, tm):
    experts = jnp.arange(N_EXPERTS, dtype=jnp.int32)
    tiles_per = (counts + tm - 1) // tm
    tile_end = jnp.cumsum(tiles_per).astype(jnp.int32)
    tile_off = tile_end - tiles_per
    dest = jnp.sum(jnp.where(idx[..., None] == experts, tile_off * tm, 0), axis=-1) + rank

    nonempty = counts > 0
    first_e = jnp.min(jnp.where(nonempty, experts, N_EXPERTS))
    last_e = jnp.max(jnp.where(nonempty, experts, 0))
    group_of = jnp.cumsum(nonempty.astype(jnp.int32)) - 1
    later = (experts[None, :] > experts[:, None]) & nonempty[None, :]
    next_e = jnp.min(jnp.where(later, experts[None, :], N_EXPERTS), axis=1)
    next_e = jnp.where(next_e == N_EXPERTS, first_e, next_e)

    tid = jnp.arange(n_tiles, dtype=jnp.int32)
    valid = tid < tile_end[-1]
    te = jnp.sum((tid[:, None] >= tile_end[None, :]).astype(jnp.int32), axis=1)
    te = jnp.where(valid, jnp.minimum(te, N_EXPERTS - 1), last_e)
    onehot = te[:, None] == experts[None, :]

    def lookup(table):
        return jnp.sum(jnp.where(onehot, table[None, :], 0), axis=1)

    local = tid - lookup(tile_off)
    rows_used = jnp.where(valid, jnp.clip(lookup(counts) - local * tm, 0, tm), 0)
    meta = jnp.stack([
        te,
        (valid & (local == 0)).astype(jnp.int32),
        (te == last_e).astype(jnp.int32),
        lookup(next_e),
        lookup(group_of),
        rows_used,
        jnp.broadcast_to(jnp.sum(nonempty.astype(jnp.int32)), (n_tiles,)),
    ]).astype(jnp.int32)
    return dest.reshape(-1).astype(jnp.int32), meta


def _moe(h, routing, w_up, b_up, w_down, b_down, g, b):
    t, d = h.shape
    tm = EXPERT_TILE
    idx_p, gate_p, rank_p, cnt_p = routing
    n_rows = t * TOP_K + N_EXPERTS * tm
    dest, meta = _expert_layout(idx_p[:, :TOP_K], rank_p[:, :TOP_K],
                                cnt_p[0, :N_EXPERTS].astype(jnp.int32), n_rows // tm, tm)
    xs = _dispatch(h, dest, meta, n_rows)
    act = _moe_up(xs, w_up, b_up, meta)
    y_rows = _moe_down(act, w_down, b_down, meta)
    return _combine(y_rows, dest, gate_p, h, g, b)


def kernel(x, mem, ln_in_g, ln_in_b, rel_bias, w_in, b_in, attn_sinks, w_a_out, w_b_out,
           w_mix_out, ln1_g, ln1_b, w_xq, w_xkv, w_xo, ln2_g, ln2_b, w_router, b_router,
           w_up, b_up, w_down, b_down, ln3_g, ln3_b):
    batch, seq, d = x.shape
    mem_len = mem.shape[1]
    t = batch * seq
    a_cols = A_WIDTH + 2 * A_KV_WIDTH
    b_cols = 3 * B_WIDTH

    g_col = a_cols + b_cols
    h, hb, qkv_a = _ln_proj(x.reshape(t, d), ln_in_g, ln_in_b, w_in[0], b_in[0, :a_cols],
                            a_cols, BF16)
    for l in range(DEPTH):
        if l > 0:
            qkv_a = _mm_bias(hb, w_in[l], b_in[l, :a_cols], 0, a_cols, BF16, "in_proj_a")
        qkv_b = _mm_bias(hb, w_in[l], b_in[l, a_cols:g_col], a_cols, b_cols, BF16, "in_proj_b")
        gates = _mm_bias(hb, w_in[l], b_in[l, g_col:], g_col, 2 * d, F32, "in_proj_gates")
        ya = _swa(qkv_a, attn_sinks[l], rel_bias, batch, seq)
        yb = _stick_breaking(qkv_b, batch, seq)
        h, hb = _mix(ya, yb, gates, h, w_a_out[l], w_b_out[l], w_mix_out[l], ln1_g[l], ln1_b[l])
        kv = _mm_bias(mem.reshape(batch * mem_len, d).astype(BF16), w_xkv[l],
                      jnp.zeros((2 * d,), F32), 0, 2 * d, BF16, "mem_kv",
                      col_tile=MEM_KV_COL_TILE)
        h, *routing = _xattn_route(hb, h, kv, w_xq[l], w_xo[l], ln2_g[l], ln2_b[l],
                                   w_router[l], b_router[l], batch, seq, mem_len)
        h = _moe(h, routing, w_up[l], b_up[l], w_down[l], b_down[l], ln3_g[l], ln3_b[l])
        if l + 1 < DEPTH:
            hb = h.astype(BF16)
    return h.reshape(batch, seq, d)
```

```python
import functools

import jax
import jax.numpy as jnp
import numpy as np
from jax import lax
from jax.experimental import pallas as pl
from jax.experimental.pallas import tpu as pltpu

DEPTH = 1
CHUNK = 64
BLK = 128
HEAD_DIM = 64
A_HEADS = 16
A_KV_HEADS = 2
A_REP = A_HEADS // A_KV_HEADS
WIN_CHUNKS = 2
B_HEADS = 16
A_WIDTH = A_HEADS * HEAD_DIM
A_KV_WIDTH = A_KV_HEADS * HEAD_DIM
B_WIDTH = B_HEADS * HEAD_DIM
REL_BUCKETS = 32
REL_MAX_DIST = 128
X_HEADS = 4
N_EXPERTS = 32
TOP_K = 4
SWIGLU_LIMIT = 7.0
SWIGLU_ALPHA = 1.702
LN_EPS = 1e-5
DEEPNORM_ALPHA = (2.0 * DEPTH) ** 0.25
QK_SCALE = HEAD_DIM ** -0.5
V7X_LANES = 128
V7X_VMEM_LIMIT_BYTES = 56 * 1024 * 1024
WEIGHT_DMA_PRIORITY = 1
ROW_TILE = 512
MM_ROW_TILE = 1024
MM_COL_TILE = 1024
EXPERT_DOWN_COL_TILE = 2048
MEM_KV_COL_TILE = 2048
ATT_TILE = 256
SB_TILE = 256
SB_PAIRS = 4
SB_ROWS = 32
EXPERT_TILE = 256
EXPERT_TILE_PARTS = 8
DISPATCH_TILE = 512
COMBINE_TILE = 256
SB_LOG_ZERO = -105.0

F32 = jnp.float32
BF16 = jnp.bfloat16
NT_DIMS = (((1,), (1,)), ((), ()))


def _params(semantics):
    return pltpu.CompilerParams(dimension_semantics=semantics,
                                vmem_limit_bytes=V7X_VMEM_LIMIT_BYTES)


def _const_spec(shape):
    nd = len(shape)
    return pl.BlockSpec(shape, lambda *_: (0,) * nd, pipeline_mode=pl.Buffered(1))


def _layer_norm(x, g, b):
    mu = jnp.mean(x, axis=-1, keepdims=True)
    xc = x - mu
    var = jnp.mean(xc * xc, axis=-1, keepdims=True)
    return xc * lax.rsqrt(var + LN_EPS) * g + b


def _ln_proj_kernel(x_ref, g_ref, b_ref, w_hbm, bias_ref, h_ref, hb_ref, o_ref, stage, wbf, sem,
                    *, n):
    @pl.when(pl.program_id(0) == 0)
    def _():
        slab = pltpu.make_async_copy(w_hbm.at[:, pl.ds(0, n)], stage, sem)
        slab.start()
        slab.wait()
        wbf[...] = stage[...].astype(BF16)

    y = _layer_norm(x_ref[...], g_ref[...], b_ref[...])
    yb = y.astype(BF16)
    h_ref[...] = y
    hb_ref[...] = yb
    acc = jnp.dot(yb, wbf[...], preferred_element_type=F32)
    o_ref[...] = (acc + bias_ref[...]).astype(o_ref.dtype)


def _ln_proj(x2d, g, b, w, bias, n, out_dtype):
    t, d = x2d.shape
    tm = min(ROW_TILE, t)
    row = pl.BlockSpec((tm, d), lambda i: (i, 0))
    return pl.pallas_call(
        functools.partial(_ln_proj_kernel, n=n),
        grid=(t // tm,),
        in_specs=[row, _const_spec((1, d)), _const_spec((1, d)),
                  pl.BlockSpec(memory_space=pl.ANY), _const_spec((1, n))],
        out_specs=[row, row, pl.BlockSpec((tm, n), lambda i: (i, 0))],
        out_shape=[jax.ShapeDtypeStruct((t, d), F32), jax.ShapeDtypeStruct((t, d), BF16),
                   jax.ShapeDtypeStruct((t, n), out_dtype)],
        scratch_shapes=[pltpu.VMEM((d, n), F32), pltpu.VMEM((d, n), BF16),
                        pltpu.SemaphoreType.DMA(())],
        compiler_params=_params(("arbitrary",)),
        name="ln_in_proj_a",
    )(x2d, g.reshape(1, d), b.reshape(1, d), w, bias.reshape(1, n).astype(F32))


def _mm_bias_kernel(a_ref, w_hbm, b_ref, o_ref, stage, wbf, sem, *, col0, tn, nj):
    j = pl.program_id(0)
    i = pl.program_id(1)

    def slab_copy(jj):
        col = pl.multiple_of(col0 + jj * tn, V7X_LANES)
        return pltpu.make_async_copy(w_hbm.at[:, pl.ds(col, tn)], stage, sem)

    @pl.when(i == 0)
    def _():
        @pl.when(j == 0)
        def _():
            slab_copy(0).start(priority=WEIGHT_DMA_PRIORITY)

        slab_copy(j).wait()
        wbf[...] = stage[...].astype(BF16)

        @pl.when(j + 1 < nj)
        def _():
            slab_copy(j + 1).start(priority=WEIGHT_DMA_PRIORITY)

    acc = jnp.dot(a_ref[...], wbf[...], preferred_element_type=F32)
    o_ref[...] = (acc + b_ref[...]).astype(o_ref.dtype)


def _pick_tile(n, target):
    best = None
    for c in range(V7X_LANES, min(n, target) + 1, V7X_LANES):
        if n % c == 0:
            best = c
    return n if best is None else best


def _mm_bias(a, w, b, col0, n, out_dtype, name, col_tile=MM_COL_TILE):
    m, k = a.shape
    tm = min(MM_ROW_TILE, m)
    tn = _pick_tile(n, col_tile)
    nj = n // tn
    return pl.pallas_call(
        functools.partial(_mm_bias_kernel, col0=col0, tn=tn, nj=nj),
        grid=(nj, m // tm),
        in_specs=[pl.BlockSpec((tm, k), lambda j, i: (i, 0)),
                  pl.BlockSpec(memory_space=pl.ANY),
                  pl.BlockSpec((1, tn), lambda j, i: (0, j))],
        out_specs=pl.BlockSpec((tm, tn), lambda j, i: (i, j)),
        out_shape=jax.ShapeDtypeStruct((m, n), out_dtype),
        scratch_shapes=[pltpu.VMEM((k, tn), F32), pltpu.VMEM((k, tn), BF16),
                        pltpu.SemaphoreType.DMA(())],
        compiler_params=_params(("arbitrary", "arbitrary")),
        name=name,
    )(a, w, b.reshape(1, n).astype(F32))


def _swa_kernel(sink_ref, q_ref, kp_ref, kc_ref, vp_ref, vc_ref, bias_ref, o_ref):
    n = pl.program_id(1)
    kk = jnp.concatenate([kp_ref[...], kc_ref[...]], axis=0)
    vv = jnp.concatenate([vp_ref[...], vc_ref[...]], axis=0)
    col = lax.broadcasted_iota(jnp.int32, (BLK, 2 * BLK), 1)
    exists = (col >= BLK) | (n > 0)
    for h in range(A_HEADS):
        g = h // A_REP
        qh = q_ref[:, h * HEAD_DIM:(h + 1) * HEAD_DIM] * QK_SCALE
        kg = kk[:, g * HEAD_DIM:(g + 1) * HEAD_DIM]
        vg = vv[:, g * HEAD_DIM:(g + 1) * HEAD_DIM]
        s = lax.dot_general(qh, kg, NT_DIMS, preferred_element_type=F32)
        s = jnp.where(exists, s + bias_ref[h], -jnp.inf)
        sink = sink_ref[h]
        m = jnp.maximum(jnp.max(s, axis=-1, keepdims=True), sink)
        p = jnp.exp(s - m)
        denom = jnp.sum(p, axis=-1, keepdims=True) + jnp.exp(sink - m)
        o = jnp.dot(p.astype(BF16), vg, preferred_element_type=F32) / denom
        o_ref[:, h * HEAD_DIM:(h + 1) * HEAD_DIM] = o.astype(o_ref.dtype)


def _rel_bucket(rel):
    half = REL_BUCKETS // 2
    max_exact = half // 2
    base = jnp.where(rel > 0, half, 0)
    n = jnp.abs(rel)
    nf = jnp.maximum(n, 1).astype(F32)
    large = max_exact + (jnp.log(nf / max_exact) / np.log(REL_MAX_DIST / max_exact)
                         * (half - max_exact)).astype(jnp.int32)
    large = jnp.minimum(large, half - 1)
    return base + jnp.where(n < max_exact, n, large)


def _swa_bias_table(rel_bias):
    rel = jnp.arange(-(2 * BLK - 1), BLK)
    line = jnp.transpose(rel_bias[_rel_bucket(rel)]).astype(F32)
    bias = jnp.stack([line[:, BLK - 1 - q:3 * BLK - 1 - q] for q in range(BLK)], axis=1)
    qi = np.arange(BLK)[:, None]
    kj = np.arange(2 * BLK)[None, :]
    dchunk = (kj // CHUNK - BLK // CHUNK) - qi // CHUNK
    band = (dchunk <= 0) & (dchunk >= -WIN_CHUNKS)
    return jnp.where(jnp.asarray(band)[None], bias, -jnp.inf)


def _swa(qkv, sinks, rel_bias, batch, seq):
    nb = seq // BLK
    kcol = A_WIDTH // A_KV_WIDTH
    vcol = kcol + 1
    kv_blk = (BLK, A_KV_WIDTH)
    grid_spec = pltpu.PrefetchScalarGridSpec(
        num_scalar_prefetch=1,
        grid=(batch, nb),
        in_specs=[
            pl.BlockSpec((BLK, A_WIDTH), lambda b, n, s: (b * nb + n, 0)),
            pl.BlockSpec(kv_blk, lambda b, n, s: (b * nb + jnp.maximum(n - 1, 0), kcol)),
            pl.BlockSpec(kv_blk, lambda b, n, s: (b * nb + n, kcol)),
            pl.BlockSpec(kv_blk, lambda b, n, s: (b * nb + jnp.maximum(n - 1, 0), vcol)),
            pl.BlockSpec(kv_blk, lambda b, n, s: (b * nb + n, vcol)),
            pl.BlockSpec((A_HEADS, BLK, 2 * BLK), lambda b, n, s: (0, 0, 0),
                         pipeline_mode=pl.Buffered(1)),
        ],
        out_specs=pl.BlockSpec((BLK, A_WIDTH), lambda b, n, s: (b * nb + n, 0)),
    )
    return pl.pallas_call(
        _swa_kernel,
        grid_spec=grid_spec,
        out_shape=jax.ShapeDtypeStruct((batch * seq, A_WIDTH), BF16),
        compiler_params=_params(("arbitrary", "arbitrary")),
        name="swa_attn",
    )(sinks.astype(F32), qkv, qkv, qkv, qkv, qkv, _swa_bias_table(rel_bias))


def _sb_kernel(q_ref, k_ref, v_ref, tri_ref, o_ref, qh_scr, z_scr, hl_scr, s_scr, w_scr,
               carry_scr, acc_scr):
    t = q_ref.shape[0]
    i = pl.program_id(2)
    n_heads = 2 * SB_PAIRS
    pair_w = 2 * HEAD_DIM
    lane = lax.broadcasted_iota(jnp.int32, (1, pair_w), 1)
    first = lane < HEAD_DIM
    zero = jnp.zeros((), BF16)

    def pair_cols(h):
        return slice((h // 2) * pair_w, (h // 2 + 1) * pair_w)

    def own_half(h, x):
        return jnp.where(first, x, zero) if h % 2 == 0 else jnp.where(first, zero, x)

    q2 = q_ref[...] * QK_SCALE
    for h in range(n_heads):
        qh_scr[h] = own_half(h, q2[:, pair_cols(h)])
    carry_scr[...] = jnp.zeros_like(carry_scr)
    acc_scr[...] = jnp.zeros_like(acc_scr)
    row = lax.broadcasted_iota(jnp.int32, (SB_ROWS, t), 0)
    col = lax.broadcasted_iota(jnp.int32, (SB_ROWS, t), 1)

    def key_tile(j, diagonal):
        start = pl.multiple_of(j * t, t)
        k2 = k_ref[pl.ds(start, t), :]
        v2 = v_ref[pl.ds(start, t), :]
        chunks = [slice(r0, r0 + SB_ROWS) for r0 in range(0, t, SB_ROWS)]
        for h in range(n_heads):
            z_scr[h] = lax.dot_general(qh_scr[h], k2[:, pair_cols(h)], NT_DIMS,
                                       preferred_element_type=F32)
        for rows in chunks:
            for h in range(n_heads):
                z = z_scr[h, rows, :]
                drop = jnp.maximum(z, 0.0) + jnp.log(1.0 + jnp.exp(-jnp.abs(z)))
                if diagonal:
                    drop = jnp.where(col < row + rows.start, drop, 0.0)
                hi = drop.astype(BF16)
                hl_scr[h, rows, 0:t] = hi
                hl_scr[h, rows, t:2 * t] = (drop - hi.astype(F32)).astype(BF16)
        tri = tri_ref[...]
        for h in range(n_heads):
            s_scr[h] = jnp.dot(hl_scr[h], tri, preferred_element_type=F32)
        top = jnp.full((SB_ROWS, V7X_LANES), -jnp.inf, F32)
        for rows in chunks:
            for h in range(n_heads):
                carry = carry_scr[h, rows, :]
                log_w = (z_scr[h, rows, :] + s_scr[h, rows, :]
                         + jnp.concatenate([carry] * (t // V7X_LANES), axis=1))
                w = jnp.exp(log_w)
                if diagonal:
                    w = jnp.where(col < row + rows.start, w, 0.0)
                w_scr[h // 2, rows, (h % 2) * t:(h % 2 + 1) * t] = w.astype(BF16)
                carry = carry + jnp.broadcast_to(s_scr[h, rows, 0:1], (SB_ROWS, V7X_LANES))
                carry_scr[h, rows, :] = carry
                top = jnp.maximum(top, carry)
        for p in range(SB_PAIRS):
            cols = pair_cols(2 * p)
            v_pair = jnp.concatenate([own_half(0, v2[:, cols]), own_half(1, v2[:, cols])], axis=0)
            acc_scr[:, cols] += jnp.dot(w_scr[p], v_pair, preferred_element_type=F32)
        return (jnp.max(top) > SB_LOG_ZERO).astype(jnp.int32)

    live = key_tile(i, True)

    def cond(state):
        return (state[0] >= 0) & (state[1] > 0)

    def body(state):
        return state[0] - 1, key_tile(state[0], False)

    lax.while_loop(cond, body, (i - 1, live))
    o_ref[...] = acc_scr[...].astype(o_ref.dtype)


def _stick_breaking(qkv, batch, seq):
    t = min(SB_TILE, seq)
    nq = seq // t
    width = SB_PAIRS * 2 * HEAD_DIM
    groups = B_WIDTH // width
    heads = 2 * SB_PAIRS
    j = np.arange(2 * t)[:, None] % t
    s = np.arange(t)[None, :]
    tri = jnp.asarray(-(j >= s).astype(np.float32), dtype=BF16)
    return pl.pallas_call(
        _sb_kernel,
        grid=(batch, groups, nq),
        in_specs=[
            pl.BlockSpec((t, width), lambda b, p, i: (b * nq + i, p)),
            pl.BlockSpec((seq, width), lambda b, p, i: (b, groups + p)),
            pl.BlockSpec((seq, width), lambda b, p, i: (b, 2 * groups + p)),
            _const_spec((2 * t, t)),
        ],
        out_specs=pl.BlockSpec((t, width), lambda b, p, i: (b * nq + i, p)),
        out_shape=jax.ShapeDtypeStruct((batch * seq, B_WIDTH), BF16),
        scratch_shapes=[
            pltpu.VMEM((heads, t, 2 * HEAD_DIM), BF16),
            pltpu.VMEM((heads, t, t), F32),
            pltpu.VMEM((heads, t, 2 * t), BF16),
            pltpu.VMEM((heads, t, t), F32),
            pltpu.VMEM((SB_PAIRS, t, 2 * t), BF16),
            pltpu.VMEM((heads, t, V7X_LANES), F32),
            pltpu.VMEM((t, width), F32),
        ],
        compiler_params=_params(("arbitrary", "arbitrary", "arbitrary")),
        name="stick_breaking",
    )(qkv, qkv, qkv, tri)


def _load_weight_bf16(w_hbm, w_bf, stage, sems):
    rows = stage.shape[1]
    n = w_hbm.shape[0] // rows

    def chunk_copy(c):
        return pltpu.make_async_copy(w_hbm.at[pl.ds(c * rows, rows)], stage.at[c % 2],
                                     sems.at[c % 2])

    chunk_copy(0).start()
    for c in range(n):
        if c + 1 < n:
            chunk_copy(c + 1).start()
        chunk_copy(c).wait()
        w_bf[pl.ds(c * rows, rows), :] = stage[c % 2].astype(BF16)


WEIGHT_STAGE_ROWS = 512


def _mix_kernel(ya_ref, yb_ref, gates_a_ref, gates_b_ref, h_ref, wa_hbm, wb_hbm, wm_hbm,
                g_ref, b_ref, o_ref, ob_ref, wa_ref, wb_ref, wm_ref, stage, sems):
    @pl.when(pl.program_id(0) == 0)
    def _():
        _load_weight_bf16(wa_hbm, wa_ref, stage, sems)
        _load_weight_bf16(wb_hbm, wb_ref, stage, sems)
        _load_weight_bf16(wm_hbm, wm_ref, stage, sems)

    pa = jnp.dot(ya_ref[...], wa_ref[...], preferred_element_type=F32)
    pb = jnp.dot(yb_ref[...], wb_ref[...], preferred_element_type=F32)
    merged = jax.nn.sigmoid(gates_a_ref[...]) * pa + jax.nn.sigmoid(gates_b_ref[...]) * pb
    mixed = jnp.dot(merged.astype(BF16), wm_ref[...], preferred_element_type=F32)
    y = _layer_norm(DEEPNORM_ALPHA * h_ref[...] + mixed, g_ref[...], b_ref[...])
    o_ref[...] = y
    ob_ref[...] = y.astype(BF16)


def _mix(ya, yb, gates, h, wa, wb, wm, g, b):
    t, d = h.shape
    tm = min(ATT_TILE, t)
    row = pl.BlockSpec((tm, d), lambda i: (i, 0))
    return pl.pallas_call(
        _mix_kernel,
        grid=(t // tm,),
        in_specs=[pl.BlockSpec((tm, A_WIDTH), lambda i: (i, 0)),
                  pl.BlockSpec((tm, B_WIDTH), lambda i: (i, 0)),
                  pl.BlockSpec((tm, d), lambda i: (i, 0)),
                  pl.BlockSpec((tm, d), lambda i: (i, 1)),
                  row,
                  pl.BlockSpec(memory_space=pl.ANY), pl.BlockSpec(memory_space=pl.ANY),
                  pl.BlockSpec(memory_space=pl.ANY),
                  _const_spec((1, d)), _const_spec((1, d))],
        out_specs=[row, row],
        out_shape=[jax.ShapeDtypeStruct((t, d), F32), jax.ShapeDtypeStruct((t, d), BF16)],
        scratch_shapes=[pltpu.VMEM((A_WIDTH, d), BF16), pltpu.VMEM((B_WIDTH, d), BF16),
                        pltpu.VMEM((d, d), BF16),
                        pltpu.VMEM((2, min(WEIGHT_STAGE_ROWS, d), d), F32),
                        pltpu.SemaphoreType.DMA((2,))],
        compiler_params=_params(("arbitrary",)),
        name="mix_out",
    )(ya, yb, gates, gates, h, wa, wb, wm, g.reshape(1, d), b.reshape(1, d))


def _xattn_kernel(hb_ref, h_ref, k_ref, v_ref, wq_hbm, wo_hbm, g_ref, b_ref, wr_ref,
                  rb_ref, ltri_ref, o_ref, idx_ref, gate_ref, rank_ref, cnt_ref, run_ref,
                  wq_ref, wo_ref, stage, sems):
    first_step = (pl.program_id(0) == 0) & (pl.program_id(1) == 0)

    @pl.when(first_step)
    def _():
        _load_weight_bf16(wq_hbm, wq_ref, stage, sems)
        _load_weight_bf16(wo_hbm, wo_ref, stage, sems)

    d = h_ref.shape[1]
    hd = d // X_HEADS
    scale = hd ** -0.5
    q = jnp.dot(hb_ref[...], wq_ref[...], preferred_element_type=F32).astype(BF16)
    outs = []
    for x in range(X_HEADS):
        sl = slice(x * hd, (x + 1) * hd)
        s = lax.dot_general(q[:, sl], k_ref[:, sl], NT_DIMS, preferred_element_type=F32) * scale
        m = jnp.max(s, axis=-1, keepdims=True)
        p = jnp.exp(s - m)
        denom = jnp.sum(p, axis=-1, keepdims=True)
        o = jnp.dot(p.astype(BF16), v_ref[:, sl], preferred_element_type=F32) / denom
        outs.append(o.astype(BF16))
    xo = jnp.dot(jnp.concatenate(outs, axis=1), wo_ref[...], preferred_element_type=F32)
    h_out = _layer_norm(DEEPNORM_ALPHA * h_ref[...] + xo, g_ref[...], b_ref[...])
    o_ref[...] = h_out
    _route_tile(h_out, first_step, wr_ref, rb_ref, ltri_ref, idx_ref, gate_ref,
                rank_ref, cnt_ref, run_ref)


def _xattn_route(hb, h, kv, wq, wo, g, b, w_router, b_router, batch, seq, mem_len):
    t, d = h.shape
    tm = min(ATT_TILE, seq)
    ns = seq // tm
    w_pad = jnp.zeros((d, V7X_LANES), F32).at[:, :N_EXPERTS].set(w_router)
    w_hi = w_pad.astype(BF16)
    w_lo = (w_pad - w_hi.astype(F32)).astype(BF16)
    w_split = jnp.concatenate([w_hi, w_lo], axis=1)
    b_pad = jnp.zeros((1, V7X_LANES), F32).at[0, :N_EXPERTS].set(b_router)
    ltri = jnp.asarray(np.tril(np.ones((tm, tm), np.float32), -1), dtype=BF16)
    row = pl.BlockSpec((tm, d), lambda bi, i: (bi * ns + i, 0))
    small = pl.BlockSpec((tm, V7X_LANES), lambda bi, i: (bi * ns + i, 0))
    return pl.pallas_call(
        _xattn_kernel,
        grid=(batch, ns),
        in_specs=[row, row,
                  pl.BlockSpec((mem_len, d), lambda bi, i: (bi, 0)),
                  pl.BlockSpec((mem_len, d), lambda bi, i: (bi, 1)),
                  pl.BlockSpec(memory_space=pl.ANY), pl.BlockSpec(memory_space=pl.ANY),
                  _const_spec((1, d)), _const_spec((1, d)),
                  _const_spec((d, 2 * V7X_LANES)),
                  _const_spec((1, V7X_LANES)), _const_spec((tm, tm))],
        out_specs=[row, small, small, small,
                   pl.BlockSpec((1, V7X_LANES), lambda bi, i: (0, 0))],
        out_shape=[jax.ShapeDtypeStruct((t, d), F32),
                   jax.ShapeDtypeStruct((t, V7X_LANES), jnp.int32),
                   jax.ShapeDtypeStruct((t, V7X_LANES), F32),
                   jax.ShapeDtypeStruct((t, V7X_LANES), jnp.int32),
                   jax.ShapeDtypeStruct((1, V7X_LANES), F32)],
        scratch_shapes=[pltpu.VMEM((1, V7X_LANES), F32),
                        pltpu.VMEM((d, d), BF16), pltpu.VMEM((d, d), BF16),
                        pltpu.VMEM((2, min(WEIGHT_STAGE_ROWS, d), d), F32),
                        pltpu.SemaphoreType.DMA((2,))],
        compiler_params=_params(("arbitrary", "arbitrary")),
        name="xattn_route",
    )(hb, h, kv, kv, wq, wo, g.reshape(1, d), b.reshape(1, d), w_split, b_pad, ltri)


def _route_tile(h, first_step, w_ref, b_ref, ltri_ref, idx_ref, gate_ref, rank_ref,
                cnt_ref, run_ref):
    @pl.when(first_step)
    def _():
        run_ref[...] = jnp.zeros_like(run_ref)

    h_hi = h.astype(BF16)
    h_lo = (h - h_hi.astype(F32)).astype(BF16)
    hi_terms = jnp.dot(h_hi, w_ref[...], preferred_element_type=F32)
    lo_hi = jnp.dot(h_lo, w_ref[:, 0:V7X_LANES], preferred_element_type=F32)
    logits = (hi_terms[:, 0:V7X_LANES] + (lo_hi + hi_terms[:, V7X_LANES:2 * V7X_LANES])
              + b_ref[...])
    tm = logits.shape[0]
    lane_i = lax.broadcasted_iota(jnp.int32, (tm, V7X_LANES), 1)
    lane = lane_i.astype(F32)
    rem = jnp.where(lane_i < N_EXPERTS, logits, -jnp.inf)
    vals, idxs = [], []
    for _ in range(TOP_K):
        m = jnp.max(rem, axis=-1, keepdims=True)
        ik = jnp.min(jnp.where(rem == m, lane, float(V7X_LANES)), axis=-1, keepdims=True)
        vals.append(m)
        idxs.append(ik)
        rem = jnp.where(lane == ik, -jnp.inf, rem)
    exps = [jnp.exp(v - vals[0]) for v in vals]
    denom = exps[0] + exps[1] + exps[2] + exps[3]
    sel = jnp.zeros((tm, V7X_LANES), F32)
    for ik in idxs:
        sel = sel + (lane == ik).astype(F32)
    before = jnp.dot(ltri_ref[...], sel.astype(BF16), preferred_element_type=F32) + run_ref[...]
    idx_out = jnp.zeros((tm, V7X_LANES), jnp.int32)
    gate_out = jnp.zeros((tm, V7X_LANES), F32)
    rank_out = jnp.zeros((tm, V7X_LANES), jnp.int32)
    for k in range(TOP_K):
        rank_k = jnp.sum(jnp.where(lane == idxs[k], before, 0.0), axis=-1, keepdims=True)
        idx_out = jnp.where(lane_i == k, idxs[k].astype(jnp.int32), idx_out)
        gate_out = jnp.where(lane_i == k, exps[k] / denom, gate_out)
        rank_out = jnp.where(lane_i == k, rank_k.astype(jnp.int32), rank_out)
    idx_ref[...] = idx_out
    gate_ref[...] = gate_out
    rank_ref[...] = rank_out
    run_ref[...] = run_ref[...] + jnp.sum(sel, axis=0, keepdims=True)
    cnt_ref[...] = run_ref[...]


DMA_UNROLL = 8
NUM_DMA_QUEUES = 2
(META_EXPERT, META_FIRST, META_LAST_GROUP, META_NEXT_EXPERT, META_GROUP,
 META_ROWS_USED, META_NUM_GROUPS) = range(7)


def _pack_bf16_pairs(x):
    half = x.shape[1] // 2
    lo = lax.bitcast_convert_type(x[:, :half].astype(BF16).astype(F32), jnp.uint32)
    hi = lax.bitcast_convert_type(x[:, half:].astype(BF16).astype(F32), jnp.uint32)
    return (lo >> 16) | (hi & jnp.uint32(0xFFFF0000))


def _unpack_bf16_pairs(words):
    lo = lax.bitcast_convert_type(words << 16, F32).astype(BF16)
    hi = lax.bitcast_convert_type(words & jnp.uint32(0xFFFF0000), F32).astype(BF16)
    return jnp.concatenate([lo, hi], axis=1)


def _dispatch_kernel(dest_ref, meta_ref, h_ref, xs_hbm, pack_ref, zero_ref, sem, zsem):
    tm = h_ref.shape[0]
    base = pl.program_id(0) * tm
    et = zero_ref.shape[0]
    pack_ref[...] = _pack_bf16_pairs(h_ref[...])

    @pl.when(pl.program_id(0) == 0)
    def _():
        zero_ref[...] = jnp.zeros_like(zero_ref)

        def fill_copy(tile):
            start_row = pl.multiple_of(tile * et, et)
            return pltpu.make_async_copy(zero_ref, xs_hbm.at[pl.ds(start_row, et)], zsem)

        def fill_start(tile, c):
            @pl.when(meta_ref[META_ROWS_USED, tile] < et)
            def _():
                fill_copy(tile).start()
            return c

        def fill_wait(tile, c):
            @pl.when(meta_ref[META_ROWS_USED, tile] < et)
            def _():
                fill_copy(tile).wait()
            return c

        n_tiles = xs_hbm.shape[0] // et
        lax.fori_loop(0, n_tiles, fill_start, 0)
        lax.fori_loop(0, n_tiles, fill_wait, 0)

    def row_copy(r, k):
        return pltpu.make_async_copy(pack_ref.at[pl.ds(r, 1)],
                                     xs_hbm.at[pl.ds(dest_ref[(base + r) * TOP_K + k], 1)], sem)

    def start(r, c):
        for k in range(TOP_K):
            row_copy(r, k).start(priority=k % NUM_DMA_QUEUES)
        return c

    def wait(r, c):
        for k in range(TOP_K):
            row_copy(r, k).wait()
        return c

    lax.fori_loop(0, tm, start, 0, unroll=DMA_UNROLL)
    lax.fori_loop(0, tm, wait, 0, unroll=DMA_UNROLL)


def _dispatch(h, dest, meta, n_rows):
    t, d = h.shape
    tm = min(DISPATCH_TILE, t)
    grid_spec = pltpu.PrefetchScalarGridSpec(
        num_scalar_prefetch=2,
        grid=(t // tm,),
        in_specs=[pl.BlockSpec((tm, d), lambda i, dst, m: (i, 0))],
        out_specs=pl.BlockSpec(memory_space=pl.ANY),
        scratch_shapes=[pltpu.VMEM((tm, d // 2), jnp.uint32),
                        pltpu.VMEM((EXPERT_TILE, d // 2), jnp.uint32),
                        pltpu.SemaphoreType.DMA(()), pltpu.SemaphoreType.DMA(())],
    )
    return pl.pallas_call(
        _dispatch_kernel,
        grid_spec=grid_spec,
        out_shape=jax.ShapeDtypeStruct((n_rows, d // 2), jnp.uint32),
        compiler_params=_params(("arbitrary",)),
        name="moe_dispatch",
    )(dest, meta, h)


def _stream_expert_weights(meta_ref, j, t, nj, slots, make_copies, on_arrival):
    @pl.when((j == 0) & (t == 0))
    def _():
        for c in make_copies(meta_ref[META_EXPERT, 0], 0, 0):
            c.start(priority=WEIGHT_DMA_PRIORITY)

    @pl.when(meta_ref[META_FIRST, t] == 1)
    def _():
        if slots == 1:
            slot = next_slot = 0
        else:
            slot = (j * meta_ref[META_NUM_GROUPS, t] + meta_ref[META_GROUP, t]) & 1
            next_slot = 1 - slot
        for c in make_copies(meta_ref[META_EXPERT, t], j, slot):
            c.wait()
        if slots == 1:
            on_arrival(slot)
        last_group = meta_ref[META_LAST_GROUP, t] == 1

        @pl.when(jnp.logical_not(last_group & (j == nj - 1)))
        def _():
            next_chunk = jnp.where(last_group, j + 1, j)
            for c in make_copies(meta_ref[META_NEXT_EXPERT, t], next_chunk, next_slot):
                c.start(priority=WEIGHT_DMA_PRIORITY)

        if slots == 2:
            on_arrival(slot)


def _for_used_rows(rows_used, o_ref, compute):
    tm = o_ref.shape[0]
    part = tm // EXPERT_TILE_PARTS
    for parts in range(EXPERT_TILE_PARTS + 1):
        used = parts * part

        @pl.when((rows_used > used - part) & (rows_used <= used))
        def _(used=used):
            if used > 0:
                compute(slice(0, used))
            if used < tm:
                o_ref[used:tm, :] = jnp.zeros((tm - used, o_ref.shape[1]), o_ref.dtype)


def _moe_up_kernel(meta_ref, x_ref, w_hbm, b_ref, o_ref, wf32, wbf, sem, *, col_chunk):
    t = pl.program_id(0)
    f = o_ref.shape[1]

    def make_copies(expert, chunk, slot):
        del chunk, slot
        return (pltpu.make_async_copy(w_hbm.at[expert], wf32, sem),)

    def on_arrival(slot):
        del slot
        wbf[...] = wf32[...].astype(BF16)

    _stream_expert_weights(meta_ref, 0, t, 1, 1, make_copies, on_arrival)

    def expert_rows(rows):
        x = _unpack_bf16_pairs(x_ref[rows, :])
        for c0 in range(0, f, col_chunk):
            glu_cols = slice(c0, c0 + col_chunk)
            lin_cols = slice(f + c0, f + c0 + col_chunk)
            glu = jnp.dot(x, wbf[:, glu_cols], preferred_element_type=F32) + b_ref[:, glu_cols]
            lin = jnp.dot(x, wbf[:, lin_cols], preferred_element_type=F32) + b_ref[:, lin_cols]
            glu = jnp.minimum(glu, SWIGLU_LIMIT)
            lin = jnp.clip(lin, -SWIGLU_LIMIT, SWIGLU_LIMIT)
            act = glu * jax.nn.sigmoid(SWIGLU_ALPHA * glu) * (lin + 1.0)
            o_ref[rows, glu_cols] = act.astype(o_ref.dtype)

    _for_used_rows(meta_ref[META_ROWS_USED, t], o_ref, expert_rows)


def _moe_up(xs, w_up, b_up, meta):
    r = xs.shape[0]
    e, d, f2 = w_up.shape
    f = f2 // 2
    tm = EXPERT_TILE
    grid_spec = pltpu.PrefetchScalarGridSpec(
        num_scalar_prefetch=1,
        grid=(r // tm,),
        in_specs=[
            pl.BlockSpec((tm, d // 2), lambda t, m: (t, 0)),
            pl.BlockSpec(memory_space=pl.ANY),
            pl.BlockSpec((None, 1, f2), lambda t, m: (m[META_EXPERT, t], 0, 0)),
        ],
        out_specs=pl.BlockSpec((tm, f), lambda t, m: (t, 0)),
        scratch_shapes=[pltpu.VMEM((d, f2), F32), pltpu.VMEM((d, f2), BF16),
                        pltpu.SemaphoreType.DMA(())],
    )
    return pl.pallas_call(
        functools.partial(_moe_up_kernel, col_chunk=_pick_tile(f, MM_COL_TILE)),
        grid_spec=grid_spec,
        out_shape=jax.ShapeDtypeStruct((r, f), BF16),
        compiler_params=_params(("arbitrary",)),
        name="moe_up",
    )(meta, xs, w_up, b_up.reshape(e, 1, f2))


def _moe_down_kernel(meta_ref, a_ref, w_hbm, b_ref, o_ref, wf32, wbf, sems, *, tn, nj):
    j = pl.program_id(0)
    t = pl.program_id(1)

    def make_copies(expert, chunk, slot):
        col = pl.multiple_of(chunk * tn, tn)
        return (pltpu.make_async_copy(w_hbm.at[expert, :, pl.ds(col, tn)],
                                      wf32.at[slot], sems.at[slot]),)

    def on_arrival(slot):
        wbf[...] = wf32[slot].astype(BF16)

    _stream_expert_weights(meta_ref, j, t, nj, 2, make_copies, on_arrival)

    def expert_rows(rows):
        o_ref[rows, :] = (jnp.dot(a_ref[rows, :], wbf[...], preferred_element_type=F32)
                          + b_ref[...])

    _for_used_rows(meta_ref[META_ROWS_USED, t], o_ref, expert_rows)


def _moe_down(act, w_down, b_down, meta):
    r, f = act.shape
    e, _, d = w_down.shape
    tm = EXPERT_TILE
    tn = _pick_tile(d, EXPERT_DOWN_COL_TILE)
    nj = d // tn
    grid_spec = pltpu.PrefetchScalarGridSpec(
        num_scalar_prefetch=1,
        grid=(nj, r // tm),
        in_specs=[
            pl.BlockSpec((tm, f), lambda j, t, m: (t, 0)),
            pl.BlockSpec(memory_space=pl.ANY),
            pl.BlockSpec((None, 1, tn), lambda j, t, m: (m[META_EXPERT, t], 0, j)),
        ],
        out_specs=pl.BlockSpec((tm, tn), lambda j, t, m: (t, j)),
        scratch_shapes=[pltpu.VMEM((2, f, tn), F32), pltpu.VMEM((f, tn), BF16),
                        pltpu.SemaphoreType.DMA((2,))],
    )
    return pl.pallas_call(
        functools.partial(_moe_down_kernel, tn=tn, nj=nj),
        grid_spec=grid_spec,
        out_shape=jax.ShapeDtypeStruct((r, d), F32),
        compiler_params=_params(("arbitrary", "arbitrary")),
        name="moe_down",
    )(meta, act, w_down, b_down.reshape(e, 1, d))


def _combine_kernel(pos_ref, y_hbm, gate_ref, h_ref, g_ref, b_ref, o_ref, buf_ref, sems):
    tm = o_ref.shape[0]
    i = pl.program_id(0)

    def row_copy(step, r, k, slot):
        src_row = pos_ref[(step * tm + r) * TOP_K + k]
        return pltpu.make_async_copy(y_hbm.at[pl.ds(src_row, 1)],
                                     buf_ref.at[slot, k, pl.ds(r, 1)], sems.at[slot])

    def gather_start(step, slot):
        def body(r, c):
            for k in range(TOP_K):
                row_copy(step, r, k, slot).start(priority=k % NUM_DMA_QUEUES)
            return c
        lax.fori_loop(0, tm, body, 0, unroll=DMA_UNROLL)

    def gather_wait(step, slot):
        def body(r, c):
            for k in range(TOP_K):
                row_copy(step, r, k, slot).wait()
            return c
        lax.fori_loop(0, tm, body, 0, unroll=DMA_UNROLL)

    @pl.when(i == 0)
    def _():
        gather_start(0, 0)

    for next_slot in range(2):
        @pl.when((i + 1 < pl.num_programs(0)) & ((i + 1) & 1 == next_slot))
        def _(next_slot=next_slot):
            gather_start(i + 1, next_slot)

    slot = i & 1
    gather_wait(i, slot)
    gates = gate_ref[...]
    ff = gates[:, 0:1] * buf_ref[slot, 0]
    for k in range(1, TOP_K):
        ff = ff + gates[:, k:k + 1] * buf_ref[slot, k]
    o_ref[...] = _layer_norm(DEEPNORM_ALPHA * h_ref[...] + ff, g_ref[...], b_ref[...])


def _combine(y_rows, pos, gates, h, g, b):
    t, d = h.shape
    tm = min(COMBINE_TILE, t)
    grid_spec = pltpu.PrefetchScalarGridSpec(
        num_scalar_prefetch=1,
        grid=(t // tm,),
        in_specs=[pl.BlockSpec(memory_space=pl.ANY),
                  pl.BlockSpec((tm, V7X_LANES), lambda i, p: (i, 0)),
                  pl.BlockSpec((tm, d), lambda i, p: (i, 0)),
                  pl.BlockSpec((1, d), lambda i, p: (0, 0)),
                  pl.BlockSpec((1, d), lambda i, p: (0, 0))],
        out_specs=pl.BlockSpec((tm, d), lambda i, p: (i, 0)),
        scratch_shapes=[pltpu.VMEM((2, TOP_K, tm, d), F32), pltpu.SemaphoreType.DMA((2,))],
    )
    return pl.pallas_call(
        _combine_kernel,
        grid_spec=grid_spec,
        out_shape=jax.ShapeDtypeStruct((t, d), F32),
        compiler_params=_params(("arbitrary",)),
        name="moe_combine",
    )(pos, y_rows, gates, h, g.reshape(1, d), b.reshape(1, d))


def _expert_layout(idx, rank, counts, n_tiles, tm):
    experts = jnp.arange(N_EXPERTS, dtype=jnp.int32)
    tiles_per = (counts + tm - 1) // tm
    tile_end = jnp.cumsum(tiles_per).astype(jnp.int32)
    tile_off = tile_end - tiles_per
    dest = jnp.sum(jnp.where(idx[..., None] == experts, tile_off * tm, 0), axis=-1) + rank

    nonempty = counts > 0
    first_e = jnp.min(jnp.where(nonempty, experts, N_EXPERTS))
    last_e = jnp.max(jnp.where(nonempty, experts, 0))
    group_of = jnp.cumsum(nonempty.astype(jnp.int32)) - 1
    later = (experts[None, :] > experts[:, None]) & nonempty[None, :]
    next_e = jnp.min(jnp.where(later, experts[None, :], N_EXPERTS), axis=1)
    next_e = jnp.where(next_e == N_EXPERTS, first_e, next_e)

    tid = jnp.arange(n_tiles, dtype=jnp.int32)
    valid = tid < tile_end[-1]
    te = jnp.sum((tid[:, None] >= tile_end[None, :]).astype(jnp.int32), axis=1)
    te = jnp.where(valid, jnp.minimum(te, N_EXPERTS - 1), last_e)
    onehot = te[:, None] == experts[None, :]

    def lookup(table):
        return jnp.sum(jnp.where(onehot, table[None, :], 0), axis=1)

    local = tid - lookup(tile_off)
    rows_used = jnp.where(valid, jnp.clip(lookup(counts) - local * tm, 0, tm), 0)
    meta = jnp.stack([
        te,
        (valid & (local == 0)).astype(jnp.int32),
        (te == last_e).astype(jnp.int32),
        lookup(next_e),
        lookup(group_of),
        rows_used,
        jnp.broadcast_to(jnp.sum(nonempty.astype(jnp.int32)), (n_tiles,)),
    ]).astype(jnp.int32)
    return dest.reshape(-1).astype(jnp.int32), meta


def _moe(h, routing, w_up, b_up, w_down, b_down, g, b):
    t, d = h.shape
    tm = EXPERT_TILE
    idx_p, gate_p, rank_p, cnt_p = routing
    n_rows = t * TOP_K + N_EXPERTS * tm
    dest, meta = _expert_layout(idx_p[:, :TOP_K], rank_p[:, :TOP_K],
                                cnt_p[0, :N_EXPERTS].astype(jnp.int32), n_rows // tm, tm)
    xs = _dispatch(h, dest, meta, n_rows)
    act = _moe_up(xs, w_up, b_up, meta)
    y_rows = _moe_down(act, w_down, b_down, meta)
    return _combine(y_rows, dest, gate_p, h, g, b)


def kernel(x, mem, ln_in_g, ln_in_b, rel_bias, w_in, b_in, attn_sinks, w_a_out, w_b_out,
           w_mix_out, ln1_g, ln1_b, w_xq, w_xkv, w_xo, ln2_g, ln2_b, w_router, b_router,
           w_up, b_up, w_down, b_down, ln3_g, ln3_b):
    batch, seq, d = x.shape
    mem_len = mem.shape[1]
    t = batch * seq
    a_cols = A_WIDTH + 2 * A_KV_WIDTH
    b_cols = 3 * B_WIDTH

    g_col = a_cols + b_cols
    h, hb, qkv_a = _ln_proj(x.reshape(t, d), ln_in_g, ln_in_b, w_in[0], b_in[0, :a_cols],
                            a_cols, BF16)
    for l in range(DEPTH):
        if l > 0:
            qkv_a = _mm_bias(hb, w_in[l], b_in[l, :a_cols], 0, a_cols, BF16, "in_proj_a")
        qkv_b = _mm_bias(hb, w_in[l], b_in[l, a_cols:g_col], a_cols, b_cols, BF16, "in_proj_b")
        gates = _mm_bias(hb, w_in[l], b_in[l, g_col:], g_col, 2 * d, F32, "in_proj_gates")
        ya = _swa(qkv_a, attn_sinks[l], rel_bias, batch, seq)
        yb = _stick_breaking(qkv_b, batch, seq)
        h, hb = _mix(ya, yb, gates, h, w_a_out[l], w_b_out[l], w_mix_out[l], ln1_g[l], ln1_b[l])
        kv = _mm_bias(mem.reshape(batch * mem_len, d).astype(BF16), w_xkv[l],
                      jnp.zeros((2 * d,), F32), 0, 2 * d, BF16, "mem_kv",
                      col_tile=MEM_KV_COL_TILE)
        h, *routing = _xattn_route(hb, h, kv, w_xq[l], w_xo[l], ln2_g[l], ln2_b[l],
                                   w_router[l], b_router[l], batch, seq, mem_len)
        h = _moe(h, routing, w_up[l], b_up[l], w_down[l], b_down[l], ln3_g[l], ln3_b[l])
        if l + 1 < DEPTH:
            hb = h.astype(BF16)
    return h.reshape(batch, seq, d)
```
